```python
import math
import jax, jax.numpy as jnp
from jax import lax
import numpy as np

D_MODEL = 1024
BATCH = 2
SEQ = 8192
DEPTH = 1
DEC_BATCH = 128
DEC_SEQ = 4
PAST_LEN = 16384
PAGE_SIZE = 128

NORM_EPS = 1e-6
GDN_HEADS = 8
GDN_DK = 128
GDN_DV = 128
GDN_CONV = 4
GDN_CHUNK = 64
GDN_QK_WIDTH = GDN_HEADS * GDN_DK
GDN_WIDTH = GDN_HEADS * GDN_DV
GDN_CONV_DIM = 2 * GDN_QK_WIDTH + GDN_WIDTH
SWA_Q_HEADS = 16
SWA_KV_HEADS = 4
SWA_HEAD_DIM = 64
SWA_GROUP = SWA_Q_HEADS // SWA_KV_HEADS
SWA_WIDTH = SWA_Q_HEADS * SWA_HEAD_DIM
SWA_KV_WIDTH = SWA_KV_HEADS * SWA_HEAD_DIM
WINDOW = 128
IN_SIZES = (GDN_QK_WIDTH, GDN_QK_WIDTH, GDN_WIDTH, GDN_WIDTH, GDN_HEADS, GDN_HEADS, SWA_WIDTH, SWA_KV_WIDTH, SWA_KV_WIDTH, D_MODEL, D_MODEL)
IN_DIM = sum(IN_SIZES)
N_GROUPS = 4
EXPERTS_PER_GROUP = 8
N_EXPERTS = N_GROUPS * EXPERTS_PER_GROUP
TOP_K = 2
EXPERT_FF = 512
MOE_BLOCK = 128

kernel_name = 'hybrid_gdn_swa_sink_hmoe_step'


def rmsnorm(x, gain):
    xf = x.astype(jnp.float32)
    y = xf * lax.rsqrt(jnp.mean(xf * xf, axis=-1, keepdims=True) + NORM_EPS)
    return (y * gain.astype(jnp.float32)).astype(x.dtype)


def l2norm(x):
    return x * lax.rsqrt(jnp.sum(x * x, axis=-1, keepdims=True) + NORM_EPS)


def causal_conv(x, buf, w):
    T = x.shape[1]
    xp = jnp.concatenate([buf.astype(x.dtype), x], axis=1)
    y = sum(w[j] * xp[:, j:j + T] for j in range(GDN_CONV))
    return y, xp[:, T:]


def gated_delta_rule(q, k, v, g, beta, S0):
    B, T, H, DK = q.shape
    DV = v.shape[-1]
    C = min(GDN_CHUNK, T)
    Tp = -(-T // C) * C
    pad = Tp - T

    def prep(x):
        x = jnp.pad(x, [(0, 0), (0, pad)] + [(0, 0)] * (x.ndim - 2))
        x = x.reshape((B, Tp // C, C) + x.shape[2:])
        return x.transpose((1, 0, 3, 2) + tuple(range(4, x.ndim)))

    q, k, v, g, beta = prep(q), prep(k), prep(v), prep(g), prep(beta)
    gc = jnp.cumsum(g, axis=-1)
    idx = jnp.arange(C)
    causal = idx[:, None] >= idx[None, :]
    strict = idx[:, None] > idx[None, :]
    diff = gc[..., :, None] - gc[..., None, :]
    decay = jnp.where(causal, jnp.exp(jnp.where(causal, diff, 0.0)), 0.0)
    kb = k * beta[..., None]
    A = jnp.where(strict, jnp.einsum('nbhid,nbhjd->nbhij', kb, k) * decay, 0.0)
    L = A + jnp.eye(C, dtype=A.dtype)
    rhs = jnp.concatenate([v * beta[..., None], kb * jnp.exp(gc)[..., None]], axis=-1)
    sol = lax.linalg.triangular_solve(L, rhs, left_side=True, lower=True, unit_diagonal=True)
    u, w = sol[..., :DV], sol[..., DV:]
    qk = jnp.einsum('nbhid,nbhjd->nbhij', q, k) * decay
    q_dec = q * jnp.exp(gc)[..., None]
    k_dec = k * jnp.exp(gc[..., -1:] - gc)[..., None]
    g_end = jnp.exp(gc[..., -1])

    def chunk_step(S, inp):
        u_n, w_n, q_n, k_n, qk_n, ge = inp
        v_new = u_n - jnp.einsum('bhck,bhkv->bhcv', w_n, S)
        o_n = jnp.einsum('bhck,bhkv->bhcv', q_n, S) + jnp.einsum('bhij,bhjv->bhiv', qk_n, v_new)
        S = S * ge[..., None, None] + jnp.einsum('bhck,bhcv->bhkv', k_n, v_new)
        return S, o_n

    S, o = lax.scan(chunk_step, S0, (u, w, q_dec, k_dec, qk, g_end))
    o = o.transpose(1, 0, 3, 2, 4).reshape(B, Tp, H, DV)[:, :T]
    return o, S


def sink_softmax(s, mask, sink):
    s = jnp.where(mask, s, -jnp.inf)
    m = jnp.maximum(jnp.max(s, axis=-1, keepdims=True), sink)
    p = jnp.exp(s - m)
    return p / (jnp.sum(p, axis=-1, keepdims=True) + jnp.exp(sink - m))


def swa_banded(q, k, v, sinks):
    B, T = q.shape[:2]
    NB = T // WINDOW
    qb = q.reshape(B, NB, WINDOW, SWA_KV_HEADS, SWA_GROUP, SWA_HEAD_DIM)
    kb = k.reshape(B, NB, WINDOW, SWA_KV_HEADS, SWA_HEAD_DIM)
    vb = v.reshape(B, NB, WINDOW, SWA_KV_HEADS, SWA_HEAD_DIM)
    shift = ((0, 0), (1, 0), (0, 0), (0, 0), (0, 0))
    kk = jnp.concatenate([jnp.pad(kb, shift)[:, :-1], kb], axis=2)
    vv = jnp.concatenate([jnp.pad(vb, shift)[:, :-1], vb], axis=2)
    s = jnp.einsum('bnqhgd,bnkhd->bnhgqk', qb, kk, preferred_element_type=jnp.float32) * SWA_HEAD_DIM ** -0.5
    qi = jnp.arange(WINDOW)[:, None]
    kj = jnp.arange(2 * WINDOW)[None, :]
    dist = qi + WINDOW - kj
    band = (dist >= 0) & (dist <= WINDOW)
    valid = (jnp.arange(NB)[:, None, None] * WINDOW + kj[None] - WINDOW) >= 0
    mask = (band[None] & valid)[None, :, None, None]
    sink = sinks.astype(jnp.float32).reshape(1, 1, SWA_KV_HEADS, SWA_GROUP, 1, 1)
    p = sink_softmax(s, mask, sink)
    o = jnp.einsum('bnhgqk,bnkhd->bnqhgd', p.astype(v.dtype), vv)
    return o.reshape(B, T, SWA_WIDTH)


def swa_with_buffer(q, k, v, k_buf, v_buf, sinks):
    B, T = q.shape[:2]
    kk = jnp.concatenate([k_buf.astype(k.dtype), k], axis=1)
    vv = jnp.concatenate([v_buf.astype(v.dtype), v], axis=1)
    s = jnp.einsum('bqhgd,bkhd->bhgqk', q, kk, preferred_element_type=jnp.float32) * SWA_HEAD_DIM ** -0.5
    dist = (jnp.arange(T)[:, None] + WINDOW) - jnp.arange(WINDOW + T)[None, :]
    mask = (dist >= 0) & (dist <= WINDOW)
    sink = sinks.astype(jnp.float32).reshape(1, SWA_KV_HEADS, SWA_GROUP, 1, 1)
    p = sink_softmax(s, mask, sink)
    o = jnp.einsum('bhgqk,bkhd->bqhgd', p.astype(v.dtype), vv)
    return o.reshape(B, T, SWA_WIDTH), kk[:, -WINDOW:], vv[:, -WINDOW:]


def token_mixer(h, past, w_in, conv_w, a_log, dt_bias, gdn_norm, sinks, w_br_gdn, w_br_swa, w_out):
    B, T, _ = h.shape
    f32 = jnp.float32
    proj = h @ w_in
    splits = [int(s) for s in np.cumsum(IN_SIZES)[:-1]]
    q_a, k_a, v_a, z_a, a_a, b_a, q_b, k_b, v_b, gate_a, gate_b = jnp.split(proj, splits, axis=-1)
    if past is None:
        S0 = jnp.zeros((B, GDN_HEADS, GDN_DK, GDN_DV), f32)
        conv_buf = jnp.zeros((B, GDN_CONV - 1, GDN_CONV_DIM), h.dtype)
    else:
        S0, conv_buf = past[0].astype(f32), past[1]
    qkv, new_conv = causal_conv(jnp.concatenate([q_a, k_a, v_a], axis=-1), conv_buf, conv_w)
    qkv = jax.nn.silu(qkv.astype(f32))
    q, k, v = jnp.split(qkv, [GDN_QK_WIDTH, 2 * GDN_QK_WIDTH], axis=-1)
    q = l2norm(q.reshape(B, T, GDN_HEADS, GDN_DK)) * GDN_DK ** -0.5
    k = l2norm(k.reshape(B, T, GDN_HEADS, GDN_DK))
    v = v.reshape(B, T, GDN_HEADS, GDN_DV)
    beta = jax.nn.sigmoid(b_a.astype(f32))
    g = -jnp.exp(a_log.astype(f32)) * jax.nn.softplus(a_a.astype(f32) + dt_bias.astype(f32))
    o_a, S = gated_delta_rule(q, k, v, g, beta, S0)
    o_a = rmsnorm(o_a, gdn_norm) * jax.nn.silu(z_a.astype(f32).reshape(B, T, GDN_HEADS, GDN_DV))
    o_a = o_a.reshape(B, T, GDN_WIDTH).astype(h.dtype)
    qs = q_b.reshape(B, T, SWA_KV_HEADS, SWA_GROUP, SWA_HEAD_DIM)
    ks = k_b.reshape(B, T, SWA_KV_HEADS, SWA_HEAD_DIM)
    vs = v_b.reshape(B, T, SWA_KV_HEADS, SWA_HEAD_DIM)
    if past is None:
        o_b = swa_banded(qs, ks, vs, sinks)
        new_k, new_v = ks[:, -WINDOW:], vs[:, -WINDOW:]
    else:
        o_b, new_k, new_v = swa_with_buffer(qs, ks, vs, past[2], past[3], sinks)
    merged = jax.nn.sigmoid(gate_a) * (o_a @ w_br_gdn) + jax.nn.sigmoid(gate_b) * (o_b @ w_br_swa)
    return merged @ w_out, (S.astype(h.dtype), new_conv, new_k, new_v)


def hier_route(h, w_group, b_group, w_router, b_router):
    f32 = jnp.float32
    hf = h.astype(f32)
    g_logits = hf @ w_group.astype(f32) + b_group.astype(f32)
    _, g_idx = lax.top_k(g_logits, 1)
    p_group = jnp.take_along_axis(jax.nn.softmax(g_logits, axis=-1), g_idx, axis=-1)
    e_logits = (hf @ w_router.astype(f32) + b_router.astype(f32)).reshape(-1, N_GROUPS, EXPERTS_PER_GROUP)
    e_logits = jnp.take_along_axis(e_logits, g_idx[:, :, None], axis=1)[:, 0]
    top_p, top_i = lax.top_k(jax.nn.softmax(e_logits, axis=-1), TOP_K)
    weights = p_group * top_p / jnp.sum(top_p, axis=-1, keepdims=True)
    ids = (g_idx * EXPERTS_PER_GROUP + top_i).astype(jnp.int32)
    return ids, weights


def moe_ffn(h, ids, weights, w_gate, w_up, w_down):
    N, D = h.shape
    A = N * TOP_K
    flat_e = ids.reshape(A)
    flat_t = jnp.repeat(jnp.arange(N, dtype=jnp.int32), TOP_K)
    flat_w = weights.reshape(A).astype(h.dtype)
    order = jnp.argsort(flat_e)
    se = flat_e[order]
    counts = jnp.zeros((N_EXPERTS,), jnp.int32).at[flat_e].add(1)
    pcounts = (counts + MOE_BLOCK - 1) // MOE_BLOCK * MOE_BLOCK
    starts = jnp.cumsum(counts) - counts
    pends = jnp.cumsum(pcounts)
    pstarts = pends - pcounts
    dest = pstarts[se] + jnp.arange(A, dtype=jnp.int32) - starts[se]
    n_blocks = -(-A // MOE_BLOCK) + N_EXPERTS
    P = n_blocks * MOE_BLOCK
    slot_tok = jnp.full((P,), N, jnp.int32).at[dest].set(flat_t[order])
    slot_w = jnp.zeros((P,), h.dtype).at[dest].set(flat_w[order])
    block_start = jnp.arange(n_blocks, dtype=jnp.int32) * MOE_BLOCK
    block_e = jnp.minimum(jnp.sum(block_start[:, None] >= pends[None, :], axis=1), N_EXPERTS - 1)
    hp = jnp.concatenate([h, jnp.zeros((1, D), h.dtype)], axis=0)
    xb = hp[slot_tok].reshape(n_blocks, MOE_BLOCK, D)

    def expert_block(args):
        xblk, e = args
        return (jax.nn.silu(xblk @ w_gate[e]) * (xblk @ w_up[e])) @ w_down[e]

    yb = lax.map(expert_block, (xb, block_e)).reshape(P, D)
    y = jnp.zeros((N + 1, D), h.dtype).at[slot_tok].add(yb * slot_w[:, None])
    return y[:N]


def decoder_layer(x, c, past, w_ada, b_ada, g_mix_pre, g_mix_post, g_ffn_pre, g_ffn_post,
                  w_in, conv_w, a_log, dt_bias, gdn_norm, sinks, w_br_gdn, w_br_swa, w_out,
                  w_group, b_group, w_router, b_router, w_gate, w_up, w_down):
    mod = (jax.nn.silu(c) @ w_ada + b_ada)[:, None, :]
    sh1, sc1, gt1, sh2, sc2, gt2 = jnp.split(mod, 6, axis=-1)
    h = rmsnorm(x, g_mix_pre) * (1 + sc1) + sh1
    mix, new_state = token_mixer(h, past, w_in, conv_w, a_log, dt_bias, gdn_norm, sinks, w_br_gdn, w_br_swa, w_out)
    x = x + gt1 * rmsnorm(mix, g_mix_post)
    h = rmsnorm(x, g_ffn_pre) * (1 + sc2) + sh2
    B, T, D = h.shape
    hf = h.reshape(B * T, D)
    ids, weights = hier_route(hf, w_group, b_group, w_router, b_router)
    f = moe_ffn(hf, ids, weights, w_gate, w_up, w_down).reshape(B, T, D)
    x = x + gt2 * rmsnorm(f, g_ffn_post)
    return x, new_state


def setup_inputs(seed: int = 0) -> dict:
    key = jax.random.key(seed)
    ks = list(jax.random.split(key, 40))
    f32 = jnp.float32

    def nrm(i, shape, scale):
        return jax.random.normal(ks[i], shape, f32) * scale

    def gain(i, shape):
        return 1.0 + 0.05 * jax.random.normal(ks[i], shape, f32)

    dt = jnp.exp(jax.random.uniform(ks[20], (DEPTH, GDN_HEADS), f32, minval=math.log(1e-3), maxval=math.log(1e-1)))
    return {
        'x_prompt': nrm(0, (BATCH, SEQ, D_MODEL), 1.0),
        'x_sample': nrm(1, (DEC_BATCH, DEC_SEQ, D_MODEL), 1.0),
        'state_gdn': nrm(2, (DEPTH, DEC_BATCH, GDN_HEADS, GDN_DK, GDN_DV), 0.1),
        'state_conv': nrm(3, (DEPTH, DEC_BATCH, GDN_CONV - 1, GDN_CONV_DIM), 1.0),
        'cache_k_win': nrm(4, (DEPTH, DEC_BATCH, WINDOW, SWA_KV_HEADS, SWA_HEAD_DIM), 1.0),
        'cache_v_win': nrm(5, (DEPTH, DEC_BATCH, WINDOW, SWA_KV_HEADS, SWA_HEAD_DIM), 1.0),
        'c_prompt': nrm(6, (BATCH, D_MODEL), 1.0),
        'c_sample': nrm(7, (DEC_BATCH, D_MODEL), 1.0),
        'w_ada': nrm(8, (DEPTH, D_MODEL, 6 * D_MODEL), 0.3 * D_MODEL ** -0.5),
        'b_ada': nrm(9, (DEPTH, 6 * D_MODEL), 0.02),
        'g_mix_pre': gain(10, (DEPTH, D_MODEL)),
        'g_mix_post': gain(11, (DEPTH, D_MODEL)),
        'g_ffn_pre': gain(12, (DEPTH, D_MODEL)),
        'g_ffn_post': gain(13, (DEPTH, D_MODEL)),
        'w_in': nrm(14, (DEPTH, D_MODEL, IN_DIM), D_MODEL ** -0.5),
        'conv_w': nrm(15, (DEPTH, GDN_CONV, GDN_CONV_DIM), GDN_CONV ** -0.5),
        'a_log': jnp.log(jax.random.uniform(ks[16], (DEPTH, GDN_HEADS), f32, minval=1.0, maxval=16.0)),
        'dt_bias': dt + jnp.log(-jnp.expm1(-dt)),
        'gdn_norm': gain(17, (DEPTH, GDN_DV)),
        'sinks': nrm(18, (DEPTH, SWA_Q_HEADS), 0.5),
        'w_br_gdn': nrm(19, (DEPTH, GDN_WIDTH, D_MODEL), GDN_WIDTH ** -0.5),
        'w_br_swa': nrm(21, (DEPTH, SWA_WIDTH, D_MODEL), SWA_WIDTH ** -0.5),
        'w_out': nrm(22, (DEPTH, D_MODEL, D_MODEL), D_MODEL ** -0.5),
        'w_group': nrm(23, (DEPTH, D_MODEL, N_GROUPS), D_MODEL ** -0.5),
        'b_group': nrm(24, (DEPTH, N_GROUPS), 0.01),
        'w_router': nrm(25, (DEPTH, D_MODEL, N_EXPERTS), D_MODEL ** -0.5),
        'b_router': nrm(26, (DEPTH, N_EXPERTS), 0.01),
        'w_gate': nrm(27, (DEPTH, N_EXPERTS, D_MODEL, EXPERT_FF), D_MODEL ** -0.5),
        'w_up': nrm(28, (DEPTH, N_EXPERTS, D_MODEL, EXPERT_FF), D_MODEL ** -0.5),
        'w_down': nrm(29, (DEPTH, N_EXPERTS, EXPERT_FF, D_MODEL), EXPERT_FF ** -0.5),
    }


def reference(x_prompt, x_sample, state_gdn, state_conv, cache_k_win, cache_v_win, c_prompt, c_sample,
              w_ada, b_ada, g_mix_pre, g_mix_post, g_ffn_pre, g_ffn_post, w_in, conv_w, a_log, dt_bias,
              gdn_norm, sinks, w_br_gdn, w_br_swa, w_out, w_group, b_group, w_router, b_router,
              w_gate, w_up, w_down):
    y_prompt, y_sample = x_prompt, x_sample
    new_p, new_s = [], []
    for l in range(DEPTH):
        lw = (w_ada[l], b_ada[l], g_mix_pre[l], g_mix_post[l], g_ffn_pre[l], g_ffn_post[l],
              w_in[l], conv_w[l], a_log[l], dt_bias[l], gdn_norm[l], sinks[l], w_br_gdn[l], w_br_swa[l], w_out[l],
              w_group[l], b_group[l], w_router[l], b_router[l], w_gate[l], w_up[l], w_down[l])
        y_prompt, st_p = decoder_layer(y_prompt, c_prompt, None, *lw)
        past = (state_gdn[l], state_conv[l], cache_k_win[l], cache_v_win[l])
        y_sample, st_s = decoder_layer(y_sample, c_sample, past, *lw)
        new_p.append(st_p)
        new_s.append(st_s)

    def stack(states, i):
        return jnp.stack([s[i] for s in states])

    return (y_prompt, y_sample,
            stack(new_p, 0), stack(new_p, 1), stack(new_p, 2), stack(new_p, 3),
            stack(new_s, 0), stack(new_s, 1), stack(new_s, 2), stack(new_s, 3))
```

```python
import functools

import jax
import jax.numpy as jnp
from jax import lax
from jax.experimental import pallas as pl
from jax.experimental.pallas import tpu as pltpu

F32 = jnp.float32
BF16 = jnp.bfloat16
I32 = jnp.int32

D_MODEL = 1024
NORM_EPS = 1e-6
GDN_HEADS = 8
GDN_DK = 128
GDN_DV = 128
GDN_CONV = 4
GDN_CHUNK = 64
GDN_QK_WIDTH = GDN_HEADS * GDN_DK
GDN_WIDTH = GDN_HEADS * GDN_DV
GDN_CONV_DIM = 2 * GDN_QK_WIDTH + GDN_WIDTH
SWA_Q_HEADS = 16
SWA_KV_HEADS = 4
SWA_HEAD_DIM = 64
SWA_GROUP = SWA_Q_HEADS // SWA_KV_HEADS
SWA_WIDTH = SWA_Q_HEADS * SWA_HEAD_DIM
SWA_KV_WIDTH = SWA_KV_HEADS * SWA_HEAD_DIM
WINDOW = 128
N_GROUPS = 4
EXPERTS_PER_GROUP = 8
N_EXPERTS = N_GROUPS * EXPERTS_PER_GROUP
TOP_K = 2
EXPERT_FF = 512

LANES = 128
BF16_ROWS = 16
VMEM_LIMIT = 56 * 1024 * 1024

_C_QKV = 0
_C_Z = _C_QKV + GDN_CONV_DIM
_C_QB = _C_Z + GDN_WIDTH
_C_KVB = _C_QB + SWA_WIDTH
_C_GA = _C_KVB + 2 * SWA_KV_WIDTH
_C_GB = _C_GA + D_MODEL
_C_AB = _C_GB + D_MODEL
IN_COLS = _C_AB + LANES
PROJ_TILE = 512

ROW_TILE = 512
GDN_PREP_ROWS = 256
GDN_SCAN_ROWS = 512
GDN_SAMPLE_CHUNK = 16
GDN_SAMPLE_GROUP = 4
SWA_SAMPLE_GROUP = 8
MOE_ROWS = 256
COMBINE_ROWS = 256


def _cparams(*sem):
    return pltpu.CompilerParams(dimension_semantics=sem, vmem_limit_bytes=VMEM_LIMIT)


def _bdot(a, b):
    return jnp.dot(a.astype(BF16), b.astype(BF16), preferred_element_type=F32)


def _bdot_nt(a, b):
    return lax.dot_general(a.astype(BF16), b.astype(BF16), (((1,), (1,)), ((), ())),
                           preferred_element_type=F32)


def _bdot_tn(a, b):
    return lax.dot_general(a.astype(BF16), b.astype(BF16), (((0,), (0,)), ((), ())),
                           preferred_element_type=F32)


def _sigmoid(x):
    return 1.0 / (1.0 + jnp.exp(-x))


def _silu(x):
    return x * _sigmoid(x)


def _rms(x, gain):
    return x * lax.rsqrt(jnp.mean(x * x, axis=-1, keepdims=True) + NORM_EPS) * gain


def _iota2(shape, dim):
    return lax.broadcasted_iota(I32, shape, dim)


def _adaln_kernel(c_ref, w_ref, b_ref, o_ref):
    o_ref[...] = _bdot(_silu(c_ref[...]), w_ref[...]) + b_ref[...]


def _adaln(c_all, w_ada, b_ada):
    rows = c_all.shape[0]
    n_out = w_ada.shape[1]
    tn = D_MODEL
    return pl.pallas_call(
        _adaln_kernel,
        grid=(n_out // tn,),
        in_specs=[pl.BlockSpec((rows, D_MODEL), lambda j: (0, 0)),
                  pl.BlockSpec((D_MODEL, tn), lambda j: (0, j)),
                  pl.BlockSpec((1, tn), lambda j: (0, j))],
        out_specs=pl.BlockSpec((rows, tn), lambda j: (0, j)),
        out_shape=jax.ShapeDtypeStruct((rows, n_out), F32),
        compiler_params=_cparams("arbitrary"),
        name="adaln",
    )(c_all, w_ada, b_ada.reshape(1, n_out))


def _inproj_kernel(x_ref, g_ref, sc_ref, sh_ref, w_ref,
                   qkv_ref, z_ref, qb_ref, kvb_ref, ga_ref, gb_ref, ab_ref):
    h = (_rms(x_ref[...], g_ref[...]) * (1.0 + sc_ref[0]) + sh_ref[0]).astype(BF16)

    def fill(ref, c0, width, fn):
        step = min(PROJ_TILE, width)
        for c in range(0, width, step):
            acc = jnp.dot(h, w_ref[:, c0 + c:c0 + c + step], preferred_element_type=F32)
            ref[:, c:c + step] = fn(acc).astype(ref.dtype)

    ident = lambda v: v
    fill(qkv_ref, _C_QKV, GDN_CONV_DIM, ident)
    fill(z_ref, _C_Z, GDN_WIDTH, _silu)
    fill(qb_ref, _C_QB, SWA_WIDTH, ident)
    fill(kvb_ref, _C_KVB, 2 * SWA_KV_WIDTH, ident)
    fill(ga_ref, _C_GA, D_MODEL, _sigmoid)
    fill(gb_ref, _C_GB, D_MODEL, _sigmoid)
    fill(ab_ref, _C_AB, LANES, ident)


def _mod_spec(mod, n_tiles):
    tiles_per_mod = n_tiles // mod.shape[0]
    return pl.BlockSpec((1, mod.shape[1], D_MODEL), lambda i: (i // tiles_per_mod, 0, 0))


def _inproj(x2d, gain, sc, sh, w_prep, tm):
    rows = x2d.shape[0]
    n_tiles = rows // tm
    widths = (GDN_CONV_DIM, GDN_WIDTH, SWA_WIDTH, 2 * SWA_KV_WIDTH, D_MODEL, D_MODEL, LANES)
    dtypes = (BF16, BF16, BF16, BF16, BF16, BF16, F32)
    return pl.pallas_call(
        _inproj_kernel,
        grid=(n_tiles,),
        in_specs=[pl.BlockSpec((tm, D_MODEL), lambda i: (i, 0)),
                  pl.BlockSpec((1, D_MODEL), lambda i: (0, 0)),
                  _mod_spec(sc, n_tiles), _mod_spec(sh, n_tiles),
                  pl.BlockSpec((D_MODEL, IN_COLS), lambda i: (0, 0))],
        out_specs=[pl.BlockSpec((tm, w), lambda i: (i, 0)) for w in widths],
        out_shape=[jax.ShapeDtypeStruct((rows, w), dt) for w, dt in zip(widths, dtypes)],
        compiler_params=_cparams("arbitrary"),
        name="inproj",
    )(x2d, gain.reshape(1, D_MODEL), sc, sh, w_prep)


def _cumsum_rows(g):
    c = g.shape[0]
    tril = (_iota2((c, c), 0) >= _iota2((c, c), 1)).astype(BF16)
    hi = g.astype(BF16)
    r1 = g - hi.astype(F32)
    mid = r1.astype(BF16)
    lo = (r1 - mid.astype(F32)).astype(BF16)
    dot = lambda p: jnp.dot(tril, p, preferred_element_type=F32)
    return dot(hi) + dot(mid) + dot(lo)


def _unit_lower_inverse_offset(a):
    c = a.shape[0]
    ii = _iota2((c, c), 0)
    jj = _iota2((c, c), 1)

    def same_block(shift):
        return lax.shift_right_logical(ii, shift) == lax.shift_right_logical(jj, shift)

    n = jnp.where(same_block(1), -a, 0.0)
    shift = 1
    while (1 << shift) < c:
        off = jnp.where(same_block(shift + 1), jnp.where(same_block(shift), 0.0, a), 0.0)
        x = off + _bdot(off, n)
        n = n - x - _bdot(n, x)
        shift += 1
    return n


def _chunk_prep(q, k, v, gcol, grow, bcol):
    c = q.shape[0]
    ii = _iota2((c, c), 0)
    jj = _iota2((c, c), 1)
    causal = ii >= jj
    decay = jnp.where(causal, jnp.exp(jnp.where(causal, gcol - grow, 0.0)), 0.0)
    kb = k * bcol
    both = _bdot_nt(jnp.concatenate([q, kb], axis=0), k)
    qk = both[:c] * decay
    a = jnp.where(ii > jj, both[c:] * decay, 0.0)
    n = _unit_lower_inverse_offset(a)
    eg = jnp.exp(gcol)
    rhs = jnp.concatenate([v * bcol, kb * eg], axis=1)
    uw = rhs + _bdot(n, rhs)
    g_last = gcol[c - 1:c, :]
    return uw[:, :GDN_DV], uw[:, GDN_DV:], q * eg, k * jnp.exp(g_last - gcol), qk


def _chunk_step(s, u, w, qd, kd, qk, ge):
    c = u.shape[0]
    both = _bdot(jnp.concatenate([w, qd], axis=0), s)
    v_new = u.astype(F32) - both[:c]
    o = both[c:] + _bdot(qk, v_new)
    s_new = s * ge + _bdot_tn(kd, v_new)
    return o, s_new


def _conv_act(xp_ref, cw_ref, r0, rows, c0):
    cols = slice(c0, c0 + LANES)
    acc = cw_ref[3:4, cols] * xp_ref[r0:r0 + rows, cols]
    for j in range(GDN_CONV - 1):
        acc = acc + cw_ref[j:j + 1, cols] * xp_ref[r0 - 3 + j:r0 - 3 + j + rows, cols]
    return _silu(acc)


def _l2n(x):
    return x * lax.rsqrt(jnp.sum(x * x, axis=-1, keepdims=True) + NORM_EPS)


def _softplus(x):
    return jnp.maximum(x, 0.0) + jnp.log1p(jnp.exp(-jnp.abs(x)))


def _head_cols(hd):
    return (hd * GDN_DK, GDN_QK_WIDTH + hd * GDN_DK, 2 * GDN_QK_WIDTH + hd * GDN_DV)


def _activate_qkv(xp_ref, cw_ref, act_ref, r0, rows):
    for hd in range(GDN_HEADS):
        cq, ck, cv = _head_cols(hd)
        act_ref[0:rows, cq:cq + LANES] = _l2n(_conv_act(xp_ref, cw_ref, r0, rows, cq)) * (GDN_DK ** -0.5)
        act_ref[0:rows, ck:ck + LANES] = _l2n(_conv_act(xp_ref, cw_ref, r0, rows, ck))
        act_ref[0:rows, cv:cv + LANES] = _conv_act(xp_ref, cw_ref, r0, rows, cv)


def _decay_beta(ab, hp_ref):
    g = -jnp.exp(hp_ref[0:1, :]) * _softplus(ab + hp_ref[1:2, :])
    return g, _sigmoid(ab)


def _gdn_prep_kernel(qkv_ref, halo_ref, ab_ref, cw_ref, hp_ref,
                     u_ref, w_ref, qd_ref, kd_ref, qk_ref, ge_ref, xp_ref, act_ref):
    tb = qkv_ref.shape[1]
    cc = GDN_CHUNK
    first = pl.program_id(1) == 0
    xp_ref[0:BF16_ROWS, :] = jnp.where(first, 0.0, halo_ref[0].astype(F32))
    xp_ref[BF16_ROWS:BF16_ROWS + tb, :] = qkv_ref[0].astype(F32)
    _activate_qkv(xp_ref, cw_ref, act_ref, BF16_ROWS, tb)
    g_all, beta_all = _decay_beta(ab_ref[0], hp_ref)

    for ci in range(tb // cc):
        r0 = ci * cc
        rows = slice(r0, r0 + cc)
        gc = _cumsum_rows(g_all[rows, :])
        gct = gc.T
        ge_ref[0, ci] = jnp.exp(gc[cc - 1:cc, :])
        for hd in range(GDN_HEADS):
            cq, ck, cv = _head_cols(hd)
            u, w, qd, kd, qk = _chunk_prep(
                act_ref[rows, cq:cq + LANES], act_ref[rows, ck:ck + LANES], act_ref[rows, cv:cv + LANES],
                gc[:, hd:hd + 1], gct[hd:hd + 1, :], beta_all[rows, GDN_HEADS + hd:GDN_HEADS + hd + 1])
            oc = slice(hd * GDN_DV, (hd + 1) * GDN_DV)
            u_ref[0, rows, oc] = u.astype(BF16)
            w_ref[0, rows, oc] = w.astype(BF16)
            qd_ref[0, rows, oc] = qd.astype(BF16)
            kd_ref[0, rows, oc] = kd.astype(BF16)
            qk_ref[0, rows, hd * cc:(hd + 1) * cc] = qk.astype(BF16)


def _gdn_prep(qkv, ab, conv_w, head_params):
    b, t, _ = qkv.shape
    tb = min(GDN_PREP_ROWS, t)
    nch = tb // GDN_CHUNK
    halo_per_block = tb // BF16_ROWS
    blk = lambda w: pl.BlockSpec((1, tb, w), lambda bi, i: (bi, i, 0))
    out_shapes = [jax.ShapeDtypeStruct((b, t, GDN_WIDTH), BF16)] * 4 + [
        jax.ShapeDtypeStruct((b, t, GDN_HEADS * GDN_CHUNK), BF16),
        jax.ShapeDtypeStruct((b, t // GDN_CHUNK, 1, LANES), F32)]
    return pl.pallas_call(
        _gdn_prep_kernel,
        grid=(b, t // tb),
        in_specs=[blk(GDN_CONV_DIM),
                  pl.BlockSpec((1, BF16_ROWS, GDN_CONV_DIM),
                               lambda bi, i: (bi, jnp.maximum(i * halo_per_block - 1, 0), 0)),
                  blk(LANES),
                  pl.BlockSpec((GDN_CONV, GDN_CONV_DIM), lambda bi, i: (0, 0)),
                  pl.BlockSpec((8, LANES), lambda bi, i: (0, 0))],
        out_specs=[blk(GDN_WIDTH)] * 4 + [
            blk(GDN_HEADS * GDN_CHUNK),
            pl.BlockSpec((1, nch, 1, LANES), lambda bi, i: (bi, i, 0, 0))],
        out_shape=out_shapes,
        scratch_shapes=[pltpu.VMEM((tb + BF16_ROWS, GDN_CONV_DIM), F32),
                        pltpu.VMEM((tb, GDN_CONV_DIM), F32)],
        compiler_params=_cparams("parallel", "parallel"),
        name="gdn_prep",
    )(qkv, qkv, ab, conv_w, head_params)


def _gated_norm_store(o_ref, idx, o, gain, zs):
    o_ref[idx] = (_rms(o, gain) * zs.astype(F32)).astype(o_ref.dtype)


def _gdn_scan_kernel(u_ref, w_ref, qd_ref, kd_ref, qk_ref, ge_ref, zs_ref, gain_ref,
                     o_ref, s_out_ref, s_ref):
    nb, tb, _ = u_ref.shape
    cc = GDN_CHUNK
    step = pl.program_id(0)

    @pl.when(step == 0)
    def _():
        s_ref[...] = jnp.zeros_like(s_ref)

    gain = gain_ref[...]

    def chunk_body(ci, carry):
        rows = pl.ds(pl.multiple_of(ci * cc, cc), cc)
        for bi in range(nb):
            ge_row = ge_ref[bi, ci]
            for hd in range(GDN_HEADS):
                oc = slice(hd * GDN_DV, (hd + 1) * GDN_DV)
                si = bi * GDN_HEADS + hd
                o, s_new = _chunk_step(
                    s_ref[si], u_ref[bi, rows, oc], w_ref[bi, rows, oc], qd_ref[bi, rows, oc],
                    kd_ref[bi, rows, oc], qk_ref[bi, rows, hd * cc:(hd + 1) * cc], ge_row[:, hd:hd + 1])
                s_ref[si] = s_new
                _gated_norm_store(o_ref, (bi, rows, oc), o, gain, zs_ref[bi, rows, oc])
        return carry

    lax.fori_loop(0, tb // cc, chunk_body, 0)

    @pl.when(step == pl.num_programs(0) - 1)
    def _():
        s_out_ref[...] = s_ref[...]


def _gdn_scan(u, w, qd, kd, qk, ge, zs, gain):
    b, t, _ = u.shape
    tb = min(GDN_SCAN_ROWS, t)
    nch = tb // GDN_CHUNK
    blk = lambda wd: pl.BlockSpec((b, tb, wd), lambda i: (0, i, 0))
    o, s = pl.pallas_call(
        _gdn_scan_kernel,
        grid=(t // tb,),
        in_specs=[blk(GDN_WIDTH)] * 4 + [
            blk(GDN_HEADS * GDN_CHUNK),
            pl.BlockSpec((b, nch, 1, LANES), lambda i: (0, i, 0, 0)),
            blk(GDN_WIDTH),
            pl.BlockSpec((1, GDN_DV), lambda i: (0, 0))],
        out_specs=[blk(GDN_WIDTH),
                   pl.BlockSpec((b * GDN_HEADS, GDN_DK, GDN_DV), lambda i: (0, 0, 0))],
        out_shape=[jax.ShapeDtypeStruct((b, t, GDN_WIDTH), BF16),
                   jax.ShapeDtypeStruct((b * GDN_HEADS, GDN_DK, GDN_DV), F32)],
        scratch_shapes=[pltpu.VMEM((b * GDN_HEADS, GDN_DK, GDN_DV), F32)],
        compiler_params=_cparams("arbitrary"),
        name="gdn_scan",
    )(u, w, qd, kd, qk, ge, zs, gain.reshape(1, GDN_DV))
    return o, s.reshape(b, GDN_HEADS, GDN_DK, GDN_DV)


def _gdn_sample_kernel(new_rows, xp_ref, ab_ref, zs_ref, s0_ref, cw_ref, hp_ref, gain_ref,
                       o_ref, s_out_ref, xs_ref, act_ref):
    grp = xp_ref.shape[0]
    cc = GDN_SAMPLE_CHUNK
    gain = gain_ref[...]
    rowmask = (_iota2((cc, 1), 0) >= cc - new_rows).astype(F32)
    for bi in range(grp):
        xs = xs_ref.at[bi]
        act = act_ref.at[bi]
        xs[0:8, :] = jnp.zeros((8, GDN_CONV_DIM), F32)
        xs[8:8 + cc, :] = xp_ref[bi].astype(F32)
        _activate_qkv(xs, cw_ref, act, 8, cc)
        g_all, beta_all = _decay_beta(ab_ref[bi], hp_ref)
        gc = _cumsum_rows(g_all * rowmask)
        gct = gc.T
        ge_row = jnp.exp(gc[cc - 1:cc, :])
        beta_all = beta_all * rowmask
        for hd in range(GDN_HEADS):
            cq, ck, cv = _head_cols(hd)
            u, w, qd, kd, qk = _chunk_prep(
                act[:, cq:cq + LANES] * rowmask, act[:, ck:ck + LANES] * rowmask, act[:, cv:cv + LANES] * rowmask,
                gc[:, hd:hd + 1], gct[hd:hd + 1, :], beta_all[:, GDN_HEADS + hd:GDN_HEADS + hd + 1])
            o, s_new = _chunk_step(s0_ref[bi, hd], u, w, qd, kd, qk, ge_row[:, hd:hd + 1])
            s_out_ref[bi, hd] = s_new
            oc = slice(hd * GDN_DV, (hd + 1) * GDN_DV)
            _gated_norm_store(o_ref, (bi, slice(None), oc), o, gain, zs_ref[bi, :, oc])


def _gdn_sample(xp, ab, zs, s0, conv_w, head_params, gain, new_rows):
    b = xp.shape[0]
    cc = GDN_SAMPLE_CHUNK
    grp = GDN_SAMPLE_GROUP
    blk3 = lambda w: pl.BlockSpec((grp, cc, w), lambda i: (i, 0, 0))
    sblk = pl.BlockSpec((grp, GDN_HEADS, GDN_DK, GDN_DV), lambda i: (i, 0, 0, 0))
    return pl.pallas_call(
        functools.partial(_gdn_sample_kernel, new_rows),
        grid=(b // grp,),
        in_specs=[blk3(GDN_CONV_DIM), blk3(LANES), blk3(GDN_WIDTH), sblk,
                  pl.BlockSpec((GDN_CONV, GDN_CONV_DIM), lambda i: (0, 0)),
                  pl.BlockSpec((8, LANES), lambda i: (0, 0)),
                  pl.BlockSpec((1, GDN_DV), lambda i: (0, 0))],
        out_specs=[blk3(GDN_WIDTH), sblk],
        out_shape=[jax.ShapeDtypeStruct((b, cc, GDN_WIDTH), BF16),
                   jax.ShapeDtypeStruct((b, GDN_HEADS, GDN_DK, GDN_DV), F32)],
        scratch_shapes=[pltpu.VMEM((grp, cc + 8, GDN_CONV_DIM), F32),
                        pltpu.VMEM((grp, cc, GDN_CONV_DIM), F32)],
        compiler_params=_cparams("parallel"),
        name="gdn_sample",
    )(xp, ab, zs, s0, conv_w, head_params, gain.reshape(1, GDN_DV))


def _sink_attention(q, k, v, mask, sink_col):
    s = _bdot_nt(q, k) * (SWA_HEAD_DIM ** -0.5)
    s = jnp.where(mask, s, -jnp.inf)
    m = jnp.maximum(jnp.max(s, axis=-1, keepdims=True), sink_col)
    p = jnp.exp(s - m)
    denom = jnp.sum(p, axis=-1, keepdims=True) + jnp.exp(sink_col - m)
    return _bdot(p, v) / denom


def _sink_column(sinks_ref, kv_head, rows_per_head):
    parts = [jnp.full((rows_per_head, 1), sinks_ref[kv_head * SWA_GROUP + g], F32) for g in range(SWA_GROUP)]
    return jnp.concatenate(parts, axis=0)


def _swa_prompt_kernel(sinks_ref, q_ref, kvp_ref, kvc_ref, o_ref):
    wnd = WINDOW
    blk = pl.program_id(1)
    kv = jnp.concatenate([kvp_ref[0], kvc_ref[0]], axis=0)
    rows = SWA_GROUP * wnd
    qi = _iota2((rows, 2 * wnd), 0) & (wnd - 1)
    kj = _iota2((rows, 2 * wnd), 1)
    dist = qi + wnd - kj
    first_key = jnp.where(blk > 0, 0, wnd)
    mask = (dist >= 0) & (dist <= wnd) & (kj >= first_key)
    for hk in range(SWA_KV_HEADS):
        k = kv[:, hk * SWA_HEAD_DIM:(hk + 1) * SWA_HEAD_DIM]
        v = kv[:, SWA_KV_WIDTH + hk * SWA_HEAD_DIM:SWA_KV_WIDTH + (hk + 1) * SWA_HEAD_DIM]
        heads = [hk * SWA_GROUP + g for g in range(SWA_GROUP)]
        q = jnp.concatenate([q_ref[0, :, h * SWA_HEAD_DIM:(h + 1) * SWA_HEAD_DIM] for h in heads], axis=0)
        o = _sink_attention(q, k, v, mask, _sink_column(sinks_ref, hk, wnd))
        for g, h in enumerate(heads):
            o_ref[0, :, h * SWA_HEAD_DIM:(h + 1) * SWA_HEAD_DIM] = o[g * wnd:(g + 1) * wnd].astype(o_ref.dtype)


def _swa_prompt(q, kv, sinks):
    b, t, _ = q.shape
    return pl.pallas_call(
        _swa_prompt_kernel,
        grid=(b, t // WINDOW),
        in_specs=[pl.BlockSpec(memory_space=pltpu.SMEM),
                  pl.BlockSpec((1, WINDOW, SWA_WIDTH), lambda bi, i: (bi, i, 0)),
                  pl.BlockSpec((1, WINDOW, 2 * SWA_KV_WIDTH), lambda bi, i: (bi, jnp.maximum(i - 1, 0), 0)),
                  pl.BlockSpec((1, WINDOW, 2 * SWA_KV_WIDTH), lambda bi, i: (bi, i, 0))],
        out_specs=pl.BlockSpec((1, WINDOW, SWA_WIDTH), lambda bi, i: (bi, i, 0)),
        out_shape=jax.ShapeDtypeStruct((b, t, SWA_WIDTH), BF16),
        compiler_params=_cparams("parallel", "parallel"),
        name="swa_prompt",
    )(sinks, q, kv, kv)


def _swa_sample_kernel(sinks_ref, q_ref, kvn_ref, kc_ref, vc_ref, o_ref, ko_ref, vo_ref):
    grp, t, _ = q_ref.shape
    wnd = WINDOW
    nk = wnd + BF16_ROWS
    rows = SWA_GROUP * t
    tq = _iota2((rows, nk), 0) & (t - 1)
    kj = _iota2((rows, nk), 1)
    dist = tq + wnd - kj
    mask = (dist >= 0) & (dist <= wnd)
    zpad = jnp.zeros((BF16_ROWS - t, SWA_KV_WIDTH), F32)
    for bi in range(grp):
        kvn = kvn_ref[bi].astype(F32)
        kk = jnp.concatenate([kc_ref[bi], kvn[:, :SWA_KV_WIDTH], zpad], axis=0)
        vv = jnp.concatenate([vc_ref[bi], kvn[:, SWA_KV_WIDTH:], zpad], axis=0)
        ko_ref[bi] = kk[t:t + wnd, :]
        vo_ref[bi] = vv[t:t + wnd, :]
        for hk in range(SWA_KV_HEADS):
            cs = slice(hk * SWA_HEAD_DIM, (hk + 1) * SWA_HEAD_DIM)
            heads = [hk * SWA_GROUP + g for g in range(SWA_GROUP)]
            q = jnp.concatenate([q_ref[bi, :, h * SWA_HEAD_DIM:(h + 1) * SWA_HEAD_DIM] for h in heads], axis=0)
            o = _sink_attention(q, kk[:, cs], vv[:, cs], mask, _sink_column(sinks_ref, hk, t))
            for g, h in enumerate(heads):
                o_ref[bi, :, h * SWA_HEAD_DIM:(h + 1) * SWA_HEAD_DIM] = o[g * t:(g + 1) * t].astype(o_ref.dtype)


def _swa_sample(q, kv_new, k_cache, v_cache, sinks):
    b, t, _ = q.shape
    grp = SWA_SAMPLE_GROUP
    blk = lambda r, w: pl.BlockSpec((grp, r, w), lambda i: (i, 0, 0))
    return pl.pallas_call(
        _swa_sample_kernel,
        grid=(b // grp,),
        in_specs=[pl.BlockSpec(memory_space=pltpu.SMEM),
                  blk(t, SWA_WIDTH), blk(t, 2 * SWA_KV_WIDTH), blk(WINDOW, SWA_KV_WIDTH), blk(WINDOW, SWA_KV_WIDTH)],
        out_specs=[blk(t, SWA_WIDTH), blk(WINDOW, SWA_KV_WIDTH), blk(WINDOW, SWA_KV_WIDTH)],
        out_shape=[jax.ShapeDtypeStruct((b, t, SWA_WIDTH), BF16),
                   jax.ShapeDtypeStruct((b, WINDOW, SWA_KV_WIDTH), F32),
                   jax.ShapeDtypeStruct((b, WINDOW, SWA_KV_WIDTH), F32)],
        compiler_params=_cparams("parallel"),
        name="swa_sample",
    )(sinks, q, kv_new, k_cache, v_cache)


def _route(logits):
    lane = _iota2(logits.shape, 1).astype(F32)
    neg = -jnp.inf

    def first_argmax(vals, valid):
        v = jnp.where(valid, vals, neg)
        m = jnp.max(v, axis=-1, keepdims=True)
        idx = jnp.min(jnp.where(jnp.logical_and(valid, v == m), lane, float(LANES)), axis=-1, keepdims=True)
        return m, idx

    is_group = lane < N_GROUPS
    gmax, gidx = first_argmax(logits, is_group)
    p_group = 1.0 / jnp.sum(jnp.where(is_group, jnp.exp(logits - gmax), 0.0), axis=-1, keepdims=True)
    lo = N_GROUPS + gidx * EXPERTS_PER_GROUP
    in_group = jnp.logical_and(lane >= lo, lane < lo + EXPERTS_PER_GROUP)
    m1, i1 = first_argmax(logits, in_group)
    esum = jnp.sum(jnp.where(in_group, jnp.exp(logits - m1), 0.0), axis=-1, keepdims=True)
    m2, i2 = first_argmax(logits, jnp.logical_and(in_group, lane != i1))
    p1 = 1.0 / esum
    p2 = jnp.exp(m2 - m1) / esum
    tot = p1 + p2
    return i1 - N_GROUPS, i2 - N_GROUPS, p_group * p1 / tot, p_group * p2 / tot


def _post_mixer_kernel(oa_ref, ob_ref, ga_ref, gb_ref, x_ref, gt_ref, sc_ref, sh_ref,
                       wa_ref, wb_ref, wo_ref, gpost_ref, gpre_ref, wr_ref, br_ref,
                       x1_ref, h2_ref, rt_ref):
    merged = (ga_ref[...].astype(F32) * jnp.dot(oa_ref[...], wa_ref[...], preferred_element_type=F32)
              + gb_ref[...].astype(F32) * jnp.dot(ob_ref[...], wb_ref[...], preferred_element_type=F32))
    mix = _bdot(merged, wo_ref[...])
    x1 = x_ref[...] + gt_ref[0] * _rms(mix, gpost_ref[...])
    x1_ref[...] = x1
    h2 = _rms(x1, gpre_ref[...]) * (1.0 + sc_ref[0]) + sh_ref[0]
    h2_ref[...] = h2
    logits = jnp.dot(h2, wr_ref[...], preferred_element_type=F32, precision=lax.Precision.HIGHEST) + br_ref[...]
    ia, ib, wa, wb = _route(logits)
    lane = _iota2(logits.shape, 1)
    rt_ref[...] = jnp.where(lane == 0, ia, jnp.where(lane == 1, ib, jnp.where(lane == 2, wa,
                            jnp.where(lane == 3, wb, 0.0))))


def _post_mixer(oa, ob, ga, gb, x2d, gt, sc, sh, w_a, w_b, w_o, g_post, g_pre, w_rt, b_rt, tm):
    rows = x2d.shape[0]
    n_tiles = rows // tm
    row_blk = lambda w: pl.BlockSpec((tm, w), lambda i: (i, 0))
    full = lambda r, c: pl.BlockSpec((r, c), lambda i: (0, 0))
    return pl.pallas_call(
        _post_mixer_kernel,
        grid=(n_tiles,),
        in_specs=[row_blk(GDN_WIDTH), row_blk(SWA_WIDTH), row_blk(D_MODEL), row_blk(D_MODEL), row_blk(D_MODEL),
                  _mod_spec(gt, n_tiles), _mod_spec(sc, n_tiles), _mod_spec(sh, n_tiles),
                  full(GDN_WIDTH, D_MODEL), full(SWA_WIDTH, D_MODEL), full(D_MODEL, D_MODEL),
                  full(1, D_MODEL), full(1, D_MODEL), full(D_MODEL, LANES), full(1, LANES)],
        out_specs=[row_blk(D_MODEL), row_blk(D_MODEL), row_blk(LANES)],
        out_shape=[jax.ShapeDtypeStruct((rows, D_MODEL), F32),
                   jax.ShapeDtypeStruct((rows, D_MODEL), F32),
                   jax.ShapeDtypeStruct((rows, LANES), F32)],
        compiler_params=_cparams("arbitrary"),
        name="post_mixer",
    )(oa, ob, ga, gb, x2d, gt, sc, sh, w_a, w_b, w_o,
      g_post.reshape(1, D_MODEL), g_pre.reshape(1, D_MODEL), w_rt, b_rt)


def _row_gather(src_hbm, idx_ref, base, dst, sem, n_rows, start):
    def body(r, carry):
        cp = pltpu.make_async_copy(src_hbm.at[pl.ds(idx_ref[base + r], 1)], dst.at[pl.ds(r, 1)], sem)
        if start:
            cp.start()
        else:
            cp.wait()
        return carry

    lax.fori_loop(0, n_rows, body, 0, unroll=8)


def _moe_kernel(blk_e_ref, n_used_ref, slot_tok_ref, h_hbm, wg_ref, wu_ref, wd_ref, y_ref,
                xbuf, wgb, wub, wdb, sems):
    b = pl.program_id(0)
    nb = pl.num_programs(0)
    rows = xbuf.shape[1]
    slot = b % 2

    @pl.when(b == 0)
    def _():
        _row_gather(h_hbm, slot_tok_ref, 0, xbuf.at[0], sems.at[0], rows, True)

    @pl.when(b + 1 < nb)
    def _():
        _row_gather(h_hbm, slot_tok_ref, (b + 1) * rows, xbuf.at[1 - slot], sems.at[1 - slot], rows, True)

    changed = jnp.logical_or(b == 0, blk_e_ref[b] != blk_e_ref[jnp.maximum(b - 1, 0)])

    @pl.when(changed)
    def _():
        wgb[...] = wg_ref[0].astype(BF16)
        wub[...] = wu_ref[0].astype(BF16)
        wdb[...] = wd_ref[0].astype(BF16)

    _row_gather(h_hbm, slot_tok_ref, b * rows, xbuf.at[slot], sems.at[slot], rows, False)

    @pl.when(b < n_used_ref[0])
    def _():
        x = xbuf[slot].astype(BF16)
        gate = jnp.dot(x, wgb[...], preferred_element_type=F32)
        up = jnp.dot(x, wub[...], preferred_element_type=F32)
        y_ref[...] = _bdot(_silu(gate) * up, wdb[...])

    @pl.when(b >= n_used_ref[0])
    def _():
        y_ref[...] = jnp.zeros_like(y_ref)


def _moe(h2, blk_e, n_used, slot_tok, w_gate, w_up, w_down):
    n_blocks = blk_e.shape[0]
    rows = MOE_ROWS
    wspec = lambda r, c: pl.BlockSpec((1, r, c), lambda b, be, nu, st: (be[b], 0, 0))
    return pl.pallas_call(
        _moe_kernel,
        grid_spec=pltpu.PrefetchScalarGridSpec(
            num_scalar_prefetch=3,
            grid=(n_blocks,),
            in_specs=[pl.BlockSpec(memory_space=pl.ANY),
                      wspec(D_MODEL, EXPERT_FF), wspec(D_MODEL, EXPERT_FF), wspec(EXPERT_FF, D_MODEL)],
            out_specs=pl.BlockSpec((rows, D_MODEL), lambda b, be, nu, st: (b, 0)),
            scratch_shapes=[pltpu.VMEM((2, rows, D_MODEL), F32),
                            pltpu.VMEM((D_MODEL, EXPERT_FF), BF16),
                            pltpu.VMEM((D_MODEL, EXPERT_FF), BF16),
                            pltpu.VMEM((EXPERT_FF, D_MODEL), BF16),
                            pltpu.SemaphoreType.DMA((2,))]),
        out_shape=jax.ShapeDtypeStruct((n_blocks * rows, D_MODEL), F32),
        compiler_params=_cparams("arbitrary"),
        name="moe_experts",
    )(blk_e, n_used, slot_tok, h2, w_gate, w_up, w_down)


def _combine_kernel(dest_ref, y_hbm, x1_ref, rt_ref, gt_ref, gpost_ref, o_ref, ybuf, sems):
    i = pl.program_id(0)
    n = pl.num_programs(0)
    rows = ybuf.shape[1]
    slot = i % 2

    @pl.when(i == 0)
    def _():
        _row_gather(y_hbm, dest_ref, 0, ybuf.at[0], sems.at[0], rows, True)

    @pl.when(i + 1 < n)
    def _():
        _row_gather(y_hbm, dest_ref, (i + 1) * rows, ybuf.at[1 - slot], sems.at[1 - slot], rows, True)

    _row_gather(y_hbm, dest_ref, i * rows, ybuf.at[slot], sems.at[slot], rows, False)
    half = rows // 2
    rt = rt_ref[...]
    f = rt[:, 2:3] * ybuf[slot, 0:half, :] + rt[:, 3:4] * ybuf[slot, half:rows, :]
    o_ref[...] = x1_ref[...] + gt_ref[0] * _rms(f, gpost_ref[...])


def _combine(dest, yb, x1, rt, gt, g_post, tm):
    rows = x1.shape[0]
    n_tiles = rows // tm
    tiles_per_mod = n_tiles // gt.shape[0]
    return pl.pallas_call(
        _combine_kernel,
        grid_spec=pltpu.PrefetchScalarGridSpec(
            num_scalar_prefetch=1,
            grid=(n_tiles,),
            in_specs=[pl.BlockSpec(memory_space=pl.ANY),
                      pl.BlockSpec((tm, D_MODEL), lambda i, d: (i, 0)),
                      pl.BlockSpec((tm, LANES), lambda i, d: (i, 0)),
                      pl.BlockSpec((1, gt.shape[1], D_MODEL), lambda i, d: (i // tiles_per_mod, 0, 0)),
                      pl.BlockSpec((1, D_MODEL), lambda i, d: (0, 0))],
            out_specs=pl.BlockSpec((tm, D_MODEL), lambda i, d: (i, 0)),
            scratch_shapes=[pltpu.VMEM((2, 2 * tm, D_MODEL), F32),
                            pltpu.SemaphoreType.DMA((2,))]),
        out_shape=jax.ShapeDtypeStruct((rows, D_MODEL), F32),
        compiler_params=_cparams("arbitrary"),
        name="moe_combine",
    )(dest, yb, x1, rt, gt, g_post.reshape(1, D_MODEL))


def _dispatch_plan(ids, n_tok, tm):
    n_assign = n_tok * TOP_K
    flat_e = ids.reshape(n_assign)
    onehot = (flat_e[:, None] == jnp.arange(N_EXPERTS, dtype=I32)[None, :]).astype(I32)
    csum = jnp.cumsum(onehot, axis=0)
    rank = jnp.take_along_axis(csum, flat_e[:, None], axis=1)[:, 0] - 1
    counts = csum[-1]
    pcounts = (counts + MOE_ROWS - 1) // MOE_ROWS * MOE_ROWS
    pends = jnp.cumsum(pcounts)
    pstarts = pends - pcounts
    dest = pstarts[flat_e] + rank
    n_blocks = n_assign // MOE_ROWS + N_EXPERTS
    slot_tok = jnp.zeros((n_blocks * MOE_ROWS,), I32).at[dest].set(jnp.arange(n_assign, dtype=I32) // TOP_K)
    blk_start = jnp.arange(n_blocks, dtype=I32) * MOE_ROWS
    blk_e = jnp.minimum(jnp.sum(blk_start[:, None] >= pends[None, :], axis=1), N_EXPERTS - 1).astype(I32)
    n_used = (pends[-1] // MOE_ROWS).astype(I32).reshape(1)
    dest_tiles = dest.reshape(n_tok // tm, tm, TOP_K).transpose(0, 2, 1).reshape(-1)
    return blk_e, n_used, slot_tok, dest_tiles


def _prep_in_weight(w_in):
    a0 = 4 * GDN_QK_WIDTH
    a1 = a0 + 2 * GDN_HEADS
    pad = jnp.zeros((D_MODEL, LANES - 2 * GDN_HEADS), w_in.dtype)
    return jnp.concatenate([w_in[:, :a0], w_in[:, a1:], w_in[:, a0:a1], pad], axis=1).astype(BF16)


def _head_param_tile(a_log, dt_bias):
    tile = jnp.zeros((8, LANES), F32)
    return tile.at[0, :GDN_HEADS].set(a_log.astype(F32)).at[1, :GDN_HEADS].set(dt_bias.astype(F32))


def _router_weight(w_group, b_group, w_router, b_router):
    w = jnp.zeros((D_MODEL, LANES), F32)
    w = w.at[:, :N_GROUPS].set(w_group).at[:, N_GROUPS:N_GROUPS + N_EXPERTS].set(w_router)
    b = jnp.zeros((1, LANES), F32)
    b = b.at[0, :N_GROUPS].set(b_group).at[0, N_GROUPS:N_GROUPS + N_EXPERTS].set(b_router)
    return w, b


def kernel(x_prompt, x_sample, state_gdn, state_conv, cache_k_win, cache_v_win, c_prompt, c_sample, w_ada, b_ada, g_mix_pre, g_mix_post, g_ffn_pre, g_ffn_post, w_in, conv_w, a_log, dt_bias, gdn_norm, sinks, w_br_gdn, w_br_swa, w_out, w_group, b_group, w_router, b_router, w_gate, w_up, w_down):
    depth = w_ada.shape[0]
    assert depth == 1, "single-layer trunk"
    bp, tp, _ = x_prompt.shape
    bs, ts, _ = x_sample.shape
    n_p = bp * tp
    n_s = bs * ts
    tm = ROW_TILE
    assert tp % tm == 0 and n_s % tm == 0 and ts >= GDN_CONV - 1 and ts + GDN_CONV - 1 <= GDN_SAMPLE_CHUNK
    assert ts & (ts - 1) == 0 and ts <= BF16_ROWS

    c_all = jnp.concatenate([c_prompt, c_sample], axis=0)
    c_rows = -(-c_all.shape[0] // 8) * 8
    c_all = jnp.pad(c_all, ((0, c_rows - c_all.shape[0]), (0, 0)))
    mod = _adaln(c_all, w_ada[0], b_ada[0])
    mods_p = [m[:bp].reshape(bp, 1, D_MODEL) for m in jnp.split(mod, 6, axis=-1)]
    mods_s = [jnp.repeat(m[bp:bp + bs], ts, axis=0).reshape(n_s // tm, tm, D_MODEL)
              for m in jnp.split(mod, 6, axis=-1)]

    w_prep = _prep_in_weight(w_in[0])
    head_params = _head_param_tile(a_log[0], dt_bias[0])
    w_a, w_b, w_o = w_br_gdn[0].astype(BF16), w_br_swa[0].astype(BF16), w_out[0].astype(BF16)
    w_rt, b_rt = _router_weight(w_group[0], b_group[0], w_router[0], b_router[0])
    sinks0 = sinks[0].astype(F32)

    xp2d = x_prompt.reshape(n_p, D_MODEL)
    sh1, sc1, gt1, sh2, sc2, gt2 = mods_p
    qkv_p, zs_p, qb_p, kvb_p, ga_p, gb_p, ab_p = _inproj(xp2d, g_mix_pre[0], sc1, sh1, w_prep, tm)
    qkv_p3 = qkv_p.reshape(bp, tp, GDN_CONV_DIM)
    u, w, qd, kd, qk, ge = _gdn_prep(qkv_p3, ab_p.reshape(bp, tp, LANES), conv_w[0], head_params)
    oa_p, s_prompt = _gdn_scan(u, w, qd, kd, qk, ge, zs_p.reshape(bp, tp, GDN_WIDTH), gdn_norm[0])
    kvb_p3 = kvb_p.reshape(bp, tp, 2 * SWA_KV_WIDTH)
    ob_p = _swa_prompt(qb_p.reshape(bp, tp, SWA_WIDTH), kvb_p3, sinks0)
    x1_p, h2_p, rt_p = _post_mixer(oa_p.reshape(n_p, GDN_WIDTH), ob_p.reshape(n_p, SWA_WIDTH), ga_p, gb_p, xp2d,
                                   gt1, sc2, sh2, w_a, w_b, w_o, g_mix_post[0], g_ffn_pre[0], w_rt, b_rt, tm)

    xs2d = x_sample.reshape(n_s, D_MODEL)
    sh1s, sc1s, gt1s, sh2s, sc2s, gt2s = mods_s
    qkv_s, zs_s, qb_s, kvb_s, ga_s, gb_s, ab_s = _inproj(xs2d, g_mix_pre[0], sc1s, sh1s, w_prep, tm)
    cc = GDN_SAMPLE_CHUNK
    pad_rows = cc - ts - (GDN_CONV - 1)
    qkv_s3 = qkv_s.reshape(bs, ts, GDN_CONV_DIM)
    xp_s = jnp.concatenate([jnp.zeros((bs, pad_rows, GDN_CONV_DIM), BF16), state_conv[0].astype(BF16), qkv_s3],
                           axis=1)
    front = lambda a: jnp.pad(a, ((0, 0), (cc - ts, 0), (0, 0)))
    oa_s16, s_sample = _gdn_sample(xp_s, front(ab_s.reshape(bs, ts, LANES)), front(zs_s.reshape(bs, ts, GDN_WIDTH)),
                                   state_gdn[0].astype(F32), conv_w[0], head_params, gdn_norm[0], ts)
    oa_s = oa_s16[:, cc - ts:, :].reshape(n_s, GDN_WIDTH)
    ob_s, k_new_s, v_new_s = _swa_sample(
        qb_s.reshape(bs, ts, SWA_WIDTH), kvb_s.reshape(bs, ts, 2 * SWA_KV_WIDTH),
        cache_k_win[0].reshape(bs, WINDOW, SWA_KV_WIDTH).astype(F32),
        cache_v_win[0].reshape(bs, WINDOW, SWA_KV_WIDTH).astype(F32), sinks0)
    x1_s, h2_s, rt_s = _post_mixer(oa_s, ob_s.reshape(n_s, SWA_WIDTH), ga_s, gb_s, xs2d,
                                   gt1s, sc2s, sh2s, w_a, w_b, w_o, g_mix_post[0], g_ffn_pre[0], w_rt, b_rt, tm)

    n_all = n_p + n_s
    h2 = jnp.concatenate([h2_p, h2_s], axis=0)
    rt = jnp.concatenate([rt_p, rt_s], axis=0)
    ids = rt[:, :TOP_K].astype(I32)
    ct = COMBINE_ROWS
    blk_e, n_used, slot_tok, dest_tiles = _dispatch_plan(ids, n_all, ct)
    yb = _moe(h2, blk_e, n_used, slot_tok, w_gate[0], w_up[0], w_down[0])
    y_p = _combine(dest_tiles[:n_p * TOP_K], yb, x1_p, rt_p, gt2, g_ffn_post[0], ct)
    y_s = _combine(dest_tiles[n_p * TOP_K:], yb, x1_s, rt_s,
                   gt2s.reshape(n_s // ct, ct, D_MODEL), g_ffn_post[0], ct)

    f32 = lambda a: a.astype(F32)
    kv_tail = kvb_p3[:, tp - WINDOW:, :]
    kv_heads = lambda a: f32(a).reshape(a.shape[0], WINDOW, SWA_KV_HEADS, SWA_HEAD_DIM)[None]
    return (y_p.reshape(bp, tp, D_MODEL), y_s.reshape(bs, ts, D_MODEL),
            s_prompt[None], f32(qkv_p3[:, tp - (GDN_CONV - 1):, :])[None],
            kv_heads(kv_tail[:, :, :SWA_KV_WIDTH]), kv_heads(kv_tail[:, :, SWA_KV_WIDTH:]),
            s_sample[None], f32(qkv_s3[:, ts - (GDN_CONV - 1):, :])[None],
            kv_heads(k_new_s), kv_heads(v_new_s))
```

```python
import functools

import jax
import jax.numpy as jnp
from jax import lax
from jax.experimental import pallas as pl
from jax.experimental.pallas import tpu as pltpu

F32 = jnp.float32
BF16 = jnp.bfloat16
I32 = jnp.int32

D_MODEL = 1024
NORM_EPS = 1e-6
GDN_HEADS = 8
GDN_DK = 128
GDN_DV = 128
GDN_CONV = 4
GDN_CHUNK = 64
GDN_QK_WIDTH = GDN_HEADS * GDN_DK
GDN_WIDTH = GDN_HEADS * GDN_DV
GDN_CONV_DIM = 2 * GDN_QK_WIDTH + GDN_WIDTH
SWA_Q_HEADS = 16
SWA_KV_HEADS = 4
SWA_HEAD_DIM = 64
SWA_GROUP = SWA_Q_HEADS // SWA_KV_HEADS
SWA_WIDTH = SWA_Q_HEADS * SWA_HEAD_DIM
SWA_KV_WIDTH = SWA_KV_HEADS * SWA_HEAD_DIM
WINDOW = 128
N_GROUPS = 4
EXPERTS_PER_GROUP = 8
N_EXPERTS = N_GROUPS * EXPERTS_PER_GROUP
TOP_K = 2
EXPERT_FF = 512

LANES = 128
BF16_ROWS = 16
VMEM_LIMIT = 56 * 1024 * 1024

_C_QKV = 0
_C_Z = _C_QKV + GDN_CONV_DIM
_C_QB = _C_Z + GDN_WIDTH
_C_KVB = _C_QB + SWA_WIDTH
_C_GA = _C_KVB + 2 * SWA_KV_WIDTH
_C_GB = _C_GA + D_MODEL
_C_AB = _C_GB + D_MODEL
IN_COLS = _C_AB + LANES
PROJ_TILE = 512

ROW_TILE = 512
GDN_PREP_ROWS = 256
GDN_SCAN_ROWS = 512
GDN_SAMPLE_CHUNK = 16
GDN_SAMPLE_GROUP = 4
SWA_SAMPLE_GROUP = 8
MOE_ROWS = 256
COMBINE_ROWS = 256


def _cparams(*sem):
    return pltpu.CompilerParams(dimension_semantics=sem, vmem_limit_bytes=VMEM_LIMIT)


def _bdot(a, b):
    return jnp.dot(a.astype(BF16), b.astype(BF16), preferred_element_type=F32)


def _bdot_nt(a, b):
    return lax.dot_general(a.astype(BF16), b.astype(BF16), (((1,), (1,)), ((), ())),
                           preferred_element_type=F32)


def _bdot_tn(a, b):
    return lax.dot_general(a.astype(BF16), b.astype(BF16), (((0,), (0,)), ((), ())),
                           preferred_element_type=F32)


def _sigmoid(x):
    return 1.0 / (1.0 + jnp.exp(-x))


def _silu(x):
    return x * _sigmoid(x)


def _rms(x, gain):
    return x * lax.rsqrt(jnp.mean(x * x, axis=-1, keepdims=True) + NORM_EPS) * gain


def _iota2(shape, dim):
    return lax.broadcasted_iota(I32, shape, dim)


def _adaln_kernel(c_ref, w_ref, b_ref, o_ref):
    o_ref[...] = _bdot(_silu(c_ref[...]), w_ref[...]) + b_ref[...]


def _adaln(c_all, w_ada, b_ada):
    rows = c_all.shape[0]
    n_out = w_ada.shape[1]
    tn = D_MODEL
    return pl.pallas_call(
        _adaln_kernel,
        grid=(n_out // tn,),
        in_specs=[pl.BlockSpec((rows, D_MODEL), lambda j: (0, 0)),
                  pl.BlockSpec((D_MODEL, tn), lambda j: (0, j)),
                  pl.BlockSpec((1, tn), lambda j: (0, j))],
        out_specs=pl.BlockSpec((rows, tn), lambda j: (0, j)),
        out_shape=jax.ShapeDtypeStruct((rows, n_out), F32),
        compiler_params=_cparams("arbitrary"),
        name="adaln",
    )(c_all, w_ada, b_ada.reshape(1, n_out))


def _inproj_kernel(x_ref, g_ref, sc_ref, sh_ref, w_ref,
                   qkv_ref, z_ref, qb_ref, kvb_ref, ga_ref, gb_ref, ab_ref):
    h = (_rms(x_ref[...], g_ref[...]) * (1.0 + sc_ref[0]) + sh_ref[0]).astype(BF16)

    def fill(ref, c0, width, fn):
        step = min(PROJ_TILE, width)
        for c in range(0, width, step):
            acc = jnp.dot(h, w_ref[:, c0 + c:c0 + c + step], preferred_element_type=F32)
            ref[:, c:c + step] = fn(acc).astype(ref.dtype)

    ident = lambda v: v
    fill(qkv_ref, _C_QKV, GDN_CONV_DIM, ident)
    fill(z_ref, _C_Z, GDN_WIDTH, _silu)
    fill(qb_ref, _C_QB, SWA_WIDTH, ident)
    fill(kvb_ref, _C_KVB, 2 * SWA_KV_WIDTH, ident)
    fill(ga_ref, _C_GA, D_MODEL, _sigmoid)
    fill(gb_ref, _C_GB, D_MODEL, _sigmoid)
    fill(ab_ref, _C_AB, LANES, ident)


def _mod_spec(mod, n_tiles):
    tiles_per_mod = n_tiles // mod.shape[0]
    return pl.BlockSpec((1, mod.shape[1], D_MODEL), lambda i: (i // tiles_per_mod, 0, 0))


def _inproj(x2d, gain, sc, sh, w_prep, tm):
    rows = x2d.shape[0]
    n_tiles = rows // tm
    widths = (GDN_CONV_DIM, GDN_WIDTH, SWA_WIDTH, 2 * SWA_KV_WIDTH, D_MODEL, D_MODEL, LANES)
    dtypes = (BF16, BF16, BF16, BF16, BF16, BF16, F32)
    return pl.pallas_call(
        _inproj_kernel,
        grid=(n_tiles,),
        in_specs=[pl.BlockSpec((tm, D_MODEL), lambda i: (i, 0)),
                  pl.BlockSpec((1, D_MODEL), lambda i: (0, 0)),
                  _mod_spec(sc, n_tiles), _mod_spec(sh, n_tiles),
                  pl.BlockSpec((D_MODEL, IN_COLS), lambda i: (0, 0))],
        out_specs=[pl.BlockSpec((tm, w), lambda i: (i, 0)) for w in widths],
        out_shape=[jax.ShapeDtypeStruct((rows, w), dt) for w, dt in zip(widths, dtypes)],
        compiler_params=_cparams("arbitrary"),
        name="inproj",
    )(x2d, gain.reshape(1, D_MODEL), sc, sh, w_prep)


def _cumsum_rows(g):
    c = g.shape[0]
    tril = (_iota2((c, c), 0) >= _iota2((c, c), 1)).astype(BF16)
    hi = g.astype(BF16)
    r1 = g - hi.astype(F32)
    mid = r1.astype(BF16)
    lo = (r1 - mid.astype(F32)).astype(BF16)
    dot = lambda p: jnp.dot(tril, p, preferred_element_type=F32)
    return dot(hi) + dot(mid) + dot(lo)


def _each(fn, *lists):
    return [fn(*args) for args in zip(*lists)]


def _unit_lower_inverse_offset(a_list):
    c = a_list[0].shape[0]
    ii = _iota2((c, c), 0)
    jj = _iota2((c, c), 1)

    def same_block(shift):
        return lax.shift_right_logical(ii, shift) == lax.shift_right_logical(jj, shift)

    base = same_block(1)
    n_list = _each(lambda a: jnp.where(base, -a, 0.0), a_list)
    shift = 1
    while (1 << shift) < c:
        outer, inner = same_block(shift + 1), same_block(shift)
        off_list = _each(lambda a: jnp.where(outer, jnp.where(inner, 0.0, a), 0.0), a_list)
        x_list = _each(lambda off, n: off + _bdot(off, n), off_list, n_list)
        n_list = _each(lambda n, x: n - x - _bdot(n, x), n_list, x_list)
        shift += 1
    return n_list


def _chunk_prep(q, k, v, gcol, grow, bcol):
    c = q[0].shape[0]
    ii = _iota2((c, c), 0)
    jj = _iota2((c, c), 1)
    causal = ii >= jj
    strict = ii > jj
    decay = _each(lambda gc, gr: jnp.where(causal, jnp.exp(jnp.where(causal, gc - gr, 0.0)), 0.0), gcol, grow)
    kb = _each(lambda kk, b: kk * b, k, bcol)
    both = _each(lambda qq, kbb, kk: _bdot_nt(jnp.concatenate([qq, kbb], axis=0), kk), q, kb, k)
    qk = _each(lambda bo, d: bo[:c] * d, both, decay)
    a = _each(lambda bo, d: jnp.where(strict, bo[c:] * d, 0.0), both, decay)
    n = _unit_lower_inverse_offset(a)
    eg = _each(jnp.exp, gcol)
    rhs = _each(lambda vv, b, kbb, e: jnp.concatenate([vv * b, kbb * e], axis=1), v, bcol, kb, eg)
    uw = _each(lambda r, nn: r + _bdot(nn, r), rhs, n)
    u = [x[:, :GDN_DV] for x in uw]
    w = [x[:, GDN_DV:] for x in uw]
    qd = _each(lambda qq, e: qq * e, q, eg)
    kd = _each(lambda kk, gc: kk * jnp.exp(gc[c - 1:c, :] - gc), k, gcol)
    return u, w, qd, kd, qk


def _chunk_step(s, u, w, qd, kd, qk, ge):
    c = u[0].shape[0]
    both = _each(lambda ww, qq, ss: _bdot(jnp.concatenate([ww, qq], axis=0), ss), w, qd, s)
    v_new = _each(lambda uu, bo: uu.astype(F32) - bo[:c], u, both)
    o = _each(lambda bo, m, vn: bo[c:] + _bdot(m, vn), both, qk, v_new)
    s_new = _each(lambda ss, g, kk, vn: ss * g + _bdot_tn(kk, vn), s, ge, kd, v_new)
    return o, s_new


def _conv_act(xp_ref, cw_ref, r0, rows, c0):
    cols = slice(c0, c0 + LANES)
    acc = cw_ref[3:4, cols] * xp_ref[r0:r0 + rows, cols]
    for j in range(GDN_CONV - 1):
        acc = acc + cw_ref[j:j + 1, cols] * xp_ref[r0 - 3 + j:r0 - 3 + j + rows, cols]
    return _silu(acc)


def _l2n(x):
    return x * lax.rsqrt(jnp.sum(x * x, axis=-1, keepdims=True) + NORM_EPS)


def _softplus(x):
    return jnp.maximum(x, 0.0) + jnp.log1p(jnp.exp(-jnp.abs(x)))


def _head_cols(hd):
    return (hd * GDN_DK, GDN_QK_WIDTH + hd * GDN_DK, 2 * GDN_QK_WIDTH + hd * GDN_DV)


def _activate_qkv(xp_ref, cw_ref, act_ref, r0, rows):
    for hd in range(GDN_HEADS):
        cq, ck, cv = _head_cols(hd)
        act_ref[0:rows, cq:cq + LANES] = _l2n(_conv_act(xp_ref, cw_ref, r0, rows, cq)) * (GDN_DK ** -0.5)
        act_ref[0:rows, ck:ck + LANES] = _l2n(_conv_act(xp_ref, cw_ref, r0, rows, ck))
        act_ref[0:rows, cv:cv + LANES] = _conv_act(xp_ref, cw_ref, r0, rows, cv)


def _decay_beta(ab, hp_ref):
    g = -jnp.exp(hp_ref[0:1, :]) * _softplus(ab + hp_ref[1:2, :])
    return g, _sigmoid(ab)


def _gdn_prep_kernel(qkv_ref, halo_ref, ab_ref, cw_ref, hp_ref,
                     u_ref, w_ref, qd_ref, kd_ref, qk_ref, ge_ref, xp_ref, act_ref):
    tb = qkv_ref.shape[1]
    cc = GDN_CHUNK
    first = pl.program_id(1) == 0
    xp_ref[0:BF16_ROWS, :] = jnp.where(first, 0.0, halo_ref[0].astype(F32))
    xp_ref[BF16_ROWS:BF16_ROWS + tb, :] = qkv_ref[0].astype(F32)
    _activate_qkv(xp_ref, cw_ref, act_ref, BF16_ROWS, tb)
    g_all, beta_all = _decay_beta(ab_ref[0], hp_ref)

    chunks = [slice(ci * cc, (ci + 1) * cc) for ci in range(tb // cc)]
    gcs = _each(lambda rows: _cumsum_rows(g_all[rows, :]), chunks)
    gcts = _each(lambda gc: gc.T, gcs)
    for ci, gc in enumerate(gcs):
        ge_ref[0, ci] = jnp.exp(gc[cc - 1:cc, :])
    items = [(ci, hd) for ci in range(len(chunks)) for hd in range(GDN_HEADS)]
    col = lambda which: [act_ref[chunks[ci], _head_cols(hd)[which]:_head_cols(hd)[which] + LANES]
                         for ci, hd in items]
    u, w, qd, kd, qk = _chunk_prep(
        col(0), col(1), col(2),
        [gcs[ci][:, hd:hd + 1] for ci, hd in items], [gcts[ci][hd:hd + 1, :] for ci, hd in items],
        [beta_all[chunks[ci], GDN_HEADS + hd:GDN_HEADS + hd + 1] for ci, hd in items])
    for idx, (ci, hd) in enumerate(items):
        rows = chunks[ci]
        oc = slice(hd * GDN_DV, (hd + 1) * GDN_DV)
        u_ref[0, rows, oc] = u[idx].astype(BF16)
        w_ref[0, rows, oc] = w[idx].astype(BF16)
        qd_ref[0, rows, oc] = qd[idx].astype(BF16)
        kd_ref[0, rows, oc] = kd[idx].astype(BF16)
        qk_ref[0, rows, hd * cc:(hd + 1) * cc] = qk[idx].astype(BF16)


def _gdn_prep(qkv, ab, conv_w, head_params):
    b, t, _ = qkv.shape
    tb = min(GDN_PREP_ROWS, t)
    nch = tb // GDN_CHUNK
    halo_per_block = tb // BF16_ROWS
    blk = lambda w: pl.BlockSpec((1, tb, w), lambda bi, i: (bi, i, 0))
    out_shapes = [jax.ShapeDtypeStruct((b, t, GDN_WIDTH), BF16)] * 4 + [
        jax.ShapeDtypeStruct((b, t, GDN_HEADS * GDN_CHUNK), BF16),
        jax.ShapeDtypeStruct((b, t // GDN_CHUNK, 1, LANES), F32)]
    return pl.pallas_call(
        _gdn_prep_kernel,
        grid=(b, t // tb),
        in_specs=[blk(GDN_CONV_DIM),
                  pl.BlockSpec((1, BF16_ROWS, GDN_CONV_DIM),
                               lambda bi, i: (bi, jnp.maximum(i * halo_per_block - 1, 0), 0)),
                  blk(LANES),
                  pl.BlockSpec((GDN_CONV, GDN_CONV_DIM), lambda bi, i: (0, 0)),
                  pl.BlockSpec((8, LANES), lambda bi, i: (0, 0))],
        out_specs=[blk(GDN_WIDTH)] * 4 + [
            blk(GDN_HEADS * GDN_CHUNK),
            pl.BlockSpec((1, nch, 1, LANES), lambda bi, i: (bi, i, 0, 0))],
        out_shape=out_shapes,
        scratch_shapes=[pltpu.VMEM((tb + BF16_ROWS, GDN_CONV_DIM), F32),
                        pltpu.VMEM((tb, GDN_CONV_DIM), F32)],
        compiler_params=_cparams("parallel", "parallel"),
        name="gdn_prep",
    )(qkv, qkv, ab, conv_w, head_params)


def _gated_norm_store(o_ref, idx, o, gain, zs):
    o_ref[idx] = (_rms(o, gain) * zs.astype(F32)).astype(o_ref.dtype)


def _gdn_scan_kernel(u_ref, w_ref, qd_ref, kd_ref, qk_ref, ge_ref, zs_ref, gain_ref,
                     o_ref, s_out_ref, s_ref):
    nb, tb, _ = u_ref.shape
    cc = GDN_CHUNK
    step = pl.program_id(0)

    @pl.when(step == 0)
    def _():
        s_ref[...] = jnp.zeros_like(s_ref)

    gain = gain_ref[...]

    def chunk_body(ci, carry):
        rows = pl.ds(pl.multiple_of(ci * cc, cc), cc)
        items = [(bi, hd) for bi in range(nb) for hd in range(GDN_HEADS)]
        oc = lambda hd: slice(hd * GDN_DV, (hd + 1) * GDN_DV)
        ge_rows = [ge_ref[bi, ci] for bi in range(nb)]
        o, s_new = _chunk_step(
            [s_ref[bi * GDN_HEADS + hd] for bi, hd in items],
            [u_ref[bi, rows, oc(hd)] for bi, hd in items], [w_ref[bi, rows, oc(hd)] for bi, hd in items],
            [qd_ref[bi, rows, oc(hd)] for bi, hd in items], [kd_ref[bi, rows, oc(hd)] for bi, hd in items],
            [qk_ref[bi, rows, hd * cc:(hd + 1) * cc] for bi, hd in items],
            [ge_rows[bi][:, hd:hd + 1] for bi, hd in items])
        for idx, (bi, hd) in enumerate(items):
            s_ref[bi * GDN_HEADS + hd] = s_new[idx]
            _gated_norm_store(o_ref, (bi, rows, oc(hd)), o[idx], gain, zs_ref[bi, rows, oc(hd)])
        return carry

    lax.fori_loop(0, tb // cc, chunk_body, 0)

    @pl.when(step == pl.num_programs(0) - 1)
    def _():
        s_out_ref[...] = s_ref[...]


def _gdn_scan(u, w, qd, kd, qk, ge, zs, gain):
    b, t, _ = u.shape
    tb = min(GDN_SCAN_ROWS, t)
    nch = tb // GDN_CHUNK
    blk = lambda wd: pl.BlockSpec((b, tb, wd), lambda i: (0, i, 0))
    o, s = pl.pallas_call(
        _gdn_scan_kernel,
        grid=(t // tb,),
        in_specs=[blk(GDN_WIDTH)] * 4 + [
            blk(GDN_HEADS * GDN_CHUNK),
            pl.BlockSpec((b, nch, 1, LANES), lambda i: (0, i, 0, 0)),
            blk(GDN_WIDTH),
            pl.BlockSpec((1, GDN_DV), lambda i: (0, 0))],
        out_specs=[blk(GDN_WIDTH),
                   pl.BlockSpec((b * GDN_HEADS, GDN_DK, GDN_DV), lambda i: (0, 0, 0))],
        out_shape=[jax.ShapeDtypeStruct((b, t, GDN_WIDTH), BF16),
                   jax.ShapeDtypeStruct((b * GDN_HEADS, GDN_DK, GDN_DV), F32)],
        scratch_shapes=[pltpu.VMEM((b * GDN_HEADS, GDN_DK, GDN_DV), F32)],
        compiler_params=_cparams("arbitrary"),
        name="gdn_scan",
    )(u, w, qd, kd, qk, ge, zs, gain.reshape(1, GDN_DV))
    return o, s.reshape(b, GDN_HEADS, GDN_DK, GDN_DV)


def _gdn_sample_kernel(new_rows, xp_ref, ab_ref, zs_ref, s0_ref, cw_ref, hp_ref, gain_ref,
                       o_ref, s_out_ref, xs_ref, act_ref):
    grp = xp_ref.shape[0]
    cc = GDN_SAMPLE_CHUNK
    gain = gain_ref[...]
    rowmask = (_iota2((cc, 1), 0) >= cc - new_rows).astype(F32)
    seqs = list(range(grp))
    for bi in seqs:
        xs = xs_ref.at[bi]
        xs[0:8, :] = jnp.zeros((8, GDN_CONV_DIM), F32)
        xs[8:8 + cc, :] = xp_ref[bi].astype(F32)
        _activate_qkv(xs, cw_ref, act_ref.at[bi], 8, cc)
    gb = _each(lambda bi: _decay_beta(ab_ref[bi], hp_ref), seqs)
    gcs = _each(lambda x: _cumsum_rows(x[0] * rowmask), gb)
    gcts = _each(lambda gc: gc.T, gcs)
    ge_rows = _each(lambda gc: jnp.exp(gc[cc - 1:cc, :]), gcs)
    betas = _each(lambda x: x[1] * rowmask, gb)
    items = [(bi, hd) for bi in seqs for hd in range(GDN_HEADS)]
    col = lambda which: [act_ref[bi, :, _head_cols(hd)[which]:_head_cols(hd)[which] + LANES] * rowmask
                         for bi, hd in items]
    u, w, qd, kd, qk = _chunk_prep(
        col(0), col(1), col(2),
        [gcs[bi][:, hd:hd + 1] for bi, hd in items], [gcts[bi][hd:hd + 1, :] for bi, hd in items],
        [betas[bi][:, GDN_HEADS + hd:GDN_HEADS + hd + 1] for bi, hd in items])
    o, s_new = _chunk_step([s0_ref[bi, hd] for bi, hd in items], u, w, qd, kd, qk,
                           [ge_rows[bi][:, hd:hd + 1] for bi, hd in items])
    for idx, (bi, hd) in enumerate(items):
        s_out_ref[bi, hd] = s_new[idx]
        oc = slice(hd * GDN_DV, (hd + 1) * GDN_DV)
        _gated_norm_store(o_ref, (bi, slice(None), oc), o[idx], gain, zs_ref[bi, :, oc])


def _gdn_sample(xp, ab, zs, s0, conv_w, head_params, gain, new_rows):
    b = xp.shape[0]
    cc = GDN_SAMPLE_CHUNK
    grp = GDN_SAMPLE_GROUP
    blk3 = lambda w: pl.BlockSpec((grp, cc, w), lambda i: (i, 0, 0))
    sblk = pl.BlockSpec((grp, GDN_HEADS, GDN_DK, GDN_DV), lambda i: (i, 0, 0, 0))
    return pl.pallas_call(
        functools.partial(_gdn_sample_kernel, new_rows),
        grid=(b // grp,),
        in_specs=[blk3(GDN_CONV_DIM), blk3(LANES), blk3(GDN_WIDTH), sblk,
                  pl.BlockSpec((GDN_CONV, GDN_CONV_DIM), lambda i: (0, 0)),
                  pl.BlockSpec((8, LANES), lambda i: (0, 0)),
                  pl.BlockSpec((1, GDN_DV), lambda i: (0, 0))],
        out_specs=[blk3(GDN_WIDTH), sblk],
        out_shape=[jax.ShapeDtypeStruct((b, cc, GDN_WIDTH), BF16),
                   jax.ShapeDtypeStruct((b, GDN_HEADS, GDN_DK, GDN_DV), F32)],
        scratch_shapes=[pltpu.VMEM((grp, cc + 8, GDN_CONV_DIM), F32),
                        pltpu.VMEM((grp, cc, GDN_CONV_DIM), F32)],
        compiler_params=_cparams("parallel"),
        name="gdn_sample",
    )(xp, ab, zs, s0, conv_w, head_params, gain.reshape(1, GDN_DV))


def _sink_attention(q, k, v, mask, sink_col):
    s = _each(lambda qq, kk: jnp.where(mask, _bdot_nt(qq, kk) * (SWA_HEAD_DIM ** -0.5), -jnp.inf), q, k)
    m = _each(lambda ss, sk: jnp.maximum(jnp.max(ss, axis=-1, keepdims=True), sk), s, sink_col)
    p = _each(lambda ss, mm: jnp.exp(ss - mm), s, m)
    denom = _each(lambda pp, sk, mm: jnp.sum(pp, axis=-1, keepdims=True) + jnp.exp(sk - mm), p, sink_col, m)
    return _each(lambda pp, vv, dd: _bdot(pp, vv) / dd, p, v, denom)


def _sink_column(sinks_ref, kv_head, rows_per_head):
    parts = [jnp.full((rows_per_head, 1), sinks_ref[kv_head * SWA_GROUP + g], F32) for g in range(SWA_GROUP)]
    return jnp.concatenate(parts, axis=0)


def _swa_prompt_kernel(sinks_ref, q_ref, kvp_ref, kvc_ref, o_ref):
    wnd = WINDOW
    blk = pl.program_id(1)
    kv = jnp.concatenate([kvp_ref[0], kvc_ref[0]], axis=0)
    rows = SWA_GROUP * wnd
    qi = _iota2((rows, 2 * wnd), 0) & (wnd - 1)
    kj = _iota2((rows, 2 * wnd), 1)
    dist = qi + wnd - kj
    first_key = jnp.where(blk > 0, 0, wnd)
    mask = (dist >= 0) & (dist <= wnd) & (kj >= first_key)
    kv_heads = list(range(SWA_KV_HEADS))
    head_cols = lambda h: slice(h * SWA_HEAD_DIM, (h + 1) * SWA_HEAD_DIM)
    q_heads = lambda hk: [hk * SWA_GROUP + g for g in range(SWA_GROUP)]
    o = _sink_attention(
        [jnp.concatenate([q_ref[0, :, head_cols(h)] for h in q_heads(hk)], axis=0) for hk in kv_heads],
        [kv[:, head_cols(hk)] for hk in kv_heads],
        [kv[:, SWA_KV_WIDTH + hk * SWA_HEAD_DIM:SWA_KV_WIDTH + (hk + 1) * SWA_HEAD_DIM] for hk in kv_heads],
        mask, [_sink_column(sinks_ref, hk, wnd) for hk in kv_heads])
    for hk in kv_heads:
        for g, h in enumerate(q_heads(hk)):
            o_ref[0, :, head_cols(h)] = o[hk][g * wnd:(g + 1) * wnd].astype(o_ref.dtype)


def _swa_prompt(q, kv, sinks):
    b, t, _ = q.shape
    return pl.pallas_call(
        _swa_prompt_kernel,
        grid=(b, t // WINDOW),
        in_specs=[pl.BlockSpec(memory_space=pltpu.SMEM),
                  pl.BlockSpec((1, WINDOW, SWA_WIDTH), lambda bi, i: (bi, i, 0)),
                  pl.BlockSpec((1, WINDOW, 2 * SWA_KV_WIDTH), lambda bi, i: (bi, jnp.maximum(i - 1, 0), 0)),
                  pl.BlockSpec((1, WINDOW, 2 * SWA_KV_WIDTH), lambda bi, i: (bi, i, 0))],
        out_specs=pl.BlockSpec((1, WINDOW, SWA_WIDTH), lambda bi, i: (bi, i, 0)),
        out_shape=jax.ShapeDtypeStruct((b, t, SWA_WIDTH), BF16),
        compiler_params=_cparams("parallel", "parallel"),
        name="swa_prompt",
    )(sinks, q, kv, kv)


def _swa_sample_kernel(sinks_ref, q_ref, kvn_ref, kc_ref, vc_ref, o_ref, ko_ref, vo_ref):
    grp, t, _ = q_ref.shape
    wnd = WINDOW
    nk = wnd + BF16_ROWS
    rows = SWA_GROUP * t
    tq = _iota2((rows, nk), 0) & (t - 1)
    kj = _iota2((rows, nk), 1)
    dist = tq + wnd - kj
    mask = (dist >= 0) & (dist <= wnd)
    zpad = jnp.zeros((BF16_ROWS - t, SWA_KV_WIDTH), F32)
    kks, vvs = [], []
    for bi in range(grp):
        kvn = kvn_ref[bi].astype(F32)
        kk = jnp.concatenate([kc_ref[bi], kvn[:, :SWA_KV_WIDTH], zpad], axis=0)
        vv = jnp.concatenate([vc_ref[bi], kvn[:, SWA_KV_WIDTH:], zpad], axis=0)
        ko_ref[bi] = kk[t:t + wnd, :]
        vo_ref[bi] = vv[t:t + wnd, :]
        kks.append(kk)
        vvs.append(vv)
    items = [(bi, hk) for bi in range(grp) for hk in range(SWA_KV_HEADS)]
    head_cols = lambda h: slice(h * SWA_HEAD_DIM, (h + 1) * SWA_HEAD_DIM)
    q_heads = lambda hk: [hk * SWA_GROUP + g for g in range(SWA_GROUP)]
    sink_cols = [_sink_column(sinks_ref, hk, t) for hk in range(SWA_KV_HEADS)]
    o = _sink_attention(
        [jnp.concatenate([q_ref[bi, :, head_cols(h)] for h in q_heads(hk)], axis=0) for bi, hk in items],
        [kks[bi][:, head_cols(hk)] for bi, hk in items], [vvs[bi][:, head_cols(hk)] for bi, hk in items],
        mask, [sink_cols[hk] for bi, hk in items])
    for idx, (bi, hk) in enumerate(items):
        for g, h in enumerate(q_heads(hk)):
            o_ref[bi, :, head_cols(h)] = o[idx][g * t:(g + 1) * t].astype(o_ref.dtype)


def _swa_sample(q, kv_new, k_cache, v_cache, sinks):
    b, t, _ = q.shape
    grp = SWA_SAMPLE_GROUP
    blk = lambda r, w: pl.BlockSpec((grp, r, w), lambda i: (i, 0, 0))
    return pl.pallas_call(
        _swa_sample_kernel,
        grid=(b // grp,),
        in_specs=[pl.BlockSpec(memory_space=pltpu.SMEM),
                  blk(t, SWA_WIDTH), blk(t, 2 * SWA_KV_WIDTH), blk(WINDOW, SWA_KV_WIDTH), blk(WINDOW, SWA_KV_WIDTH)],
        out_specs=[blk(t, SWA_WIDTH), blk(WINDOW, SWA_KV_WIDTH), blk(WINDOW, SWA_KV_WIDTH)],
        out_shape=[jax.ShapeDtypeStruct((b, t, SWA_WIDTH), BF16),
                   jax.ShapeDtypeStruct((b, WINDOW, SWA_KV_WIDTH), F32),
                   jax.ShapeDtypeStruct((b, WINDOW, SWA_KV_WIDTH), F32)],
        compiler_params=_cparams("parallel"),
        name="swa_sample",
    )(sinks, q, kv_new, k_cache, v_cache)


def _route(logits):
    lane = _iota2(logits.shape, 1).astype(F32)
    neg = -jnp.inf

    def first_argmax(vals, valid):
        v = jnp.where(valid, vals, neg)
        m = jnp.max(v, axis=-1, keepdims=True)
        idx = jnp.min(jnp.where(jnp.logical_and(valid, v == m), lane, float(LANES)), axis=-1, keepdims=True)
        return m, idx

    is_group = lane < N_GROUPS
    gmax, gidx = first_argmax(logits, is_group)
    p_group = 1.0 / jnp.sum(jnp.where(is_group, jnp.exp(logits - gmax), 0.0), axis=-1, keepdims=True)
    lo = N_GROUPS + gidx * EXPERTS_PER_GROUP
    in_group = jnp.logical_and(lane >= lo, lane < lo + EXPERTS_PER_GROUP)
    m1, i1 = first_argmax(logits, in_group)
    esum = jnp.sum(jnp.where(in_group, jnp.exp(logits - m1), 0.0), axis=-1, keepdims=True)
    m2, i2 = first_argmax(logits, jnp.logical_and(in_group, lane != i1))
    p1 = 1.0 / esum
    p2 = jnp.exp(m2 - m1) / esum
    tot = p1 + p2
    return i1 - N_GROUPS, i2 - N_GROUPS, p_group * p1 / tot, p_group * p2 / tot


def _post_mixer_kernel(oa_ref, ob_ref, ga_ref, gb_ref, x_ref, gt_ref, sc_ref, sh_ref,
                       wa_ref, wb_ref, wo_ref, gpost_ref, gpre_ref, wr_ref, br_ref,
                       x1_ref, h2_ref, rt_ref):
    merged = (ga_ref[...].astype(F32) * jnp.dot(oa_ref[...], wa_ref[...], preferred_element_type=F32)
              + gb_ref[...].astype(F32) * jnp.dot(ob_ref[...], wb_ref[...], preferred_element_type=F32))
    mix = _bdot(merged, wo_ref[...])
    x1 = x_ref[...] + gt_ref[0] * _rms(mix, gpost_ref[...])
    x1_ref[...] = x1
    h2 = _rms(x1, gpre_ref[...]) * (1.0 + sc_ref[0]) + sh_ref[0]
    h2_ref[...] = h2
    logits = jnp.dot(h2, wr_ref[...], preferred_element_type=F32, precision=lax.Precision.HIGHEST) + br_ref[...]
    ia, ib, wa, wb = _route(logits)
    lane = _iota2(logits.shape, 1)
    rt_ref[...] = jnp.where(lane == 0, ia, jnp.where(lane == 1, ib, jnp.where(lane == 2, wa,
                            jnp.where(lane == 3, wb, 0.0))))


def _post_mixer(oa, ob, ga, gb, x2d, gt, sc, sh, w_a, w_b, w_o, g_post, g_pre, w_rt, b_rt, tm):
    rows = x2d.shape[0]
    n_tiles = rows // tm
    row_blk = lambda w: pl.BlockSpec((tm, w), lambda i: (i, 0))
    full = lambda r, c: pl.BlockSpec((r, c), lambda i: (0, 0))
    return pl.pallas_call(
        _post_mixer_kernel,
        grid=(n_tiles,),
        in_specs=[row_blk(GDN_WIDTH), row_blk(SWA_WIDTH), row_blk(D_MODEL), row_blk(D_MODEL), row_blk(D_MODEL),
                  _mod_spec(gt, n_tiles), _mod_spec(sc, n_tiles), _mod_spec(sh, n_tiles),
                  full(GDN_WIDTH, D_MODEL), full(SWA_WIDTH, D_MODEL), full(D_MODEL, D_MODEL),
                  full(1, D_MODEL), full(1, D_MODEL), full(D_MODEL, LANES), full(1, LANES)],
        out_specs=[row_blk(D_MODEL), row_blk(D_MODEL), row_blk(LANES)],
        out_shape=[jax.ShapeDtypeStruct((rows, D_MODEL), F32),
                   jax.ShapeDtypeStruct((rows, D_MODEL), F32),
                   jax.ShapeDtypeStruct((rows, LANES), F32)],
        compiler_params=_cparams("arbitrary"),
        name="post_mixer",
    )(oa, ob, ga, gb, x2d, gt, sc, sh, w_a, w_b, w_o,
      g_post.reshape(1, D_MODEL), g_pre.reshape(1, D_MODEL), w_rt, b_rt)


def _row_gather(src_hbm, idx_ref, base, dst, sem, n_rows, start):
    def body(r, carry):
        cp = pltpu.make_async_copy(src_hbm.at[pl.ds(idx_ref[base + r], 1)], dst.at[pl.ds(r, 1)], sem)
        if start:
            cp.start()
        else:
            cp.wait()
        return carry

    lax.fori_loop(0, n_rows, body, 0, unroll=8)


def _moe_kernel(blk_e_ref, n_used_ref, slot_tok_ref, h_hbm, wg_ref, wu_ref, wd_ref, y_ref,
                xbuf, wgb, wub, wdb, sems):
    b = pl.program_id(0)
    nb = pl.num_programs(0)
    rows = xbuf.shape[1]
    slot = b % 2

    @pl.when(b == 0)
    def _():
        _row_gather(h_hbm, slot_tok_ref, 0, xbuf.at[0], sems.at[0], rows, True)

    @pl.when(b + 1 < nb)
    def _():
        _row_gather(h_hbm, slot_tok_ref, (b + 1) * rows, xbuf.at[1 - slot], sems.at[1 - slot], rows, True)

    changed = jnp.logical_or(b == 0, blk_e_ref[b] != blk_e_ref[jnp.maximum(b - 1, 0)])

    @pl.when(changed)
    def _():
        wgb[...] = wg_ref[0].astype(BF16)
        wub[...] = wu_ref[0].astype(BF16)
        wdb[...] = wd_ref[0].astype(BF16)

    _row_gather(h_hbm, slot_tok_ref, b * rows, xbuf.at[slot], sems.at[slot], rows, False)

    @pl.when(b < n_used_ref[0])
    def _():
        x = xbuf[slot].astype(BF16)
        gate = jnp.dot(x, wgb[...], preferred_element_type=F32)
        up = jnp.dot(x, wub[...], preferred_element_type=F32)
        y_ref[...] = _bdot(_silu(gate) * up, wdb[...])

    @pl.when(b >= n_used_ref[0])
    def _():
        y_ref[...] = jnp.zeros_like(y_ref)


def _moe(h2, blk_e, n_used, slot_tok, w_gate, w_up, w_down):
    n_blocks = blk_e.shape[0]
    rows = MOE_ROWS
    wspec = lambda r, c: pl.BlockSpec((1, r, c), lambda b, be, nu, st: (be[b], 0, 0))
    return pl.pallas_call(
        _moe_kernel,
        grid_spec=pltpu.PrefetchScalarGridSpec(
            num_scalar_prefetch=3,
            grid=(n_blocks,),
            in_specs=[pl.BlockSpec(memory_space=pl.ANY),
                      wspec(D_MODEL, EXPERT_FF), wspec(D_MODEL, EXPERT_FF), wspec(EXPERT_FF, D_MODEL)],
            out_specs=pl.BlockSpec((rows, D_MODEL), lambda b, be, nu, st: (b, 0)),
            scratch_shapes=[pltpu.VMEM((2, rows, D_MODEL), F32),
                            pltpu.VMEM((D_MODEL, EXPERT_FF), BF16),
                            pltpu.VMEM((D_MODEL, EXPERT_FF), BF16),
                            pltpu.VMEM((EXPERT_FF, D_MODEL), BF16),
                            pltpu.SemaphoreType.DMA((2,))]),
        out_shape=jax.ShapeDtypeStruct((n_blocks * rows, D_MODEL), F32),
        compiler_params=_cparams("arbitrary"),
        name="moe_experts",
    )(blk_e, n_used, slot_tok, h2, w_gate, w_up, w_down)


def _combine_kernel(dest_ref, y_hbm, x1_ref, rt_ref, gt_ref, gpost_ref, o_ref, ybuf, sems):
    i = pl.program_id(0)
    n = pl.num_programs(0)
    rows = ybuf.shape[1]
    slot = i % 2

    @pl.when(i == 0)
    def _():
        _row_gather(y_hbm, dest_ref, 0, ybuf.at[0], sems.at[0], rows, True)

    @pl.when(i + 1 < n)
    def _():
        _row_gather(y_hbm, dest_ref, (i + 1) * rows, ybuf.at[1 - slot], sems.at[1 - slot], rows, True)

    _row_gather(y_hbm, dest_ref, i * rows, ybuf.at[slot], sems.at[slot], rows, False)
    half = rows // 2
    rt = rt_ref[...]
    f = rt[:, 2:3] * ybuf[slot, 0:half, :] + rt[:, 3:4] * ybuf[slot, half:rows, :]
    o_ref[...] = x1_ref[...] + gt_ref[0] * _rms(f, gpost_ref[...])


def _combine(dest, yb, x1, rt, gt, g_post, tm):
    rows = x1.shape[0]
    n_tiles = rows // tm
    tiles_per_mod = n_tiles // gt.shape[0]
    return pl.pallas_call(
        _combine_kernel,
        grid_spec=pltpu.PrefetchScalarGridSpec(
            num_scalar_prefetch=1,
            grid=(n_tiles,),
            in_specs=[pl.BlockSpec(memory_space=pl.ANY),
                      pl.BlockSpec((tm, D_MODEL), lambda i, d: (i, 0)),
                      pl.BlockSpec((tm, LANES), lambda i, d: (i, 0)),
                      pl.BlockSpec((1, gt.shape[1], D_MODEL), lambda i, d: (i // tiles_per_mod, 0, 0)),
                      pl.BlockSpec((1, D_MODEL), lambda i, d: (0, 0))],
            out_specs=pl.BlockSpec((tm, D_MODEL), lambda i, d: (i, 0)),
            scratch_shapes=[pltpu.VMEM((2, 2 * tm, D_MODEL), F32),
                            pltpu.SemaphoreType.DMA((2,))]),
        out_shape=jax.ShapeDtypeStruct((rows, D_MODEL), F32),
        compiler_params=_cparams("arbitrary"),
        name="moe_combine",
    )(dest, yb, x1, rt, gt, g_post.reshape(1, D_MODEL))


def _dispatch_plan(ids, n_tok, tm):
    n_assign = n_tok * TOP_K
    flat_e = ids.reshape(n_assign)
    onehot = (flat_e[:, None] == jnp.arange(N_EXPERTS, dtype=I32)[None, :]).astype(I32)
    csum = jnp.cumsum(onehot, axis=0)
    rank = jnp.take_along_axis(csum, flat_e[:, None], axis=1)[:, 0] - 1
    counts = csum[-1]
    pcounts = (counts + MOE_ROWS - 1) // MOE_ROWS * MOE_ROWS
    pends = jnp.cumsum(pcounts)
    pstarts = pends - pcounts
    dest = pstarts[flat_e] + rank
    n_blocks = n_assign // MOE_ROWS + N_EXPERTS
    slot_tok = jnp.zeros((n_blocks * MOE_ROWS,), I32).at[dest].set(jnp.arange(n_assign, dtype=I32) // TOP_K)
    blk_start = jnp.arange(n_blocks, dtype=I32) * MOE_ROWS
    blk_e = jnp.minimum(jnp.sum(blk_start[:, None] >= pends[None, :], axis=1), N_EXPERTS - 1).astype(I32)
    n_used = (pends[-1] // MOE_ROWS).astype(I32).reshape(1)
    dest_tiles = dest.reshape(n_tok // tm, tm, TOP_K).transpose(0, 2, 1).reshape(-1)
    return blk_e, n_used, slot_tok, dest_tiles


def _prep_in_weight(w_in):
    a0 = 4 * GDN_QK_WIDTH
    a1 = a0 + 2 * GDN_HEADS
    pad = jnp.zeros((D_MODEL, LANES - 2 * GDN_HEADS), w_in.dtype)
    return jnp.concatenate([w_in[:, :a0], w_in[:, a1:], w_in[:, a0:a1], pad], axis=1).astype(BF16)


def _head_param_tile(a_log, dt_bias):
    tile = jnp.zeros((8, LANES), F32)
    return tile.at[0, :GDN_HEADS].set(a_log.astype(F32)).at[1, :GDN_HEADS].set(dt_bias.astype(F32))


def _router_weight(w_group, b_group, w_router, b_router):
    w = jnp.zeros((D_MODEL, LANES), F32)
    w = w.at[:, :N_GROUPS].set(w_group).at[:, N_GROUPS:N_GROUPS + N_EXPERTS].set(w_router)
    b = jnp.zeros((1, LANES), F32)
    b = b.at[0, :N_GROUPS].set(b_group).at[0, N_GROUPS:N_GROUPS + N_EXPERTS].set(b_router)
    return w, b


def kernel(x_prompt, x_sample, state_gdn, state_conv, cache_k_win, cache_v_win, c_prompt, c_sample, w_ada, b_ada, g_mix_pre, g_mix_post, g_ffn_pre, g_ffn_post, w_in, conv_w, a_log, dt_bias, gdn_norm, sinks, w_br_gdn, w_br_swa, w_out, w_group, b_group, w_router, b_router, w_gate, w_up, w_down):
    depth = w_ada.shape[0]
    assert depth == 1, "single-layer trunk"
    bp, tp, _ = x_prompt.shape
    bs, ts, _ = x_sample.shape
    n_p = bp * tp
    n_s = bs * ts
    tm = ROW_TILE
    assert tp % tm == 0 and n_s % tm == 0 and ts >= GDN_CONV - 1 and ts + GDN_CONV - 1 <= GDN_SAMPLE_CHUNK
    assert ts & (ts - 1) == 0 and ts <= BF16_ROWS

    c_all = jnp.concatenate([c_prompt, c_sample], axis=0)
    c_rows = -(-c_all.shape[0] // 8) * 8
    c_all = jnp.pad(c_all, ((0, c_rows - c_all.shape[0]), (0, 0)))
    mod = _adaln(c_all, w_ada[0], b_ada[0])
    mods_p = [m[:bp].reshape(bp, 1, D_MODEL) for m in jnp.split(mod, 6, axis=-1)]
    mods_s = [jnp.repeat(m[bp:bp + bs], ts, axis=0).reshape(n_s // tm, tm, D_MODEL)
              for m in jnp.split(mod, 6, axis=-1)]

    w_prep = _prep_in_weight(w_in[0])
    head_params = _head_param_tile(a_log[0], dt_bias[0])
    w_a, w_b, w_o = w_br_gdn[0].astype(BF16), w_br_swa[0].astype(BF16), w_out[0].astype(BF16)
    w_rt, b_rt = _router_weight(w_group[0], b_group[0], w_router[0], b_router[0])
    sinks0 = sinks[0].astype(F32)

    xp2d = x_prompt.reshape(n_p, D_MODEL)
    sh1, sc1, gt1, sh2, sc2, gt2 = mods_p
    qkv_p, zs_p, qb_p, kvb_p, ga_p, gb_p, ab_p = _inproj(xp2d, g_mix_pre[0], sc1, sh1, w_prep, tm)
    qkv_p3 = qkv_p.reshape(bp, tp, GDN_CONV_DIM)
    u, w, qd, kd, qk, ge = _gdn_prep(qkv_p3, ab_p.reshape(bp, tp, LANES), conv_w[0], head_params)
    oa_p, s_prompt = _gdn_scan(u, w, qd, kd, qk, ge, zs_p.reshape(bp, tp, GDN_WIDTH), gdn_norm[0])
    kvb_p3 = kvb_p.reshape(bp, tp, 2 * SWA_KV_WIDTH)
    ob_p = _swa_prompt(qb_p.reshape(bp, tp, SWA_WIDTH), kvb_p3, sinks0)
    x1_p, h2_p, rt_p = _post_mixer(oa_p.reshape(n_p, GDN_WIDTH), ob_p.reshape(n_p, SWA_WIDTH), ga_p, gb_p, xp2d,
                                   gt1, sc2, sh2, w_a, w_b, w_o, g_mix_post[0], g_ffn_pre[0], w_rt, b_rt, tm)

    xs2d = x_sample.reshape(n_s, D_MODEL)
    sh1s, sc1s, gt1s, sh2s, sc2s, gt2s = mods_s
    qkv_s, zs_s, qb_s, kvb_s, ga_s, gb_s, ab_s = _inproj(xs2d, g_mix_pre[0], sc1s, sh1s, w_prep, tm)
    cc = GDN_SAMPLE_CHUNK
    pad_rows = cc - ts - (GDN_CONV - 1)
    qkv_s3 = qkv_s.reshape(bs, ts, GDN_CONV_DIM)
    xp_s = jnp.concatenate([jnp.zeros((bs, pad_rows, GDN_CONV_DIM), BF16), state_conv[0].astype(BF16), qkv_s3],
                           axis=1)
    front = lambda a: jnp.pad(a, ((0, 0), (cc - ts, 0), (0, 0)))
    oa_s16, s_sample = _gdn_sample(xp_s, front(ab_s.reshape(bs, ts, LANES)), front(zs_s.reshape(bs, ts, GDN_WIDTH)),
                                   state_gdn[0].astype(F32), conv_w[0], head_params, gdn_norm[0], ts)
    oa_s = oa_s16[:, cc - ts:, :].reshape(n_s, GDN_WIDTH)
    ob_s, k_new_s, v_new_s = _swa_sample(
        qb_s.reshape(bs, ts, SWA_WIDTH), kvb_s.reshape(bs, ts, 2 * SWA_KV_WIDTH),
        cache_k_win[0].reshape(bs, WINDOW, SWA_KV_WIDTH).astype(F32),
        cache_v_win[0].reshape(bs, WINDOW, SWA_KV_WIDTH).astype(F32), sinks0)
    x1_s, h2_s, rt_s = _post_mixer(oa_s, ob_s.reshape(n_s, SWA_WIDTH), ga_s, gb_s, xs2d,
                                   gt1s, sc2s, sh2s, w_a, w_b, w_o, g_mix_post[0], g_ffn_pre[0], w_rt, b_rt, tm)

    n_all = n_p + n_s
    h2 = jnp.concatenate([h2_p, h2_s], axis=0)
    rt = jnp.concatenate([rt_p, rt_s], axis=0)
    ids = rt[:, :TOP_K].astype(I32)
    ct = COMBINE_ROWS
    blk_e, n_used, slot_tok, dest_tiles = _dispatch_plan(ids, n_all, ct)
    yb = _moe(h2, blk_e, n_used, slot_tok, w_gate[0], w_up[0], w_down[0])
    y_p = _combine(dest_tiles[:n_p * TOP_K], yb, x1_p, rt_p, gt2, g_ffn_post[0], ct)
    y_s = _combine(dest_tiles[n_p * TOP_K:], yb, x1_s, rt_s,
                   gt2s.reshape(n_s // ct, ct, D_MODEL), g_ffn_post[0], ct)

    f32 = lambda a: a.astype(F32)
    kv_tail = kvb_p3[:, tp - WINDOW:, :]
    kv_heads = lambda a: f32(a).reshape(a.shape[0], WINDOW, SWA_KV_HEADS, SWA_HEAD_DIM)[None]
    return (y_p.reshape(bp, tp, D_MODEL), y_s.reshape(bs, ts, D_MODEL),
            s_prompt[None], f32(qkv_p3[:, tp - (GDN_CONV - 1):, :])[None],
            kv_heads(kv_tail[:, :, :SWA_KV_WIDTH]), kv_heads(kv_tail[:, :, SWA_KV_WIDTH:]),
            s_sample[None], f32(qkv_s3[:, ts - (GDN_CONV - 1):, :])[None],
            kv_heads(k_new_s), kv_heads(v_new_s))
```

```python
import functools

import jax
import jax.numpy as jnp
from jax import lax
from jax.experimental import pallas as pl
from jax.experimental.pallas import tpu as pltpu

F32 = jnp.float32
BF16 = jnp.bfloat16
I32 = jnp.int32

D_MODEL = 1024
NORM_EPS = 1e-6
GDN_HEADS = 8
GDN_DK = 128
GDN_DV = 128
GDN_CONV = 4
GDN_CHUNK = 64
GDN_QK_WIDTH = GDN_HEADS * GDN_DK
GDN_WIDTH = GDN_HEADS * GDN_DV
GDN_CONV_DIM = 2 * GDN_QK_WIDTH + GDN_WIDTH
SWA_Q_HEADS = 16
SWA_KV_HEADS = 4
SWA_HEAD_DIM = 64
SWA_GROUP = SWA_Q_HEADS // SWA_KV_HEADS
SWA_WIDTH = SWA_Q_HEADS * SWA_HEAD_DIM
SWA_KV_WIDTH = SWA_KV_HEADS * SWA_HEAD_DIM
WINDOW = 128
N_GROUPS = 4
EXPERTS_PER_GROUP = 8
N_EXPERTS = N_GROUPS * EXPERTS_PER_GROUP
TOP_K = 2
EXPERT_FF = 512

LANES = 128
BF16_ROWS = 16
VMEM_LIMIT = 56 * 1024 * 1024

_C_QKV = 0
_C_Z = _C_QKV + GDN_CONV_DIM
_C_QB = _C_Z + GDN_WIDTH
_C_KVB = _C_QB + SWA_WIDTH
_C_GA = _C_KVB + 2 * SWA_KV_WIDTH
_C_GB = _C_GA + D_MODEL
_C_AB = _C_GB + D_MODEL
IN_COLS = _C_AB + LANES
PROJ_TILE = 512

ROW_TILE = 512
GDN_PREP_ROWS = 256
GDN_SCAN_ROWS = 512
GDN_SAMPLE_CHUNK = 16
GDN_SAMPLE_GROUP = 4
SWA_SAMPLE_GROUP = 8
MOE_ROWS = 256
COMBINE_ROWS = 256


def _cparams(*sem):
    return pltpu.CompilerParams(dimension_semantics=sem, vmem_limit_bytes=VMEM_LIMIT)


def _bdot(a, b):
    return jnp.dot(a.astype(BF16), b.astype(BF16), preferred_element_type=F32)


def _bdot_nt(a, b):
    return lax.dot_general(a.astype(BF16), b.astype(BF16), (((1,), (1,)), ((), ())),
                           preferred_element_type=F32)


def _bdot_tn(a, b):
    return lax.dot_general(a.astype(BF16), b.astype(BF16), (((0,), (0,)), ((), ())),
                           preferred_element_type=F32)


def _sigmoid(x):
    return 1.0 / (1.0 + jnp.exp(-x))


def _silu(x):
    return x * _sigmoid(x)


def _rms(x, gain):
    return x * lax.rsqrt(jnp.mean(x * x, axis=-1, keepdims=True) + NORM_EPS) * gain


def _iota2(shape, dim):
    return lax.broadcasted_iota(I32, shape, dim)


def _adaln_kernel(c_ref, w_ref, b_ref, o_ref):
    o_ref[...] = _bdot(_silu(c_ref[...]), w_ref[...]) + b_ref[...]


def _adaln(c_all, w_ada, b_ada):
    rows = c_all.shape[0]
    n_out = w_ada.shape[1]
    tn = D_MODEL
    return pl.pallas_call(
        _adaln_kernel,
        grid=(n_out // tn,),
        in_specs=[pl.BlockSpec((rows, D_MODEL), lambda j: (0, 0)),
                  pl.BlockSpec((D_MODEL, tn), lambda j: (0, j)),
                  pl.BlockSpec((1, tn), lambda j: (0, j))],
        out_specs=pl.BlockSpec((rows, tn), lambda j: (0, j)),
        out_shape=jax.ShapeDtypeStruct((rows, n_out), F32),
        compiler_params=_cparams("arbitrary"),
        name="adaln",
    )(c_all, w_ada, b_ada.reshape(1, n_out))


def _inproj_kernel(x_ref, g_ref, sc_ref, sh_ref, w_ref,
                   qkv_ref, z_ref, qb_ref, kvb_ref, ga_ref, gb_ref, ab_ref):
    h = (_rms(x_ref[...], g_ref[...]) * (1.0 + sc_ref[0]) + sh_ref[0]).astype(BF16)

    def fill(ref, c0, width, fn):
        step = min(PROJ_TILE, width)
        for c in range(0, width, step):
            acc = jnp.dot(h, w_ref[:, c0 + c:c0 + c + step], preferred_element_type=F32)
            ref[:, c:c + step] = fn(acc).astype(ref.dtype)

    ident = lambda v: v
    fill(qkv_ref, _C_QKV, GDN_CONV_DIM, ident)
    fill(z_ref, _C_Z, GDN_WIDTH, _silu)
    fill(qb_ref, _C_QB, SWA_WIDTH, ident)
    fill(kvb_ref, _C_KVB, 2 * SWA_KV_WIDTH, ident)
    fill(ga_ref, _C_GA, D_MODEL, _sigmoid)
    fill(gb_ref, _C_GB, D_MODEL, _sigmoid)
    fill(ab_ref, _C_AB, LANES, ident)


def _mod_spec(mod, n_tiles):
    tiles_per_mod = n_tiles // mod.shape[0]
    return pl.BlockSpec((1, mod.shape[1], D_MODEL), lambda i: (i // tiles_per_mod, 0, 0))


def _inproj(x2d, gain, sc, sh, w_prep, tm):
    rows = x2d.shape[0]
    n_tiles = rows // tm
    widths = (GDN_CONV_DIM, GDN_WIDTH, SWA_WIDTH, 2 * SWA_KV_WIDTH, D_MODEL, D_MODEL, LANES)
    dtypes = (BF16, BF16, BF16, BF16, BF16, BF16, F32)
    return pl.pallas_call(
        _inproj_kernel,
        grid=(n_tiles,),
        in_specs=[pl.BlockSpec((tm, D_MODEL), lambda i: (i, 0)),
                  pl.BlockSpec((1, D_MODEL), lambda i: (0, 0)),
                  _mod_spec(sc, n_tiles), _mod_spec(sh, n_tiles),
                  pl.BlockSpec((D_MODEL, IN_COLS), lambda i: (0, 0))],
        out_specs=[pl.BlockSpec((tm, w), lambda i: (i, 0)) for w in widths],
        out_shape=[jax.ShapeDtypeStruct((rows, w), dt) for w, dt in zip(widths, dtypes)],
        compiler_params=_cparams("arbitrary"),
        name="inproj",
    )(x2d, gain.reshape(1, D_MODEL), sc, sh, w_prep)


def _cumsum_rows(g):
    c = g.shape[0]
    tril = (_iota2((c, c), 0) >= _iota2((c, c), 1)).astype(BF16)
    hi = g.astype(BF16)
    r1 = g - hi.astype(F32)
    mid = r1.astype(BF16)
    lo = (r1 - mid.astype(F32)).astype(BF16)
    dot = lambda p: jnp.dot(tril, p, preferred_element_type=F32)
    return dot(hi) + dot(mid) + dot(lo)


def _each(fn, *lists):
    return [fn(*args) for args in zip(*lists)]


def _unit_lower_inverse_offset(a_list):
    c = a_list[0].shape[0]
    ii = _iota2((c, c), 0)
    jj = _iota2((c, c), 1)

    def same_block(shift):
        return lax.shift_right_logical(ii, shift) == lax.shift_right_logical(jj, shift)

    base = same_block(1)
    n_list = _each(lambda a: jnp.where(base, -a, 0.0), a_list)
    shift = 1
    while (1 << shift) < c:
        outer, inner = same_block(shift + 1), same_block(shift)
        off_list = _each(lambda a: jnp.where(outer, jnp.where(inner, 0.0, a), 0.0), a_list)
        x_list = _each(lambda off, n: off + _bdot(off, n), off_list, n_list)
        n_list = _each(lambda n, x: n - x - _bdot(n, x), n_list, x_list)
        shift += 1
    return n_list


def _chunk_prep(q, k, v, gcol, grow, bcol):
    c = q[0].shape[0]
    ii = _iota2((c, c), 0)
    jj = _iota2((c, c), 1)
    causal = ii >= jj
    strict = ii > jj
    decay = _each(lambda gc, gr: jnp.where(causal, jnp.exp(jnp.where(causal, gc - gr, 0.0)), 0.0), gcol, grow)
    kb = _each(lambda kk, b: kk * b, k, bcol)
    both = _each(lambda qq, kbb, kk: _bdot_nt(jnp.concatenate([qq, kbb], axis=0), kk), q, kb, k)
    qk = _each(lambda bo, d: bo[:c] * d, both, decay)
    a = _each(lambda bo, d: jnp.where(strict, bo[c:] * d, 0.0), both, decay)
    n = _unit_lower_inverse_offset(a)
    eg = _each(jnp.exp, gcol)
    rhs = _each(lambda vv, b, kbb, e: jnp.concatenate([vv * b, kbb * e], axis=1), v, bcol, kb, eg)
    uw = _each(lambda r, nn: r + _bdot(nn, r), rhs, n)
    u = [x[:, :GDN_DV] for x in uw]
    w = [x[:, GDN_DV:] for x in uw]
    qd = _each(lambda qq, e: qq * e, q, eg)
    kd = _each(lambda kk, gc: kk * jnp.exp(gc[c - 1:c, :] - gc), k, gcol)
    return u, w, qd, kd, qk


def _chunk_step(s, u, w, qd, kd, qk, ge):
    c = u[0].shape[0]
    both = _each(lambda ww, qq, ss: _bdot(jnp.concatenate([ww, qq], axis=0), ss), w, qd, s)
    v_new = _each(lambda uu, bo: uu.astype(F32) - bo[:c], u, both)
    o = _each(lambda bo, m, vn: bo[c:] + _bdot(m, vn), both, qk, v_new)
    s_new = _each(lambda ss, g, kk, vn: ss * g + _bdot_tn(kk, vn), s, ge, kd, v_new)
    return o, s_new


def _conv_act(xp_ref, cw_ref, r0, rows, c0):
    cols = slice(c0, c0 + LANES)
    acc = cw_ref[3:4, cols] * xp_ref[r0:r0 + rows, cols]
    for j in range(GDN_CONV - 1):
        acc = acc + cw_ref[j:j + 1, cols] * xp_ref[r0 - 3 + j:r0 - 3 + j + rows, cols]
    return _silu(acc)


def _l2n(x):
    return x * lax.rsqrt(jnp.sum(x * x, axis=-1, keepdims=True) + NORM_EPS)


def _softplus(x):
    return jnp.maximum(x, 0.0) + jnp.log1p(jnp.exp(-jnp.abs(x)))


def _head_cols(hd):
    return (hd * GDN_DK, GDN_QK_WIDTH + hd * GDN_DK, 2 * GDN_QK_WIDTH + hd * GDN_DV)


def _activate_qkv(xp_ref, cw_ref, act_ref, r0, rows):
    for hd in range(GDN_HEADS):
        cq, ck, cv = _head_cols(hd)
        act_ref[0:rows, cq:cq + LANES] = _l2n(_conv_act(xp_ref, cw_ref, r0, rows, cq)) * (GDN_DK ** -0.5)
        act_ref[0:rows, ck:ck + LANES] = _l2n(_conv_act(xp_ref, cw_ref, r0, rows, ck))
        act_ref[0:rows, cv:cv + LANES] = _conv_act(xp_ref, cw_ref, r0, rows, cv)


def _decay_beta(ab, hp_ref):
    g = -jnp.exp(hp_ref[0:1, :]) * _softplus(ab + hp_ref[1:2, :])
    return g, _sigmoid(ab)


def _gdn_prep_kernel(qkv_ref, halo_ref, ab_ref, cw_ref, hp_ref,
                     u_ref, w_ref, qd_ref, kd_ref, qk_ref, ge_ref, xp_ref, act_ref):
    tb = qkv_ref.shape[1]
    cc = GDN_CHUNK
    first = pl.program_id(1) == 0
    xp_ref[0:BF16_ROWS, :] = jnp.where(first, 0.0, halo_ref[0].astype(F32))
    xp_ref[BF16_ROWS:BF16_ROWS + tb, :] = qkv_ref[0].astype(F32)
    _activate_qkv(xp_ref, cw_ref, act_ref, BF16_ROWS, tb)
    g_all, beta_all = _decay_beta(ab_ref[0], hp_ref)

    chunks = [slice(ci * cc, (ci + 1) * cc) for ci in range(tb // cc)]
    gcs = _each(lambda rows: _cumsum_rows(g_all[rows, :]), chunks)
    gcts = _each(lambda gc: gc.T, gcs)
    for ci, gc in enumerate(gcs):
        ge_ref[0, ci] = jnp.exp(gc[cc - 1:cc, :])
    items = [(ci, hd) for ci in range(len(chunks)) for hd in range(GDN_HEADS)]
    col = lambda which: [act_ref[chunks[ci], _head_cols(hd)[which]:_head_cols(hd)[which] + LANES]
                         for ci, hd in items]
    u, w, qd, kd, qk = _chunk_prep(
        col(0), col(1), col(2),
        [gcs[ci][:, hd:hd + 1] for ci, hd in items], [gcts[ci][hd:hd + 1, :] for ci, hd in items],
        [beta_all[chunks[ci], GDN_HEADS + hd:GDN_HEADS + hd + 1] for ci, hd in items])
    for idx, (ci, hd) in enumerate(items):
        rows = chunks[ci]
        oc = slice(hd * GDN_DV, (hd + 1) * GDN_DV)
        u_ref[0, rows, oc] = u[idx].astype(BF16)
        w_ref[0, rows, oc] = w[idx].astype(BF16)
        qd_ref[0, rows, oc] = qd[idx].astype(BF16)
        kd_ref[0, rows, oc] = kd[idx].astype(BF16)
        qk_ref[0, rows, hd * cc:(hd + 1) * cc] = qk[idx].astype(BF16)


def _gdn_prep(qkv, ab, conv_w, head_params):
    b, t, _ = qkv.shape
    tb = min(GDN_PREP_ROWS, t)
    nch = tb // GDN_CHUNK
    halo_per_block = tb // BF16_ROWS
    blk = lambda w: pl.BlockSpec((1, tb, w), lambda bi, i: (bi, i, 0))
    out_shapes = [jax.ShapeDtypeStruct((b, t, GDN_WIDTH), BF16)] * 4 + [
        jax.ShapeDtypeStruct((b, t, GDN_HEADS * GDN_CHUNK), BF16),
        jax.ShapeDtypeStruct((b, t // GDN_CHUNK, 1, LANES), F32)]
    return pl.pallas_call(
        _gdn_prep_kernel,
        grid=(b, t // tb),
        in_specs=[blk(GDN_CONV_DIM),
                  pl.BlockSpec((1, BF16_ROWS, GDN_CONV_DIM),
                               lambda bi, i: (bi, jnp.maximum(i * halo_per_block - 1, 0), 0)),
                  blk(LANES),
                  pl.BlockSpec((GDN_CONV, GDN_CONV_DIM), lambda bi, i: (0, 0)),
                  pl.BlockSpec((8, LANES), lambda bi, i: (0, 0))],
        out_specs=[blk(GDN_WIDTH)] * 4 + [
            blk(GDN_HEADS * GDN_CHUNK),
            pl.BlockSpec((1, nch, 1, LANES), lambda bi, i: (bi, i, 0, 0))],
        out_shape=out_shapes,
        scratch_shapes=[pltpu.VMEM((tb + BF16_ROWS, GDN_CONV_DIM), F32),
                        pltpu.VMEM((tb, GDN_CONV_DIM), F32)],
        compiler_params=_cparams("parallel", "parallel"),
        name="gdn_prep",
    )(qkv, qkv, ab, conv_w, head_params)


def _gated_norm_store(o_ref, idx, o, gain, zs):
    o_ref[idx] = (_rms(o, gain) * zs.astype(F32)).astype(o_ref.dtype)


def _gdn_scan_kernel(u_ref, w_ref, qd_ref, kd_ref, qk_ref, ge_ref, zs_ref, gain_ref,
                     o_ref, s_out_ref, s_ref):
    nb, tb, _ = u_ref.shape
    cc = GDN_CHUNK
    step = pl.program_id(0)

    @pl.when(step == 0)
    def _():
        s_ref[...] = jnp.zeros_like(s_ref)

    gain = gain_ref[...]

    def chunk_body(ci, carry):
        rows = pl.ds(pl.multiple_of(ci * cc, cc), cc)
        items = [(bi, hd) for bi in range(nb) for hd in range(GDN_HEADS)]
        oc = lambda hd: slice(hd * GDN_DV, (hd + 1) * GDN_DV)
        ge_rows = [ge_ref[bi, ci] for bi in range(nb)]
        o, s_new = _chunk_step(
            [s_ref[bi * GDN_HEADS + hd] for bi, hd in items],
            [u_ref[bi, rows, oc(hd)] for bi, hd in items], [w_ref[bi, rows, oc(hd)] for bi, hd in items],
            [qd_ref[bi, rows, oc(hd)] for bi, hd in items], [kd_ref[bi, rows, oc(hd)] for bi, hd in items],
            [qk_ref[bi, rows, hd * cc:(hd + 1) * cc] for bi, hd in items],
            [ge_rows[bi][:, hd:hd + 1] for bi, hd in items])
        for idx, (bi, hd) in enumerate(items):
            s_ref[bi * GDN_HEADS + hd] = s_new[idx]
            _gated_norm_store(o_ref, (bi, rows, oc(hd)), o[idx], gain, zs_ref[bi, rows, oc(hd)])
        return carry

    lax.fori_loop(0, tb // cc, chunk_body, 0)

    @pl.when(step == pl.num_programs(0) - 1)
    def _():
        s_out_ref[...] = s_ref[...]


def _gdn_scan(u, w, qd, kd, qk, ge, zs, gain):
    b, t, _ = u.shape
    tb = min(GDN_SCAN_ROWS, t)
    nch = tb // GDN_CHUNK
    blk = lambda wd: pl.BlockSpec((b, tb, wd), lambda i: (0, i, 0))
    o, s = pl.pallas_call(
        _gdn_scan_kernel,
        grid=(t // tb,),
        in_specs=[blk(GDN_WIDTH)] * 4 + [
            blk(GDN_HEADS * GDN_CHUNK),
            pl.BlockSpec((b, nch, 1, LANES), lambda i: (0, i, 0, 0)),
            blk(GDN_WIDTH),
            pl.BlockSpec((1, GDN_DV), lambda i: (0, 0))],
        out_specs=[blk(GDN_WIDTH),
                   pl.BlockSpec((b * GDN_HEADS, GDN_DK, GDN_DV), lambda i: (0, 0, 0))],
        out_shape=[jax.ShapeDtypeStruct((b, t, GDN_WIDTH), BF16),
                   jax.ShapeDtypeStruct((b * GDN_HEADS, GDN_DK, GDN_DV), F32)],
        scratch_shapes=[pltpu.VMEM((b * GDN_HEADS, GDN_DK, GDN_DV), F32)],
        compiler_params=_cparams("arbitrary"),
        name="gdn_scan",
    )(u, w, qd, kd, qk, ge, zs, gain.reshape(1, GDN_DV))
    return o, s.reshape(b, GDN_HEADS, GDN_DK, GDN_DV)


def _gdn_sample_kernel(new_rows, xp_ref, ab_ref, zs_ref, s0_ref, cw_ref, hp_ref, gain_ref,
                       o_ref, s_out_ref, xs_ref, act_ref):
    grp = xp_ref.shape[0]
    cc = GDN_SAMPLE_CHUNK
    gain = gain_ref[...]
    rowmask = (_iota2((cc, 1), 0) >= cc - new_rows).astype(F32)
    seqs = list(range(grp))
    for bi in seqs:
        xs = xs_ref.at[bi]
        xs[0:8, :] = jnp.zeros((8, GDN_CONV_DIM), F32)
        xs[8:8 + cc, :] = xp_ref[bi].astype(F32)
        _activate_qkv(xs, cw_ref, act_ref.at[bi], 8, cc)
    gb = _each(lambda bi: _decay_beta(ab_ref[bi], hp_ref), seqs)
    gcs = _each(lambda x: _cumsum_rows(x[0] * rowmask), gb)
    gcts = _each(lambda gc: gc.T, gcs)
    ge_rows = _each(lambda gc: jnp.exp(gc[cc - 1:cc, :]), gcs)
    betas = _each(lambda x: x[1] * rowmask, gb)
    items = [(bi, hd) for bi in seqs for hd in range(GDN_HEADS)]
    col = lambda which: [act_ref[bi, :, _head_cols(hd)[which]:_head_cols(hd)[which] + LANES] * rowmask
                         for bi, hd in items]
    u, w, qd, kd, qk = _chunk_prep(
        col(0), col(1), col(2),
        [gcs[bi][:, hd:hd + 1] for bi, hd in items], [gcts[bi][hd:hd + 1, :] for bi, hd in items],
        [betas[bi][:, GDN_HEADS + hd:GDN_HEADS + hd + 1] for bi, hd in items])
    o, s_new = _chunk_step([s0_ref[bi, hd] for bi, hd in items], u, w, qd, kd, qk,
                           [ge_rows[bi][:, hd:hd + 1] for bi, hd in items])
    for idx, (bi, hd) in enumerate(items):
        s_out_ref[bi, hd] = s_new[idx]
        oc = slice(hd * GDN_DV, (hd + 1) * GDN_DV)
        _gated_norm_store(o_ref, (bi, slice(None), oc), o[idx], gain, zs_ref[bi, :, oc])


def _gdn_sample(xp, ab, zs, s0, conv_w, head_params, gain, new_rows):
    b = xp.shape[0]
    cc = GDN_SAMPLE_CHUNK
    grp = GDN_SAMPLE_GROUP
    blk3 = lambda w: pl.BlockSpec((grp, cc, w), lambda i: (i, 0, 0))
    sblk = pl.BlockSpec((grp, GDN_HEADS, GDN_DK, GDN_DV), lambda i: (i, 0, 0, 0))
    return pl.pallas_call(
        functools.partial(_gdn_sample_kernel, new_rows),
        grid=(b // grp,),
        in_specs=[blk3(GDN_CONV_DIM), blk3(LANES), blk3(GDN_WIDTH), sblk,
                  pl.BlockSpec((GDN_CONV, GDN_CONV_DIM), lambda i: (0, 0)),
                  pl.BlockSpec((8, LANES), lambda i: (0, 0)),
                  pl.BlockSpec((1, GDN_DV), lambda i: (0, 0))],
        out_specs=[blk3(GDN_WIDTH), sblk],
        out_shape=[jax.ShapeDtypeStruct((b, cc, GDN_WIDTH), BF16),
                   jax.ShapeDtypeStruct((b, GDN_HEADS, GDN_DK, GDN_DV), F32)],
        scratch_shapes=[pltpu.VMEM((grp, cc + 8, GDN_CONV_DIM), F32),
                        pltpu.VMEM((grp, cc, GDN_CONV_DIM), F32)],
        compiler_params=_cparams("parallel"),
        name="gdn_sample",
    )(xp, ab, zs, s0, conv_w, head_params, gain.reshape(1, GDN_DV))


def _sink_attention(q, k, v, mask, sink_col):
    s = _each(lambda qq, kk: jnp.where(mask, _bdot_nt(qq, kk) * (SWA_HEAD_DIM ** -0.5), -jnp.inf), q, k)
    m = _each(lambda ss, sk: jnp.maximum(jnp.max(ss, axis=-1, keepdims=True), sk), s, sink_col)
    p = _each(lambda ss, mm: jnp.exp(ss - mm), s, m)
    denom = _each(lambda pp, sk, mm: jnp.sum(pp, axis=-1, keepdims=True) + jnp.exp(sk - mm), p, sink_col, m)
    return _each(lambda pp, vv, dd: _bdot(pp, vv) / dd, p, v, denom)


def _sink_column(sinks_ref, kv_head, rows_per_head):
    parts = [jnp.full((rows_per_head, 1), sinks_ref[kv_head * SWA_GROUP + g], F32) for g in range(SWA_GROUP)]
    return jnp.concatenate(parts, axis=0)


def _swa_prompt_kernel(sinks_ref, q_ref, kvp_ref, kvc_ref, o_ref):
    wnd = WINDOW
    blk = pl.program_id(1)
    kv = jnp.concatenate([kvp_ref[0], kvc_ref[0]], axis=0)
    rows = SWA_GROUP * wnd
    qi = _iota2((rows, 2 * wnd), 0) & (wnd - 1)
    kj = _iota2((rows, 2 * wnd), 1)
    dist = qi + wnd - kj
    first_key = jnp.where(blk > 0, 0, wnd)
    mask = (dist >= 0) & (dist <= wnd) & (kj >= first_key)
    kv_heads = list(range(SWA_KV_HEADS))
    head_cols = lambda h: slice(h * SWA_HEAD_DIM, (h + 1) * SWA_HEAD_DIM)
    q_heads = lambda hk: [hk * SWA_GROUP + g for g in range(SWA_GROUP)]
    o = _sink_attention(
        [jnp.concatenate([q_ref[0, :, head_cols(h)] for h in q_heads(hk)], axis=0) for hk in kv_heads],
        [kv[:, head_cols(hk)] for hk in kv_heads],
        [kv[:, SWA_KV_WIDTH + hk * SWA_HEAD_DIM:SWA_KV_WIDTH + (hk + 1) * SWA_HEAD_DIM] for hk in kv_heads],
        mask, [_sink_column(sinks_ref, hk, wnd) for hk in kv_heads])
    for hk in kv_heads:
        for g, h in enumerate(q_heads(hk)):
            o_ref[0, :, head_cols(h)] = o[hk][g * wnd:(g + 1) * wnd].astype(o_ref.dtype)


def _swa_prompt(q, kv, sinks):
    b, t, _ = q.shape
    return pl.pallas_call(
        _swa_prompt_kernel,
        grid=(b, t // WINDOW),
        in_specs=[pl.BlockSpec(memory_space=pltpu.SMEM),
                  pl.BlockSpec((1, WINDOW, SWA_WIDTH), lambda bi, i: (bi, i, 0)),
                  pl.BlockSpec((1, WINDOW, 2 * SWA_KV_WIDTH), lambda bi, i: (bi, jnp.maximum(i - 1, 0), 0)),
                  pl.BlockSpec((1, WINDOW, 2 * SWA_KV_WIDTH), lambda bi, i: (bi, i, 0))],
        out_specs=pl.BlockSpec((1, WINDOW, SWA_WIDTH), lambda bi, i: (bi, i, 0)),
        out_shape=jax.ShapeDtypeStruct((b, t, SWA_WIDTH), BF16),
        compiler_params=_cparams("parallel", "parallel"),
        name="swa_prompt",
    )(sinks, q, kv, kv)


def _swa_sample_kernel(sinks_ref, q_ref, kvn_ref, kc_ref, vc_ref, o_ref, ko_ref, vo_ref):
    grp, t, _ = q_ref.shape
    wnd = WINDOW
    nk = wnd + BF16_ROWS
    rows = SWA_GROUP * t
    tq = _iota2((rows, nk), 0) & (t - 1)
    kj = _iota2((rows, nk), 1)
    dist = tq + wnd - kj
    mask = (dist >= 0) & (dist <= wnd)
    zpad = jnp.zeros((BF16_ROWS - t, SWA_KV_WIDTH), F32)
    kks, vvs = [], []
    for bi in range(grp):
        kvn = kvn_ref[bi].astype(F32)
        kk = jnp.concatenate([kc_ref[bi], kvn[:, :SWA_KV_WIDTH], zpad], axis=0)
        vv = jnp.concatenate([vc_ref[bi], kvn[:, SWA_KV_WIDTH:], zpad], axis=0)
        ko_ref[bi] = kk[t:t + wnd, :]
        vo_ref[bi] = vv[t:t + wnd, :]
        kks.append(kk)
        vvs.append(vv)
    items = [(bi, hk) for bi in range(grp) for hk in range(SWA_KV_HEADS)]
    head_cols = lambda h: slice(h * SWA_HEAD_DIM, (h + 1) * SWA_HEAD_DIM)
    q_heads = lambda hk: [hk * SWA_GROUP + g for g in range(SWA_GROUP)]
    sink_cols = [_sink_column(sinks_ref, hk, t) for hk in range(SWA_KV_HEADS)]
    o = _sink_attention(
        [jnp.concatenate([q_ref[bi, :, head_cols(h)] for h in q_heads(hk)], axis=0) for bi, hk in items],
        [kks[bi][:, head_cols(hk)] for bi, hk in items], [vvs[bi][:, head_cols(hk)] for bi, hk in items],
        mask, [sink_cols[hk] for bi, hk in items])
    for idx, (bi, hk) in enumerate(items):
        for g, h in enumerate(q_heads(hk)):
            o_ref[bi, :, head_cols(h)] = o[idx][g * t:(g + 1) * t].astype(o_ref.dtype)


def _swa_sample(q, kv_new, k_cache, v_cache, sinks):
    b, t, _ = q.shape
    grp = SWA_SAMPLE_GROUP
    blk = lambda r, w: pl.BlockSpec((grp, r, w), lambda i: (i, 0, 0))
    return pl.pallas_call(
        _swa_sample_kernel,
        grid=(b // grp,),
        in_specs=[pl.BlockSpec(memory_space=pltpu.SMEM),
                  blk(t, SWA_WIDTH), blk(t, 2 * SWA_KV_WIDTH), blk(WINDOW, SWA_KV_WIDTH), blk(WINDOW, SWA_KV_WIDTH)],
        out_specs=[blk(t, SWA_WIDTH), blk(WINDOW, SWA_KV_WIDTH), blk(WINDOW, SWA_KV_WIDTH)],
        out_shape=[jax.ShapeDtypeStruct((b, t, SWA_WIDTH), BF16),
                   jax.ShapeDtypeStruct((b, WINDOW, SWA_KV_WIDTH), F32),
                   jax.ShapeDtypeStruct((b, WINDOW, SWA_KV_WIDTH), F32)],
        compiler_params=_cparams("parallel"),
        name="swa_sample",
    )(sinks, q, kv_new, k_cache, v_cache)


def _route(logits):
    lane = _iota2(logits.shape, 1).astype(F32)
    neg = -jnp.inf

    def first_argmax(vals, valid):
        v = jnp.where(valid, vals, neg)
        m = jnp.max(v, axis=-1, keepdims=True)
        idx = jnp.min(jnp.where(jnp.logical_and(valid, v == m), lane, float(LANES)), axis=-1, keepdims=True)
        return m, idx

    is_group = lane < N_GROUPS
    gmax, gidx = first_argmax(logits, is_group)
    p_group = 1.0 / jnp.sum(jnp.where(is_group, jnp.exp(logits - gmax), 0.0), axis=-1, keepdims=True)
    lo = N_GROUPS + gidx * EXPERTS_PER_GROUP
    in_group = jnp.logical_and(lane >= lo, lane < lo + EXPERTS_PER_GROUP)
    m1, i1 = first_argmax(logits, in_group)
    esum = jnp.sum(jnp.where(in_group, jnp.exp(logits - m1), 0.0), axis=-1, keepdims=True)
    m2, i2 = first_argmax(logits, jnp.logical_and(in_group, lane != i1))
    p1 = 1.0 / esum
    p2 = jnp.exp(m2 - m1) / esum
    tot = p1 + p2
    return i1 - N_GROUPS, i2 - N_GROUPS, p_group * p1 / tot, p_group * p2 / tot


def _post_mixer_kernel(oa_ref, ob_ref, ga_ref, gb_ref, x_ref, gt_ref, sc_ref, sh_ref,
                       wa_ref, wb_ref, wo_ref, gpost_ref, gpre_ref, wr_ref, br_ref,
                       x1_ref, h2_ref, rt_ref):
    merged = (ga_ref[...].astype(F32) * jnp.dot(oa_ref[...], wa_ref[...], preferred_element_type=F32)
              + gb_ref[...].astype(F32) * jnp.dot(ob_ref[...], wb_ref[...], preferred_element_type=F32))
    mix = _bdot(merged, wo_ref[...])
    x1 = x_ref[...] + gt_ref[0] * _rms(mix, gpost_ref[...])
    x1_ref[...] = x1
    h2 = _rms(x1, gpre_ref[...]) * (1.0 + sc_ref[0]) + sh_ref[0]
    _rows_to_tiles(h2_ref, h2)
    h_hi = h2.astype(BF16)
    h_lo = (h2 - h_hi.astype(F32)).astype(BF16)
    part = jnp.dot(h_hi, wr_ref[...], preferred_element_type=F32)
    logits = (part[:, :LANES] + part[:, LANES:]
              + jnp.dot(h_lo, wr_ref[:, :LANES], preferred_element_type=F32) + br_ref[...])
    ia, ib, wa, wb = _route(logits)
    lane = _iota2(logits.shape, 1)
    rt_ref[...] = jnp.where(lane == 0, ia, jnp.where(lane == 1, ib, jnp.where(lane == 2, wa,
                            jnp.where(lane == 3, wb, 0.0))))


def _post_mixer(oa, ob, ga, gb, x2d, gt, sc, sh, w_a, w_b, w_o, g_post, g_pre, w_rt, b_rt, tm):
    rows = x2d.shape[0]
    n_tiles = rows // tm
    row_blk = lambda w: pl.BlockSpec((tm, w), lambda i: (i, 0))
    full = lambda r, c: pl.BlockSpec((r, c), lambda i: (0, 0))
    return pl.pallas_call(
        _post_mixer_kernel,
        grid=(n_tiles,),
        in_specs=[row_blk(GDN_WIDTH), row_blk(SWA_WIDTH), row_blk(D_MODEL), row_blk(D_MODEL), row_blk(D_MODEL),
                  _mod_spec(gt, n_tiles), _mod_spec(sc, n_tiles), _mod_spec(sh, n_tiles),
                  full(GDN_WIDTH, D_MODEL), full(SWA_WIDTH, D_MODEL), full(D_MODEL, D_MODEL),
                  full(1, D_MODEL), full(1, D_MODEL), full(D_MODEL, 2 * LANES), full(1, LANES)],
        out_specs=[row_blk(D_MODEL), pl.BlockSpec((tm * TILE_ROWS, LANES), lambda i: (i, 0)), row_blk(LANES)],
        out_shape=[jax.ShapeDtypeStruct((rows, D_MODEL), F32),
                   jax.ShapeDtypeStruct((rows * TILE_ROWS, LANES), F32),
                   jax.ShapeDtypeStruct((rows, LANES), F32)],
        compiler_params=_cparams("arbitrary"),
        name="post_mixer",
    )(oa, ob, ga, gb, x2d, gt, sc, sh, w_a, w_b, w_o,
      g_post.reshape(1, D_MODEL), g_pre.reshape(1, D_MODEL), w_rt, b_rt)


TILE_ROWS = D_MODEL // LANES


def _tiles_to_rows(ref, first, rows):
    base = first * TILE_ROWS
    return jnp.concatenate([ref[pl.ds(base + c, rows, stride=TILE_ROWS), :] for c in range(TILE_ROWS)], axis=1)


def _rows_to_tiles(ref, mat):
    rows = mat.shape[0]
    for c in range(TILE_ROWS):
        ref[pl.ds(c, rows, stride=TILE_ROWS), :] = mat[:, c * LANES:(c + 1) * LANES]


def _tile_copy_loop(n, copies, start):
    def body(t, carry):
        for cp in copies(t):
            if start:
                cp.start()
            else:
                cp.wait()
        return carry

    lax.fori_loop(0, n, body, 0, unroll=8)


def _dispatch_kernel(dest_ref, h_ref, xs_in_hbm, xs_hbm, sem):
    del xs_in_hbm
    i = pl.program_id(0)
    tm = h_ref.shape[0] // TILE_ROWS

    def copies(t):
        src = h_ref.at[pl.ds(pl.multiple_of(t * TILE_ROWS, TILE_ROWS), TILE_ROWS)]
        base = (i * tm + t) * TOP_K
        return [pltpu.make_async_copy(src, xs_hbm.at[dest_ref[base + k]], sem) for k in range(TOP_K)]

    _tile_copy_loop(tm, copies, True)
    _tile_copy_loop(tm, copies, False)


def _dispatch(dest, h_tiles, xs, tm):
    n_tiles = h_tiles.shape[0] // (tm * TILE_ROWS)
    return pl.pallas_call(
        _dispatch_kernel,
        grid_spec=pltpu.PrefetchScalarGridSpec(
            num_scalar_prefetch=1,
            grid=(n_tiles,),
            in_specs=[pl.BlockSpec((tm * TILE_ROWS, LANES), lambda i, d: (i, 0)),
                      pl.BlockSpec(memory_space=pl.ANY)],
            out_specs=pl.BlockSpec(memory_space=pl.ANY),
            scratch_shapes=[pltpu.SemaphoreType.DMA(())]),
        out_shape=jax.ShapeDtypeStruct(xs.shape, xs.dtype),
        input_output_aliases={2: 0},
        compiler_params=_cparams("arbitrary"),
        name="moe_dispatch",
    )(dest, h_tiles, xs)


def _moe_kernel(blk_e_ref, n_used_ref, x_ref, wg_ref, wu_ref, wd_ref, y_ref, wgb, wub, wdb):
    b = pl.program_id(0)
    rows = x_ref.shape[0] // TILE_ROWS
    changed = jnp.logical_or(b == 0, blk_e_ref[b] != blk_e_ref[jnp.maximum(b - 1, 0)])

    @pl.when(changed)
    def _():
        wgb[...] = wg_ref[0].astype(BF16)
        wub[...] = wu_ref[0].astype(BF16)
        wdb[...] = wd_ref[0].astype(BF16)

    @pl.when(b < n_used_ref[0])
    def _():
        x = _tiles_to_rows(x_ref, 0, rows).astype(BF16)
        gate = jnp.dot(x, wgb[...], preferred_element_type=F32)
        up = jnp.dot(x, wub[...], preferred_element_type=F32)
        _rows_to_tiles(y_ref, _bdot(_silu(gate) * up, wdb[...]))

    @pl.when(b >= n_used_ref[0])
    def _():
        y_ref[...] = jnp.zeros_like(y_ref)


def _moe(xs_tiles, blk_e, n_used, w_gate, w_up, w_down):
    n_blocks = blk_e.shape[0]
    rows = MOE_ROWS
    wspec = lambda r, c: pl.BlockSpec((1, r, c), lambda b, be, nu: (be[b], 0, 0))
    xspec = pl.BlockSpec((rows * TILE_ROWS, LANES), lambda b, be, nu: (b, 0))
    return pl.pallas_call(
        _moe_kernel,
        grid_spec=pltpu.PrefetchScalarGridSpec(
            num_scalar_prefetch=2,
            grid=(n_blocks,),
            in_specs=[xspec, wspec(D_MODEL, EXPERT_FF), wspec(D_MODEL, EXPERT_FF), wspec(EXPERT_FF, D_MODEL)],
            out_specs=xspec,
            scratch_shapes=[pltpu.VMEM((D_MODEL, EXPERT_FF), BF16),
                            pltpu.VMEM((D_MODEL, EXPERT_FF), BF16),
                            pltpu.VMEM((EXPERT_FF, D_MODEL), BF16)]),
        out_shape=jax.ShapeDtypeStruct(xs_tiles.shape, F32),
        compiler_params=_cparams("arbitrary"),
        name="moe_experts",
    )(blk_e, n_used, xs_tiles, w_gate, w_up, w_down)


def _combine_kernel(dest_ref, y_hbm, x1_ref, rt_ref, gt_ref, gpost_ref, o_ref, ybuf, sems):
    i = pl.program_id(0)
    n = pl.num_programs(0)
    rows = ybuf.shape[1] // TILE_ROWS
    slot = i % 2

    def gather(step, buf_slot, start):
        def copies(r):
            dst = ybuf.at[buf_slot, pl.ds(pl.multiple_of(r * TILE_ROWS, TILE_ROWS), TILE_ROWS)]
            return [pltpu.make_async_copy(y_hbm.at[dest_ref[step * rows + r]], dst, sems.at[buf_slot])]

        _tile_copy_loop(rows, copies, start)

    @pl.when(i == 0)
    def _():
        gather(0, 0, True)

    @pl.when(i + 1 < n)
    def _():
        gather(i + 1, 1 - slot, True)

    gather(i, slot, False)
    half = rows // 2
    rt = rt_ref[...]
    buf = ybuf.at[slot]
    f = rt[:, 2:3] * _tiles_to_rows(buf, 0, half) + rt[:, 3:4] * _tiles_to_rows(buf, half, half)
    o_ref[...] = x1_ref[...] + gt_ref[0] * _rms(f, gpost_ref[...])


def _combine(dest, yb, x1, rt, gt, g_post, tm):
    rows = x1.shape[0]
    n_tiles = rows // tm
    tiles_per_mod = n_tiles // gt.shape[0]
    return pl.pallas_call(
        _combine_kernel,
        grid_spec=pltpu.PrefetchScalarGridSpec(
            num_scalar_prefetch=1,
            grid=(n_tiles,),
            in_specs=[pl.BlockSpec(memory_space=pl.ANY),
                      pl.BlockSpec((tm, D_MODEL), lambda i, d: (i, 0)),
                      pl.BlockSpec((tm, LANES), lambda i, d: (i, 0)),
                      pl.BlockSpec((1, gt.shape[1], D_MODEL), lambda i, d: (i // tiles_per_mod, 0, 0)),
                      pl.BlockSpec((1, D_MODEL), lambda i, d: (0, 0))],
            out_specs=pl.BlockSpec((tm, D_MODEL), lambda i, d: (i, 0)),
            scratch_shapes=[pltpu.VMEM((2, TOP_K * tm * TILE_ROWS, LANES), F32),
                            pltpu.SemaphoreType.DMA((2,))]),
        out_shape=jax.ShapeDtypeStruct((rows, D_MODEL), F32),
        compiler_params=_cparams("arbitrary"),
        name="moe_combine",
    )(dest, yb, x1, rt, gt, g_post.reshape(1, D_MODEL))


def _dispatch_plan(ids, n_tok, tm):
    n_assign = n_tok * TOP_K
    flat_e = ids.reshape(n_assign)
    onehot = (flat_e[:, None] == jnp.arange(N_EXPERTS, dtype=I32)[None, :]).astype(I32)
    csum = jnp.cumsum(onehot, axis=0)
    rank = jnp.take_along_axis(csum, flat_e[:, None], axis=1)[:, 0] - 1
    counts = csum[-1]
    pcounts = (counts + MOE_ROWS - 1) // MOE_ROWS * MOE_ROWS
    pends = jnp.cumsum(pcounts)
    pstarts = pends - pcounts
    dest = pstarts[flat_e] + rank
    n_blocks = n_assign // MOE_ROWS + N_EXPERTS
    blk_start = jnp.arange(n_blocks, dtype=I32) * MOE_ROWS
    blk_e = jnp.minimum(jnp.sum(blk_start[:, None] >= pends[None, :], axis=1), N_EXPERTS - 1).astype(I32)
    n_used = (pends[-1] // MOE_ROWS).astype(I32).reshape(1)
    dest_tiles = dest.reshape(n_tok // tm, tm, TOP_K).transpose(0, 2, 1).reshape(-1)
    return blk_e, n_used, dest, dest_tiles


def _prep_in_weight(w_in):
    a0 = 4 * GDN_QK_WIDTH
    a1 = a0 + 2 * GDN_HEADS
    pad = jnp.zeros((D_MODEL, LANES - 2 * GDN_HEADS), w_in.dtype)
    return jnp.concatenate([w_in[:, :a0], w_in[:, a1:], w_in[:, a0:a1], pad], axis=1).astype(BF16)


def _head_param_tile(a_log, dt_bias):
    tile = jnp.zeros((8, LANES), F32)
    return tile.at[0, :GDN_HEADS].set(a_log.astype(F32)).at[1, :GDN_HEADS].set(dt_bias.astype(F32))


def _router_weight(w_group, b_group, w_router, b_router):
    w = jnp.zeros((D_MODEL, LANES), F32)
    w = w.at[:, :N_GROUPS].set(w_group).at[:, N_GROUPS:N_GROUPS + N_EXPERTS].set(w_router)
    b = jnp.zeros((1, LANES), F32)
    b = b.at[0, :N_GROUPS].set(b_group).at[0, N_GROUPS:N_GROUPS + N_EXPERTS].set(b_router)
    w_hi = w.astype(BF16)
    w_lo = (w - w_hi.astype(F32)).astype(BF16)
    return jnp.concatenate([w_hi, w_lo], axis=1), b


def kernel(x_prompt, x_sample, state_gdn, state_conv, cache_k_win, cache_v_win, c_prompt, c_sample, w_ada, b_ada, g_mix_pre, g_mix_post, g_ffn_pre, g_ffn_post, w_in, conv_w, a_log, dt_bias, gdn_norm, sinks, w_br_gdn, w_br_swa, w_out, w_group, b_group, w_router, b_router, w_gate, w_up, w_down):
    depth = w_ada.shape[0]
    assert depth == 1, "single-layer trunk"
    bp, tp, _ = x_prompt.shape
    bs, ts, _ = x_sample.shape
    n_p = bp * tp
    n_s = bs * ts
    tm = ROW_TILE
    assert tp % tm == 0 and n_s % tm == 0 and ts >= GDN_CONV - 1 and ts + GDN_CONV - 1 <= GDN_SAMPLE_CHUNK
    assert ts & (ts - 1) == 0 and ts <= BF16_ROWS

    c_all = jnp.concatenate([c_prompt, c_sample], axis=0)
    c_rows = -(-c_all.shape[0] // 8) * 8
    c_all = jnp.pad(c_all, ((0, c_rows - c_all.shape[0]), (0, 0)))
    mod = _adaln(c_all, w_ada[0], b_ada[0])
    mods_p = [m[:bp].reshape(bp, 1, D_MODEL) for m in jnp.split(mod, 6, axis=-1)]
    mods_s = [jnp.repeat(m[bp:bp + bs], ts, axis=0).reshape(n_s // tm, tm, D_MODEL)
              for m in jnp.split(mod, 6, axis=-1)]

    w_prep = _prep_in_weight(w_in[0])
    head_params = _head_param_tile(a_log[0], dt_bias[0])
    w_a, w_b, w_o = w_br_gdn[0].astype(BF16), w_br_swa[0].astype(BF16), w_out[0].astype(BF16)
    w_rt, b_rt = _router_weight(w_group[0], b_group[0], w_router[0], b_router[0])
    sinks0 = sinks[0].astype(F32)

    xp2d = x_prompt.reshape(n_p, D_MODEL)
    sh1, sc1, gt1, sh2, sc2, gt2 = mods_p
    qkv_p, zs_p, qb_p, kvb_p, ga_p, gb_p, ab_p = _inproj(xp2d, g_mix_pre[0], sc1, sh1, w_prep, tm)
    qkv_p3 = qkv_p.reshape(bp, tp, GDN_CONV_DIM)
    u, w, qd, kd, qk, ge = _gdn_prep(qkv_p3, ab_p.reshape(bp, tp, LANES), conv_w[0], head_params)
    oa_p, s_prompt = _gdn_scan(u, w, qd, kd, qk, ge, zs_p.reshape(bp, tp, GDN_WIDTH), gdn_norm[0])
    kvb_p3 = kvb_p.reshape(bp, tp, 2 * SWA_KV_WIDTH)
    ob_p = _swa_prompt(qb_p.reshape(bp, tp, SWA_WIDTH), kvb_p3, sinks0)
    x1_p, h2_p, rt_p = _post_mixer(oa_p.reshape(n_p, GDN_WIDTH), ob_p.reshape(n_p, SWA_WIDTH), ga_p, gb_p, xp2d,
                                   gt1, sc2, sh2, w_a, w_b, w_o, g_mix_post[0], g_ffn_pre[0], w_rt, b_rt, tm)

    xs2d = x_sample.reshape(n_s, D_MODEL)
    sh1s, sc1s, gt1s, sh2s, sc2s, gt2s = mods_s
    qkv_s, zs_s, qb_s, kvb_s, ga_s, gb_s, ab_s = _inproj(xs2d, g_mix_pre[0], sc1s, sh1s, w_prep, tm)
    cc = GDN_SAMPLE_CHUNK
    pad_rows = cc - ts - (GDN_CONV - 1)
    qkv_s3 = qkv_s.reshape(bs, ts, GDN_CONV_DIM)
    xp_s = jnp.concatenate([jnp.zeros((bs, pad_rows, GDN_CONV_DIM), BF16), state_conv[0].astype(BF16), qkv_s3],
                           axis=1)
    front = lambda a: jnp.pad(a, ((0, 0), (cc - ts, 0), (0, 0)))
    oa_s16, s_sample = _gdn_sample(xp_s, front(ab_s.reshape(bs, ts, LANES)), front(zs_s.reshape(bs, ts, GDN_WIDTH)),
                                   state_gdn[0].astype(F32), conv_w[0], head_params, gdn_norm[0], ts)
    oa_s = oa_s16[:, cc - ts:, :].reshape(n_s, GDN_WIDTH)
    ob_s, k_new_s, v_new_s = _swa_sample(
        qb_s.reshape(bs, ts, SWA_WIDTH), kvb_s.reshape(bs, ts, 2 * SWA_KV_WIDTH),
        cache_k_win[0].reshape(bs, WINDOW, SWA_KV_WIDTH).astype(F32),
        cache_v_win[0].reshape(bs, WINDOW, SWA_KV_WIDTH).astype(F32), sinks0)
    x1_s, h2_s, rt_s = _post_mixer(oa_s, ob_s.reshape(n_s, SWA_WIDTH), ga_s, gb_s, xs2d,
                                   gt1s, sc2s, sh2s, w_a, w_b, w_o, g_mix_post[0], g_ffn_pre[0], w_rt, b_rt, tm)

    n_all = n_p + n_s
    ids = jnp.concatenate([rt_p[:, :TOP_K], rt_s[:, :TOP_K]], axis=0).astype(I32)
    ct = COMBINE_ROWS
    blk_e, n_used, dest, dest_tiles = _dispatch_plan(ids, n_all, ct)
    n_slots = blk_e.shape[0] * MOE_ROWS
    xs = jnp.zeros((n_slots, TILE_ROWS, LANES), F32)
    xs = _dispatch(dest[:n_p * TOP_K], h2_p, xs, tm)
    xs = _dispatch(dest[n_p * TOP_K:], h2_s, xs, tm)
    yb = _moe(xs.reshape(n_slots * TILE_ROWS, LANES), blk_e, n_used, w_gate[0], w_up[0], w_down[0])
    yb = yb.reshape(n_slots, TILE_ROWS, LANES)
    y_p = _combine(dest_tiles[:n_p * TOP_K], yb, x1_p, rt_p, gt2, g_ffn_post[0], ct)
    y_s = _combine(dest_tiles[n_p * TOP_K:], yb, x1_s, rt_s,
                   gt2s.reshape(n_s // ct, ct, D_MODEL), g_ffn_post[0], ct)

    f32 = lambda a: a.astype(F32)
    kv_tail = kvb_p3[:, tp - WINDOW:, :]
    kv_heads = lambda a: f32(a).reshape(a.shape[0], WINDOW, SWA_KV_HEADS, SWA_HEAD_DIM)[None]
    return (y_p.reshape(bp, tp, D_MODEL), y_s.reshape(bs, ts, D_MODEL),
            s_prompt[None], f32(qkv_p3[:, tp - (GDN_CONV - 1):, :])[None],
            kv_heads(kv_tail[:, :, :SWA_KV_WIDTH]), kv_heads(kv_tail[:, :, SWA_KV_WIDTH:]),
            s_sample[None], f32(qkv_s3[:, ts - (GDN_CONV - 1):, :])[None],
            kv_heads(k_new_s), kv_heads(v_new_s))
```

```python
import functools

import jax
import jax.numpy as jnp
from jax import lax
from jax.experimental import pallas as pl
from jax.experimental.pallas import tpu as pltpu

F32 = jnp.float32
BF16 = jnp.bfloat16
I32 = jnp.int32

D_MODEL = 1024
NORM_EPS = 1e-6
GDN_HEADS = 8
GDN_DK = 128
GDN_DV = 128
GDN_CONV = 4
GDN_CHUNK = 64
GDN_QK_WIDTH = GDN_HEADS * GDN_DK
GDN_WIDTH = GDN_HEADS * GDN_DV
GDN_CONV_DIM = 2 * GDN_QK_WIDTH + GDN_WIDTH
SWA_Q_HEADS = 16
SWA_KV_HEADS = 4
SWA_HEAD_DIM = 64
SWA_GROUP = SWA_Q_HEADS // SWA_KV_HEADS
SWA_WIDTH = SWA_Q_HEADS * SWA_HEAD_DIM
SWA_KV_WIDTH = SWA_KV_HEADS * SWA_HEAD_DIM
WINDOW = 128
N_GROUPS = 4
EXPERTS_PER_GROUP = 8
N_EXPERTS = N_GROUPS * EXPERTS_PER_GROUP
TOP_K = 2
EXPERT_FF = 512

LANES = 128
BF16_ROWS = 16
VMEM_LIMIT = 56 * 1024 * 1024

_C_QKV = 0
_C_Z = _C_QKV + GDN_CONV_DIM
_C_QB = _C_Z + GDN_WIDTH
_C_KVB = _C_QB + SWA_WIDTH
_C_GA = _C_KVB + 2 * SWA_KV_WIDTH
_C_GB = _C_GA + D_MODEL
_C_AB = _C_GB + D_MODEL
IN_COLS = _C_AB + LANES
PROJ_TILE = 512

ROW_TILE = 512
GDN_PREP_ROWS = 256
GDN_SCAN_ROWS = 512
GDN_SAMPLE_CHUNK = 16
GDN_SAMPLE_GROUP = 4
SWA_SAMPLE_GROUP = 8
MOE_ROWS = 256
COMBINE_ROWS = 256


def _cparams(*sem):
    return pltpu.CompilerParams(dimension_semantics=sem, vmem_limit_bytes=VMEM_LIMIT)


def _bdot(a, b):
    return jnp.dot(a.astype(BF16), b.astype(BF16), preferred_element_type=F32)


def _bdot_nt(a, b):
    return lax.dot_general(a.astype(BF16), b.astype(BF16), (((1,), (1,)), ((), ())),
                           preferred_element_type=F32)


def _bdot_tn(a, b):
    return lax.dot_general(a.astype(BF16), b.astype(BF16), (((0,), (0,)), ((), ())),
                           preferred_element_type=F32)


def _sigmoid(x):
    return 1.0 / (1.0 + jnp.exp(-x))


def _silu(x):
    return x * _sigmoid(x)


def _rms(x, gain):
    return x * lax.rsqrt(jnp.mean(x * x, axis=-1, keepdims=True) + NORM_EPS) * gain


def _iota2(shape, dim):
    return lax.broadcasted_iota(I32, shape, dim)


def _adaln_kernel(c_ref, w_ref, b_ref, o_ref):
    o_ref[...] = _bdot(_silu(c_ref[...]), w_ref[...]) + b_ref[...]


def _adaln(c_all, w_ada, b_ada):
    rows = c_all.shape[0]
    n_out = w_ada.shape[1]
    tn = D_MODEL
    return pl.pallas_call(
        _adaln_kernel,
        grid=(n_out // tn,),
        in_specs=[pl.BlockSpec((rows, D_MODEL), lambda j: (0, 0)),
                  pl.BlockSpec((D_MODEL, tn), lambda j: (0, j)),
                  pl.BlockSpec((1, tn), lambda j: (0, j))],
        out_specs=pl.BlockSpec((rows, tn), lambda j: (0, j)),
        out_shape=jax.ShapeDtypeStruct((rows, n_out), F32),
        compiler_params=_cparams("arbitrary"),
        name="adaln",
    )(c_all, w_ada, b_ada.reshape(1, n_out))


def _inproj_kernel(tiles_per_seq, x_ref, g_ref, sc_ref, sh_ref, w_ref, *rest):
    if tiles_per_seq:
        cw_ref, qkv_ref, z_ref, qb_ref, kvb_ref, ga_ref, gb_ref, ab_ref, tail_ref, carry_ref = rest
    else:
        qkv_ref, z_ref, qb_ref, kvb_ref, ga_ref, gb_ref, ab_ref = rest
    tm = x_ref.shape[0]
    h = (_rms(x_ref[...], g_ref[...]) * (1.0 + sc_ref[0]) + sh_ref[0]).astype(BF16)

    def fill(ref, c0, width, fn):
        step = min(PROJ_TILE, width)
        for c in range(0, width, step):
            acc = jnp.dot(h, w_ref[:, c0 + c:c0 + c + step], preferred_element_type=F32)
            ref[:, c:c + step] = fn(acc, c, step).astype(ref.dtype)

    def conv_act(acc, c, step):
        cols = slice(c, c + step)
        seq_start = pl.program_id(0) % tiles_per_seq == 0
        prev = jnp.where(seq_start, 0.0, carry_ref[:, cols])
        last = acc[tm - 8:tm]
        carry_ref[:, cols] = last
        tail_ref[0, :, cols] = last
        ext = jnp.concatenate([prev, acc], axis=0)
        y = cw_ref[GDN_CONV - 1:GDN_CONV, cols] * acc
        for j in range(GDN_CONV - 1):
            y = y + cw_ref[j:j + 1, cols] * ext[8 - (GDN_CONV - 1) + j:8 - (GDN_CONV - 1) + j + tm]
        y = _silu(y)
        if c >= 2 * GDN_QK_WIDTH:
            return y
        scale = GDN_DK ** -0.5 if c < GDN_QK_WIDTH else 1.0
        heads = [_l2n(y[:, d:d + GDN_DK]) * scale for d in range(0, step, GDN_DK)]
        return jnp.concatenate(heads, axis=1)

    ident = lambda v, c, step: v
    silu = lambda v, c, step: _silu(v)
    sigmoid = lambda v, c, step: _sigmoid(v)
    fill(qkv_ref, _C_QKV, GDN_CONV_DIM, conv_act if tiles_per_seq else ident)
    fill(z_ref, _C_Z, GDN_WIDTH, silu)
    fill(qb_ref, _C_QB, SWA_WIDTH, ident)
    fill(kvb_ref, _C_KVB, 2 * SWA_KV_WIDTH, ident)
    fill(ga_ref, _C_GA, D_MODEL, sigmoid)
    fill(gb_ref, _C_GB, D_MODEL, sigmoid)
    fill(ab_ref, _C_AB, LANES, ident)


def _mod_spec(mod, n_tiles):
    tiles_per_mod = n_tiles // mod.shape[0]
    return pl.BlockSpec((1, mod.shape[1], D_MODEL), lambda i: (i // tiles_per_mod, 0, 0))


def _inproj(x2d, gain, sc, sh, w_prep, tm, conv_w=None, n_seq=0):
    rows = x2d.shape[0]
    n_tiles = rows // tm
    widths = (GDN_CONV_DIM, GDN_WIDTH, SWA_WIDTH, 2 * SWA_KV_WIDTH, D_MODEL, D_MODEL, LANES)
    dtypes = (BF16, BF16, BF16, BF16, BF16, BF16, F32)
    in_specs = [pl.BlockSpec((tm, D_MODEL), lambda i: (i, 0)),
                pl.BlockSpec((1, D_MODEL), lambda i: (0, 0)),
                _mod_spec(sc, n_tiles), _mod_spec(sh, n_tiles),
                pl.BlockSpec((D_MODEL, IN_COLS), lambda i: (0, 0))]
    out_specs = [pl.BlockSpec((tm, w), lambda i: (i, 0)) for w in widths]
    out_shape = [jax.ShapeDtypeStruct((rows, w), dt) for w, dt in zip(widths, dtypes)]
    args = [x2d, gain.reshape(1, D_MODEL), sc, sh, w_prep]
    scratch = []
    tiles_per_seq = 0
    if conv_w is not None:
        tiles_per_seq = n_tiles // n_seq
        in_specs.append(pl.BlockSpec((GDN_CONV, GDN_CONV_DIM), lambda i: (0, 0)))
        out_specs.append(pl.BlockSpec((1, 8, GDN_CONV_DIM), lambda i: (i // tiles_per_seq, 0, 0)))
        out_shape.append(jax.ShapeDtypeStruct((n_seq, 8, GDN_CONV_DIM), F32))
        args.append(conv_w)
        scratch.append(pltpu.VMEM((8, GDN_CONV_DIM), F32))
    return pl.pallas_call(
        functools.partial(_inproj_kernel, tiles_per_seq),
        grid=(n_tiles,),
        in_specs=in_specs,
        out_specs=out_specs,
        out_shape=out_shape,
        scratch_shapes=scratch,
        compiler_params=_cparams("arbitrary"),
        name="inproj",
    )(*args)


def _cumsum_rows(g):
    c = g.shape[0]
    tril = (_iota2((c, c), 0) >= _iota2((c, c), 1)).astype(BF16)
    hi = g.astype(BF16)
    r1 = g - hi.astype(F32)
    mid = r1.astype(BF16)
    lo = (r1 - mid.astype(F32)).astype(BF16)
    dot = lambda p: jnp.dot(tril, p, preferred_element_type=F32)
    return dot(hi) + dot(mid) + dot(lo)


def _each(fn, *lists):
    return [fn(*args) for args in zip(*lists)]


def _unit_lower_inverse_offset(a_list):
    c = a_list[0].shape[0]
    ii = _iota2((c, c), 0)
    jj = _iota2((c, c), 1)

    def same_block(shift):
        return lax.shift_right_logical(ii, shift) == lax.shift_right_logical(jj, shift)

    base = same_block(1)
    n_list = _each(lambda a: jnp.where(base, -a, 0.0), a_list)
    shift = 1
    while (1 << shift) < c:
        outer, inner = same_block(shift + 1), same_block(shift)
        off_list = _each(lambda a: jnp.where(outer, jnp.where(inner, 0.0, a), 0.0), a_list)
        x_list = _each(lambda off, n: off + _bdot(off, n), off_list, n_list)
        n_list = _each(lambda n, x: n - x - _bdot(n, x), n_list, x_list)
        shift += 1
    return n_list


def _chunk_prep(q, k, v, gcol, grow, bcol):
    c = q[0].shape[0]
    ii = _iota2((c, c), 0)
    jj = _iota2((c, c), 1)
    causal = ii >= jj
    strict = ii > jj
    decay = _each(lambda gc, gr: jnp.where(causal, jnp.exp(jnp.where(causal, gc - gr, 0.0)), 0.0), gcol, grow)
    kb = _each(lambda kk, b: kk * b, k, bcol)
    both = _each(lambda qq, kbb, kk: _bdot_nt(jnp.concatenate([qq, kbb], axis=0), kk), q, kb, k)
    qk = _each(lambda bo, d: bo[:c] * d, both, decay)
    a = _each(lambda bo, d: jnp.where(strict, bo[c:] * d, 0.0), both, decay)
    n = _unit_lower_inverse_offset(a)
    eg = _each(jnp.exp, gcol)
    rhs = _each(lambda vv, b, kbb, e: jnp.concatenate([vv * b, kbb * e], axis=1), v, bcol, kb, eg)
    uw = _each(lambda r, nn: r + _bdot(nn, r), rhs, n)
    u = [x[:, :GDN_DV] for x in uw]
    w = [x[:, GDN_DV:] for x in uw]
    qd = _each(lambda qq, e: qq * e, q, eg)
    kd = _each(lambda kk, gc: kk * jnp.exp(gc[c - 1:c, :] - gc), k, gcol)
    return u, w, qd, kd, qk


def _chunk_step(s, u, w, qd, kd, qk, ge):
    c = u[0].shape[0]
    both = _each(lambda ww, qq, ss: _bdot(jnp.concatenate([ww, qq], axis=0), ss), w, qd, s)
    v_new = _each(lambda uu, bo: uu.astype(F32) - bo[:c], u, both)
    o = _each(lambda bo, m, vn: bo[c:] + _bdot(m, vn), both, qk, v_new)
    s_new = _each(lambda ss, g, kk, vn: ss * g + _bdot_tn(kk, vn), s, ge, kd, v_new)
    return o, s_new


def _conv_act(xp_ref, cw_ref, r0, rows, c0):
    cols = slice(c0, c0 + LANES)
    acc = cw_ref[3:4, cols] * xp_ref[r0:r0 + rows, cols]
    for j in range(GDN_CONV - 1):
        acc = acc + cw_ref[j:j + 1, cols] * xp_ref[r0 - 3 + j:r0 - 3 + j + rows, cols]
    return _silu(acc)


def _l2n(x):
    return x * lax.rsqrt(jnp.sum(x * x, axis=-1, keepdims=True) + NORM_EPS)


def _softplus(x):
    return jnp.maximum(x, 0.0) + jnp.log1p(jnp.exp(-jnp.abs(x)))


def _head_cols(hd):
    return (hd * GDN_DK, GDN_QK_WIDTH + hd * GDN_DK, 2 * GDN_QK_WIDTH + hd * GDN_DV)


def _activate_qkv(xp_ref, cw_ref, act_ref, r0, rows):
    for hd in range(GDN_HEADS):
        cq, ck, cv = _head_cols(hd)
        act_ref[0:rows, cq:cq + LANES] = _l2n(_conv_act(xp_ref, cw_ref, r0, rows, cq)) * (GDN_DK ** -0.5)
        act_ref[0:rows, ck:ck + LANES] = _l2n(_conv_act(xp_ref, cw_ref, r0, rows, ck))
        act_ref[0:rows, cv:cv + LANES] = _conv_act(xp_ref, cw_ref, r0, rows, cv)


def _decay_beta(ab, hp_ref):
    g = -jnp.exp(hp_ref[0:1, :]) * _softplus(ab + hp_ref[1:2, :])
    return g, _sigmoid(ab)


def _gdn_prep_kernel(act_ref, ab_ref, hp_ref, u_ref, w_ref, qd_ref, kd_ref, qk_ref, ge_ref):
    tb = act_ref.shape[1]
    cc = GDN_CHUNK
    g_all, beta_all = _decay_beta(ab_ref[0], hp_ref)

    chunks = [slice(ci * cc, (ci + 1) * cc) for ci in range(tb // cc)]
    gcs = _each(lambda rows: _cumsum_rows(g_all[rows, :]), chunks)
    gcts = _each(lambda gc: gc.T, gcs)
    for ci, gc in enumerate(gcs):
        ge_ref[0, ci] = jnp.exp(gc[cc - 1:cc, :])
    items = [(ci, hd) for ci in range(len(chunks)) for hd in range(GDN_HEADS)]
    col = lambda which: [act_ref[0, chunks[ci], _head_cols(hd)[which]:_head_cols(hd)[which] + LANES].astype(F32)
                         for ci, hd in items]
    u, w, qd, kd, qk = _chunk_prep(
        col(0), col(1), col(2),
        [gcs[ci][:, hd:hd + 1] for ci, hd in items], [gcts[ci][hd:hd + 1, :] for ci, hd in items],
        [beta_all[chunks[ci], GDN_HEADS + hd:GDN_HEADS + hd + 1] for ci, hd in items])
    for idx, (ci, hd) in enumerate(items):
        rows = chunks[ci]
        oc = slice(hd * GDN_DV, (hd + 1) * GDN_DV)
        u_ref[0, rows, oc] = u[idx].astype(BF16)
        w_ref[0, rows, oc] = w[idx].astype(BF16)
        qd_ref[0, rows, oc] = qd[idx].astype(BF16)
        kd_ref[0, rows, oc] = kd[idx].astype(BF16)
        qk_ref[0, rows, hd * cc:(hd + 1) * cc] = qk[idx].astype(BF16)


def _gdn_prep(qkv, ab, head_params):
    b, t, _ = qkv.shape
    tb = min(GDN_PREP_ROWS, t)
    nch = tb // GDN_CHUNK
    blk = lambda w: pl.BlockSpec((1, tb, w), lambda bi, i: (bi, i, 0))
    out_shapes = [jax.ShapeDtypeStruct((b, t, GDN_WIDTH), BF16)] * 4 + [
        jax.ShapeDtypeStruct((b, t, GDN_HEADS * GDN_CHUNK), BF16),
        jax.ShapeDtypeStruct((b, t // GDN_CHUNK, 1, LANES), F32)]
    return pl.pallas_call(
        _gdn_prep_kernel,
        grid=(b, t // tb),
        in_specs=[blk(GDN_CONV_DIM), blk(LANES), pl.BlockSpec((8, LANES), lambda bi, i: (0, 0))],
        out_specs=[blk(GDN_WIDTH)] * 4 + [
            blk(GDN_HEADS * GDN_CHUNK),
            pl.BlockSpec((1, nch, 1, LANES), lambda bi, i: (bi, i, 0, 0))],
        out_shape=out_shapes,
        compiler_params=_cparams("parallel", "parallel"),
        name="gdn_prep",
    )(qkv, ab, head_params)


def _gated_norm_store(o_ref, idx, o, gain, zs):
    o_ref[idx] = (_rms(o, gain) * zs.astype(F32)).astype(o_ref.dtype)


def _gdn_scan_kernel(u_ref, w_ref, qd_ref, kd_ref, qk_ref, ge_ref, zs_ref, gain_ref,
                     o_ref, s_out_ref, s_ref):
    nb, tb, _ = u_ref.shape
    cc = GDN_CHUNK
    step = pl.program_id(0)

    @pl.when(step == 0)
    def _():
        s_ref[...] = jnp.zeros_like(s_ref)

    gain = gain_ref[...]

    def chunk_body(ci, carry):
        rows = pl.ds(pl.multiple_of(ci * cc, cc), cc)
        items = [(bi, hd) for bi in range(nb) for hd in range(GDN_HEADS)]
        oc = lambda hd: slice(hd * GDN_DV, (hd + 1) * GDN_DV)
        ge_rows = [ge_ref[bi, ci] for bi in range(nb)]
        o, s_new = _chunk_step(
            [s_ref[bi * GDN_HEADS + hd] for bi, hd in items],
            [u_ref[bi, rows, oc(hd)] for bi, hd in items], [w_ref[bi, rows, oc(hd)] for bi, hd in items],
            [qd_ref[bi, rows, oc(hd)] for bi, hd in items], [kd_ref[bi, rows, oc(hd)] for bi, hd in items],
            [qk_ref[bi, rows, hd * cc:(hd + 1) * cc] for bi, hd in items],
            [ge_rows[bi][:, hd:hd + 1] for bi, hd in items])
        for idx, (bi, hd) in enumerate(items):
            s_ref[bi * GDN_HEADS + hd] = s_new[idx]
            _gated_norm_store(o_ref, (bi, rows, oc(hd)), o[idx], gain, zs_ref[bi, rows, oc(hd)])
        return carry

    lax.fori_loop(0, tb // cc, chunk_body, 0)

    @pl.when(step == pl.num_programs(0) - 1)
    def _():
        s_out_ref[...] = s_ref[...]


def _gdn_scan(u, w, qd, kd, qk, ge, zs, gain):
    b, t, _ = u.shape
    tb = min(GDN_SCAN_ROWS, t)
    nch = tb // GDN_CHUNK
    blk = lambda wd: pl.BlockSpec((b, tb, wd), lambda i: (0, i, 0))
    o, s = pl.pallas_call(
        _gdn_scan_kernel,
        grid=(t // tb,),
        in_specs=[blk(GDN_WIDTH)] * 4 + [
            blk(GDN_HEADS * GDN_CHUNK),
            pl.BlockSpec((b, nch, 1, LANES), lambda i: (0, i, 0, 0)),
            blk(GDN_WIDTH),
            pl.BlockSpec((1, GDN_DV), lambda i: (0, 0))],
        out_specs=[blk(GDN_WIDTH),
                   pl.BlockSpec((b * GDN_HEADS, GDN_DK, GDN_DV), lambda i: (0, 0, 0))],
        out_shape=[jax.ShapeDtypeStruct((b, t, GDN_WIDTH), BF16),
                   jax.ShapeDtypeStruct((b * GDN_HEADS, GDN_DK, GDN_DV), F32)],
        scratch_shapes=[pltpu.VMEM((b * GDN_HEADS, GDN_DK, GDN_DV), F32)],
        compiler_params=_cparams("arbitrary"),
        name="gdn_scan",
    )(u, w, qd, kd, qk, ge, zs, gain.reshape(1, GDN_DV))
    return o, s.reshape(b, GDN_HEADS, GDN_DK, GDN_DV)


def _gdn_sample_kernel(new_rows, xp_ref, ab_ref, zs_ref, s0_ref, cw_ref, hp_ref, gain_ref,
                       o_ref, s_out_ref, xs_ref, act_ref):
    grp = xp_ref.shape[0]
    cc = GDN_SAMPLE_CHUNK
    gain = gain_ref[...]
    rowmask = (_iota2((cc, 1), 0) >= cc - new_rows).astype(F32)
    seqs = list(range(grp))
    for bi in seqs:
        xs = xs_ref.at[bi]
        xs[0:8, :] = jnp.zeros((8, GDN_CONV_DIM), F32)
        xs[8:8 + cc, :] = xp_ref[bi].astype(F32)
        _activate_qkv(xs, cw_ref, act_ref.at[bi], 8, cc)
    gb = _each(lambda bi: _decay_beta(ab_ref[bi], hp_ref), seqs)
    gcs = _each(lambda x: _cumsum_rows(x[0] * rowmask), gb)
    gcts = _each(lambda gc: gc.T, gcs)
    ge_rows = _each(lambda gc: jnp.exp(gc[cc - 1:cc, :]), gcs)
    betas = _each(lambda x: x[1] * rowmask, gb)
    items = [(bi, hd) for bi in seqs for hd in range(GDN_HEADS)]
    col = lambda which: [act_ref[bi, :, _head_cols(hd)[which]:_head_cols(hd)[which] + LANES] * rowmask
                         for bi, hd in items]
    u, w, qd, kd, qk = _chunk_prep(
        col(0), col(1), col(2),
        [gcs[bi][:, hd:hd + 1] for bi, hd in items], [gcts[bi][hd:hd + 1, :] for bi, hd in items],
        [betas[bi][:, GDN_HEADS + hd:GDN_HEADS + hd + 1] for bi, hd in items])
    o, s_new = _chunk_step([s0_ref[bi, hd] for bi, hd in items], u, w, qd, kd, qk,
                           [ge_rows[bi][:, hd:hd + 1] for bi, hd in items])
    for idx, (bi, hd) in enumerate(items):
        s_out_ref[bi, hd] = s_new[idx]
        oc = slice(hd * GDN_DV, (hd + 1) * GDN_DV)
        _gated_norm_store(o_ref, (bi, slice(None), oc), o[idx], gain, zs_ref[bi, :, oc])


def _gdn_sample(xp, ab, zs, s0, conv_w, head_params, gain, new_rows):
    b = xp.shape[0]
    cc = GDN_SAMPLE_CHUNK
    grp = GDN_SAMPLE_GROUP
    blk3 = lambda w: pl.BlockSpec((grp, cc, w), lambda i: (i, 0, 0))
    sblk = pl.BlockSpec((grp, GDN_HEADS, GDN_DK, GDN_DV), lambda i: (i, 0, 0, 0))
    return pl.pallas_call(
        functools.partial(_gdn_sample_kernel, new_rows),
        grid=(b // grp,),
        in_specs=[blk3(GDN_CONV_DIM), blk3(LANES), blk3(GDN_WIDTH), sblk,
                  pl.BlockSpec((GDN_CONV, GDN_CONV_DIM), lambda i: (0, 0)),
                  pl.BlockSpec((8, LANES), lambda i: (0, 0)),
                  pl.BlockSpec((1, GDN_DV), lambda i: (0, 0))],
        out_specs=[blk3(GDN_WIDTH), sblk],
        out_shape=[jax.ShapeDtypeStruct((b, cc, GDN_WIDTH), BF16),
                   jax.ShapeDtypeStruct((b, GDN_HEADS, GDN_DK, GDN_DV), F32)],
        scratch_shapes=[pltpu.VMEM((grp, cc + 8, GDN_CONV_DIM), F32),
                        pltpu.VMEM((grp, cc, GDN_CONV_DIM), F32)],
        compiler_params=_cparams("parallel"),
        name="gdn_sample",
    )(xp, ab, zs, s0, conv_w, head_params, gain.reshape(1, GDN_DV))


def _sink_attention(q, k, v, mask, sink_col):
    s = _each(lambda qq, kk: jnp.where(mask, _bdot_nt(qq, kk) * (SWA_HEAD_DIM ** -0.5), -jnp.inf), q, k)
    m = _each(lambda ss, sk: jnp.maximum(jnp.max(ss, axis=-1, keepdims=True), sk), s, sink_col)
    p = _each(lambda ss, mm: jnp.exp(ss - mm), s, m)
    denom = _each(lambda pp, sk, mm: jnp.sum(pp, axis=-1, keepdims=True) + jnp.exp(sk - mm), p, sink_col, m)
    return _each(lambda pp, vv, dd: _bdot(pp, vv) / dd, p, v, denom)


def _sink_column(sinks_ref, kv_head, rows_per_head):
    parts = [jnp.full((rows_per_head, 1), sinks_ref[kv_head * SWA_GROUP + g], F32) for g in range(SWA_GROUP)]
    return jnp.concatenate(parts, axis=0)


def _swa_prompt_kernel(sinks_ref, q_ref, kvp_ref, kvc_ref, o_ref):
    wnd = WINDOW
    blk = pl.program_id(1)
    kv = jnp.concatenate([kvp_ref[0], kvc_ref[0]], axis=0)
    cols = SWA_GROUP * wnd
    kj = _iota2((2 * wnd, cols), 0)
    qi = _iota2((2 * wnd, cols), 1) & (wnd - 1)
    dist = qi + wnd - kj
    first_key = jnp.where(blk > 0, 0, wnd)
    mask = (dist >= 0) & (dist <= wnd) & (kj >= first_key)
    kv_heads = list(range(SWA_KV_HEADS))
    head_cols = lambda h: slice(h * SWA_HEAD_DIM, (h + 1) * SWA_HEAD_DIM)
    q_heads = lambda hk: [hk * SWA_GROUP + g for g in range(SWA_GROUP)]
    scale = SWA_HEAD_DIM ** -0.5
    q = [jnp.concatenate([q_ref[0, :, head_cols(h)] for h in q_heads(hk)], axis=0) for hk in kv_heads]
    k = [kv[:, head_cols(hk)] for hk in kv_heads]
    v = [kv[:, SWA_KV_WIDTH + hk * SWA_HEAD_DIM:SWA_KV_WIDTH + (hk + 1) * SWA_HEAD_DIM] for hk in kv_heads]
    sink = [jnp.concatenate([jnp.full((1, wnd), sinks_ref[h], F32) for h in q_heads(hk)], axis=1) for hk in kv_heads]
    s = _each(lambda kk, qq: jnp.where(mask, _bdot_nt(kk, qq) * scale, -jnp.inf), k, q)
    m = _each(lambda ss, sk: jnp.maximum(jnp.max(ss, axis=0, keepdims=True), sk), s, sink)
    p = _each(lambda ss, mm: jnp.exp(ss - mm), s, m)
    denom = _each(lambda pp, sk, mm: jnp.sum(pp, axis=0, keepdims=True) + jnp.exp(sk - mm), p, sink, m)
    ot = _each(lambda vv, pp, dd: _bdot_tn(vv, pp) / dd, v, p, denom)
    for hk in kv_heads:
        for g in range(0, SWA_GROUP, 2):
            pair = jnp.concatenate([ot[hk][:, g * wnd:(g + 1) * wnd], ot[hk][:, (g + 1) * wnd:(g + 2) * wnd]],
                                   axis=0)
            h0 = hk * SWA_GROUP + g
            o_ref[0, :, h0 * SWA_HEAD_DIM:(h0 + 2) * SWA_HEAD_DIM] = pair.T.astype(o_ref.dtype)


def _swa_prompt(q, kv, sinks):
    b, t, _ = q.shape
    return pl.pallas_call(
        _swa_prompt_kernel,
        grid=(b, t // WINDOW),
        in_specs=[pl.BlockSpec(memory_space=pltpu.SMEM),
                  pl.BlockSpec((1, WINDOW, SWA_WIDTH), lambda bi, i: (bi, i, 0)),
                  pl.BlockSpec((1, WINDOW, 2 * SWA_KV_WIDTH), lambda bi, i: (bi, jnp.maximum(i - 1, 0), 0)),
                  pl.BlockSpec((1, WINDOW, 2 * SWA_KV_WIDTH), lambda bi, i: (bi, i, 0))],
        out_specs=pl.BlockSpec((1, WINDOW, SWA_WIDTH), lambda bi, i: (bi, i, 0)),
        out_shape=jax.ShapeDtypeStruct((b, t, SWA_WIDTH), BF16),
        compiler_params=_cparams("parallel", "parallel"),
        name="swa_prompt",
    )(sinks, q, kv, kv)


def _swa_sample_kernel(sinks_ref, q_ref, kvn_ref, kc_ref, vc_ref, o_ref, ko_ref, vo_ref):
    grp, t, _ = q_ref.shape
    wnd = WINDOW
    nk = wnd + BF16_ROWS
    rows = SWA_GROUP * t
    tq = _iota2((rows, nk), 0) & (t - 1)
    kj = _iota2((rows, nk), 1)
    dist = tq + wnd - kj
    mask = (dist >= 0) & (dist <= wnd)
    zpad = jnp.zeros((BF16_ROWS - t, SWA_KV_WIDTH), F32)
    kks, vvs = [], []
    for bi in range(grp):
        kvn = kvn_ref[bi].astype(F32)
        kk = jnp.concatenate([kc_ref[bi], kvn[:, :SWA_KV_WIDTH], zpad], axis=0)
        vv = jnp.concatenate([vc_ref[bi], kvn[:, SWA_KV_WIDTH:], zpad], axis=0)
        ko_ref[bi] = kk[t:t + wnd, :]
        vo_ref[bi] = vv[t:t + wnd, :]
        kks.append(kk)
        vvs.append(vv)
    items = [(bi, hk) for bi in range(grp) for hk in range(SWA_KV_HEADS)]
    head_cols = lambda h: slice(h * SWA_HEAD_DIM, (h + 1) * SWA_HEAD_DIM)
    q_heads = lambda hk: [hk * SWA_GROUP + g for g in range(SWA_GROUP)]
    sink_cols = [_sink_column(sinks_ref, hk, t) for hk in range(SWA_KV_HEADS)]
    o = _sink_attention(
        [jnp.concatenate([q_ref[bi, :, head_cols(h)] for h in q_heads(hk)], axis=0) for bi, hk in items],
        [kks[bi][:, head_cols(hk)] for bi, hk in items], [vvs[bi][:, head_cols(hk)] for bi, hk in items],
        mask, [sink_cols[hk] for bi, hk in items])
    for idx, (bi, hk) in enumerate(items):
        for g, h in enumerate(q_heads(hk)):
            o_ref[bi, :, head_cols(h)] = o[idx][g * t:(g + 1) * t].astype(o_ref.dtype)


def _swa_sample(q, kv_new, k_cache, v_cache, sinks):
    b, t, _ = q.shape
    grp = SWA_SAMPLE_GROUP
    blk = lambda r, w: pl.BlockSpec((grp, r, w), lambda i: (i, 0, 0))
    return pl.pallas_call(
        _swa_sample_kernel,
        grid=(b // grp,),
        in_specs=[pl.BlockSpec(memory_space=pltpu.SMEM),
                  blk(t, SWA_WIDTH), blk(t, 2 * SWA_KV_WIDTH), blk(WINDOW, SWA_KV_WIDTH), blk(WINDOW, SWA_KV_WIDTH)],
        out_specs=[blk(t, SWA_WIDTH), blk(WINDOW, SWA_KV_WIDTH), blk(WINDOW, SWA_KV_WIDTH)],
        out_shape=[jax.ShapeDtypeStruct((b, t, SWA_WIDTH), BF16),
                   jax.ShapeDtypeStruct((b, WINDOW, SWA_KV_WIDTH), F32),
                   jax.ShapeDtypeStruct((b, WINDOW, SWA_KV_WIDTH), F32)],
        compiler_params=_cparams("parallel"),
        name="swa_sample",
    )(sinks, q, kv_new, k_cache, v_cache)


def _route(logits):
    lane = _iota2(logits.shape, 1).astype(F32)
    neg = -jnp.inf

    def first_argmax(vals, valid):
        v = jnp.where(valid, vals, neg)
        m = jnp.max(v, axis=-1, keepdims=True)
        idx = jnp.min(jnp.where(jnp.logical_and(valid, v == m), lane, float(LANES)), axis=-1, keepdims=True)
        return m, idx

    is_group = lane < N_GROUPS
    gmax, gidx = first_argmax(logits, is_group)
    p_group = 1.0 / jnp.sum(jnp.where(is_group, jnp.exp(logits - gmax), 0.0), axis=-1, keepdims=True)
    lo = N_GROUPS + gidx * EXPERTS_PER_GROUP
    in_group = jnp.logical_and(lane >= lo, lane < lo + EXPERTS_PER_GROUP)
    m1, i1 = first_argmax(logits, in_group)
    esum = jnp.sum(jnp.where(in_group, jnp.exp(logits - m1), 0.0), axis=-1, keepdims=True)
    m2, i2 = first_argmax(logits, jnp.logical_and(in_group, lane != i1))
    p1 = 1.0 / esum
    p2 = jnp.exp(m2 - m1) / esum
    tot = p1 + p2
    return i1 - N_GROUPS, i2 - N_GROUPS, p_group * p1 / tot, p_group * p2 / tot


def _post_mixer_kernel(oa_ref, ob_ref, ga_ref, gb_ref, x_ref, gt_ref, sc_ref, sh_ref,
                       wa_ref, wb_ref, wo_ref, gpost_ref, gpre_ref, wr_ref, br_ref, cnt0_ref,
                       x1_ref, h2_ref, rt_ref, cnt_out_ref, cnt_ref):
    step = pl.program_id(0)

    @pl.when(step == 0)
    def _():
        cnt_ref[...] = cnt0_ref[...]

    merged = (ga_ref[...].astype(F32) * jnp.dot(oa_ref[...], wa_ref[...], preferred_element_type=F32)
              + gb_ref[...].astype(F32) * jnp.dot(ob_ref[...], wb_ref[...], preferred_element_type=F32))
    mix = _bdot(merged, wo_ref[...])
    x1 = x_ref[...] + gt_ref[0] * _rms(mix, gpost_ref[...])
    x1_ref[...] = x1
    h2 = _rms(x1, gpre_ref[...]) * (1.0 + sc_ref[0]) + sh_ref[0]
    _rows_to_tiles(h2_ref, h2)
    h_hi = h2.astype(BF16)
    h_lo = (h2 - h_hi.astype(F32)).astype(BF16)
    part = jnp.dot(h_hi, wr_ref[...], preferred_element_type=F32)
    logits = (part[:, :LANES] + part[:, LANES:]
              + jnp.dot(h_lo, wr_ref[:, :LANES], preferred_element_type=F32) + br_ref[...])
    ia, ib, wa, wb = _route(logits)
    lane = _iota2(logits.shape, 1)
    tm = logits.shape[0]
    lane_f = lane.astype(F32)
    hot_a = (lane_f == ia).astype(F32)
    hot_b = (lane_f == ib).astype(F32)
    hot = hot_a + hot_b
    earlier = (_iota2((tm, tm), 0) > _iota2((tm, tm), 1)).astype(BF16)
    before = jnp.dot(earlier, hot.astype(BF16), preferred_element_type=F32) + cnt_ref[...]
    rank_a = jnp.sum(hot_a * before, axis=-1, keepdims=True)
    rank_b = jnp.sum(hot_b * before, axis=-1, keepdims=True)
    cnt_ref[...] = cnt_ref[...] + jnp.sum(hot, axis=0, keepdims=True)
    cnt_out_ref[...] = cnt_ref[...]
    rt_ref[...] = jnp.where(lane == 0, ia, jnp.where(lane == 1, ib, jnp.where(lane == 2, wa, jnp.where(
        lane == 3, wb, jnp.where(lane == 4, rank_a, jnp.where(lane == 5, rank_b, 0.0))))))


def _post_mixer(oa, ob, ga, gb, x2d, gt, sc, sh, w_a, w_b, w_o, g_post, g_pre, w_rt, b_rt, cnt0, tm):
    rows = x2d.shape[0]
    n_tiles = rows // tm
    row_blk = lambda w: pl.BlockSpec((tm, w), lambda i: (i, 0))
    full = lambda r, c: pl.BlockSpec((r, c), lambda i: (0, 0))
    return pl.pallas_call(
        _post_mixer_kernel,
        grid=(n_tiles,),
        in_specs=[row_blk(GDN_WIDTH), row_blk(SWA_WIDTH), row_blk(D_MODEL), row_blk(D_MODEL), row_blk(D_MODEL),
                  _mod_spec(gt, n_tiles), _mod_spec(sc, n_tiles), _mod_spec(sh, n_tiles),
                  full(GDN_WIDTH, D_MODEL), full(SWA_WIDTH, D_MODEL), full(D_MODEL, D_MODEL),
                  full(1, D_MODEL), full(1, D_MODEL), full(D_MODEL, 2 * LANES), full(1, LANES), full(1, LANES)],
        out_specs=[row_blk(D_MODEL), pl.BlockSpec((tm * TILE_ROWS, LANES), lambda i: (i, 0)), row_blk(LANES),
                   full(1, LANES)],
        out_shape=[jax.ShapeDtypeStruct((rows, D_MODEL), F32),
                   jax.ShapeDtypeStruct((rows * TILE_ROWS, LANES), F32),
                   jax.ShapeDtypeStruct((rows, LANES), F32),
                   jax.ShapeDtypeStruct((1, LANES), F32)],
        scratch_shapes=[pltpu.VMEM((1, LANES), F32)],
        compiler_params=_cparams("arbitrary"),
        name="post_mixer",
    )(oa, ob, ga, gb, x2d, gt, sc, sh, w_a, w_b, w_o,
      g_post.reshape(1, D_MODEL), g_pre.reshape(1, D_MODEL), w_rt, b_rt, cnt0)


TILE_ROWS = D_MODEL // LANES


def _tiles_to_rows(ref, first, rows):
    base = first * TILE_ROWS
    return jnp.concatenate([ref[pl.ds(base + c, rows, stride=TILE_ROWS), :] for c in range(TILE_ROWS)], axis=1)


def _rows_to_tiles(ref, mat):
    rows = mat.shape[0]
    for c in range(TILE_ROWS):
        ref[pl.ds(c, rows, stride=TILE_ROWS), :] = mat[:, c * LANES:(c + 1) * LANES]


def _tile_copy_loop(n, copies, start):
    def body(t, carry):
        for cp in copies(t):
            if start:
                cp.start()
            else:
                cp.wait()
        return carry

    lax.fori_loop(0, n, body, 0, unroll=8)


def _dispatch_kernel(dest_ref, h_ref, xs_in_hbm, xs_hbm, sem):
    del xs_in_hbm
    i = pl.program_id(0)
    tm = h_ref.shape[0] // TILE_ROWS

    def copies(t):
        src = h_ref.at[pl.ds(pl.multiple_of(t * TILE_ROWS, TILE_ROWS), TILE_ROWS)]
        base = (i * tm + t) * TOP_K
        return [pltpu.make_async_copy(src, xs_hbm.at[dest_ref[base + k]], sem) for k in range(TOP_K)]

    _tile_copy_loop(tm, copies, True)
    _tile_copy_loop(tm, copies, False)


def _dispatch(dest, h_tiles, xs, tm):
    n_tiles = h_tiles.shape[0] // (tm * TILE_ROWS)
    return pl.pallas_call(
        _dispatch_kernel,
        grid_spec=pltpu.PrefetchScalarGridSpec(
            num_scalar_prefetch=1,
            grid=(n_tiles,),
            in_specs=[pl.BlockSpec((tm * TILE_ROWS, LANES), lambda i, d: (i, 0)),
                      pl.BlockSpec(memory_space=pl.ANY)],
            out_specs=pl.BlockSpec(memory_space=pl.ANY),
            scratch_shapes=[pltpu.SemaphoreType.DMA(())]),
        out_shape=jax.ShapeDtypeStruct(xs.shape, xs.dtype),
        input_output_aliases={2: 0},
        compiler_params=_cparams("arbitrary"),
        name="moe_dispatch",
    )(dest, h_tiles, xs)


def _moe_kernel(blk_e_ref, n_used_ref, x_ref, wg_ref, wu_ref, wd_ref, y_ref, wgb, wub, wdb):
    b = pl.program_id(0)
    rows = x_ref.shape[0] // TILE_ROWS
    changed = jnp.logical_or(b == 0, blk_e_ref[b] != blk_e_ref[jnp.maximum(b - 1, 0)])

    @pl.when(changed)
    def _():
        wgb[...] = wg_ref[0].astype(BF16)
        wub[...] = wu_ref[0].astype(BF16)
        wdb[...] = wd_ref[0].astype(BF16)

    @pl.when(b < n_used_ref[0])
    def _():
        x = _tiles_to_rows(x_ref, 0, rows).astype(BF16)
        gate = jnp.dot(x, wgb[...], preferred_element_type=F32)
        up = jnp.dot(x, wub[...], preferred_element_type=F32)
        _rows_to_tiles(y_ref, _bdot(_silu(gate) * up, wdb[...]))

    @pl.when(b >= n_used_ref[0])
    def _():
        y_ref[...] = jnp.zeros_like(y_ref)


def _moe(xs_tiles, blk_e, n_used, w_gate, w_up, w_down):
    n_blocks = blk_e.shape[0]
    rows = MOE_ROWS
    wspec = lambda r, c: pl.BlockSpec((1, r, c), lambda b, be, nu: (be[b], 0, 0))
    xspec = pl.BlockSpec((rows * TILE_ROWS, LANES), lambda b, be, nu: (b, 0))
    return pl.pallas_call(
        _moe_kernel,
        grid_spec=pltpu.PrefetchScalarGridSpec(
            num_scalar_prefetch=2,
            grid=(n_blocks,),
            in_specs=[xspec, wspec(D_MODEL, EXPERT_FF), wspec(D_MODEL, EXPERT_FF), wspec(EXPERT_FF, D_MODEL)],
            out_specs=xspec,
            scratch_shapes=[pltpu.VMEM((D_MODEL, EXPERT_FF), BF16),
                            pltpu.VMEM((D_MODEL, EXPERT_FF), BF16),
                            pltpu.VMEM((EXPERT_FF, D_MODEL), BF16)]),
        out_shape=jax.ShapeDtypeStruct(xs_tiles.shape, F32),
        compiler_params=_cparams("arbitrary"),
        name="moe_experts",
    )(blk_e, n_used, xs_tiles, w_gate, w_up, w_down)


def _combine_kernel(dest_ref, y_hbm, x1_ref, rt_ref, gt_ref, gpost_ref, o_ref, ybuf, sems):
    i = pl.program_id(0)
    n = pl.num_programs(0)
    rows = ybuf.shape[1] // TILE_ROWS
    slot = i % 2

    def gather(step, buf_slot, start):
        def copies(r):
            dst = ybuf.at[buf_slot, pl.ds(pl.multiple_of(r * TILE_ROWS, TILE_ROWS), TILE_ROWS)]
            return [pltpu.make_async_copy(y_hbm.at[dest_ref[step * rows + r]], dst, sems.at[buf_slot])]

        _tile_copy_loop(rows, copies, start)

    @pl.when(i == 0)
    def _():
        gather(0, 0, True)

    @pl.when(i + 1 < n)
    def _():
        gather(i + 1, 1 - slot, True)

    gather(i, slot, False)
    half = rows // 2
    rt = rt_ref[...]
    buf = ybuf.at[slot]
    f = rt[:, 2:3] * _tiles_to_rows(buf, 0, half) + rt[:, 3:4] * _tiles_to_rows(buf, half, half)
    o_ref[...] = x1_ref[...] + gt_ref[0] * _rms(f, gpost_ref[...])


def _combine(dest, yb, x1, rt, gt, g_post, tm):
    rows = x1.shape[0]
    n_tiles = rows // tm
    tiles_per_mod = n_tiles // gt.shape[0]
    return pl.pallas_call(
        _combine_kernel,
        grid_spec=pltpu.PrefetchScalarGridSpec(
            num_scalar_prefetch=1,
            grid=(n_tiles,),
            in_specs=[pl.BlockSpec(memory_space=pl.ANY),
                      pl.BlockSpec((tm, D_MODEL), lambda i, d: (i, 0)),
                      pl.BlockSpec((tm, LANES), lambda i, d: (i, 0)),
                      pl.BlockSpec((1, gt.shape[1], D_MODEL), lambda i, d: (i // tiles_per_mod, 0, 0)),
                      pl.BlockSpec((1, D_MODEL), lambda i, d: (0, 0))],
            out_specs=pl.BlockSpec((tm, D_MODEL), lambda i, d: (i, 0)),
            scratch_shapes=[pltpu.VMEM((2, TOP_K * tm * TILE_ROWS, LANES), F32),
                            pltpu.SemaphoreType.DMA((2,))]),
        out_shape=jax.ShapeDtypeStruct((rows, D_MODEL), F32),
        compiler_params=_cparams("arbitrary"),
        name="moe_combine",
    )(dest, yb, x1, rt, gt, g_post.reshape(1, D_MODEL))


def _dispatch_plan(rt, counts, n_tok, tm):
    n_assign = n_tok * TOP_K
    flat_e = rt[:, :TOP_K].astype(I32).reshape(n_assign)
    rank = rt[:, 4:4 + TOP_K].astype(I32).reshape(n_assign)
    pcounts = (counts + MOE_ROWS - 1) // MOE_ROWS * MOE_ROWS
    pends = jnp.cumsum(pcounts)
    pstarts = pends - pcounts
    experts = jnp.arange(N_EXPERTS, dtype=I32)
    dest = jnp.sum(jnp.where(flat_e[:, None] == experts[None, :], pstarts[None, :], 0), axis=1) + rank
    n_blocks = n_assign // MOE_ROWS + N_EXPERTS
    blk_start = jnp.arange(n_blocks, dtype=I32) * MOE_ROWS
    blk_e = jnp.minimum(jnp.sum(blk_start[:, None] >= pends[None, :], axis=1), N_EXPERTS - 1).astype(I32)
    n_used = (pends[-1] // MOE_ROWS).astype(I32).reshape(1)
    dest_tiles = dest.reshape(n_tok // tm, tm, TOP_K).transpose(0, 2, 1).reshape(-1)
    return blk_e, n_used, dest, dest_tiles


def _prep_in_weight(w_in):
    a0 = 4 * GDN_QK_WIDTH
    a1 = a0 + 2 * GDN_HEADS
    pad = jnp.zeros((D_MODEL, LANES - 2 * GDN_HEADS), w_in.dtype)
    return jnp.concatenate([w_in[:, :a0], w_in[:, a1:], w_in[:, a0:a1], pad], axis=1).astype(BF16)


def _head_param_tile(a_log, dt_bias):
    tile = jnp.zeros((8, LANES), F32)
    return tile.at[0, :GDN_HEADS].set(a_log.astype(F32)).at[1, :GDN_HEADS].set(dt_bias.astype(F32))


def _router_weight(w_group, b_group, w_router, b_router):
    w = jnp.zeros((D_MODEL, LANES), F32)
    w = w.at[:, :N_GROUPS].set(w_group).at[:, N_GROUPS:N_GROUPS + N_EXPERTS].set(w_router)
    b = jnp.zeros((1, LANES), F32)
    b = b.at[0, :N_GROUPS].set(b_group).at[0, N_GROUPS:N_GROUPS + N_EXPERTS].set(b_router)
    w_hi = w.astype(BF16)
    w_lo = (w - w_hi.astype(F32)).astype(BF16)
    return jnp.concatenate([w_hi, w_lo], axis=1), b


def kernel(x_prompt, x_sample, state_gdn, state_conv, cache_k_win, cache_v_win, c_prompt, c_sample, w_ada, b_ada, g_mix_pre, g_mix_post, g_ffn_pre, g_ffn_post, w_in, conv_w, a_log, dt_bias, gdn_norm, sinks, w_br_gdn, w_br_swa, w_out, w_group, b_group, w_router, b_router, w_gate, w_up, w_down):
    depth = w_ada.shape[0]
    assert depth == 1, "single-layer trunk"
    bp, tp, _ = x_prompt.shape
    bs, ts, _ = x_sample.shape
    n_p = bp * tp
    n_s = bs * ts
    tm = ROW_TILE
    assert tp % tm == 0 and n_s % tm == 0 and ts >= GDN_CONV - 1 and ts + GDN_CONV - 1 <= GDN_SAMPLE_CHUNK
    assert ts & (ts - 1) == 0 and ts <= BF16_ROWS

    c_all = jnp.concatenate([c_prompt, c_sample], axis=0)
    c_rows = -(-c_all.shape[0] // 8) * 8
    c_all = jnp.pad(c_all, ((0, c_rows - c_all.shape[0]), (0, 0)))
    mod = _adaln(c_all, w_ada[0], b_ada[0])
    mods_p = [m[:bp].reshape(bp, 1, D_MODEL) for m in jnp.split(mod, 6, axis=-1)]
    mods_s = [jnp.repeat(m[bp:bp + bs], ts, axis=0).reshape(n_s // tm, tm, D_MODEL)
              for m in jnp.split(mod, 6, axis=-1)]

    w_prep = _prep_in_weight(w_in[0])
    head_params = _head_param_tile(a_log[0], dt_bias[0])
    w_a, w_b, w_o = w_br_gdn[0].astype(BF16), w_br_swa[0].astype(BF16), w_out[0].astype(BF16)
    w_rt, b_rt = _router_weight(w_group[0], b_group[0], w_router[0], b_router[0])
    sinks0 = sinks[0].astype(F32)

    xp2d = x_prompt.reshape(n_p, D_MODEL)
    sh1, sc1, gt1, sh2, sc2, gt2 = mods_p
    qkv_p, zs_p, qb_p, kvb_p, ga_p, gb_p, ab_p, conv_tail_p = _inproj(
        xp2d, g_mix_pre[0], sc1, sh1, w_prep, tm, conv_w=conv_w[0].astype(F32), n_seq=bp)
    qkv_p3 = qkv_p.reshape(bp, tp, GDN_CONV_DIM)
    u, w, qd, kd, qk, ge = _gdn_prep(qkv_p3, ab_p.reshape(bp, tp, LANES), head_params)
    oa_p, s_prompt = _gdn_scan(u, w, qd, kd, qk, ge, zs_p.reshape(bp, tp, GDN_WIDTH), gdn_norm[0])
    kvb_p3 = kvb_p.reshape(bp, tp, 2 * SWA_KV_WIDTH)
    ob_p = _swa_prompt(qb_p.reshape(bp, tp, SWA_WIDTH), kvb_p3, sinks0)
    x1_p, h2_p, rt_p, cnt_p = _post_mixer(
        oa_p.reshape(n_p, GDN_WIDTH), ob_p.reshape(n_p, SWA_WIDTH), ga_p, gb_p, xp2d, gt1, sc2, sh2,
        w_a, w_b, w_o, g_mix_post[0], g_ffn_pre[0], w_rt, b_rt, jnp.zeros((1, LANES), F32), tm)

    xs2d = x_sample.reshape(n_s, D_MODEL)
    sh1s, sc1s, gt1s, sh2s, sc2s, gt2s = mods_s
    qkv_s, zs_s, qb_s, kvb_s, ga_s, gb_s, ab_s = _inproj(xs2d, g_mix_pre[0], sc1s, sh1s, w_prep, tm)
    cc = GDN_SAMPLE_CHUNK
    pad_rows = cc - ts - (GDN_CONV - 1)
    qkv_s3 = qkv_s.reshape(bs, ts, GDN_CONV_DIM)
    xp_s = jnp.concatenate([jnp.zeros((bs, pad_rows, GDN_CONV_DIM), BF16), state_conv[0].astype(BF16), qkv_s3],
                           axis=1)
    front = lambda a: jnp.pad(a, ((0, 0), (cc - ts, 0), (0, 0)))
    oa_s16, s_sample = _gdn_sample(xp_s, front(ab_s.reshape(bs, ts, LANES)), front(zs_s.reshape(bs, ts, GDN_WIDTH)),
                                   state_gdn[0].astype(F32), conv_w[0], head_params, gdn_norm[0], ts)
    oa_s = oa_s16[:, cc - ts:, :].reshape(n_s, GDN_WIDTH)
    ob_s, k_new_s, v_new_s = _swa_sample(
        qb_s.reshape(bs, ts, SWA_WIDTH), kvb_s.reshape(bs, ts, 2 * SWA_KV_WIDTH),
        cache_k_win[0].reshape(bs, WINDOW, SWA_KV_WIDTH).astype(F32),
        cache_v_win[0].reshape(bs, WINDOW, SWA_KV_WIDTH).astype(F32), sinks0)
    x1_s, h2_s, rt_s, cnt_all = _post_mixer(
        oa_s, ob_s.reshape(n_s, SWA_WIDTH), ga_s, gb_s, xs2d, gt1s, sc2s, sh2s,
        w_a, w_b, w_o, g_mix_post[0], g_ffn_pre[0], w_rt, b_rt, cnt_p, tm)

    n_all = n_p + n_s
    ct = COMBINE_ROWS
    blk_e, n_used, dest, dest_tiles = _dispatch_plan(
        jnp.concatenate([rt_p[:, :8], rt_s[:, :8]], axis=0), cnt_all[0, :N_EXPERTS].astype(I32), n_all, ct)
    n_slots = blk_e.shape[0] * MOE_ROWS
    xs = jnp.zeros((n_slots, TILE_ROWS, LANES), F32)
    xs = _dispatch(dest[:n_p * TOP_K], h2_p, xs, tm)
    xs = _dispatch(dest[n_p * TOP_K:], h2_s, xs, tm)
    yb = _moe(xs.reshape(n_slots * TILE_ROWS, LANES), blk_e, n_used, w_gate[0], w_up[0], w_down[0])
    yb = yb.reshape(n_slots, TILE_ROWS, LANES)
    y_p = _combine(dest_tiles[:n_p * TOP_K], yb, x1_p, rt_p, gt2, g_ffn_post[0], ct)
    y_s = _combine(dest_tiles[n_p * TOP_K:], yb, x1_s, rt_s,
                   gt2s.reshape(n_s // ct, ct, D_MODEL), g_ffn_post[0], ct)

    f32 = lambda a: a.astype(F32)
    kv_tail = kvb_p3[:, tp - WINDOW:, :]
    kv_heads = lambda a: f32(a).reshape(a.shape[0], WINDOW, SWA_KV_HEADS, SWA_HEAD_DIM)[None]
    return (y_p.reshape(bp, tp, D_MODEL), y_s.reshape(bs, ts, D_MODEL),
            s_prompt[None], conv_tail_p[:, 8 - (GDN_CONV - 1):, :][None],
            kv_heads(kv_tail[:, :, :SWA_KV_WIDTH]), kv_heads(kv_tail[:, :, SWA_KV_WIDTH:]),
            s_sample[None], f32(qkv_s3[:, ts - (GDN_CONV - 1):, :])[None],
            kv_heads(k_new_s), kv_heads(v_new_s))
```

```python
import functools

import jax
import jax.numpy as jnp
from jax import lax
from jax.experimental import pallas as pl
from jax.experimental.pallas import tpu as pltpu

F32 = jnp.float32
BF16 = jnp.bfloat16
I32 = jnp.int32

D_MODEL = 1024
NORM_EPS = 1e-6
GDN_HEADS = 8
GDN_DK = 128
GDN_DV = 128
GDN_CONV = 4
GDN_CHUNK = 64
GDN_QK_WIDTH = GDN_HEADS * GDN_DK
GDN_WIDTH = GDN_HEADS * GDN_DV
GDN_CONV_DIM = 2 * GDN_QK_WIDTH + GDN_WIDTH
SWA_Q_HEADS = 16
SWA_KV_HEADS = 4
SWA_HEAD_DIM = 64
SWA_GROUP = SWA_Q_HEADS // SWA_KV_HEADS
SWA_WIDTH = SWA_Q_HEADS * SWA_HEAD_DIM
SWA_KV_WIDTH = SWA_KV_HEADS * SWA_HEAD_DIM
WINDOW = 128
N_GROUPS = 4
EXPERTS_PER_GROUP = 8
N_EXPERTS = N_GROUPS * EXPERTS_PER_GROUP
TOP_K = 2
EXPERT_FF = 512

LANES = 128
BF16_ROWS = 16
VMEM_LIMIT = 56 * 1024 * 1024

_C_QKV = 0
_C_Z = _C_QKV + GDN_CONV_DIM
_C_QB = _C_Z + GDN_WIDTH
_C_KVB = _C_QB + SWA_WIDTH
_C_GA = _C_KVB + 2 * SWA_KV_WIDTH
_C_GB = _C_GA + D_MODEL
_C_AB = _C_GB + D_MODEL
IN_COLS = _C_AB + LANES
PROJ_TILE = 512

ROW_TILE = 512
GDN_PREP_ROWS = 256
GDN_SCAN_ROWS = 512
GDN_SAMPLE_CHUNK = 16
GDN_SAMPLE_GROUP = 4
SWA_SAMPLE_GROUP = 8
MOE_ROWS = 256
COMBINE_ROWS = 256


def _cparams(*sem):
    return pltpu.CompilerParams(dimension_semantics=sem, vmem_limit_bytes=VMEM_LIMIT)


def _bdot(a, b):
    return jnp.dot(a.astype(BF16), b.astype(BF16), preferred_element_type=F32)


def _bdot_nt(a, b):
    return lax.dot_general(a.astype(BF16), b.astype(BF16), (((1,), (1,)), ((), ())),
                           preferred_element_type=F32)


def _bdot_tn(a, b):
    return lax.dot_general(a.astype(BF16), b.astype(BF16), (((0,), (0,)), ((), ())),
                           preferred_element_type=F32)


def _sigmoid(x):
    return 1.0 / (1.0 + jnp.exp(-x))


def _silu(x):
    return x * _sigmoid(x)


def _rms(x, gain):
    return x * lax.rsqrt(jnp.mean(x * x, axis=-1, keepdims=True) + NORM_EPS) * gain


def _iota2(shape, dim):
    return lax.broadcasted_iota(I32, shape, dim)


def _adaln_kernel(c_ref, w_ref, b_ref, o_ref):
    o_ref[...] = _bdot(_silu(c_ref[...]), w_ref[...]) + b_ref[...]


def _adaln(c_all, w_ada, b_ada):
    rows = c_all.shape[0]
    n_out = w_ada.shape[1]
    tn = D_MODEL
    return pl.pallas_call(
        _adaln_kernel,
        grid=(n_out // tn,),
        in_specs=[pl.BlockSpec((rows, D_MODEL), lambda j: (0, 0)),
                  pl.BlockSpec((D_MODEL, tn), lambda j: (0, j)),
                  pl.BlockSpec((1, tn), lambda j: (0, j))],
        out_specs=pl.BlockSpec((rows, tn), lambda j: (0, j)),
        out_shape=jax.ShapeDtypeStruct((rows, n_out), F32),
        compiler_params=_cparams("arbitrary"),
        name="adaln",
    )(c_all, w_ada, b_ada.reshape(1, n_out))


def _inproj_kernel(tiles_per_seq, x_ref, g_ref, sc_ref, sh_ref, w_ref, *rest):
    if tiles_per_seq:
        cw_ref, qkv_ref, z_ref, qb_ref, kvb_ref, ga_ref, gb_ref, ab_ref, tail_ref, carry_ref = rest
    else:
        qkv_ref, z_ref, qb_ref, kvb_ref, ga_ref, gb_ref, ab_ref = rest
    tm = x_ref.shape[0]
    h = (_rms(x_ref[...], g_ref[...]) * (1.0 + sc_ref[0]) + sh_ref[0]).astype(BF16)

    def fill(ref, c0, width, fn):
        step = min(PROJ_TILE, width)
        for c in range(0, width, step):
            acc = jnp.dot(h, w_ref[:, c0 + c:c0 + c + step], preferred_element_type=F32)
            ref[:, c:c + step] = fn(acc, c, step).astype(ref.dtype)

    def conv_act(acc, c, step):
        cols = slice(c, c + step)
        seq_start = pl.program_id(0) % tiles_per_seq == 0
        prev = jnp.where(seq_start, 0.0, carry_ref[:, cols])
        last = acc[tm - 8:tm]
        carry_ref[:, cols] = last
        tail_ref[0, :, cols] = last
        ext = jnp.concatenate([prev, acc], axis=0)
        y = cw_ref[GDN_CONV - 1:GDN_CONV, cols] * acc
        for j in range(GDN_CONV - 1):
            y = y + cw_ref[j:j + 1, cols] * ext[8 - (GDN_CONV - 1) + j:8 - (GDN_CONV - 1) + j + tm]
        y = _silu(y)
        if c >= 2 * GDN_QK_WIDTH:
            return y
        scale = GDN_DK ** -0.5 if c < GDN_QK_WIDTH else 1.0
        heads = [_l2n(y[:, d:d + GDN_DK]) * scale for d in range(0, step, GDN_DK)]
        return jnp.concatenate(heads, axis=1)

    ident = lambda v, c, step: v
    silu = lambda v, c, step: _silu(v)
    sigmoid = lambda v, c, step: _sigmoid(v)
    fill(qkv_ref, _C_QKV, GDN_CONV_DIM, conv_act if tiles_per_seq else ident)
    fill(z_ref, _C_Z, GDN_WIDTH, silu)
    fill(qb_ref, _C_QB, SWA_WIDTH, ident)
    fill(kvb_ref, _C_KVB, 2 * SWA_KV_WIDTH, ident)
    fill(ga_ref, _C_GA, D_MODEL, sigmoid)
    fill(gb_ref, _C_GB, D_MODEL, sigmoid)
    fill(ab_ref, _C_AB, LANES, ident)


def _mod_spec(mod, n_tiles):
    tiles_per_mod = n_tiles // mod.shape[0]
    return pl.BlockSpec((1, mod.shape[1], D_MODEL), lambda i: (i // tiles_per_mod, 0, 0))


def _inproj(x2d, gain, sc, sh, w_prep, tm, conv_w=None, n_seq=0):
    rows = x2d.shape[0]
    n_tiles = rows // tm
    widths = (GDN_CONV_DIM, GDN_WIDTH, SWA_WIDTH, 2 * SWA_KV_WIDTH, D_MODEL, D_MODEL, LANES)
    dtypes = (BF16, BF16, BF16, BF16, BF16, BF16, F32)
    in_specs = [pl.BlockSpec((tm, D_MODEL), lambda i: (i, 0)),
                pl.BlockSpec((1, D_MODEL), lambda i: (0, 0)),
                _mod_spec(sc, n_tiles), _mod_spec(sh, n_tiles),
                pl.BlockSpec((D_MODEL, IN_COLS), lambda i: (0, 0))]
    out_specs = [pl.BlockSpec((tm, w), lambda i: (i, 0)) for w in widths]
    out_shape = [jax.ShapeDtypeStruct((rows, w), dt) for w, dt in zip(widths, dtypes)]
    args = [x2d, gain.reshape(1, D_MODEL), sc, sh, w_prep]
    scratch = []
    tiles_per_seq = 0
    if conv_w is not None:
        tiles_per_seq = n_tiles // n_seq
        in_specs.append(pl.BlockSpec((GDN_CONV, GDN_CONV_DIM), lambda i: (0, 0)))
        out_specs.append(pl.BlockSpec((1, 8, GDN_CONV_DIM), lambda i: (i // tiles_per_seq, 0, 0)))
        out_shape.append(jax.ShapeDtypeStruct((n_seq, 8, GDN_CONV_DIM), F32))
        args.append(conv_w)
        scratch.append(pltpu.VMEM((8, GDN_CONV_DIM), F32))
    return pl.pallas_call(
        functools.partial(_inproj_kernel, tiles_per_seq),
        grid=(n_tiles,),
        in_specs=in_specs,
        out_specs=out_specs,
        out_shape=out_shape,
        scratch_shapes=scratch,
        compiler_params=_cparams("arbitrary"),
        name="inproj",
    )(*args)


def _cumsum_rows(g):
    c = g.shape[0]
    tril = (_iota2((c, c), 0) >= _iota2((c, c), 1)).astype(BF16)
    hi = g.astype(BF16)
    r1 = g - hi.astype(F32)
    mid = r1.astype(BF16)
    lo = (r1 - mid.astype(F32)).astype(BF16)
    dot = lambda p: jnp.dot(tril, p, preferred_element_type=F32)
    return dot(hi) + dot(mid) + dot(lo)


def _each(fn, *lists):
    return [fn(*args) for args in zip(*lists)]


def _pair_blockdiag(m, left):
    return jnp.concatenate([jnp.where(left, m, 0.0), jnp.where(left, 0.0, m)], axis=0)


def _unit_lower_inverse_offset(a_list, ii, jj, left):
    c = a_list[0].shape[0]

    def same_block(shift):
        return lax.shift_right_logical(ii, shift) == lax.shift_right_logical(jj, shift)

    base = same_block(1)
    n_list = _each(lambda a: jnp.where(base, -a, 0.0), a_list)
    shift = 1
    while (1 << shift) < c:
        outer, inner = same_block(shift + 1), same_block(shift)
        off_list = _each(lambda a: jnp.where(outer, jnp.where(inner, 0.0, a), 0.0), a_list)
        x_list = _each(lambda off, n: off + _bdot(off, _pair_blockdiag(n, left)), off_list, n_list)
        n_list = _each(lambda n, x: n - x - _bdot(n, _pair_blockdiag(x, left)), n_list, x_list)
        shift += 1
    return n_list


def _chunk_prep(q, k, v, gcol, grow, bcol):
    c = q[0].shape[0]
    assert len(q) % 2 == 0
    ii = _iota2((c, 2 * c), 0)
    lane = _iota2((c, 2 * c), 1)
    left = lane < c
    jj = lane & (c - 1)
    causal = ii >= jj
    strict = ii > jj
    first, second = slice(0, None, 2), slice(1, None, 2)
    kb = _each(lambda kk, b: kk * b, k, bcol)
    both = _each(lambda qa, ka, qb, kbb, x, y: _bdot_nt(jnp.concatenate([qa, ka, qb, kbb], axis=0),
                                                       jnp.concatenate([x, y], axis=0)),
                 q[first], kb[first], q[second], kb[second], k[first], k[second])
    decay = _each(lambda ga, gb, ra, rb: jnp.where(causal, jnp.exp(jnp.where(
        causal, jnp.where(left, ga, gb) - jnp.concatenate([ra, rb], axis=1), 0.0)), 0.0),
        gcol[first], gcol[second], grow[first], grow[second])
    qk = _each(lambda bo, d: jnp.where(left, bo[0:c], bo[2 * c:3 * c]) * d, both, decay)
    a = _each(lambda bo, d: jnp.where(strict, jnp.where(left, bo[c:2 * c], bo[3 * c:4 * c]) * d, 0.0), both, decay)
    n = _unit_lower_inverse_offset(a, ii, jj, left)
    eg = _each(jnp.exp, gcol)
    rhs = _each(lambda vv, b, kbb, e: jnp.concatenate([vv * b, kbb * e], axis=1), v, bcol, kb, eg)
    uw = _each(lambda ra, rb, nn: (lambda r: r + _bdot(_pair_blockdiag(nn, left), r))(
        jnp.concatenate([ra, rb], axis=0)), rhs[first], rhs[second], n)
    uw = [x[half] for x in uw for half in (slice(0, c), slice(c, 2 * c))]
    u = [x[:, :GDN_DV] for x in uw]
    w = [x[:, GDN_DV:] for x in uw]
    qd = _each(lambda qq, e: qq * e, q, eg)
    kd = _each(lambda kk, gc: kk * jnp.exp(gc[c - 1:c, :] - gc), k, gcol)
    return u, w, qd, kd, qk


def _chunk_step(s, u, w, qd, kd, qk, ge):
    c = u[0].shape[0]
    both = _each(lambda ww, qq, ss: _bdot(jnp.concatenate([ww, qq], axis=0), ss), w, qd, s)
    v_new = _each(lambda uu, bo: uu.astype(F32) - bo[:c], u, both)
    o = _each(lambda bo, m, vn: bo[c:] + _bdot(m, vn), both, qk, v_new)
    s_new = _each(lambda ss, g, kk, vn: ss * g + _bdot_tn(kk, vn), s, ge, kd, v_new)
    return o, s_new


def _conv_act(xp_ref, cw_ref, r0, rows, c0):
    cols = slice(c0, c0 + LANES)
    acc = cw_ref[3:4, cols] * xp_ref[r0:r0 + rows, cols]
    for j in range(GDN_CONV - 1):
        acc = acc + cw_ref[j:j + 1, cols] * xp_ref[r0 - 3 + j:r0 - 3 + j + rows, cols]
    return _silu(acc)


def _l2n(x):
    return x * lax.rsqrt(jnp.sum(x * x, axis=-1, keepdims=True) + NORM_EPS)


def _softplus(x):
    return jnp.maximum(x, 0.0) + jnp.log1p(jnp.exp(-jnp.abs(x)))


def _head_cols(hd):
    return (hd * GDN_DK, GDN_QK_WIDTH + hd * GDN_DK, 2 * GDN_QK_WIDTH + hd * GDN_DV)


def _activate_qkv(xp_ref, cw_ref, act_ref, r0, rows):
    for hd in range(GDN_HEADS):
        cq, ck, cv = _head_cols(hd)
        act_ref[0:rows, cq:cq + LANES] = _l2n(_conv_act(xp_ref, cw_ref, r0, rows, cq)) * (GDN_DK ** -0.5)
        act_ref[0:rows, ck:ck + LANES] = _l2n(_conv_act(xp_ref, cw_ref, r0, rows, ck))
        act_ref[0:rows, cv:cv + LANES] = _conv_act(xp_ref, cw_ref, r0, rows, cv)


def _decay_beta(ab, hp_ref):
    g = -jnp.exp(hp_ref[0:1, :]) * _softplus(ab + hp_ref[1:2, :])
    return g, _sigmoid(ab)


def _gdn_prep_kernel(act_ref, ab_ref, hp_ref, u_ref, w_ref, qd_ref, kd_ref, qk_ref, ge_ref):
    tb = act_ref.shape[1]
    cc = GDN_CHUNK
    g_all, beta_all = _decay_beta(ab_ref[0], hp_ref)

    chunks = [slice(ci * cc, (ci + 1) * cc) for ci in range(tb // cc)]
    gcs = _each(lambda rows: _cumsum_rows(g_all[rows, :]), chunks)
    gcts = _each(lambda gc: gc.T, gcs)
    for ci, gc in enumerate(gcs):
        ge_ref[0, ci] = jnp.exp(gc[cc - 1:cc, :])
    items = [(ci, hd) for ci in range(len(chunks)) for hd in range(GDN_HEADS)]
    col = lambda which: [act_ref[0, chunks[ci], _head_cols(hd)[which]:_head_cols(hd)[which] + LANES].astype(F32)
                         for ci, hd in items]
    u, w, qd, kd, qk = _chunk_prep(
        col(0), col(1), col(2),
        [gcs[ci][:, hd:hd + 1] for ci, hd in items], [gcts[ci][hd:hd + 1, :] for ci, hd in items],
        [beta_all[chunks[ci], GDN_HEADS + hd:GDN_HEADS + hd + 1] for ci, hd in items])
    for idx, (ci, hd) in enumerate(items):
        rows = chunks[ci]
        oc = slice(hd * GDN_DV, (hd + 1) * GDN_DV)
        u_ref[0, rows, oc] = u[idx].astype(BF16)
        w_ref[0, rows, oc] = w[idx].astype(BF16)
        qd_ref[0, rows, oc] = qd[idx].astype(BF16)
        kd_ref[0, rows, oc] = kd[idx].astype(BF16)
        if hd % 2 == 0:
            qk_ref[0, rows, hd * cc:(hd + 2) * cc] = qk[idx // 2].astype(BF16)


def _gdn_prep(qkv, ab, head_params):
    b, t, _ = qkv.shape
    tb = min(GDN_PREP_ROWS, t)
    nch = tb // GDN_CHUNK
    blk = lambda w: pl.BlockSpec((1, tb, w), lambda bi, i: (bi, i, 0))
    out_shapes = [jax.ShapeDtypeStruct((b, t, GDN_WIDTH), BF16)] * 4 + [
        jax.ShapeDtypeStruct((b, t, GDN_HEADS * GDN_CHUNK), BF16),
        jax.ShapeDtypeStruct((b, t // GDN_CHUNK, 1, LANES), F32)]
    return pl.pallas_call(
        _gdn_prep_kernel,
        grid=(b, t // tb),
        in_specs=[blk(GDN_CONV_DIM), blk(LANES), pl.BlockSpec((8, LANES), lambda bi, i: (0, 0))],
        out_specs=[blk(GDN_WIDTH)] * 4 + [
            blk(GDN_HEADS * GDN_CHUNK),
            pl.BlockSpec((1, nch, 1, LANES), lambda bi, i: (bi, i, 0, 0))],
        out_shape=out_shapes,
        compiler_params=_cparams("parallel", "parallel"),
        name="gdn_prep",
    )(qkv, ab, head_params)


def _gated_norm_store(o_ref, idx, o, gain, zs):
    o_ref[idx] = (_rms(o, gain) * zs.astype(F32)).astype(o_ref.dtype)


def _gdn_scan_kernel(u_ref, w_ref, qd_ref, kd_ref, qk_ref, ge_ref, zs_ref, gain_ref,
                     o_ref, s_out_ref, s_ref):
    nb, tb, _ = u_ref.shape
    cc = GDN_CHUNK
    step = pl.program_id(0)

    @pl.when(step == 0)
    def _():
        s_ref[...] = jnp.zeros_like(s_ref)

    gain = gain_ref[...]

    def chunk_body(ci, carry):
        rows = pl.ds(pl.multiple_of(ci * cc, cc), cc)
        items = [(bi, hd) for bi in range(nb) for hd in range(GDN_HEADS)]
        oc = lambda hd: slice(hd * GDN_DV, (hd + 1) * GDN_DV)
        ge_rows = [ge_ref[bi, ci] for bi in range(nb)]
        o, s_new = _chunk_step(
            [s_ref[bi * GDN_HEADS + hd] for bi, hd in items],
            [u_ref[bi, rows, oc(hd)] for bi, hd in items], [w_ref[bi, rows, oc(hd)] for bi, hd in items],
            [qd_ref[bi, rows, oc(hd)] for bi, hd in items], [kd_ref[bi, rows, oc(hd)] for bi, hd in items],
            [qk_ref[bi, rows, hd * cc:(hd + 1) * cc] for bi, hd in items],
            [ge_rows[bi][:, hd:hd + 1] for bi, hd in items])
        for idx, (bi, hd) in enumerate(items):
            s_ref[bi * GDN_HEADS + hd] = s_new[idx]
            _gated_norm_store(o_ref, (bi, rows, oc(hd)), o[idx], gain, zs_ref[bi, rows, oc(hd)])
        return carry

    lax.fori_loop(0, tb // cc, chunk_body, 0)

    @pl.when(step == pl.num_programs(0) - 1)
    def _():
        s_out_ref[...] = s_ref[...]


def _gdn_scan(u, w, qd, kd, qk, ge, zs, gain):
    b, t, _ = u.shape
    tb = min(GDN_SCAN_ROWS, t)
    nch = tb // GDN_CHUNK
    blk = lambda wd: pl.BlockSpec((b, tb, wd), lambda i: (0, i, 0))
    o, s = pl.pallas_call(
        _gdn_scan_kernel,
        grid=(t // tb,),
        in_specs=[blk(GDN_WIDTH)] * 4 + [
            blk(GDN_HEADS * GDN_CHUNK),
            pl.BlockSpec((b, nch, 1, LANES), lambda i: (0, i, 0, 0)),
            blk(GDN_WIDTH),
            pl.BlockSpec((1, GDN_DV), lambda i: (0, 0))],
        out_specs=[blk(GDN_WIDTH),
                   pl.BlockSpec((b * GDN_HEADS, GDN_DK, GDN_DV), lambda i: (0, 0, 0))],
        out_shape=[jax.ShapeDtypeStruct((b, t, GDN_WIDTH), BF16),
                   jax.ShapeDtypeStruct((b * GDN_HEADS, GDN_DK, GDN_DV), F32)],
        scratch_shapes=[pltpu.VMEM((b * GDN_HEADS, GDN_DK, GDN_DV), F32)],
        compiler_params=_cparams("arbitrary"),
        name="gdn_scan",
    )(u, w, qd, kd, qk, ge, zs, gain.reshape(1, GDN_DV))
    return o, s.reshape(b, GDN_HEADS, GDN_DK, GDN_DV)


def _gdn_sample_kernel(new_rows, xp_ref, ab_ref, zs_ref, s0_ref, cw_ref, hp_ref, gain_ref,
                       o_ref, s_out_ref, xs_ref, act_ref):
    grp = xp_ref.shape[0]
    cc = GDN_SAMPLE_CHUNK
    gain = gain_ref[...]
    rowmask = (_iota2((cc, 1), 0) >= cc - new_rows).astype(F32)
    seqs = list(range(grp))
    for bi in seqs:
        xs = xs_ref.at[bi]
        xs[0:8, :] = jnp.zeros((8, GDN_CONV_DIM), F32)
        xs[8:8 + cc, :] = xp_ref[bi].astype(F32)
        _activate_qkv(xs, cw_ref, act_ref.at[bi], 8, cc)
    gb = _each(lambda bi: _decay_beta(ab_ref[bi], hp_ref), seqs)
    gcs = _each(lambda x: _cumsum_rows(x[0] * rowmask), gb)
    gcts = _each(lambda gc: gc.T, gcs)
    ge_rows = _each(lambda gc: jnp.exp(gc[cc - 1:cc, :]), gcs)
    betas = _each(lambda x: x[1] * rowmask, gb)
    items = [(bi, hd) for bi in seqs for hd in range(GDN_HEADS)]
    col = lambda which: [act_ref[bi, :, _head_cols(hd)[which]:_head_cols(hd)[which] + LANES] * rowmask
                         for bi, hd in items]
    u, w, qd, kd, qk = _chunk_prep(
        col(0), col(1), col(2),
        [gcs[bi][:, hd:hd + 1] for bi, hd in items], [gcts[bi][hd:hd + 1, :] for bi, hd in items],
        [betas[bi][:, GDN_HEADS + hd:GDN_HEADS + hd + 1] for bi, hd in items])
    qk = [pair[:, half] for pair in qk for half in (slice(0, cc), slice(cc, 2 * cc))]
    o, s_new = _chunk_step([s0_ref[bi, hd] for bi, hd in items], u, w, qd, kd, qk,
                           [ge_rows[bi][:, hd:hd + 1] for bi, hd in items])
    for idx, (bi, hd) in enumerate(items):
        s_out_ref[bi, hd] = s_new[idx]
        oc = slice(hd * GDN_DV, (hd + 1) * GDN_DV)
        _gated_norm_store(o_ref, (bi, slice(None), oc), o[idx], gain, zs_ref[bi, :, oc])


def _gdn_sample(xp, ab, zs, s0, conv_w, head_params, gain, new_rows):
    b = xp.shape[0]
    cc = GDN_SAMPLE_CHUNK
    grp = GDN_SAMPLE_GROUP
    blk3 = lambda w: pl.BlockSpec((grp, cc, w), lambda i: (i, 0, 0))
    sblk = pl.BlockSpec((grp, GDN_HEADS, GDN_DK, GDN_DV), lambda i: (i, 0, 0, 0))
    return pl.pallas_call(
        functools.partial(_gdn_sample_kernel, new_rows),
        grid=(b // grp,),
        in_specs=[blk3(GDN_CONV_DIM), blk3(LANES), blk3(GDN_WIDTH), sblk,
                  pl.BlockSpec((GDN_CONV, GDN_CONV_DIM), lambda i: (0, 0)),
                  pl.BlockSpec((8, LANES), lambda i: (0, 0)),
                  pl.BlockSpec((1, GDN_DV), lambda i: (0, 0))],
        out_specs=[blk3(GDN_WIDTH), sblk],
        out_shape=[jax.ShapeDtypeStruct((b, cc, GDN_WIDTH), BF16),
                   jax.ShapeDtypeStruct((b, GDN_HEADS, GDN_DK, GDN_DV), F32)],
        scratch_shapes=[pltpu.VMEM((grp, cc + 8, GDN_CONV_DIM), F32),
                        pltpu.VMEM((grp, cc, GDN_CONV_DIM), F32)],
        compiler_params=_cparams("parallel"),
        name="gdn_sample",
    )(xp, ab, zs, s0, conv_w, head_params, gain.reshape(1, GDN_DV))


def _sink_attention(q, k, v, mask, sink_col):
    s = _each(lambda qq, kk: jnp.where(mask, _bdot_nt(qq, kk) * (SWA_HEAD_DIM ** -0.5), -jnp.inf), q, k)
    m = _each(lambda ss, sk: jnp.maximum(jnp.max(ss, axis=-1, keepdims=True), sk), s, sink_col)
    p = _each(lambda ss, mm: jnp.exp(ss - mm), s, m)
    denom = _each(lambda pp, sk, mm: jnp.sum(pp, axis=-1, keepdims=True) + jnp.exp(sk - mm), p, sink_col, m)
    return _each(lambda pp, vv, dd: _bdot(pp, vv) / dd, p, v, denom)


def _sink_column(sinks_ref, kv_head, rows_per_head):
    parts = [jnp.full((rows_per_head, 1), sinks_ref[kv_head * SWA_GROUP + g], F32) for g in range(SWA_GROUP)]
    return jnp.concatenate(parts, axis=0)


def _swa_prompt_kernel(sinks_ref, q_ref, kvp_ref, kvc_ref, o_ref):
    wnd = WINDOW
    blk = pl.program_id(1)
    kv = jnp.concatenate([kvp_ref[0], kvc_ref[0]], axis=0)
    cols = SWA_GROUP * wnd
    kj = _iota2((2 * wnd, cols), 0)
    qi = _iota2((2 * wnd, cols), 1) & (wnd - 1)
    dist = qi + wnd - kj
    first_key = jnp.where(blk > 0, 0, wnd)
    mask = (dist >= 0) & (dist <= wnd) & (kj >= first_key)
    kv_heads = list(range(SWA_KV_HEADS))
    head_cols = lambda h: slice(h * SWA_HEAD_DIM, (h + 1) * SWA_HEAD_DIM)
    q_heads = lambda hk: [hk * SWA_GROUP + g for g in range(SWA_GROUP)]
    scale = SWA_HEAD_DIM ** -0.5
    q = [jnp.concatenate([q_ref[0, :, head_cols(h)] for h in q_heads(hk)], axis=0) for hk in kv_heads]
    k = [kv[:, head_cols(hk)] for hk in kv_heads]
    v = [kv[:, SWA_KV_WIDTH + hk * SWA_HEAD_DIM:SWA_KV_WIDTH + (hk + 1) * SWA_HEAD_DIM] for hk in kv_heads]
    sink = [jnp.concatenate([jnp.full((1, wnd), sinks_ref[h], F32) for h in q_heads(hk)], axis=1) for hk in kv_heads]
    s = _each(lambda kk, qq: jnp.where(mask, _bdot_nt(kk, qq) * scale, -jnp.inf), k, q)
    m = _each(lambda ss, sk: jnp.maximum(jnp.max(ss, axis=0, keepdims=True), sk), s, sink)
    p = _each(lambda ss, mm: jnp.exp(ss - mm), s, m)
    denom = _each(lambda pp, sk, mm: jnp.sum(pp, axis=0, keepdims=True) + jnp.exp(sk - mm), p, sink, m)
    ot = _each(lambda vv, pp, dd: _bdot_tn(vv, pp) / dd, v, p, denom)
    for hk in kv_heads:
        for g in range(0, SWA_GROUP, 2):
            pair = jnp.concatenate([ot[hk][:, g * wnd:(g + 1) * wnd], ot[hk][:, (g + 1) * wnd:(g + 2) * wnd]],
                                   axis=0)
            h0 = hk * SWA_GROUP + g
            o_ref[0, :, h0 * SWA_HEAD_DIM:(h0 + 2) * SWA_HEAD_DIM] = pair.T.astype(o_ref.dtype)


def _swa_prompt(q, kv, sinks):
    b, t, _ = q.shape
    return pl.pallas_call(
        _swa_prompt_kernel,
        grid=(b, t // WINDOW),
        in_specs=[pl.BlockSpec(memory_space=pltpu.SMEM),
                  pl.BlockSpec((1, WINDOW, SWA_WIDTH), lambda bi, i: (bi, i, 0)),
                  pl.BlockSpec((1, WINDOW, 2 * SWA_KV_WIDTH), lambda bi, i: (bi, jnp.maximum(i - 1, 0), 0)),
                  pl.BlockSpec((1, WINDOW, 2 * SWA_KV_WIDTH), lambda bi, i: (bi, i, 0))],
        out_specs=pl.BlockSpec((1, WINDOW, SWA_WIDTH), lambda bi, i: (bi, i, 0)),
        out_shape=jax.ShapeDtypeStruct((b, t, SWA_WIDTH), BF16),
        compiler_params=_cparams("parallel", "parallel"),
        name="swa_prompt",
    )(sinks, q, kv, kv)


def _swa_sample_kernel(sinks_ref, q_ref, kvn_ref, kc_ref, vc_ref, o_ref, ko_ref, vo_ref):
    grp, t, _ = q_ref.shape
    wnd = WINDOW
    nk = wnd + BF16_ROWS
    rows = SWA_GROUP * t
    tq = _iota2((rows, nk), 0) & (t - 1)
    kj = _iota2((rows, nk), 1)
    dist = tq + wnd - kj
    mask = (dist >= 0) & (dist <= wnd)
    zpad = jnp.zeros((BF16_ROWS - t, SWA_KV_WIDTH), F32)
    kks, vvs = [], []
    for bi in range(grp):
        kvn = kvn_ref[bi].astype(F32)
        kk = jnp.concatenate([kc_ref[bi], kvn[:, :SWA_KV_WIDTH], zpad], axis=0)
        vv = jnp.concatenate([vc_ref[bi], kvn[:, SWA_KV_WIDTH:], zpad], axis=0)
        ko_ref[bi] = kk[t:t + wnd, :]
        vo_ref[bi] = vv[t:t + wnd, :]
        kks.append(kk)
        vvs.append(vv)
    items = [(bi, hk) for bi in range(grp) for hk in range(SWA_KV_HEADS)]
    head_cols = lambda h: slice(h * SWA_HEAD_DIM, (h + 1) * SWA_HEAD_DIM)
    q_heads = lambda hk: [hk * SWA_GROUP + g for g in range(SWA_GROUP)]
    sink_cols = [_sink_column(sinks_ref, hk, t) for hk in range(SWA_KV_HEADS)]
    o = _sink_attention(
        [jnp.concatenate([q_ref[bi, :, head_cols(h)] for h in q_heads(hk)], axis=0) for bi, hk in items],
        [kks[bi][:, head_cols(hk)] for bi, hk in items], [vvs[bi][:, head_cols(hk)] for bi, hk in items],
        mask, [sink_cols[hk] for bi, hk in items])
    for idx, (bi, hk) in enumerate(items):
        for g, h in enumerate(q_heads(hk)):
            o_ref[bi, :, head_cols(h)] = o[idx][g * t:(g + 1) * t].astype(o_ref.dtype)


def _swa_sample(q, kv_new, k_cache, v_cache, sinks):
    b, t, _ = q.shape
    grp = SWA_SAMPLE_GROUP
    blk = lambda r, w: pl.BlockSpec((grp, r, w), lambda i: (i, 0, 0))
    return pl.pallas_call(
        _swa_sample_kernel,
        grid=(b // grp,),
        in_specs=[pl.BlockSpec(memory_space=pltpu.SMEM),
                  blk(t, SWA_WIDTH), blk(t, 2 * SWA_KV_WIDTH), blk(WINDOW, SWA_KV_WIDTH), blk(WINDOW, SWA_KV_WIDTH)],
        out_specs=[blk(t, SWA_WIDTH), blk(WINDOW, SWA_KV_WIDTH), blk(WINDOW, SWA_KV_WIDTH)],
        out_shape=[jax.ShapeDtypeStruct((b, t, SWA_WIDTH), BF16),
                   jax.ShapeDtypeStruct((b, WINDOW, SWA_KV_WIDTH), F32),
                   jax.ShapeDtypeStruct((b, WINDOW, SWA_KV_WIDTH), F32)],
        compiler_params=_cparams("parallel"),
        name="swa_sample",
    )(sinks, q, kv_new, k_cache, v_cache)


def _route(logits):
    lane = _iota2(logits.shape, 1).astype(F32)
    neg = -jnp.inf

    def first_argmax(vals, valid):
        v = jnp.where(valid, vals, neg)
        m = jnp.max(v, axis=-1, keepdims=True)
        idx = jnp.min(jnp.where(jnp.logical_and(valid, v == m), lane, float(LANES)), axis=-1, keepdims=True)
        return m, idx

    is_group = lane < N_GROUPS
    gmax, gidx = first_argmax(logits, is_group)
    p_group = 1.0 / jnp.sum(jnp.where(is_group, jnp.exp(logits - gmax), 0.0), axis=-1, keepdims=True)
    lo = N_GROUPS + gidx * EXPERTS_PER_GROUP
    in_group = jnp.logical_and(lane >= lo, lane < lo + EXPERTS_PER_GROUP)
    m1, i1 = first_argmax(logits, in_group)
    esum = jnp.sum(jnp.where(in_group, jnp.exp(logits - m1), 0.0), axis=-1, keepdims=True)
    m2, i2 = first_argmax(logits, jnp.logical_and(in_group, lane != i1))
    p1 = 1.0 / esum
    p2 = jnp.exp(m2 - m1) / esum
    tot = p1 + p2
    return i1 - N_GROUPS, i2 - N_GROUPS, p_group * p1 / tot, p_group * p2 / tot


def _post_mixer_kernel(oa_ref, ob_ref, ga_ref, gb_ref, x_ref, gt_ref, sc_ref, sh_ref,
                       wa_ref, wb_ref, wo_ref, gpost_ref, gpre_ref, wr_ref, br_ref, cnt0_ref,
                       x1_ref, h2_ref, rt_ref, cnt_out_ref, *rest):
    cnt_ref = rest[-1]
    if len(rest) == 2:
        rest[0][...] = jnp.zeros_like(rest[0])
    step = pl.program_id(0)

    @pl.when(step == 0)
    def _():
        cnt_ref[...] = cnt0_ref[...]

    merged = (ga_ref[...].astype(F32) * jnp.dot(oa_ref[...], wa_ref[...], preferred_element_type=F32)
              + gb_ref[...].astype(F32) * jnp.dot(ob_ref[...], wb_ref[...], preferred_element_type=F32))
    mix = _bdot(merged, wo_ref[...])
    x1 = x_ref[...] + gt_ref[0] * _rms(mix, gpost_ref[...])
    x1_ref[...] = x1
    h2 = _rms(x1, gpre_ref[...]) * (1.0 + sc_ref[0]) + sh_ref[0]
    _rows_to_tiles(h2_ref, h2)
    h_hi = h2.astype(BF16)
    h_lo = (h2 - h_hi.astype(F32)).astype(BF16)
    part = jnp.dot(h_hi, wr_ref[...], preferred_element_type=F32)
    logits = (part[:, :LANES] + part[:, LANES:]
              + jnp.dot(h_lo, wr_ref[:, :LANES], preferred_element_type=F32) + br_ref[...])
    ia, ib, wa, wb = _route(logits)
    lane = _iota2(logits.shape, 1)
    tm = logits.shape[0]
    lane_f = lane.astype(F32)
    hot_a = (lane_f == ia).astype(F32)
    hot_b = (lane_f == ib).astype(F32)
    hot = hot_a + hot_b
    earlier = (_iota2((tm, tm), 0) > _iota2((tm, tm), 1)).astype(BF16)
    before = jnp.dot(earlier, hot.astype(BF16), preferred_element_type=F32) + cnt_ref[...]
    rank_a = jnp.sum(hot_a * before, axis=-1, keepdims=True)
    rank_b = jnp.sum(hot_b * before, axis=-1, keepdims=True)
    cnt_ref[...] = cnt_ref[...] + jnp.sum(hot, axis=0, keepdims=True)
    cnt_out_ref[...] = cnt_ref[...]
    rt_ref[...] = jnp.where(lane == 0, ia, jnp.where(lane == 1, ib, jnp.where(lane == 2, wa, jnp.where(
        lane == 3, wb, jnp.where(lane == 4, rank_a, jnp.where(lane == 5, rank_b, 0.0))))))


def _post_mixer(oa, ob, ga, gb, x2d, gt, sc, sh, w_a, w_b, w_o, g_post, g_pre, w_rt, b_rt, cnt0, tm,
                zero_rows=0):
    rows = x2d.shape[0]
    n_tiles = rows // tm
    row_blk = lambda w: pl.BlockSpec((tm, w), lambda i: (i, 0))
    full = lambda r, c: pl.BlockSpec((r, c), lambda i: (0, 0))
    out_specs = [row_blk(D_MODEL), pl.BlockSpec((tm * TILE_ROWS, LANES), lambda i: (i, 0)), row_blk(LANES),
                 full(1, LANES)]
    out_shape = [jax.ShapeDtypeStruct((rows, D_MODEL), F32),
                 jax.ShapeDtypeStruct((rows * TILE_ROWS, LANES), F32),
                 jax.ShapeDtypeStruct((rows, LANES), F32),
                 jax.ShapeDtypeStruct((1, LANES), F32)]
    if zero_rows:
        assert zero_rows % (n_tiles * TILE_ROWS) == 0
        out_specs.append(pl.BlockSpec((zero_rows // n_tiles, LANES), lambda i: (i, 0)))
        out_shape.append(jax.ShapeDtypeStruct((zero_rows, LANES), F32))
    return pl.pallas_call(
        _post_mixer_kernel,
        grid=(n_tiles,),
        in_specs=[row_blk(GDN_WIDTH), row_blk(SWA_WIDTH), row_blk(D_MODEL), row_blk(D_MODEL), row_blk(D_MODEL),
                  _mod_spec(gt, n_tiles), _mod_spec(sc, n_tiles), _mod_spec(sh, n_tiles),
                  full(GDN_WIDTH, D_MODEL), full(SWA_WIDTH, D_MODEL), full(D_MODEL, D_MODEL),
                  full(1, D_MODEL), full(1, D_MODEL), full(D_MODEL, 2 * LANES), full(1, LANES), full(1, LANES)],
        out_specs=out_specs,
        out_shape=out_shape,
        scratch_shapes=[pltpu.VMEM((1, LANES), F32)],
        compiler_params=_cparams("arbitrary"),
        name="post_mixer",
    )(oa, ob, ga, gb, x2d, gt, sc, sh, w_a, w_b, w_o,
      g_post.reshape(1, D_MODEL), g_pre.reshape(1, D_MODEL), w_rt, b_rt, cnt0)


TILE_ROWS = D_MODEL // LANES


def _tiles_to_rows(ref, first, rows):
    base = first * TILE_ROWS
    return jnp.concatenate([ref[pl.ds(base + c, rows, stride=TILE_ROWS), :] for c in range(TILE_ROWS)], axis=1)


def _rows_to_tiles(ref, mat):
    rows = mat.shape[0]
    for c in range(TILE_ROWS):
        ref[pl.ds(c, rows, stride=TILE_ROWS), :] = mat[:, c * LANES:(c + 1) * LANES]


def _tile_copy_loop(n, copies, start):
    def body(t, carry):
        for j, cp in enumerate(copies(t)):
            if start:
                cp.start(priority=j % 2)
            else:
                cp.wait()
        return carry

    lax.fori_loop(0, n, body, 0, unroll=8)


def _dispatch_kernel(dest_ref, h_ref, xs_in_hbm, xs_hbm, sem):
    del xs_in_hbm
    i = pl.program_id(0)
    tm = h_ref.shape[0] // TILE_ROWS

    def copies(t):
        src = h_ref.at[pl.ds(pl.multiple_of(t * TILE_ROWS, TILE_ROWS), TILE_ROWS)]
        base = (i * tm + t) * TOP_K
        return [pltpu.make_async_copy(src, xs_hbm.at[dest_ref[base + k]], sem) for k in range(TOP_K)]

    _tile_copy_loop(tm, copies, True)
    _tile_copy_loop(tm, copies, False)


def _dispatch(dest, h_tiles, xs, tm):
    n_tiles = h_tiles.shape[0] // (tm * TILE_ROWS)
    return pl.pallas_call(
        _dispatch_kernel,
        grid_spec=pltpu.PrefetchScalarGridSpec(
            num_scalar_prefetch=1,
            grid=(n_tiles,),
            in_specs=[pl.BlockSpec((tm * TILE_ROWS, LANES), lambda i, d: (i, 0)),
                      pl.BlockSpec(memory_space=pl.ANY)],
            out_specs=pl.BlockSpec(memory_space=pl.ANY),
            scratch_shapes=[pltpu.SemaphoreType.DMA(())]),
        out_shape=jax.ShapeDtypeStruct(xs.shape, xs.dtype),
        input_output_aliases={2: 0},
        compiler_params=_cparams("arbitrary"),
        name="moe_dispatch",
    )(dest, h_tiles, xs)


def _moe_kernel(blk_e_ref, n_used_ref, x_ref, wg_ref, wu_ref, wd_ref, y_ref, wgb, wub, wdb):
    b = pl.program_id(0)
    rows = x_ref.shape[0] // TILE_ROWS
    changed = jnp.logical_or(b == 0, blk_e_ref[b] != blk_e_ref[jnp.maximum(b - 1, 0)])

    @pl.when(changed)
    def _():
        wgb[...] = wg_ref[0].astype(BF16)
        wub[...] = wu_ref[0].astype(BF16)
        wdb[...] = wd_ref[0].astype(BF16)

    @pl.when(b < n_used_ref[0])
    def _():
        x = _tiles_to_rows(x_ref, 0, rows).astype(BF16)
        gate = jnp.dot(x, wgb[...], preferred_element_type=F32)
        up = jnp.dot(x, wub[...], preferred_element_type=F32)
        _rows_to_tiles(y_ref, _bdot(_silu(gate) * up, wdb[...]))

    @pl.when(b >= n_used_ref[0])
    def _():
        y_ref[...] = jnp.zeros_like(y_ref)


def _moe(xs_tiles, blk_e, n_used, w_gate, w_up, w_down):
    n_blocks = blk_e.shape[0]
    rows = MOE_ROWS
    wspec = lambda r, c: pl.BlockSpec((1, r, c), lambda b, be, nu: (be[b], 0, 0))
    xspec = pl.BlockSpec((rows * TILE_ROWS, LANES), lambda b, be, nu: (b, 0))
    return pl.pallas_call(
        _moe_kernel,
        grid_spec=pltpu.PrefetchScalarGridSpec(
            num_scalar_prefetch=2,
            grid=(n_blocks,),
            in_specs=[xspec, wspec(D_MODEL, EXPERT_FF), wspec(D_MODEL, EXPERT_FF), wspec(EXPERT_FF, D_MODEL)],
            out_specs=xspec,
            scratch_shapes=[pltpu.VMEM((D_MODEL, EXPERT_FF), BF16),
                            pltpu.VMEM((D_MODEL, EXPERT_FF), BF16),
                            pltpu.VMEM((EXPERT_FF, D_MODEL), BF16)]),
        out_shape=jax.ShapeDtypeStruct(xs_tiles.shape, F32),
        compiler_params=_cparams("arbitrary"),
        name="moe_experts",
    )(blk_e, n_used, xs_tiles, w_gate, w_up, w_down)


def _combine_kernel(dest_ref, y_hbm, x1_ref, rt_ref, gt_ref, gpost_ref, o_ref, ybuf, sems):
    i = pl.program_id(0)
    n = pl.num_programs(0)
    rows = ybuf.shape[1] // TILE_ROWS
    slot = i % 2

    def gather(step, buf_slot, start):
        def copy(r):
            dst = ybuf.at[buf_slot, pl.ds(pl.multiple_of(r * TILE_ROWS, TILE_ROWS), TILE_ROWS)]
            return pltpu.make_async_copy(y_hbm.at[dest_ref[step * rows + r]], dst, sems.at[buf_slot])

        _tile_copy_loop(rows // 2, lambda t: [copy(2 * t), copy(2 * t + 1)], start)

    @pl.when(i == 0)
    def _():
        gather(0, 0, True)

    @pl.when(i + 1 < n)
    def _():
        gather(i + 1, 1 - slot, True)

    gather(i, slot, False)
    half = rows // 2
    rt = rt_ref[...]
    buf = ybuf.at[slot]
    f = rt[:, 2:3] * _tiles_to_rows(buf, 0, half) + rt[:, 3:4] * _tiles_to_rows(buf, half, half)
    o_ref[...] = x1_ref[...] + gt_ref[0] * _rms(f, gpost_ref[...])


def _combine(dest, yb, x1, rt, gt, g_post, tm):
    rows = x1.shape[0]
    n_tiles = rows // tm
    tiles_per_mod = n_tiles // gt.shape[0]
    return pl.pallas_call(
        _combine_kernel,
        grid_spec=pltpu.PrefetchScalarGridSpec(
            num_scalar_prefetch=1,
            grid=(n_tiles,),
            in_specs=[pl.BlockSpec(memory_space=pl.ANY),
                      pl.BlockSpec((tm, D_MODEL), lambda i, d: (i, 0)),
                      pl.BlockSpec((tm, LANES), lambda i, d: (i, 0)),
                      pl.BlockSpec((1, gt.shape[1], D_MODEL), lambda i, d: (i // tiles_per_mod, 0, 0)),
                      pl.BlockSpec((1, D_MODEL), lambda i, d: (0, 0))],
            out_specs=pl.BlockSpec((tm, D_MODEL), lambda i, d: (i, 0)),
            scratch_shapes=[pltpu.VMEM((2, TOP_K * tm * TILE_ROWS, LANES), F32),
                            pltpu.SemaphoreType.DMA((2,))]),
        out_shape=jax.ShapeDtypeStruct((rows, D_MODEL), F32),
        compiler_params=_cparams("arbitrary"),
        name="moe_combine",
    )(dest, yb, x1, rt, gt, g_post.reshape(1, D_MODEL))


def _dispatch_plan(rt, counts, n_tok, tm):
    n_assign = n_tok * TOP_K
    flat_e = rt[:, :TOP_K].astype(I32).reshape(n_assign)
    rank = rt[:, 4:4 + TOP_K].astype(I32).reshape(n_assign)
    pcounts = (counts + MOE_ROWS - 1) // MOE_ROWS * MOE_ROWS
    pends = jnp.cumsum(pcounts)
    pstarts = pends - pcounts
    experts = jnp.arange(N_EXPERTS, dtype=I32)
    dest = jnp.sum(jnp.where(flat_e[:, None] == experts[None, :], pstarts[None, :], 0), axis=1) + rank
    n_blocks = n_assign // MOE_ROWS + N_EXPERTS
    blk_start = jnp.arange(n_blocks, dtype=I32) * MOE_ROWS
    blk_e = jnp.minimum(jnp.sum(blk_start[:, None] >= pends[None, :], axis=1), N_EXPERTS - 1).astype(I32)
    n_used = (pends[-1] // MOE_ROWS).astype(I32).reshape(1)
    dest_tiles = dest.reshape(n_tok // tm, tm, TOP_K).transpose(0, 2, 1).reshape(-1)
    return blk_e, n_used, dest, dest_tiles


def _prep_in_weight_kernel(w_ref, o_ref):
    a0 = 4 * GDN_QK_WIDTH
    a1 = a0 + 2 * GDN_HEADS
    rows = w_ref.shape[0]
    for c in range(0, a0, PROJ_TILE):
        o_ref[:, c:c + PROJ_TILE] = w_ref[:, c:c + PROJ_TILE].astype(BF16)
    for c in range(a0, _C_AB, PROJ_TILE):
        o_ref[:, c:c + PROJ_TILE] = w_ref[:, c + a1 - a0:c + a1 - a0 + PROJ_TILE].astype(BF16)
    ab = jnp.concatenate([w_ref[:, a0:a1], jnp.zeros((rows, LANES - (a1 - a0)), F32)], axis=1)
    o_ref[:, _C_AB:IN_COLS] = ab.astype(BF16)


def _prep_in_weight(w_in):
    rows = 128
    return pl.pallas_call(
        _prep_in_weight_kernel,
        grid=(D_MODEL // rows,),
        in_specs=[pl.BlockSpec((rows, w_in.shape[1]), lambda i: (i, 0))],
        out_specs=pl.BlockSpec((rows, IN_COLS), lambda i: (i, 0)),
        out_shape=jax.ShapeDtypeStruct((D_MODEL, IN_COLS), BF16),
        compiler_params=_cparams("parallel"),
        name="prep_in_weight",
    )(w_in)


def _head_param_tile(a_log, dt_bias):
    tile = jnp.zeros((8, LANES), F32)
    return tile.at[0, :GDN_HEADS].set(a_log.astype(F32)).at[1, :GDN_HEADS].set(dt_bias.astype(F32))


def _router_weight(w_group, b_group, w_router, b_router):
    w = jnp.zeros((D_MODEL, LANES), F32)
    w = w.at[:, :N_GROUPS].set(w_group).at[:, N_GROUPS:N_GROUPS + N_EXPERTS].set(w_router)
    b = jnp.zeros((1, LANES), F32)
    b = b.at[0, :N_GROUPS].set(b_group).at[0, N_GROUPS:N_GROUPS + N_EXPERTS].set(b_router)
    w_hi = w.astype(BF16)
    w_lo = (w - w_hi.astype(F32)).astype(BF16)
    return jnp.concatenate([w_hi, w_lo], axis=1), b


def kernel(x_prompt, x_sample, state_gdn, state_conv, cache_k_win, cache_v_win, c_prompt, c_sample, w_ada, b_ada, g_mix_pre, g_mix_post, g_ffn_pre, g_ffn_post, w_in, conv_w, a_log, dt_bias, gdn_norm, sinks, w_br_gdn, w_br_swa, w_out, w_group, b_group, w_router, b_router, w_gate, w_up, w_down):
    depth = w_ada.shape[0]
    assert depth == 1, "single-layer trunk"
    bp, tp, _ = x_prompt.shape
    bs, ts, _ = x_sample.shape
    n_p = bp * tp
    n_s = bs * ts
    tm = ROW_TILE
    assert tp % tm == 0 and n_s % tm == 0 and ts >= GDN_CONV - 1 and ts + GDN_CONV - 1 <= GDN_SAMPLE_CHUNK
    assert ts & (ts - 1) == 0 and ts <= BF16_ROWS

    c_all = jnp.concatenate([c_prompt, c_sample], axis=0)
    c_rows = -(-c_all.shape[0] // 8) * 8
    c_all = jnp.pad(c_all, ((0, c_rows - c_all.shape[0]), (0, 0)))
    mod = _adaln(c_all, w_ada[0], b_ada[0])
    mods_p = [m[:bp].reshape(bp, 1, D_MODEL) for m in jnp.split(mod, 6, axis=-1)]
    mods_s = [jnp.repeat(m[bp:bp + bs], ts, axis=0).reshape(n_s // tm, tm, D_MODEL)
              for m in jnp.split(mod, 6, axis=-1)]

    w_prep = _prep_in_weight(w_in[0])
    head_params = _head_param_tile(a_log[0], dt_bias[0])
    w_a, w_b, w_o = w_br_gdn[0].astype(BF16), w_br_swa[0].astype(BF16), w_out[0].astype(BF16)
    w_rt, b_rt = _router_weight(w_group[0], b_group[0], w_router[0], b_router[0])
    sinks0 = sinks[0].astype(F32)

    xp2d = x_prompt.reshape(n_p, D_MODEL)
    sh1, sc1, gt1, sh2, sc2, gt2 = mods_p
    qkv_p, zs_p, qb_p, kvb_p, ga_p, gb_p, ab_p, conv_tail_p = _inproj(
        xp2d, g_mix_pre[0], sc1, sh1, w_prep, tm, conv_w=conv_w[0].astype(F32), n_seq=bp)
    qkv_p3 = qkv_p.reshape(bp, tp, GDN_CONV_DIM)
    u, w, qd, kd, qk, ge = _gdn_prep(qkv_p3, ab_p.reshape(bp, tp, LANES), head_params)
    oa_p, s_prompt = _gdn_scan(u, w, qd, kd, qk, ge, zs_p.reshape(bp, tp, GDN_WIDTH), gdn_norm[0])
    kvb_p3 = kvb_p.reshape(bp, tp, 2 * SWA_KV_WIDTH)
    ob_p = _swa_prompt(qb_p.reshape(bp, tp, SWA_WIDTH), kvb_p3, sinks0)
    n_slots = ((n_p + n_s) * TOP_K // MOE_ROWS + N_EXPERTS) * MOE_ROWS
    x1_p, h2_p, rt_p, cnt_p, xs = _post_mixer(
        oa_p.reshape(n_p, GDN_WIDTH), ob_p.reshape(n_p, SWA_WIDTH), ga_p, gb_p, xp2d, gt1, sc2, sh2,
        w_a, w_b, w_o, g_mix_post[0], g_ffn_pre[0], w_rt, b_rt, jnp.zeros((1, LANES), F32), tm,
        zero_rows=n_slots * TILE_ROWS)

    xs2d = x_sample.reshape(n_s, D_MODEL)
    sh1s, sc1s, gt1s, sh2s, sc2s, gt2s = mods_s
    qkv_s, zs_s, qb_s, kvb_s, ga_s, gb_s, ab_s = _inproj(xs2d, g_mix_pre[0], sc1s, sh1s, w_prep, tm)
    cc = GDN_SAMPLE_CHUNK
    pad_rows = cc - ts - (GDN_CONV - 1)
    qkv_s3 = qkv_s.reshape(bs, ts, GDN_CONV_DIM)
    xp_s = jnp.concatenate([jnp.zeros((bs, pad_rows, GDN_CONV_DIM), BF16), state_conv[0].astype(BF16), qkv_s3],
                           axis=1)
    front = lambda a: jnp.pad(a, ((0, 0), (cc - ts, 0), (0, 0)))
    oa_s16, s_sample = _gdn_sample(xp_s, front(ab_s.reshape(bs, ts, LANES)), front(zs_s.reshape(bs, ts, GDN_WIDTH)),
                                   state_gdn[0].astype(F32), conv_w[0], head_params, gdn_norm[0], ts)
    oa_s = oa_s16[:, cc - ts:, :].reshape(n_s, GDN_WIDTH)
    ob_s, k_new_s, v_new_s = _swa_sample(
        qb_s.reshape(bs, ts, SWA_WIDTH), kvb_s.reshape(bs, ts, 2 * SWA_KV_WIDTH),
        cache_k_win[0].reshape(bs, WINDOW, SWA_KV_WIDTH).astype(F32),
        cache_v_win[0].reshape(bs, WINDOW, SWA_KV_WIDTH).astype(F32), sinks0)
    x1_s, h2_s, rt_s, cnt_all = _post_mixer(
        oa_s, ob_s.reshape(n_s, SWA_WIDTH), ga_s, gb_s, xs2d, gt1s, sc2s, sh2s,
        w_a, w_b, w_o, g_mix_post[0], g_ffn_pre[0], w_rt, b_rt, cnt_p, tm)

    n_all = n_p + n_s
    ct = COMBINE_ROWS
    blk_e, n_used, dest, dest_tiles = _dispatch_plan(
        jnp.concatenate([rt_p[:, :8], rt_s[:, :8]], axis=0), cnt_all[0, :N_EXPERTS].astype(I32), n_all, ct)
    assert n_slots == blk_e.shape[0] * MOE_ROWS
    xs = _dispatch(dest[:n_p * TOP_K], h2_p, xs.reshape(n_slots, TILE_ROWS, LANES), tm)
    xs = _dispatch(dest[n_p * TOP_K:], h2_s, xs, tm)
    yb = _moe(xs.reshape(n_slots * TILE_ROWS, LANES), blk_e, n_used, w_gate[0], w_up[0], w_down[0])
    yb = yb.reshape(n_slots, TILE_ROWS, LANES)
    y_p = _combine(dest_tiles[:n_p * TOP_K], yb, x1_p, rt_p, gt2, g_ffn_post[0], ct)
    y_s = _combine(dest_tiles[n_p * TOP_K:], yb, x1_s, rt_s,
                   gt2s.reshape(n_s // ct, ct, D_MODEL), g_ffn_post[0], ct)

    f32 = lambda a: a.astype(F32)
    kv_tail = kvb_p3[:, tp - WINDOW:, :]
    kv_heads = lambda a: f32(a).reshape(a.shape[0], WINDOW, SWA_KV_HEADS, SWA_HEAD_DIM)[None]
    return (y_p.reshape(bp, tp, D_MODEL), y_s.reshape(bs, ts, D_MODEL),
            s_prompt[None], conv_tail_p[:, 8 - (GDN_CONV - 1):, :][None],
            kv_heads(kv_tail[:, :, :SWA_KV_WIDTH]), kv_heads(kv_tail[:, :, SWA_KV_WIDTH:]),
            s_sample[None], f32(qkv_s3[:, ts - (GDN_CONV - 1):, :])[None],
            kv_heads(k_new_s), kv_heads(v_new_s))
```

```python
import functools

import jax
import jax.numpy as jnp
from jax import lax
from jax.experimental import pallas as pl
from jax.experimental.pallas import tpu as pltpu

F32 = jnp.float32
BF16 = jnp.bfloat16
I32 = jnp.int32

D_MODEL = 1024
NORM_EPS = 1e-6
GDN_HEADS = 8
GDN_DK = 128
GDN_DV = 128
GDN_CONV = 4
GDN_CHUNK = 64
GDN_QK_WIDTH = GDN_HEADS * GDN_DK
GDN_WIDTH = GDN_HEADS * GDN_DV
GDN_CONV_DIM = 2 * GDN_QK_WIDTH + GDN_WIDTH
SWA_Q_HEADS = 16
SWA_KV_HEADS = 4
SWA_HEAD_DIM = 64
SWA_GROUP = SWA_Q_HEADS // SWA_KV_HEADS
SWA_WIDTH = SWA_Q_HEADS * SWA_HEAD_DIM
SWA_KV_WIDTH = SWA_KV_HEADS * SWA_HEAD_DIM
WINDOW = 128
N_GROUPS = 4
EXPERTS_PER_GROUP = 8
N_EXPERTS = N_GROUPS * EXPERTS_PER_GROUP
TOP_K = 2
EXPERT_FF = 512

LANES = 128
BF16_ROWS = 16
VMEM_LIMIT = 56 * 1024 * 1024

_C_QKV = 0
_C_Z = _C_QKV + GDN_CONV_DIM
_C_QB = _C_Z + GDN_WIDTH
_C_KVB = _C_QB + SWA_WIDTH
_C_GA = _C_KVB + 2 * SWA_KV_WIDTH
_C_GB = _C_GA + D_MODEL
_C_AB = _C_GB + D_MODEL
IN_COLS = _C_AB + LANES
PROJ_TILE = 512

ROW_TILE = 512
INPROJ_ROWS = 512
GDN_PREP_ROWS = 256
GDN_SCAN_ROWS = 512
GDN_SAMPLE_CHUNK = 16
GDN_SAMPLE_GROUP = 4
SWA_SAMPLE_GROUP = 8
SWA_BLOCKS_PER_STEP = 2
MOE_ROWS = 512
COMBINE_ROWS = 512
DISPATCH_ROWS = 1024


def _cparams(*sem):
    return pltpu.CompilerParams(dimension_semantics=sem, vmem_limit_bytes=VMEM_LIMIT)


def _bdot(a, b):
    return jnp.dot(a.astype(BF16), b.astype(BF16), preferred_element_type=F32)


def _bdot_nt(a, b):
    return lax.dot_general(a.astype(BF16), b.astype(BF16), (((1,), (1,)), ((), ())),
                           preferred_element_type=F32)


def _bdot_tn(a, b):
    return lax.dot_general(a.astype(BF16), b.astype(BF16), (((0,), (0,)), ((), ())),
                           preferred_element_type=F32)


def _sigmoid(x):
    return 1.0 / (1.0 + jnp.exp(-x))


def _silu(x):
    return x * _sigmoid(x)


def _rms(x, gain):
    return x * lax.rsqrt(jnp.mean(x * x, axis=-1, keepdims=True) + NORM_EPS) * gain


def _iota2(shape, dim):
    return lax.broadcasted_iota(I32, shape, dim)


def _adaln_kernel(c_ref, w_ref, b_ref, o_ref):
    o_ref[...] = _bdot(_silu(c_ref[...]), w_ref[...]) + b_ref[...]


def _adaln(c_all, w_ada, b_ada):
    rows = c_all.shape[0]
    n_out = w_ada.shape[1]
    tn = D_MODEL
    return pl.pallas_call(
        _adaln_kernel,
        grid=(n_out // tn,),
        in_specs=[pl.BlockSpec((rows, D_MODEL), lambda j: (0, 0)),
                  pl.BlockSpec((D_MODEL, tn), lambda j: (0, j)),
                  pl.BlockSpec((1, tn), lambda j: (0, j))],
        out_specs=pl.BlockSpec((rows, tn), lambda j: (0, j)),
        out_shape=jax.ShapeDtypeStruct((rows, n_out), F32),
        compiler_params=_cparams("arbitrary"),
        name="adaln",
    )(c_all, w_ada, b_ada.reshape(1, n_out))


def _inproj_kernel(tiles_per_seq, x_ref, g_ref, sc_ref, sh_ref, w_ref, *rest):
    if tiles_per_seq:
        cw_ref, qkv_ref, z_ref, qb_ref, kvb_ref, ga_ref, gb_ref, ab_ref, tail_ref, carry_ref = rest
    else:
        qkv_ref, z_ref, qb_ref, kvb_ref, ga_ref, gb_ref, ab_ref = rest
    tm = x_ref.shape[0]
    h = (_rms(x_ref[...], g_ref[...]) * (1.0 + sc_ref[0]) + sh_ref[0]).astype(BF16)

    def fill(ref, c0, width, fn):
        step = min(PROJ_TILE, width)
        for c in range(0, width, step):
            acc = jnp.dot(h, w_ref[:, c0 + c:c0 + c + step], preferred_element_type=F32)
            ref[:, c:c + step] = fn(acc, c, step).astype(ref.dtype)

    def conv_act(acc, c, step):
        cols = slice(c, c + step)
        seq_start = pl.program_id(0) % tiles_per_seq == 0
        prev = jnp.where(seq_start, 0.0, carry_ref[:, cols])
        last = acc[tm - 8:tm]
        carry_ref[:, cols] = last
        tail_ref[0, :, cols] = last
        ext = jnp.concatenate([prev, acc], axis=0)
        y = cw_ref[GDN_CONV - 1:GDN_CONV, cols] * acc
        for j in range(GDN_CONV - 1):
            y = y + cw_ref[j:j + 1, cols] * ext[8 - (GDN_CONV - 1) + j:8 - (GDN_CONV - 1) + j + tm]
        y = _silu(y)
        if c >= 2 * GDN_QK_WIDTH:
            return y
        scale = GDN_DK ** -0.5 if c < GDN_QK_WIDTH else 1.0
        heads = [_l2n(y[:, d:d + GDN_DK]) * scale for d in range(0, step, GDN_DK)]
        return jnp.concatenate(heads, axis=1)

    ident = lambda v, c, step: v
    silu = lambda v, c, step: _silu(v)
    sigmoid = lambda v, c, step: _sigmoid(v)
    fill(qkv_ref, _C_QKV, GDN_CONV_DIM, conv_act if tiles_per_seq else ident)
    fill(z_ref, _C_Z, GDN_WIDTH, silu)
    fill(qb_ref, _C_QB, SWA_WIDTH, ident)
    fill(kvb_ref, _C_KVB, 2 * SWA_KV_WIDTH, ident)
    fill(ga_ref, _C_GA, D_MODEL, sigmoid)
    fill(gb_ref, _C_GB, D_MODEL, sigmoid)
    fill(ab_ref, _C_AB, LANES, ident)


def _mod_spec(mod, n_tiles):
    tiles_per_mod = n_tiles // mod.shape[0]
    return pl.BlockSpec((1, mod.shape[1], D_MODEL), lambda i: (i // tiles_per_mod, 0, 0))


def _inproj(x2d, gain, sc, sh, w_prep, tm, conv_w=None, n_seq=0):
    rows = x2d.shape[0]
    n_tiles = rows // tm
    widths = (GDN_CONV_DIM, GDN_WIDTH, SWA_WIDTH, 2 * SWA_KV_WIDTH, D_MODEL, D_MODEL, LANES)
    dtypes = (BF16, BF16, BF16, BF16, BF16, BF16, F32)
    in_specs = [pl.BlockSpec((tm, D_MODEL), lambda i: (i, 0)),
                pl.BlockSpec((1, D_MODEL), lambda i: (0, 0)),
                _mod_spec(sc, n_tiles), _mod_spec(sh, n_tiles),
                pl.BlockSpec((D_MODEL, IN_COLS), lambda i: (0, 0))]
    out_specs = [pl.BlockSpec((tm, w), lambda i: (i, 0)) for w in widths]
    out_shape = [jax.ShapeDtypeStruct((rows, w), dt) for w, dt in zip(widths, dtypes)]
    args = [x2d, gain.reshape(1, D_MODEL), sc, sh, w_prep]
    scratch = []
    tiles_per_seq = 0
    if conv_w is not None:
        tiles_per_seq = n_tiles // n_seq
        in_specs.append(pl.BlockSpec((GDN_CONV, GDN_CONV_DIM), lambda i: (0, 0)))
        out_specs.append(pl.BlockSpec((1, 8, GDN_CONV_DIM), lambda i: (i // tiles_per_seq, 0, 0)))
        out_shape.append(jax.ShapeDtypeStruct((n_seq, 8, GDN_CONV_DIM), F32))
        args.append(conv_w)
        scratch.append(pltpu.VMEM((8, GDN_CONV_DIM), F32))
    return pl.pallas_call(
        functools.partial(_inproj_kernel, tiles_per_seq),
        grid=(n_tiles,),
        in_specs=in_specs,
        out_specs=out_specs,
        out_shape=out_shape,
        scratch_shapes=scratch,
        compiler_params=_cparams("arbitrary"),
        name="inproj",
    )(*args)


def _cumsum_rows(g):
    c = g.shape[0]
    tril = (_iota2((c, c), 0) >= _iota2((c, c), 1)).astype(BF16)
    hi = g.astype(BF16)
    r1 = g - hi.astype(F32)
    mid = r1.astype(BF16)
    lo = (r1 - mid.astype(F32)).astype(BF16)
    dot = lambda p: jnp.dot(tril, p, preferred_element_type=F32)
    return dot(hi) + dot(mid) + dot(lo)


def _each(fn, *lists):
    return [fn(*args) for args in zip(*lists)]


def _pair_blockdiag(m, left):
    return jnp.concatenate([jnp.where(left, m, 0.0), jnp.where(left, 0.0, m)], axis=0)


def _unit_lower_inverse_offset(a_list, ii, jj, left):
    c = a_list[0].shape[0]

    def same_block(shift):
        return lax.shift_right_logical(ii, shift) == lax.shift_right_logical(jj, shift)

    base = same_block(1)
    n_list = _each(lambda a: jnp.where(base, -a, 0.0), a_list)
    shift = 1
    while (1 << shift) < c:
        outer, inner = same_block(shift + 1), same_block(shift)
        off_list = _each(lambda a: jnp.where(outer, jnp.where(inner, 0.0, a), 0.0), a_list)
        x_list = _each(lambda off, n: off + _bdot(off, _pair_blockdiag(n, left)), off_list, n_list)
        n_list = _each(lambda n, x: n - x - _bdot(n, _pair_blockdiag(x, left)), n_list, x_list)
        shift += 1
    return n_list


def _chunk_prep(q, k, v, gcol, grow, bcol):
    c = q[0].shape[0]
    assert len(q) % 2 == 0
    ii = _iota2((c, 2 * c), 0)
    lane = _iota2((c, 2 * c), 1)
    left = lane < c
    jj = lane & (c - 1)
    causal = ii >= jj
    strict = ii > jj
    first, second = slice(0, None, 2), slice(1, None, 2)
    kb = _each(lambda kk, b: kk * b, k, bcol)
    both = _each(lambda qa, ka, qb, kbb, x, y: _bdot_nt(jnp.concatenate([qa, ka, qb, kbb], axis=0),
                                                       jnp.concatenate([x, y], axis=0)),
                 q[first], kb[first], q[second], kb[second], k[first], k[second])
    decay = _each(lambda ga, gb, ra, rb: jnp.where(causal, jnp.exp(jnp.where(
        causal, jnp.where(left, ga, gb) - jnp.concatenate([ra, rb], axis=1), 0.0)), 0.0),
        gcol[first], gcol[second], grow[first], grow[second])
    qk = _each(lambda bo, d: jnp.where(left, bo[0:c], bo[2 * c:3 * c]) * d, both, decay)
    a = _each(lambda bo, d: jnp.where(strict, jnp.where(left, bo[c:2 * c], bo[3 * c:4 * c]) * d, 0.0), both, decay)
    n = _unit_lower_inverse_offset(a, ii, jj, left)
    eg = _each(jnp.exp, gcol)
    rhs = _each(lambda vv, b, kbb, e: jnp.concatenate([vv * b, kbb * e], axis=1), v, bcol, kb, eg)
    uw = _each(lambda ra, rb, nn: (lambda r: r + _bdot(_pair_blockdiag(nn, left), r))(
        jnp.concatenate([ra, rb], axis=0)), rhs[first], rhs[second], n)
    uw = [x[half] for x in uw for half in (slice(0, c), slice(c, 2 * c))]
    u = [x[:, :GDN_DV] for x in uw]
    w = [x[:, GDN_DV:] for x in uw]
    qd = _each(lambda qq, e: qq * e, q, eg)
    kd = _each(lambda kk, gc: kk * jnp.exp(gc[c - 1:c, :] - gc), k, gcol)
    return u, w, qd, kd, qk


def _chunk_step(s, u, w, qd, kd, qk, ge):
    c = u[0].shape[0]
    both = _each(lambda ww, qq, ss: _bdot(jnp.concatenate([ww, qq], axis=0), ss), w, qd, s)
    v_new = _each(lambda uu, bo: uu.astype(F32) - bo[:c], u, both)
    o = _each(lambda bo, m, vn: bo[c:] + _bdot(m, vn), both, qk, v_new)
    s_new = _each(lambda ss, g, kk, vn: ss * g + _bdot_tn(kk, vn), s, ge, kd, v_new)
    return o, s_new


def _conv_act(xp_ref, cw_ref, r0, rows, c0):
    cols = slice(c0, c0 + LANES)
    acc = cw_ref[3:4, cols] * xp_ref[r0:r0 + rows, cols]
    for j in range(GDN_CONV - 1):
        acc = acc + cw_ref[j:j + 1, cols] * xp_ref[r0 - 3 + j:r0 - 3 + j + rows, cols]
    return _silu(acc)


def _l2n(x):
    return x * lax.rsqrt(jnp.sum(x * x, axis=-1, keepdims=True) + NORM_EPS)


def _softplus(x):
    return jnp.maximum(x, 0.0) + jnp.log1p(jnp.exp(-jnp.abs(x)))


def _head_cols(hd):
    return (hd * GDN_DK, GDN_QK_WIDTH + hd * GDN_DK, 2 * GDN_QK_WIDTH + hd * GDN_DV)


def _activate_qkv(xp_ref, cw_ref, act_ref, r0, rows):
    for hd in range(GDN_HEADS):
        cq, ck, cv = _head_cols(hd)
        act_ref[0:rows, cq:cq + LANES] = _l2n(_conv_act(xp_ref, cw_ref, r0, rows, cq)) * (GDN_DK ** -0.5)
        act_ref[0:rows, ck:ck + LANES] = _l2n(_conv_act(xp_ref, cw_ref, r0, rows, ck))
        act_ref[0:rows, cv:cv + LANES] = _conv_act(xp_ref, cw_ref, r0, rows, cv)


def _decay_beta(ab, hp_ref):
    g = -jnp.exp(hp_ref[0:1, :]) * _softplus(ab + hp_ref[1:2, :])
    return g, _sigmoid(ab)


def _gdn_prep_kernel(act_ref, ab_ref, hp_ref, u_ref, w_ref, qd_ref, kd_ref, qk_ref, ge_ref):
    tb = act_ref.shape[1]
    cc = GDN_CHUNK
    g_all, beta_all = _decay_beta(ab_ref[0], hp_ref)

    chunks = [slice(ci * cc, (ci + 1) * cc) for ci in range(tb // cc)]
    gcs = _each(lambda rows: _cumsum_rows(g_all[rows, :]), chunks)
    gcts = _each(lambda gc: gc.T, gcs)
    for ci, gc in enumerate(gcs):
        ge_ref[0, ci] = jnp.exp(gc[cc - 1:cc, :])
    items = [(ci, hd) for ci in range(len(chunks)) for hd in range(GDN_HEADS)]
    col = lambda which: [act_ref[0, chunks[ci], _head_cols(hd)[which]:_head_cols(hd)[which] + LANES].astype(F32)
                         for ci, hd in items]
    u, w, qd, kd, qk = _chunk_prep(
        col(0), col(1), col(2),
        [gcs[ci][:, hd:hd + 1] for ci, hd in items], [gcts[ci][hd:hd + 1, :] for ci, hd in items],
        [beta_all[chunks[ci], GDN_HEADS + hd:GDN_HEADS + hd + 1] for ci, hd in items])
    for idx, (ci, hd) in enumerate(items):
        rows = chunks[ci]
        oc = slice(hd * GDN_DV, (hd + 1) * GDN_DV)
        u_ref[0, rows, oc] = u[idx].astype(BF16)
        w_ref[0, rows, oc] = w[idx].astype(BF16)
        qd_ref[0, rows, oc] = qd[idx].astype(BF16)
        kd_ref[0, rows, oc] = kd[idx].astype(BF16)
        if hd % 2 == 0:
            qk_ref[0, rows, hd * cc:(hd + 2) * cc] = qk[idx // 2].astype(BF16)


def _gdn_prep(qkv, ab, head_params):
    b, t, _ = qkv.shape
    tb = min(GDN_PREP_ROWS, t)
    nch = tb // GDN_CHUNK
    blk = lambda w: pl.BlockSpec((1, tb, w), lambda bi, i: (bi, i, 0))
    out_shapes = [jax.ShapeDtypeStruct((b, t, GDN_WIDTH), BF16)] * 4 + [
        jax.ShapeDtypeStruct((b, t, GDN_HEADS * GDN_CHUNK), BF16),
        jax.ShapeDtypeStruct((b, t // GDN_CHUNK, 1, LANES), F32)]
    return pl.pallas_call(
        _gdn_prep_kernel,
        grid=(b, t // tb),
        in_specs=[blk(GDN_CONV_DIM), blk(LANES), pl.BlockSpec((8, LANES), lambda bi, i: (0, 0))],
        out_specs=[blk(GDN_WIDTH)] * 4 + [
            blk(GDN_HEADS * GDN_CHUNK),
            pl.BlockSpec((1, nch, 1, LANES), lambda bi, i: (bi, i, 0, 0))],
        out_shape=out_shapes,
        compiler_params=_cparams("parallel", "parallel"),
        name="gdn_prep",
    )(qkv, ab, head_params)


def _gated_norm_store(o_ref, idx, o, gain, zs):
    o_ref[idx] = (_rms(o, gain) * zs.astype(F32)).astype(o_ref.dtype)


def _gdn_scan_kernel(u_ref, w_ref, qd_ref, kd_ref, qk_ref, ge_ref, zs_ref, gain_ref,
                     o_ref, s_out_ref, s_ref):
    nb, tb, _ = u_ref.shape
    cc = GDN_CHUNK
    step = pl.program_id(0)

    @pl.when(step == 0)
    def _():
        s_ref[...] = jnp.zeros_like(s_ref)

    gain = gain_ref[...]

    def chunk_body(ci, carry):
        rows = pl.ds(pl.multiple_of(ci * cc, cc), cc)
        items = [(bi, hd) for bi in range(nb) for hd in range(GDN_HEADS)]
        oc = lambda hd: slice(hd * GDN_DV, (hd + 1) * GDN_DV)
        ge_rows = [ge_ref[bi, ci] for bi in range(nb)]
        o, s_new = _chunk_step(
            [s_ref[bi * GDN_HEADS + hd] for bi, hd in items],
            [u_ref[bi, rows, oc(hd)] for bi, hd in items], [w_ref[bi, rows, oc(hd)] for bi, hd in items],
            [qd_ref[bi, rows, oc(hd)] for bi, hd in items], [kd_ref[bi, rows, oc(hd)] for bi, hd in items],
            [qk_ref[bi, rows, hd * cc:(hd + 1) * cc] for bi, hd in items],
            [ge_rows[bi][:, hd:hd + 1] for bi, hd in items])
        for idx, (bi, hd) in enumerate(items):
            s_ref[bi * GDN_HEADS + hd] = s_new[idx]
            _gated_norm_store(o_ref, (bi, rows, oc(hd)), o[idx], gain, zs_ref[bi, rows, oc(hd)])
        return carry

    lax.fori_loop(0, tb // cc, chunk_body, 0)

    @pl.when(step == pl.num_programs(0) - 1)
    def _():
        s_out_ref[...] = s_ref[...]


def _gdn_scan(u, w, qd, kd, qk, ge, zs, gain):
    b, t, _ = u.shape
    tb = min(GDN_SCAN_ROWS, t)
    nch = tb // GDN_CHUNK
    blk = lambda wd: pl.BlockSpec((b, tb, wd), lambda i: (0, i, 0))
    o, s = pl.pallas_call(
        _gdn_scan_kernel,
        grid=(t // tb,),
        in_specs=[blk(GDN_WIDTH)] * 4 + [
            blk(GDN_HEADS * GDN_CHUNK),
            pl.BlockSpec((b, nch, 1, LANES), lambda i: (0, i, 0, 0)),
            blk(GDN_WIDTH),
            pl.BlockSpec((1, GDN_DV), lambda i: (0, 0))],
        out_specs=[blk(GDN_WIDTH),
                   pl.BlockSpec((b * GDN_HEADS, GDN_DK, GDN_DV), lambda i: (0, 0, 0))],
        out_shape=[jax.ShapeDtypeStruct((b, t, GDN_WIDTH), BF16),
                   jax.ShapeDtypeStruct((b * GDN_HEADS, GDN_DK, GDN_DV), F32)],
        scratch_shapes=[pltpu.VMEM((b * GDN_HEADS, GDN_DK, GDN_DV), F32)],
        compiler_params=_cparams("arbitrary"),
        name="gdn_scan",
    )(u, w, qd, kd, qk, ge, zs, gain.reshape(1, GDN_DV))
    return o, s.reshape(b, GDN_HEADS, GDN_DK, GDN_DV)


def _gdn_sample_kernel(new_rows, xp_ref, ab_ref, zs_ref, s0_ref, cw_ref, hp_ref, gain_ref,
                       o_ref, s_out_ref, xs_ref, act_ref):
    grp = xp_ref.shape[0]
    cc = GDN_SAMPLE_CHUNK
    gain = gain_ref[...]
    rowmask = (_iota2((cc, 1), 0) >= cc - new_rows).astype(F32)
    seqs = list(range(grp))
    for bi in seqs:
        xs = xs_ref.at[bi]
        xs[0:8, :] = jnp.zeros((8, GDN_CONV_DIM), F32)
        xs[8:8 + cc, :] = xp_ref[bi].astype(F32)
        _activate_qkv(xs, cw_ref, act_ref.at[bi], 8, cc)
    gb = _each(lambda bi: _decay_beta(ab_ref[bi], hp_ref), seqs)
    gcs = _each(lambda x: _cumsum_rows(x[0] * rowmask), gb)
    gcts = _each(lambda gc: gc.T, gcs)
    ge_rows = _each(lambda gc: jnp.exp(gc[cc - 1:cc, :]), gcs)
    betas = _each(lambda x: x[1] * rowmask, gb)
    items = [(bi, hd) for bi in seqs for hd in range(GDN_HEADS)]
    col = lambda which: [act_ref[bi, :, _head_cols(hd)[which]:_head_cols(hd)[which] + LANES] * rowmask
                         for bi, hd in items]
    u, w, qd, kd, qk = _chunk_prep(
        col(0), col(1), col(2),
        [gcs[bi][:, hd:hd + 1] for bi, hd in items], [gcts[bi][hd:hd + 1, :] for bi, hd in items],
        [betas[bi][:, GDN_HEADS + hd:GDN_HEADS + hd + 1] for bi, hd in items])
    qk = [pair[:, half] for pair in qk for half in (slice(0, cc), slice(cc, 2 * cc))]
    o, s_new = _chunk_step([s0_ref[bi, hd] for bi, hd in items], u, w, qd, kd, qk,
                           [ge_rows[bi][:, hd:hd + 1] for bi, hd in items])
    for idx, (bi, hd) in enumerate(items):
        s_out_ref[bi, hd] = s_new[idx]
        oc = slice(hd * GDN_DV, (hd + 1) * GDN_DV)
        _gated_norm_store(o_ref, (bi, slice(None), oc), o[idx], gain, zs_ref[bi, :, oc])


def _gdn_sample(xp, ab, zs, s0, conv_w, head_params, gain, new_rows):
    b = xp.shape[0]
    cc = GDN_SAMPLE_CHUNK
    grp = GDN_SAMPLE_GROUP
    blk3 = lambda w: pl.BlockSpec((grp, cc, w), lambda i: (i, 0, 0))
    sblk = pl.BlockSpec((grp, GDN_HEADS, GDN_DK, GDN_DV), lambda i: (i, 0, 0, 0))
    return pl.pallas_call(
        functools.partial(_gdn_sample_kernel, new_rows),
        grid=(b // grp,),
        in_specs=[blk3(GDN_CONV_DIM), blk3(LANES), blk3(GDN_WIDTH), sblk,
                  pl.BlockSpec((GDN_CONV, GDN_CONV_DIM), lambda i: (0, 0)),
                  pl.BlockSpec((8, LANES), lambda i: (0, 0)),
                  pl.BlockSpec((1, GDN_DV), lambda i: (0, 0))],
        out_specs=[blk3(GDN_WIDTH), sblk],
        out_shape=[jax.ShapeDtypeStruct((b, cc, GDN_WIDTH), BF16),
                   jax.ShapeDtypeStruct((b, GDN_HEADS, GDN_DK, GDN_DV), F32)],
        scratch_shapes=[pltpu.VMEM((grp, cc + 8, GDN_CONV_DIM), F32),
                        pltpu.VMEM((grp, cc, GDN_CONV_DIM), F32)],
        compiler_params=_cparams("parallel"),
        name="gdn_sample",
    )(xp, ab, zs, s0, conv_w, head_params, gain.reshape(1, GDN_DV))


def _sink_attention(q, k, v, mask, sink_col):
    s = _each(lambda qq, kk: jnp.where(mask, _bdot_nt(qq, kk) * (SWA_HEAD_DIM ** -0.5), -jnp.inf), q, k)
    m = _each(lambda ss, sk: jnp.maximum(jnp.max(ss, axis=-1, keepdims=True), sk), s, sink_col)
    p = _each(lambda ss, mm: jnp.exp(ss - mm), s, m)
    denom = _each(lambda pp, sk, mm: jnp.sum(pp, axis=-1, keepdims=True) + jnp.exp(sk - mm), p, sink_col, m)
    return _each(lambda pp, vv, dd: _bdot(pp, vv) / dd, p, v, denom)


def _sink_column(sinks_ref, kv_head, rows_per_head):
    parts = [jnp.full((rows_per_head, 1), sinks_ref[kv_head * SWA_GROUP + g], F32) for g in range(SWA_GROUP)]
    return jnp.concatenate(parts, axis=0)


def _swa_prompt_kernel(sinks_ref, q_ref, kvp_ref, kvc_ref, o_ref):
    wnd = WINDOW
    nblk = q_ref.shape[1] // wnd
    step = pl.program_id(1)
    kv = jnp.concatenate([kvp_ref[0], kvc_ref[0]], axis=0)
    cols = SWA_GROUP * wnd
    kj = _iota2((2 * wnd, cols), 0)
    qi = _iota2((2 * wnd, cols), 1) & (wnd - 1)
    dist = qi + wnd - kj
    band = (dist >= 0) & (dist <= wnd)
    first_key = jnp.where(step > 0, 0, wnd)
    masks = [band & (kj >= first_key)] + [band] * (nblk - 1)
    head_cols = lambda h: slice(h * SWA_HEAD_DIM, (h + 1) * SWA_HEAD_DIM)
    q_heads = lambda hk: [hk * SWA_GROUP + g for g in range(SWA_GROUP)]
    scale = SWA_HEAD_DIM ** -0.5
    items = [(j, hk) for j in range(nblk) for hk in range(SWA_KV_HEADS)]
    q_rows = lambda j: slice(j * wnd, (j + 1) * wnd)
    k_rows = lambda j: slice(j * wnd, (j + 2) * wnd)
    q = [jnp.concatenate([q_ref[0, q_rows(j), head_cols(h)] for h in q_heads(hk)], axis=0) for j, hk in items]
    k = [kv[k_rows(j), head_cols(hk)] for j, hk in items]
    v = [kv[k_rows(j), SWA_KV_WIDTH + hk * SWA_HEAD_DIM:SWA_KV_WIDTH + (hk + 1) * SWA_HEAD_DIM] for j, hk in items]
    sinks_kv = [jnp.concatenate([jnp.full((1, wnd), sinks_ref[h], F32) for h in q_heads(hk)], axis=1)
                for hk in range(SWA_KV_HEADS)]
    sink = [sinks_kv[hk] for j, hk in items]
    mask = [masks[j] for j, hk in items]
    s = _each(lambda kk, qq, mk: jnp.where(mk, _bdot_nt(kk, qq) * scale, -jnp.inf), k, q, mask)
    m = _each(lambda ss, sk: jnp.maximum(jnp.max(ss, axis=0, keepdims=True), sk), s, sink)
    p = _each(lambda ss, mm: jnp.exp(ss - mm), s, m)
    denom = _each(lambda pp, sk, mm: jnp.sum(pp, axis=0, keepdims=True) + jnp.exp(sk - mm), p, sink, m)
    ot = _each(lambda vv, pp, dd: _bdot_tn(vv, pp) / dd, v, p, denom)
    for idx, (j, hk) in enumerate(items):
        for g in range(0, SWA_GROUP, 2):
            pair = jnp.concatenate([ot[idx][:, g * wnd:(g + 1) * wnd], ot[idx][:, (g + 1) * wnd:(g + 2) * wnd]],
                                   axis=0)
            h0 = hk * SWA_GROUP + g
            o_ref[0, q_rows(j), h0 * SWA_HEAD_DIM:(h0 + 2) * SWA_HEAD_DIM] = pair.T.astype(o_ref.dtype)


def _swa_prompt(q, kv, sinks):
    b, t, _ = q.shape
    nblk = SWA_BLOCKS_PER_STEP
    rows = nblk * WINDOW
    return pl.pallas_call(
        _swa_prompt_kernel,
        grid=(b, t // rows),
        in_specs=[pl.BlockSpec(memory_space=pltpu.SMEM),
                  pl.BlockSpec((1, rows, SWA_WIDTH), lambda bi, i: (bi, i, 0)),
                  pl.BlockSpec((1, WINDOW, 2 * SWA_KV_WIDTH), lambda bi, i: (bi, jnp.maximum(i * nblk - 1, 0), 0)),
                  pl.BlockSpec((1, rows, 2 * SWA_KV_WIDTH), lambda bi, i: (bi, i, 0))],
        out_specs=pl.BlockSpec((1, rows, SWA_WIDTH), lambda bi, i: (bi, i, 0)),
        out_shape=jax.ShapeDtypeStruct((b, t, SWA_WIDTH), BF16),
        compiler_params=_cparams("parallel", "parallel"),
        name="swa_prompt",
    )(sinks, q, kv, kv)


def _swa_sample_kernel(sinks_ref, q_ref, kvn_ref, kc_ref, vc_ref, o_ref, ko_ref, vo_ref):
    grp, t, _ = q_ref.shape
    wnd = WINDOW
    nk = wnd + BF16_ROWS
    rows = SWA_GROUP * t
    tq = _iota2((rows, nk), 0) & (t - 1)
    kj = _iota2((rows, nk), 1)
    dist = tq + wnd - kj
    mask = (dist >= 0) & (dist <= wnd)
    zpad = jnp.zeros((BF16_ROWS - t, SWA_KV_WIDTH), F32)
    kks, vvs = [], []
    for bi in range(grp):
        kvn = kvn_ref[bi].astype(F32)
        kk = jnp.concatenate([kc_ref[bi], kvn[:, :SWA_KV_WIDTH], zpad], axis=0)
        vv = jnp.concatenate([vc_ref[bi], kvn[:, SWA_KV_WIDTH:], zpad], axis=0)
        ko_ref[bi] = kk[t:t + wnd, :]
        vo_ref[bi] = vv[t:t + wnd, :]
        kks.append(kk)
        vvs.append(vv)
    items = [(bi, hk) for bi in range(grp) for hk in range(SWA_KV_HEADS)]
    head_cols = lambda h: slice(h * SWA_HEAD_DIM, (h + 1) * SWA_HEAD_DIM)
    q_heads = lambda hk: [hk * SWA_GROUP + g for g in range(SWA_GROUP)]
    sink_cols = [_sink_column(sinks_ref, hk, t) for hk in range(SWA_KV_HEADS)]
    o = _sink_attention(
        [jnp.concatenate([q_ref[bi, :, head_cols(h)] for h in q_heads(hk)], axis=0) for bi, hk in items],
        [kks[bi][:, head_cols(hk)] for bi, hk in items], [vvs[bi][:, head_cols(hk)] for bi, hk in items],
        mask, [sink_cols[hk] for bi, hk in items])
    for idx, (bi, hk) in enumerate(items):
        for g, h in enumerate(q_heads(hk)):
            o_ref[bi, :, head_cols(h)] = o[idx][g * t:(g + 1) * t].astype(o_ref.dtype)


def _swa_sample(q, kv_new, k_cache, v_cache, sinks):
    b, t, _ = q.shape
    grp = SWA_SAMPLE_GROUP
    blk = lambda r, w: pl.BlockSpec((grp, r, w), lambda i: (i, 0, 0))
    return pl.pallas_call(
        _swa_sample_kernel,
        grid=(b // grp,),
        in_specs=[pl.BlockSpec(memory_space=pltpu.SMEM),
                  blk(t, SWA_WIDTH), blk(t, 2 * SWA_KV_WIDTH), blk(WINDOW, SWA_KV_WIDTH), blk(WINDOW, SWA_KV_WIDTH)],
        out_specs=[blk(t, SWA_WIDTH), blk(WINDOW, SWA_KV_WIDTH), blk(WINDOW, SWA_KV_WIDTH)],
        out_shape=[jax.ShapeDtypeStruct((b, t, SWA_WIDTH), BF16),
                   jax.ShapeDtypeStruct((b, WINDOW, SWA_KV_WIDTH), F32),
                   jax.ShapeDtypeStruct((b, WINDOW, SWA_KV_WIDTH), F32)],
        compiler_params=_cparams("parallel"),
        name="swa_sample",
    )(sinks, q, kv_new, k_cache, v_cache)


def _route(logits):
    lane = _iota2(logits.shape, 1).astype(F32)
    neg = -jnp.inf

    def first_argmax(vals, valid):
        v = jnp.where(valid, vals, neg)
        m = jnp.max(v, axis=-1, keepdims=True)
        idx = jnp.min(jnp.where(jnp.logical_and(valid, v == m), lane, float(LANES)), axis=-1, keepdims=True)
        return m, idx

    is_group = lane < N_GROUPS
    gmax, gidx = first_argmax(logits, is_group)
    p_group = 1.0 / jnp.sum(jnp.where(is_group, jnp.exp(logits - gmax), 0.0), axis=-1, keepdims=True)
    lo = N_GROUPS + gidx * EXPERTS_PER_GROUP
    in_group = jnp.logical_and(lane >= lo, lane < lo + EXPERTS_PER_GROUP)
    m1, i1 = first_argmax(logits, in_group)
    esum = jnp.sum(jnp.where(in_group, jnp.exp(logits - m1), 0.0), axis=-1, keepdims=True)
    m2, i2 = first_argmax(logits, jnp.logical_and(in_group, lane != i1))
    p1 = 1.0 / esum
    p2 = jnp.exp(m2 - m1) / esum
    tot = p1 + p2
    return i1 - N_GROUPS, i2 - N_GROUPS, p_group * p1 / tot, p_group * p2 / tot


def _post_mixer_kernel(oa_ref, ob_ref, ga_ref, gb_ref, x_ref, gt_ref, sc_ref, sh_ref,
                       wa_ref, wb_ref, wo_ref, gpost_ref, gpre_ref, wr_ref, br_ref, cnt0_ref,
                       x1_ref, h2_ref, rt_ref, cnt_out_ref, *rest):
    cnt_ref = rest[-1]
    if len(rest) == 2:
        rest[0][...] = jnp.zeros_like(rest[0])
    step = pl.program_id(0)

    @pl.when(step == 0)
    def _():
        cnt_ref[...] = cnt0_ref[...]

    merged = (ga_ref[...].astype(F32) * jnp.dot(oa_ref[...], wa_ref[...], preferred_element_type=F32)
              + gb_ref[...].astype(F32) * jnp.dot(ob_ref[...], wb_ref[...], preferred_element_type=F32))
    mix = _bdot(merged, wo_ref[...])
    x1 = x_ref[...] + gt_ref[0] * _rms(mix, gpost_ref[...])
    x1_ref[...] = x1
    h2 = _rms(x1, gpre_ref[...]) * (1.0 + sc_ref[0]) + sh_ref[0]
    _rows_to_tiles(h2_ref, h2)
    h_hi = h2.astype(BF16)
    h_lo = (h2 - h_hi.astype(F32)).astype(BF16)
    part = jnp.dot(h_hi, wr_ref[...], preferred_element_type=F32)
    logits = (part[:, :LANES] + part[:, LANES:]
              + jnp.dot(h_lo, wr_ref[:, :LANES], preferred_element_type=F32) + br_ref[...])
    ia, ib, wa, wb = _route(logits)
    lane = _iota2(logits.shape, 1)
    tm = logits.shape[0]
    lane_f = lane.astype(F32)
    hot_a = (lane_f == ia).astype(F32)
    hot_b = (lane_f == ib).astype(F32)
    hot = hot_a + hot_b
    earlier = (_iota2((tm, tm), 0) > _iota2((tm, tm), 1)).astype(BF16)
    before = jnp.dot(earlier, hot.astype(BF16), preferred_element_type=F32) + cnt_ref[...]
    rank_a = jnp.sum(hot_a * before, axis=-1, keepdims=True)
    rank_b = jnp.sum(hot_b * before, axis=-1, keepdims=True)
    cnt_ref[...] = cnt_ref[...] + jnp.sum(hot, axis=0, keepdims=True)
    cnt_out_ref[...] = cnt_ref[...]
    rt_ref[...] = jnp.where(lane == 0, ia, jnp.where(lane == 1, ib, jnp.where(lane == 2, wa, jnp.where(
        lane == 3, wb, jnp.where(lane == 4, rank_a, jnp.where(lane == 5, rank_b, 0.0))))))


def _post_mixer(oa, ob, ga, gb, x2d, gt, sc, sh, w_a, w_b, w_o, g_post, g_pre, w_rt, b_rt, cnt0, tm,
                zero_rows=0):
    rows = x2d.shape[0]
    n_tiles = rows // tm
    row_blk = lambda w: pl.BlockSpec((tm, w), lambda i: (i, 0))
    full = lambda r, c: pl.BlockSpec((r, c), lambda i: (0, 0))
    out_specs = [row_blk(D_MODEL), pl.BlockSpec((tm * TILE_ROWS, LANES), lambda i: (i, 0)), row_blk(LANES),
                 full(1, LANES)]
    out_shape = [jax.ShapeDtypeStruct((rows, D_MODEL), F32),
                 jax.ShapeDtypeStruct((rows * TILE_ROWS, LANES), F32),
                 jax.ShapeDtypeStruct((rows, LANES), F32),
                 jax.ShapeDtypeStruct((1, LANES), F32)]
    if zero_rows:
        assert zero_rows % (n_tiles * TILE_ROWS) == 0
        out_specs.append(pl.BlockSpec((zero_rows // n_tiles, LANES), lambda i: (i, 0)))
        out_shape.append(jax.ShapeDtypeStruct((zero_rows, LANES), F32))
    return pl.pallas_call(
        _post_mixer_kernel,
        grid=(n_tiles,),
        in_specs=[row_blk(GDN_WIDTH), row_blk(SWA_WIDTH), row_blk(D_MODEL), row_blk(D_MODEL), row_blk(D_MODEL),
                  _mod_spec(gt, n_tiles), _mod_spec(sc, n_tiles), _mod_spec(sh, n_tiles),
                  full(GDN_WIDTH, D_MODEL), full(SWA_WIDTH, D_MODEL), full(D_MODEL, D_MODEL),
                  full(1, D_MODEL), full(1, D_MODEL), full(D_MODEL, 2 * LANES), full(1, LANES), full(1, LANES)],
        out_specs=out_specs,
        out_shape=out_shape,
        scratch_shapes=[pltpu.VMEM((1, LANES), F32)],
        compiler_params=_cparams("arbitrary"),
        name="post_mixer",
    )(oa, ob, ga, gb, x2d, gt, sc, sh, w_a, w_b, w_o,
      g_post.reshape(1, D_MODEL), g_pre.reshape(1, D_MODEL), w_rt, b_rt, cnt0)


TILE_ROWS = D_MODEL // LANES


def _tiles_to_rows(ref, first, rows):
    base = first * TILE_ROWS
    return jnp.concatenate([ref[pl.ds(base + c, rows, stride=TILE_ROWS), :] for c in range(TILE_ROWS)], axis=1)


def _rows_to_tiles(ref, mat):
    rows = mat.shape[0]
    for c in range(TILE_ROWS):
        ref[pl.ds(c, rows, stride=TILE_ROWS), :] = mat[:, c * LANES:(c + 1) * LANES]


def _tile_copy_loop(n, copies, start):
    def body(t, carry):
        for j, cp in enumerate(copies(t)):
            if start:
                cp.start(priority=j % 2)
            else:
                cp.wait()
        return carry

    lax.fori_loop(0, n, body, 0, unroll=8)


def _dispatch_kernel(dest_ref, h_ref, xs_in_hbm, xs_hbm, sem):
    del xs_in_hbm
    i = pl.program_id(0)
    tm = h_ref.shape[0] // TILE_ROWS

    def copies(t):
        src = h_ref.at[pl.ds(pl.multiple_of(t * TILE_ROWS, TILE_ROWS), TILE_ROWS)]
        base = (i * tm + t) * TOP_K
        return [pltpu.make_async_copy(src, xs_hbm.at[dest_ref[base + k]], sem) for k in range(TOP_K)]

    _tile_copy_loop(tm, copies, True)
    _tile_copy_loop(tm, copies, False)


def _dispatch(dest, h_tiles, xs, tm):
    n_tiles = h_tiles.shape[0] // (tm * TILE_ROWS)
    return pl.pallas_call(
        _dispatch_kernel,
        grid_spec=pltpu.PrefetchScalarGridSpec(
            num_scalar_prefetch=1,
            grid=(n_tiles,),
            in_specs=[pl.BlockSpec((tm * TILE_ROWS, LANES), lambda i, d: (i, 0)),
                      pl.BlockSpec(memory_space=pl.ANY)],
            out_specs=pl.BlockSpec(memory_space=pl.ANY),
            scratch_shapes=[pltpu.SemaphoreType.DMA(())]),
        out_shape=jax.ShapeDtypeStruct(xs.shape, xs.dtype),
        input_output_aliases={2: 0},
        compiler_params=_cparams("arbitrary"),
        name="moe_dispatch",
    )(dest, h_tiles, xs)


def _moe_kernel(blk_e_ref, n_used_ref, x_ref, wg_ref, wu_ref, wd_ref, y_ref, wgb, wub, wdb):
    b = pl.program_id(0)
    rows = x_ref.shape[0] // TILE_ROWS
    changed = jnp.logical_or(b == 0, blk_e_ref[b] != blk_e_ref[jnp.maximum(b - 1, 0)])

    @pl.when(changed)
    def _():
        wgb[...] = wg_ref[0].astype(BF16)
        wub[...] = wu_ref[0].astype(BF16)
        wdb[...] = wd_ref[0].astype(BF16)

    @pl.when(b < n_used_ref[0])
    def _():
        x = _tiles_to_rows(x_ref, 0, rows).astype(BF16)
        gate = jnp.dot(x, wgb[...], preferred_element_type=F32)
        up = jnp.dot(x, wub[...], preferred_element_type=F32)
        _rows_to_tiles(y_ref, _bdot(_silu(gate) * up, wdb[...]))

    @pl.when(b >= n_used_ref[0])
    def _():
        y_ref[...] = jnp.zeros_like(y_ref)


def _moe(xs_tiles, blk_e, n_used, w_gate, w_up, w_down):
    n_blocks = blk_e.shape[0]
    rows = MOE_ROWS
    wspec = lambda r, c: pl.BlockSpec((1, r, c), lambda b, be, nu: (be[b], 0, 0))
    xspec = pl.BlockSpec((rows * TILE_ROWS, LANES), lambda b, be, nu: (b, 0))
    return pl.pallas_call(
        _moe_kernel,
        grid_spec=pltpu.PrefetchScalarGridSpec(
            num_scalar_prefetch=2,
            grid=(n_blocks,),
            in_specs=[xspec, wspec(D_MODEL, EXPERT_FF), wspec(D_MODEL, EXPERT_FF), wspec(EXPERT_FF, D_MODEL)],
            out_specs=xspec,
            scratch_shapes=[pltpu.VMEM((D_MODEL, EXPERT_FF), BF16),
                            pltpu.VMEM((D_MODEL, EXPERT_FF), BF16),
                            pltpu.VMEM((EXPERT_FF, D_MODEL), BF16)]),
        out_shape=jax.ShapeDtypeStruct(xs_tiles.shape, F32),
        compiler_params=_cparams("arbitrary"),
        name="moe_experts",
    )(blk_e, n_used, xs_tiles, w_gate, w_up, w_down)


def _combine_kernel(dest_ref, y_hbm, x1_ref, rt_ref, gt_ref, gpost_ref, o_ref, ybuf, sems):
    i = pl.program_id(0)
    n = pl.num_programs(0)
    rows = ybuf.shape[1] // TILE_ROWS
    slot = i % 2

    def gather(step, buf_slot, start):
        def copy(r):
            dst = ybuf.at[buf_slot, pl.ds(pl.multiple_of(r * TILE_ROWS, TILE_ROWS), TILE_ROWS)]
            return pltpu.make_async_copy(y_hbm.at[dest_ref[step * rows + r]], dst, sems.at[buf_slot])

        _tile_copy_loop(rows // 2, lambda t: [copy(2 * t), copy(2 * t + 1)], start)

    @pl.when(i == 0)
    def _():
        gather(0, 0, True)

    @pl.when(i + 1 < n)
    def _():
        gather(i + 1, 1 - slot, True)

    gather(i, slot, False)
    half = rows // 2
    rt = rt_ref[...]
    buf = ybuf.at[slot]
    f = rt[:, 2:3] * _tiles_to_rows(buf, 0, half) + rt[:, 3:4] * _tiles_to_rows(buf, half, half)
    o_ref[...] = x1_ref[...] + gt_ref[0] * _rms(f, gpost_ref[...])


def _combine(dest, yb, x1, rt, gt, g_post, tm):
    rows = x1.shape[0]
    n_tiles = rows // tm
    tiles_per_mod = n_tiles // gt.shape[0]
    return pl.pallas_call(
        _combine_kernel,
        grid_spec=pltpu.PrefetchScalarGridSpec(
            num_scalar_prefetch=1,
            grid=(n_tiles,),
            in_specs=[pl.BlockSpec(memory_space=pl.ANY),
                      pl.BlockSpec((tm, D_MODEL), lambda i, d: (i, 0)),
                      pl.BlockSpec((tm, LANES), lambda i, d: (i, 0)),
                      pl.BlockSpec((1, gt.shape[1], D_MODEL), lambda i, d: (i // tiles_per_mod, 0, 0)),
                      pl.BlockSpec((1, D_MODEL), lambda i, d: (0, 0))],
            out_specs=pl.BlockSpec((tm, D_MODEL), lambda i, d: (i, 0)),
            scratch_shapes=[pltpu.VMEM((2, TOP_K * tm * TILE_ROWS, LANES), F32),
                            pltpu.SemaphoreType.DMA((2,))]),
        out_shape=jax.ShapeDtypeStruct((rows, D_MODEL), F32),
        compiler_params=_cparams("arbitrary"),
        name="moe_combine",
    )(dest, yb, x1, rt, gt, g_post.reshape(1, D_MODEL))


def _dispatch_plan(rt, counts, n_tok, tm):
    n_assign = n_tok * TOP_K
    flat_e = rt[:, :TOP_K].astype(I32).reshape(n_assign)
    rank = rt[:, 4:4 + TOP_K].astype(I32).reshape(n_assign)
    pcounts = (counts + MOE_ROWS - 1) // MOE_ROWS * MOE_ROWS
    pends = jnp.cumsum(pcounts)
    pstarts = pends - pcounts
    experts = jnp.arange(N_EXPERTS, dtype=I32)
    dest = jnp.sum(jnp.where(flat_e[:, None] == experts[None, :], pstarts[None, :], 0), axis=1) + rank
    n_blocks = n_assign // MOE_ROWS + N_EXPERTS
    blk_start = jnp.arange(n_blocks, dtype=I32) * MOE_ROWS
    blk_e = jnp.minimum(jnp.sum(blk_start[:, None] >= pends[None, :], axis=1), N_EXPERTS - 1).astype(I32)
    n_used = (pends[-1] // MOE_ROWS).astype(I32).reshape(1)
    dest_tiles = dest.reshape(n_tok // tm, tm, TOP_K).transpose(0, 2, 1).reshape(-1)
    return blk_e, n_used, dest, dest_tiles


def _prep_in_weight_kernel(w_ref, o_ref):
    a0 = 4 * GDN_QK_WIDTH
    a1 = a0 + 2 * GDN_HEADS
    rows = w_ref.shape[0]
    for c in range(0, a0, PROJ_TILE):
        o_ref[:, c:c + PROJ_TILE] = w_ref[:, c:c + PROJ_TILE].astype(BF16)
    for c in range(a0, _C_AB, PROJ_TILE):
        o_ref[:, c:c + PROJ_TILE] = w_ref[:, c + a1 - a0:c + a1 - a0 + PROJ_TILE].astype(BF16)
    ab = jnp.concatenate([w_ref[:, a0:a1], jnp.zeros((rows, LANES - (a1 - a0)), F32)], axis=1)
    o_ref[:, _C_AB:IN_COLS] = ab.astype(BF16)


def _prep_in_weight(w_in):
    rows = 128
    return pl.pallas_call(
        _prep_in_weight_kernel,
        grid=(D_MODEL // rows,),
        in_specs=[pl.BlockSpec((rows, w_in.shape[1]), lambda i: (i, 0))],
        out_specs=pl.BlockSpec((rows, IN_COLS), lambda i: (i, 0)),
        out_shape=jax.ShapeDtypeStruct((D_MODEL, IN_COLS), BF16),
        compiler_params=_cparams("parallel"),
        name="prep_in_weight",
    )(w_in)


def _head_param_tile(a_log, dt_bias):
    tile = jnp.zeros((8, LANES), F32)
    return tile.at[0, :GDN_HEADS].set(a_log.astype(F32)).at[1, :GDN_HEADS].set(dt_bias.astype(F32))


def _router_weight(w_group, b_group, w_router, b_router):
    w = jnp.zeros((D_MODEL, LANES), F32)
    w = w.at[:, :N_GROUPS].set(w_group).at[:, N_GROUPS:N_GROUPS + N_EXPERTS].set(w_router)
    b = jnp.zeros((1, LANES), F32)
    b = b.at[0, :N_GROUPS].set(b_group).at[0, N_GROUPS:N_GROUPS + N_EXPERTS].set(b_router)
    w_hi = w.astype(BF16)
    w_lo = (w - w_hi.astype(F32)).astype(BF16)
    return jnp.concatenate([w_hi, w_lo], axis=1), b


def kernel(x_prompt, x_sample, state_gdn, state_conv, cache_k_win, cache_v_win, c_prompt, c_sample, w_ada, b_ada, g_mix_pre, g_mix_post, g_ffn_pre, g_ffn_post, w_in, conv_w, a_log, dt_bias, gdn_norm, sinks, w_br_gdn, w_br_swa, w_out, w_group, b_group, w_router, b_router, w_gate, w_up, w_down):
    depth = w_ada.shape[0]
    assert depth == 1, "single-layer trunk"
    bp, tp, _ = x_prompt.shape
    bs, ts, _ = x_sample.shape
    n_p = bp * tp
    n_s = bs * ts
    tm = ROW_TILE
    assert tp % tm == 0 and n_s % tm == 0 and ts >= GDN_CONV - 1 and ts + GDN_CONV - 1 <= GDN_SAMPLE_CHUNK
    assert ts & (ts - 1) == 0 and ts <= BF16_ROWS

    c_all = jnp.concatenate([c_prompt, c_sample], axis=0)
    c_rows = -(-c_all.shape[0] // 8) * 8
    c_all = jnp.pad(c_all, ((0, c_rows - c_all.shape[0]), (0, 0)))
    mod = _adaln(c_all, w_ada[0], b_ada[0])
    mods_p = [m[:bp].reshape(bp, 1, D_MODEL) for m in jnp.split(mod, 6, axis=-1)]
    mods_s = [jnp.repeat(m[bp:bp + bs], ts, axis=0).reshape(n_s // tm, tm, D_MODEL)
              for m in jnp.split(mod, 6, axis=-1)]

    w_prep = _prep_in_weight(w_in[0])
    head_params = _head_param_tile(a_log[0], dt_bias[0])
    w_a, w_b, w_o = w_br_gdn[0].astype(BF16), w_br_swa[0].astype(BF16), w_out[0].astype(BF16)
    w_rt, b_rt = _router_weight(w_group[0], b_group[0], w_router[0], b_router[0])
    sinks0 = sinks[0].astype(F32)

    xp2d = x_prompt.reshape(n_p, D_MODEL)
    sh1, sc1, gt1, sh2, sc2, gt2 = mods_p
    qkv_p, zs_p, qb_p, kvb_p, ga_p, gb_p, ab_p, conv_tail_p = _inproj(
        xp2d, g_mix_pre[0], sc1, sh1, w_prep, INPROJ_ROWS, conv_w=conv_w[0].astype(F32), n_seq=bp)
    qkv_p3 = qkv_p.reshape(bp, tp, GDN_CONV_DIM)
    u, w, qd, kd, qk, ge = _gdn_prep(qkv_p3, ab_p.reshape(bp, tp, LANES), head_params)
    oa_p, s_prompt = _gdn_scan(u, w, qd, kd, qk, ge, zs_p.reshape(bp, tp, GDN_WIDTH), gdn_norm[0])
    kvb_p3 = kvb_p.reshape(bp, tp, 2 * SWA_KV_WIDTH)
    ob_p = _swa_prompt(qb_p.reshape(bp, tp, SWA_WIDTH), kvb_p3, sinks0)
    n_slots = ((n_p + n_s) * TOP_K // MOE_ROWS + N_EXPERTS) * MOE_ROWS
    x1_p, h2_p, rt_p, cnt_p, xs = _post_mixer(
        oa_p.reshape(n_p, GDN_WIDTH), ob_p.reshape(n_p, SWA_WIDTH), ga_p, gb_p, xp2d, gt1, sc2, sh2,
        w_a, w_b, w_o, g_mix_post[0], g_ffn_pre[0], w_rt, b_rt, jnp.zeros((1, LANES), F32), tm,
        zero_rows=n_slots * TILE_ROWS)

    xs2d = x_sample.reshape(n_s, D_MODEL)
    sh1s, sc1s, gt1s, sh2s, sc2s, gt2s = mods_s
    qkv_s, zs_s, qb_s, kvb_s, ga_s, gb_s, ab_s = _inproj(xs2d, g_mix_pre[0], sc1s, sh1s, w_prep, tm)
    cc = GDN_SAMPLE_CHUNK
    pad_rows = cc - ts - (GDN_CONV - 1)
    qkv_s3 = qkv_s.reshape(bs, ts, GDN_CONV_DIM)
    xp_s = jnp.concatenate([jnp.zeros((bs, pad_rows, GDN_CONV_DIM), BF16), state_conv[0].astype(BF16), qkv_s3],
                           axis=1)
    front = lambda a: jnp.pad(a, ((0, 0), (cc - ts, 0), (0, 0)))
    oa_s16, s_sample = _gdn_sample(xp_s, front(ab_s.reshape(bs, ts, LANES)), front(zs_s.reshape(bs, ts, GDN_WIDTH)),
                                   state_gdn[0].astype(F32), conv_w[0], head_params, gdn_norm[0], ts)
    oa_s = oa_s16[:, cc - ts:, :].reshape(n_s, GDN_WIDTH)
    ob_s, k_new_s, v_new_s = _swa_sample(
        qb_s.reshape(bs, ts, SWA_WIDTH), kvb_s.reshape(bs, ts, 2 * SWA_KV_WIDTH),
        cache_k_win[0].reshape(bs, WINDOW, SWA_KV_WIDTH).astype(F32),
        cache_v_win[0].reshape(bs, WINDOW, SWA_KV_WIDTH).astype(F32), sinks0)
    x1_s, h2_s, rt_s, cnt_all = _post_mixer(
        oa_s, ob_s.reshape(n_s, SWA_WIDTH), ga_s, gb_s, xs2d, gt1s, sc2s, sh2s,
        w_a, w_b, w_o, g_mix_post[0], g_ffn_pre[0], w_rt, b_rt, cnt_p, tm)

    n_all = n_p + n_s
    ct = COMBINE_ROWS
    blk_e, n_used, dest, dest_tiles = _dispatch_plan(
        jnp.concatenate([rt_p[:, :8], rt_s[:, :8]], axis=0), cnt_all[0, :N_EXPERTS].astype(I32), n_all, ct)
    assert n_slots == blk_e.shape[0] * MOE_ROWS
    xs = _dispatch(dest[:n_p * TOP_K], h2_p, xs.reshape(n_slots, TILE_ROWS, LANES), min(DISPATCH_ROWS, n_p))
    xs = _dispatch(dest[n_p * TOP_K:], h2_s, xs, min(DISPATCH_ROWS, n_s))
    yb = _moe(xs.reshape(n_slots * TILE_ROWS, LANES), blk_e, n_used, w_gate[0], w_up[0], w_down[0])
    yb = yb.reshape(n_slots, TILE_ROWS, LANES)
    y_p = _combine(dest_tiles[:n_p * TOP_K], yb, x1_p, rt_p, gt2, g_ffn_post[0], ct)
    y_s = _combine(dest_tiles[n_p * TOP_K:], yb, x1_s, rt_s,
                   gt2s.reshape(n_s // ct, ct, D_MODEL), g_ffn_post[0], ct)

    f32 = lambda a: a.astype(F32)
    kv_tail = kvb_p3[:, tp - WINDOW:, :]
    kv_heads = lambda a: f32(a).reshape(a.shape[0], WINDOW, SWA_KV_HEADS, SWA_HEAD_DIM)[None]
    return (y_p.reshape(bp, tp, D_MODEL), y_s.reshape(bs, ts, D_MODEL),
            s_prompt[None], conv_tail_p[:, 8 - (GDN_CONV - 1):, :][None],
            kv_heads(kv_tail[:, :, :SWA_KV_WIDTH]), kv_heads(kv_tail[:, :, SWA_KV_WIDTH:]),
            s_sample[None], f32(qkv_s3[:, ts - (GDN_CONV - 1):, :])[None],
            kv_heads(k_new_s), kv_heads(v_new_s))
```

```python
import functools

import jax
import jax.numpy as jnp
from jax import lax
from jax.experimental import pallas as pl
from jax.experimental.pallas import tpu as pltpu

F32 = jnp.float32
BF16 = jnp.bfloat16
I32 = jnp.int32

D_MODEL = 1024
NORM_EPS = 1e-6
GDN_HEADS = 8
GDN_DK = 128
GDN_DV = 128
GDN_CONV = 4
GDN_CHUNK = 64
GDN_QK_WIDTH = GDN_HEADS * GDN_DK
GDN_WIDTH = GDN_HEADS * GDN_DV
GDN_CONV_DIM = 2 * GDN_QK_WIDTH + GDN_WIDTH
SWA_Q_HEADS = 16
SWA_KV_HEADS = 4
SWA_HEAD_DIM = 64
SWA_GROUP = SWA_Q_HEADS // SWA_KV_HEADS
SWA_WIDTH = SWA_Q_HEADS * SWA_HEAD_DIM
SWA_KV_WIDTH = SWA_KV_HEADS * SWA_HEAD_DIM
WINDOW = 128
N_GROUPS = 4
EXPERTS_PER_GROUP = 8
N_EXPERTS = N_GROUPS * EXPERTS_PER_GROUP
TOP_K = 2
EXPERT_FF = 512

LANES = 128
BF16_ROWS = 16
VMEM_LIMIT = 56 * 1024 * 1024

_C_QKV = 0
_C_Z = _C_QKV + GDN_CONV_DIM
_C_QB = _C_Z + GDN_WIDTH
_C_KVB = _C_QB + SWA_WIDTH
_C_GA = _C_KVB + 2 * SWA_KV_WIDTH
_C_GB = _C_GA + D_MODEL
_C_AB = _C_GB + D_MODEL
IN_COLS = _C_AB + LANES
PROJ_TILE = 512

ROW_TILE = 512
INPROJ_ROWS = 512
GDN_PREP_ROWS = 256
GDN_SCAN_ROWS = 512
GDN_SAMPLE_CHUNK = 16
GDN_SAMPLE_GROUP = 8
SWA_SAMPLE_GROUP = 8
SWA_BLOCKS_PER_STEP = 2
MOE_ROWS = 512
DISPATCH_ROWS = 1024


def _cparams(*sem):
    return pltpu.CompilerParams(dimension_semantics=sem, vmem_limit_bytes=VMEM_LIMIT)


def _bdot(a, b):
    return jnp.dot(a.astype(BF16), b.astype(BF16), preferred_element_type=F32)


def _bdot_nt(a, b):
    return lax.dot_general(a.astype(BF16), b.astype(BF16), (((1,), (1,)), ((), ())),
                           preferred_element_type=F32)


def _bdot_tn(a, b):
    return lax.dot_general(a.astype(BF16), b.astype(BF16), (((0,), (0,)), ((), ())),
                           preferred_element_type=F32)


def _sigmoid(x):
    return 1.0 / (1.0 + jnp.exp(-x))


def _silu(x):
    return x * _sigmoid(x)


def _rms(x, gain):
    return x * lax.rsqrt(jnp.mean(x * x, axis=-1, keepdims=True) + NORM_EPS) * gain


def _iota2(shape, dim):
    return lax.broadcasted_iota(I32, shape, dim)


def _adaln_kernel(c_ref, w_ref, b_ref, o_ref):
    o_ref[...] = _bdot(_silu(c_ref[...]), w_ref[...]) + b_ref[...]


def _adaln(c_all, w_ada, b_ada):
    rows = c_all.shape[0]
    n_out = w_ada.shape[1]
    tn = D_MODEL
    return pl.pallas_call(
        _adaln_kernel,
        grid=(n_out // tn,),
        in_specs=[pl.BlockSpec((rows, D_MODEL), lambda j: (0, 0)),
                  pl.BlockSpec((D_MODEL, tn), lambda j: (0, j)),
                  pl.BlockSpec((1, tn), lambda j: (0, j))],
        out_specs=pl.BlockSpec((rows, tn), lambda j: (0, j)),
        out_shape=jax.ShapeDtypeStruct((rows, n_out), F32),
        compiler_params=_cparams("arbitrary"),
        name="adaln",
    )(c_all, w_ada, b_ada.reshape(1, n_out))


def _inproj_kernel(tiles_per_seq, x_ref, g_ref, sc_ref, sh_ref, w_ref, *rest):
    if tiles_per_seq:
        cw_ref, qkv_ref, z_ref, qb_ref, kvb_ref, ga_ref, gb_ref, ab_ref, tail_ref, carry_ref = rest
    else:
        qkv_ref, z_ref, qb_ref, kvb_ref, ga_ref, gb_ref, ab_ref = rest
    tm = x_ref.shape[0]
    h = (_rms(x_ref[...], g_ref[...]) * (1.0 + sc_ref[0]) + sh_ref[0]).astype(BF16)

    def fill(ref, c0, width, fn):
        step = min(PROJ_TILE, width)
        for c in range(0, width, step):
            acc = jnp.dot(h, w_ref[:, c0 + c:c0 + c + step], preferred_element_type=F32)
            ref[:, c:c + step] = fn(acc, c, step).astype(ref.dtype)

    def conv_act(acc, c, step):
        cols = slice(c, c + step)
        seq_start = pl.program_id(0) % tiles_per_seq == 0
        prev = jnp.where(seq_start, 0.0, carry_ref[:, cols])
        last = acc[tm - 8:tm]
        carry_ref[:, cols] = last
        tail_ref[0, :, cols] = last
        ext = jnp.concatenate([prev, acc], axis=0)
        y = cw_ref[GDN_CONV - 1:GDN_CONV, cols] * acc
        for j in range(GDN_CONV - 1):
            y = y + cw_ref[j:j + 1, cols] * ext[8 - (GDN_CONV - 1) + j:8 - (GDN_CONV - 1) + j + tm]
        y = _silu(y)
        if c >= 2 * GDN_QK_WIDTH:
            return y
        scale = GDN_DK ** -0.5 if c < GDN_QK_WIDTH else 1.0
        heads = [_l2n(y[:, d:d + GDN_DK]) * scale for d in range(0, step, GDN_DK)]
        return jnp.concatenate(heads, axis=1)

    ident = lambda v, c, step: v
    silu = lambda v, c, step: _silu(v)
    sigmoid = lambda v, c, step: _sigmoid(v)
    fill(qkv_ref, _C_QKV, GDN_CONV_DIM, conv_act if tiles_per_seq else ident)
    fill(z_ref, _C_Z, GDN_WIDTH, silu)
    fill(qb_ref, _C_QB, SWA_WIDTH, ident)
    fill(kvb_ref, _C_KVB, 2 * SWA_KV_WIDTH, ident)
    fill(ga_ref, _C_GA, D_MODEL, sigmoid)
    fill(gb_ref, _C_GB, D_MODEL, sigmoid)
    fill(ab_ref, _C_AB, LANES, ident)


def _mod_spec(mod, n_tiles):
    tiles_per_mod = n_tiles // mod.shape[0]
    return pl.BlockSpec((1, mod.shape[1], D_MODEL), lambda i: (i // tiles_per_mod, 0, 0))


def _inproj(x2d, gain, sc, sh, w_prep, tm, conv_w=None, n_seq=0):
    rows = x2d.shape[0]
    n_tiles = rows // tm
    widths = (GDN_CONV_DIM, GDN_WIDTH, SWA_WIDTH, 2 * SWA_KV_WIDTH, D_MODEL, D_MODEL, LANES)
    dtypes = (BF16, BF16, BF16, BF16, BF16, BF16, F32)
    in_specs = [pl.BlockSpec((tm, D_MODEL), lambda i: (i, 0)),
                pl.BlockSpec((1, D_MODEL), lambda i: (0, 0)),
                _mod_spec(sc, n_tiles), _mod_spec(sh, n_tiles),
                pl.BlockSpec((D_MODEL, IN_COLS), lambda i: (0, 0))]
    out_specs = [pl.BlockSpec((tm, w), lambda i: (i, 0)) for w in widths]
    out_shape = [jax.ShapeDtypeStruct((rows, w), dt) for w, dt in zip(widths, dtypes)]
    args = [x2d, gain.reshape(1, D_MODEL), sc, sh, w_prep]
    scratch = []
    tiles_per_seq = 0
    if conv_w is not None:
        tiles_per_seq = n_tiles // n_seq
        in_specs.append(pl.BlockSpec((GDN_CONV, GDN_CONV_DIM), lambda i: (0, 0)))
        out_specs.append(pl.BlockSpec((1, 8, GDN_CONV_DIM), lambda i: (i // tiles_per_seq, 0, 0)))
        out_shape.append(jax.ShapeDtypeStruct((n_seq, 8, GDN_CONV_DIM), F32))
        args.append(conv_w)
        scratch.append(pltpu.VMEM((8, GDN_CONV_DIM), F32))
    return pl.pallas_call(
        functools.partial(_inproj_kernel, tiles_per_seq),
        grid=(n_tiles,),
        in_specs=in_specs,
        out_specs=out_specs,
        out_shape=out_shape,
        scratch_shapes=scratch,
        compiler_params=_cparams("arbitrary"),
        name="inproj",
    )(*args)


def _cumsum_rows(g):
    c = g.shape[0]
    tril = (_iota2((c, c), 0) >= _iota2((c, c), 1)).astype(BF16)
    hi = g.astype(BF16)
    r1 = g - hi.astype(F32)
    mid = r1.astype(BF16)
    lo = (r1 - mid.astype(F32)).astype(BF16)
    dot = lambda p: jnp.dot(tril, p, preferred_element_type=F32)
    return dot(hi) + dot(mid) + dot(lo)


def _each(fn, *lists):
    return [fn(*args) for args in zip(*lists)]


def _pair_blockdiag(m, left):
    return jnp.concatenate([jnp.where(left, m, 0.0), jnp.where(left, 0.0, m)], axis=0)


def _unit_lower_inverse_offset(a_list, ii, jj, left):
    c = a_list[0].shape[0]

    def same_block(shift):
        return lax.shift_right_logical(ii, shift) == lax.shift_right_logical(jj, shift)

    base = same_block(1)
    n_list = _each(lambda a: jnp.where(base, -a, 0.0), a_list)
    shift = 1
    while (1 << shift) < c:
        outer, inner = same_block(shift + 1), same_block(shift)
        off_list = _each(lambda a: jnp.where(outer, jnp.where(inner, 0.0, a), 0.0), a_list)
        x_list = _each(lambda off, n: off + _bdot(off, _pair_blockdiag(n, left)), off_list, n_list)
        n_list = _each(lambda n, x: n - x - _bdot(n, _pair_blockdiag(x, left)), n_list, x_list)
        shift += 1
    return n_list


def _chunk_prep(q, k, v, gcol, grow, bcol):
    c = q[0].shape[0]
    assert len(q) % 2 == 0
    ii = _iota2((c, 2 * c), 0)
    lane = _iota2((c, 2 * c), 1)
    left = lane < c
    jj = lane & (c - 1)
    causal = ii >= jj
    strict = ii > jj
    first, second = slice(0, None, 2), slice(1, None, 2)
    kb = _each(lambda kk, b: kk * b, k, bcol)
    both = _each(lambda qa, ka, qb, kbb, x, y: _bdot_nt(jnp.concatenate([qa, ka, qb, kbb], axis=0),
                                                       jnp.concatenate([x, y], axis=0)),
                 q[first], kb[first], q[second], kb[second], k[first], k[second])
    decay = _each(lambda ga, gb, ra, rb: jnp.where(causal, jnp.exp(jnp.where(
        causal, jnp.where(left, ga, gb) - jnp.concatenate([ra, rb], axis=1), 0.0)), 0.0),
        gcol[first], gcol[second], grow[first], grow[second])
    qk = _each(lambda bo, d: jnp.where(left, bo[0:c], bo[2 * c:3 * c]) * d, both, decay)
    a = _each(lambda bo, d: jnp.where(strict, jnp.where(left, bo[c:2 * c], bo[3 * c:4 * c]) * d, 0.0), both, decay)
    n = _unit_lower_inverse_offset(a, ii, jj, left)
    eg = _each(jnp.exp, gcol)
    rhs = _each(lambda vv, b, kbb, e: jnp.concatenate([vv * b, kbb * e], axis=1), v, bcol, kb, eg)
    uw = _each(lambda ra, rb, nn: (lambda r: r + _bdot(_pair_blockdiag(nn, left), r))(
        jnp.concatenate([ra, rb], axis=0)), rhs[first], rhs[second], n)
    uw = [x[half] for x in uw for half in (slice(0, c), slice(c, 2 * c))]
    u = [x[:, :GDN_DV] for x in uw]
    w = [x[:, GDN_DV:] for x in uw]
    qd = _each(lambda qq, e: qq * e, q, eg)
    kd = _each(lambda kk, gc: kk * jnp.exp(gc[c - 1:c, :] - gc), k, gcol)
    return u, w, qd, kd, qk


def _chunk_step(s, u, w, qd, kd, qk, ge):
    c = u[0].shape[0]
    both = _each(lambda ww, qq, ss: _bdot(jnp.concatenate([ww, qq], axis=0), ss), w, qd, s)
    v_new = _each(lambda uu, bo: uu.astype(F32) - bo[:c], u, both)
    o = _each(lambda bo, m, vn: bo[c:] + _bdot(m, vn), both, qk, v_new)
    s_new = _each(lambda ss, g, kk, vn: ss * g + _bdot_tn(kk, vn), s, ge, kd, v_new)
    return o, s_new


def _conv_act(xp_ref, cw_ref, r0, rows, c0):
    cols = slice(c0, c0 + LANES)
    acc = cw_ref[3:4, cols] * xp_ref[r0:r0 + rows, cols]
    for j in range(GDN_CONV - 1):
        acc = acc + cw_ref[j:j + 1, cols] * xp_ref[r0 - 3 + j:r0 - 3 + j + rows, cols]
    return _silu(acc)


def _l2n(x):
    return x * lax.rsqrt(jnp.sum(x * x, axis=-1, keepdims=True) + NORM_EPS)


def _softplus(x):
    return jnp.maximum(x, 0.0) + jnp.log1p(jnp.exp(-jnp.abs(x)))


def _head_cols(hd):
    return (hd * GDN_DK, GDN_QK_WIDTH + hd * GDN_DK, 2 * GDN_QK_WIDTH + hd * GDN_DV)


def _activate_qkv(xp_ref, cw_ref, act_ref, r0, rows):
    for hd in range(GDN_HEADS):
        cq, ck, cv = _head_cols(hd)
        act_ref[0:rows, cq:cq + LANES] = _l2n(_conv_act(xp_ref, cw_ref, r0, rows, cq)) * (GDN_DK ** -0.5)
        act_ref[0:rows, ck:ck + LANES] = _l2n(_conv_act(xp_ref, cw_ref, r0, rows, ck))
        act_ref[0:rows, cv:cv + LANES] = _conv_act(xp_ref, cw_ref, r0, rows, cv)


def _decay_beta(ab, hp_ref):
    g = -jnp.exp(hp_ref[0:1, :]) * _softplus(ab + hp_ref[1:2, :])
    return g, _sigmoid(ab)


def _gdn_prep_kernel(act_ref, ab_ref, hp_ref, u_ref, w_ref, qd_ref, kd_ref, qk_ref, ge_ref):
    tb = act_ref.shape[1]
    cc = GDN_CHUNK
    g_all, beta_all = _decay_beta(ab_ref[0], hp_ref)

    chunks = [slice(ci * cc, (ci + 1) * cc) for ci in range(tb // cc)]
    gcs = _each(lambda rows: _cumsum_rows(g_all[rows, :]), chunks)
    gcts = _each(lambda gc: gc.T, gcs)
    for ci, gc in enumerate(gcs):
        ge_ref[0, ci] = jnp.exp(gc[cc - 1:cc, :])
    items = [(ci, hd) for ci in range(len(chunks)) for hd in range(GDN_HEADS)]
    col = lambda which: [act_ref[0, chunks[ci], _head_cols(hd)[which]:_head_cols(hd)[which] + LANES].astype(F32)
                         for ci, hd in items]
    u, w, qd, kd, qk = _chunk_prep(
        col(0), col(1), col(2),
        [gcs[ci][:, hd:hd + 1] for ci, hd in items], [gcts[ci][hd:hd + 1, :] for ci, hd in items],
        [beta_all[chunks[ci], GDN_HEADS + hd:GDN_HEADS + hd + 1] for ci, hd in items])
    for idx, (ci, hd) in enumerate(items):
        rows = chunks[ci]
        oc = slice(hd * GDN_DV, (hd + 1) * GDN_DV)
        u_ref[0, rows, oc] = u[idx].astype(BF16)
        w_ref[0, rows, oc] = w[idx].astype(BF16)
        qd_ref[0, rows, oc] = qd[idx].astype(BF16)
        kd_ref[0, rows, oc] = kd[idx].astype(BF16)
        if hd % 2 == 0:
            qk_ref[0, rows, hd * cc:(hd + 2) * cc] = qk[idx // 2].astype(BF16)


def _gdn_prep(qkv, ab, head_params):
    b, t, _ = qkv.shape
    tb = min(GDN_PREP_ROWS, t)
    nch = tb // GDN_CHUNK
    blk = lambda w: pl.BlockSpec((1, tb, w), lambda bi, i: (bi, i, 0))
    out_shapes = [jax.ShapeDtypeStruct((b, t, GDN_WIDTH), BF16)] * 4 + [
        jax.ShapeDtypeStruct((b, t, GDN_HEADS * GDN_CHUNK), BF16),
        jax.ShapeDtypeStruct((b, t // GDN_CHUNK, 1, LANES), F32)]
    return pl.pallas_call(
        _gdn_prep_kernel,
        grid=(b, t // tb),
        in_specs=[blk(GDN_CONV_DIM), blk(LANES), pl.BlockSpec((8, LANES), lambda bi, i: (0, 0))],
        out_specs=[blk(GDN_WIDTH)] * 4 + [
            blk(GDN_HEADS * GDN_CHUNK),
            pl.BlockSpec((1, nch, 1, LANES), lambda bi, i: (bi, i, 0, 0))],
        out_shape=out_shapes,
        compiler_params=_cparams("parallel", "parallel"),
        name="gdn_prep",
    )(qkv, ab, head_params)


def _gated_norm_store(o_ref, idx, o, gain, zs):
    o_ref[idx] = (_rms(o, gain) * zs.astype(F32)).astype(o_ref.dtype)


def _gdn_scan_kernel(u_ref, w_ref, qd_ref, kd_ref, qk_ref, ge_ref, zs_ref, gain_ref,
                     o_ref, s_out_ref, s_ref):
    nb, tb, _ = u_ref.shape
    cc = GDN_CHUNK
    step = pl.program_id(0)

    @pl.when(step == 0)
    def _():
        s_ref[...] = jnp.zeros_like(s_ref)

    gain = gain_ref[...]

    def chunk_body(ci, carry):
        rows = pl.ds(pl.multiple_of(ci * cc, cc), cc)
        items = [(bi, hd) for bi in range(nb) for hd in range(GDN_HEADS)]
        oc = lambda hd: slice(hd * GDN_DV, (hd + 1) * GDN_DV)
        ge_rows = [ge_ref[bi, ci] for bi in range(nb)]
        o, s_new = _chunk_step(
            [s_ref[bi * GDN_HEADS + hd] for bi, hd in items],
            [u_ref[bi, rows, oc(hd)] for bi, hd in items], [w_ref[bi, rows, oc(hd)] for bi, hd in items],
            [qd_ref[bi, rows, oc(hd)] for bi, hd in items], [kd_ref[bi, rows, oc(hd)] for bi, hd in items],
            [qk_ref[bi, rows, hd * cc:(hd + 1) * cc] for bi, hd in items],
            [ge_rows[bi][:, hd:hd + 1] for bi, hd in items])
        for idx, (bi, hd) in enumerate(items):
            s_ref[bi * GDN_HEADS + hd] = s_new[idx]
            _gated_norm_store(o_ref, (bi, rows, oc(hd)), o[idx], gain, zs_ref[bi, rows, oc(hd)])
        return carry

    lax.fori_loop(0, tb // cc, chunk_body, 0)

    @pl.when(step == pl.num_programs(0) - 1)
    def _():
        s_out_ref[...] = s_ref[...]


def _gdn_scan(u, w, qd, kd, qk, ge, zs, gain):
    b, t, _ = u.shape
    tb = min(GDN_SCAN_ROWS, t)
    nch = tb // GDN_CHUNK
    blk = lambda wd: pl.BlockSpec((b, tb, wd), lambda i: (0, i, 0))
    o, s = pl.pallas_call(
        _gdn_scan_kernel,
        grid=(t // tb,),
        in_specs=[blk(GDN_WIDTH)] * 4 + [
            blk(GDN_HEADS * GDN_CHUNK),
            pl.BlockSpec((b, nch, 1, LANES), lambda i: (0, i, 0, 0)),
            blk(GDN_WIDTH),
            pl.BlockSpec((1, GDN_DV), lambda i: (0, 0))],
        out_specs=[blk(GDN_WIDTH),
                   pl.BlockSpec((b * GDN_HEADS, GDN_DK, GDN_DV), lambda i: (0, 0, 0))],
        out_shape=[jax.ShapeDtypeStruct((b, t, GDN_WIDTH), BF16),
                   jax.ShapeDtypeStruct((b * GDN_HEADS, GDN_DK, GDN_DV), F32)],
        scratch_shapes=[pltpu.VMEM((b * GDN_HEADS, GDN_DK, GDN_DV), F32)],
        compiler_params=_cparams("arbitrary"),
        name="gdn_scan",
    )(u, w, qd, kd, qk, ge, zs, gain.reshape(1, GDN_DV))
    return o, s.reshape(b, GDN_HEADS, GDN_DK, GDN_DV)


def _gdn_sample_kernel(new_rows, xp_ref, ab_ref, zs_ref, s0_ref, cw_ref, hp_ref, gain_ref,
                       o_ref, s_out_ref, xs_ref, act_ref):
    grp = xp_ref.shape[0]
    cc = GDN_SAMPLE_CHUNK
    gain = gain_ref[...]
    rowmask = (_iota2((cc, 1), 0) >= cc - new_rows).astype(F32)
    seqs = list(range(grp))
    for bi in seqs:
        xs = xs_ref.at[bi]
        xs[0:8, :] = jnp.zeros((8, GDN_CONV_DIM), F32)
        xs[8:8 + cc, :] = xp_ref[bi].astype(F32)
        _activate_qkv(xs, cw_ref, act_ref.at[bi], 8, cc)
    gb = _each(lambda bi: _decay_beta(ab_ref[bi], hp_ref), seqs)
    gcs = _each(lambda x: _cumsum_rows(x[0] * rowmask), gb)
    gcts = _each(lambda gc: gc.T, gcs)
    ge_rows = _each(lambda gc: jnp.exp(gc[cc - 1:cc, :]), gcs)
    betas = _each(lambda x: x[1] * rowmask, gb)
    items = [(bi, hd) for bi in seqs for hd in range(GDN_HEADS)]
    col = lambda which: [act_ref[bi, :, _head_cols(hd)[which]:_head_cols(hd)[which] + LANES] * rowmask
                         for bi, hd in items]
    u, w, qd, kd, qk = _chunk_prep(
        col(0), col(1), col(2),
        [gcs[bi][:, hd:hd + 1] for bi, hd in items], [gcts[bi][hd:hd + 1, :] for bi, hd in items],
        [betas[bi][:, GDN_HEADS + hd:GDN_HEADS + hd + 1] for bi, hd in items])
    qk = [pair[:, half] for pair in qk for half in (slice(0, cc), slice(cc, 2 * cc))]
    o, s_new = _chunk_step([s0_ref[bi, hd] for bi, hd in items], u, w, qd, kd, qk,
                           [ge_rows[bi][:, hd:hd + 1] for bi, hd in items])
    for idx, (bi, hd) in enumerate(items):
        s_out_ref[bi, hd] = s_new[idx]
        oc = slice(hd * GDN_DV, (hd + 1) * GDN_DV)
        _gated_norm_store(o_ref, (bi, slice(None), oc), o[idx], gain, zs_ref[bi, :, oc])


def _gdn_sample(xp, ab, zs, s0, conv_w, head_params, gain, new_rows):
    b = xp.shape[0]
    cc = GDN_SAMPLE_CHUNK
    grp = GDN_SAMPLE_GROUP
    blk3 = lambda w: pl.BlockSpec((grp, cc, w), lambda i: (i, 0, 0))
    sblk = pl.BlockSpec((grp, GDN_HEADS, GDN_DK, GDN_DV), lambda i: (i, 0, 0, 0))
    return pl.pallas_call(
        functools.partial(_gdn_sample_kernel, new_rows),
        grid=(b // grp,),
        in_specs=[blk3(GDN_CONV_DIM), blk3(LANES), blk3(GDN_WIDTH), sblk,
                  pl.BlockSpec((GDN_CONV, GDN_CONV_DIM), lambda i: (0, 0)),
                  pl.BlockSpec((8, LANES), lambda i: (0, 0)),
                  pl.BlockSpec((1, GDN_DV), lambda i: (0, 0))],
        out_specs=[blk3(GDN_WIDTH), sblk],
        out_shape=[jax.ShapeDtypeStruct((b, cc, GDN_WIDTH), BF16),
                   jax.ShapeDtypeStruct((b, GDN_HEADS, GDN_DK, GDN_DV), F32)],
        scratch_shapes=[pltpu.VMEM((grp, cc + 8, GDN_CONV_DIM), F32),
                        pltpu.VMEM((grp, cc, GDN_CONV_DIM), F32)],
        compiler_params=_cparams("parallel"),
        name="gdn_sample",
    )(xp, ab, zs, s0, conv_w, head_params, gain.reshape(1, GDN_DV))


def _sink_attention(q, k, v, mask, sink_col):
    s = _each(lambda qq, kk: jnp.where(mask, _bdot_nt(qq, kk) * (SWA_HEAD_DIM ** -0.5), -jnp.inf), q, k)
    m = _each(lambda ss, sk: jnp.maximum(jnp.max(ss, axis=-1, keepdims=True), sk), s, sink_col)
    p = _each(lambda ss, mm: jnp.exp(ss - mm), s, m)
    denom = _each(lambda pp, sk, mm: jnp.sum(pp, axis=-1, keepdims=True) + jnp.exp(sk - mm), p, sink_col, m)
    return _each(lambda pp, vv, dd: _bdot(pp, vv) / dd, p, v, denom)


def _sink_column(sinks_ref, kv_head, rows_per_head):
    parts = [jnp.full((rows_per_head, 1), sinks_ref[kv_head * SWA_GROUP + g], F32) for g in range(SWA_GROUP)]
    return jnp.concatenate(parts, axis=0)


def _swa_prompt_kernel(sinks_ref, q_ref, kvp_ref, kvc_ref, o_ref):
    wnd = WINDOW
    nblk = q_ref.shape[1] // wnd
    step = pl.program_id(1)
    kv = jnp.concatenate([kvp_ref[0], kvc_ref[0]], axis=0)
    cols = SWA_GROUP * wnd
    kj = _iota2((2 * wnd, cols), 0)
    qi = _iota2((2 * wnd, cols), 1) & (wnd - 1)
    dist = qi + wnd - kj
    band = (dist >= 0) & (dist <= wnd)
    first_key = jnp.where(step > 0, 0, wnd)
    masks = [band & (kj >= first_key)] + [band] * (nblk - 1)
    head_cols = lambda h: slice(h * SWA_HEAD_DIM, (h + 1) * SWA_HEAD_DIM)
    q_heads = lambda hk: [hk * SWA_GROUP + g for g in range(SWA_GROUP)]
    scale = SWA_HEAD_DIM ** -0.5
    items = [(j, hk) for j in range(nblk) for hk in range(SWA_KV_HEADS)]
    q_rows = lambda j: slice(j * wnd, (j + 1) * wnd)
    k_rows = lambda j: slice(j * wnd, (j + 2) * wnd)
    q = [jnp.concatenate([q_ref[0, q_rows(j), head_cols(h)] for h in q_heads(hk)], axis=0) for j, hk in items]
    k = [kv[k_rows(j), head_cols(hk)] for j, hk in items]
    v = [kv[k_rows(j), SWA_KV_WIDTH + hk * SWA_HEAD_DIM:SWA_KV_WIDTH + (hk + 1) * SWA_HEAD_DIM] for j, hk in items]
    sinks_kv = [jnp.concatenate([jnp.full((1, wnd), sinks_ref[h], F32) for h in q_heads(hk)], axis=1)
                for hk in range(SWA_KV_HEADS)]
    sink = [sinks_kv[hk] for j, hk in items]
    mask = [masks[j] for j, hk in items]
    s = _each(lambda kk, qq, mk: jnp.where(mk, _bdot_nt(kk, qq) * scale, -jnp.inf), k, q, mask)
    m = _each(lambda ss, sk: jnp.maximum(jnp.max(ss, axis=0, keepdims=True), sk), s, sink)
    p = _each(lambda ss, mm: jnp.exp(ss - mm), s, m)
    denom = _each(lambda pp, sk, mm: jnp.sum(pp, axis=0, keepdims=True) + jnp.exp(sk - mm), p, sink, m)
    ot = _each(lambda vv, pp, dd: _bdot_tn(vv, pp) / dd, v, p, denom)
    for idx, (j, hk) in enumerate(items):
        for g in range(0, SWA_GROUP, 2):
            pair = jnp.concatenate([ot[idx][:, g * wnd:(g + 1) * wnd], ot[idx][:, (g + 1) * wnd:(g + 2) * wnd]],
                                   axis=0)
            h0 = hk * SWA_GROUP + g
            o_ref[0, q_rows(j), h0 * SWA_HEAD_DIM:(h0 + 2) * SWA_HEAD_DIM] = pair.T.astype(o_ref.dtype)


def _swa_prompt(q, kv, sinks):
    b, t, _ = q.shape
    nblk = SWA_BLOCKS_PER_STEP
    rows = nblk * WINDOW
    return pl.pallas_call(
        _swa_prompt_kernel,
        grid=(b, t // rows),
        in_specs=[pl.BlockSpec(memory_space=pltpu.SMEM),
                  pl.BlockSpec((1, rows, SWA_WIDTH), lambda bi, i: (bi, i, 0)),
                  pl.BlockSpec((1, WINDOW, 2 * SWA_KV_WIDTH), lambda bi, i: (bi, jnp.maximum(i * nblk - 1, 0), 0)),
                  pl.BlockSpec((1, rows, 2 * SWA_KV_WIDTH), lambda bi, i: (bi, i, 0))],
        out_specs=pl.BlockSpec((1, rows, SWA_WIDTH), lambda bi, i: (bi, i, 0)),
        out_shape=jax.ShapeDtypeStruct((b, t, SWA_WIDTH), BF16),
        compiler_params=_cparams("parallel", "parallel"),
        name="swa_prompt",
    )(sinks, q, kv, kv)


def _swa_sample_kernel(sinks_ref, q_ref, kvn_ref, kc_ref, vc_ref, o_ref, ko_ref, vo_ref):
    grp, t, _ = q_ref.shape
    wnd = WINDOW
    nk = wnd + BF16_ROWS
    rows = SWA_GROUP * t
    tq = _iota2((rows, nk), 0) & (t - 1)
    kj = _iota2((rows, nk), 1)
    dist = tq + wnd - kj
    mask = (dist >= 0) & (dist <= wnd)
    zpad = jnp.zeros((BF16_ROWS - t, SWA_KV_WIDTH), F32)
    kks, vvs = [], []
    for bi in range(grp):
        kvn = kvn_ref[bi].astype(F32)
        kk = jnp.concatenate([kc_ref[bi], kvn[:, :SWA_KV_WIDTH], zpad], axis=0)
        vv = jnp.concatenate([vc_ref[bi], kvn[:, SWA_KV_WIDTH:], zpad], axis=0)
        ko_ref[bi] = kk[t:t + wnd, :]
        vo_ref[bi] = vv[t:t + wnd, :]
        kks.append(kk)
        vvs.append(vv)
    items = [(bi, hk) for bi in range(grp) for hk in range(SWA_KV_HEADS)]
    head_cols = lambda h: slice(h * SWA_HEAD_DIM, (h + 1) * SWA_HEAD_DIM)
    q_heads = lambda hk: [hk * SWA_GROUP + g for g in range(SWA_GROUP)]
    sink_cols = [_sink_column(sinks_ref, hk, t) for hk in range(SWA_KV_HEADS)]
    o = _sink_attention(
        [jnp.concatenate([q_ref[bi, :, head_cols(h)] for h in q_heads(hk)], axis=0) for bi, hk in items],
        [kks[bi][:, head_cols(hk)] for bi, hk in items], [vvs[bi][:, head_cols(hk)] for bi, hk in items],
        mask, [sink_cols[hk] for bi, hk in items])
    for idx, (bi, hk) in enumerate(items):
        for g, h in enumerate(q_heads(hk)):
            o_ref[bi, :, head_cols(h)] = o[idx][g * t:(g + 1) * t].astype(o_ref.dtype)


def _swa_sample(q, kv_new, k_cache, v_cache, sinks):
    b, t, _ = q.shape
    grp = SWA_SAMPLE_GROUP
    blk = lambda r, w: pl.BlockSpec((grp, r, w), lambda i: (i, 0, 0))
    return pl.pallas_call(
        _swa_sample_kernel,
        grid=(b // grp,),
        in_specs=[pl.BlockSpec(memory_space=pltpu.SMEM),
                  blk(t, SWA_WIDTH), blk(t, 2 * SWA_KV_WIDTH), blk(WINDOW, SWA_KV_WIDTH), blk(WINDOW, SWA_KV_WIDTH)],
        out_specs=[blk(t, SWA_WIDTH), blk(WINDOW, SWA_KV_WIDTH), blk(WINDOW, SWA_KV_WIDTH)],
        out_shape=[jax.ShapeDtypeStruct((b, t, SWA_WIDTH), BF16),
                   jax.ShapeDtypeStruct((b, WINDOW, SWA_KV_WIDTH), F32),
                   jax.ShapeDtypeStruct((b, WINDOW, SWA_KV_WIDTH), F32)],
        compiler_params=_cparams("parallel"),
        name="swa_sample",
    )(sinks, q, kv_new, k_cache, v_cache)


def _route(logits):
    lane = _iota2(logits.shape, 1).astype(F32)
    neg = -jnp.inf

    def first_argmax(vals, valid):
        v = jnp.where(valid, vals, neg)
        m = jnp.max(v, axis=-1, keepdims=True)
        idx = jnp.min(jnp.where(jnp.logical_and(valid, v == m), lane, float(LANES)), axis=-1, keepdims=True)
        return m, idx

    is_group = lane < N_GROUPS
    gmax, gidx = first_argmax(logits, is_group)
    p_group = 1.0 / jnp.sum(jnp.where(is_group, jnp.exp(logits - gmax), 0.0), axis=-1, keepdims=True)
    lo = N_GROUPS + gidx * EXPERTS_PER_GROUP
    in_group = jnp.logical_and(lane >= lo, lane < lo + EXPERTS_PER_GROUP)
    m1, i1 = first_argmax(logits, in_group)
    esum = jnp.sum(jnp.where(in_group, jnp.exp(logits - m1), 0.0), axis=-1, keepdims=True)
    m2, i2 = first_argmax(logits, jnp.logical_and(in_group, lane != i1))
    p1 = 1.0 / esum
    p2 = jnp.exp(m2 - m1) / esum
    tot = p1 + p2
    return i1 - N_GROUPS, i2 - N_GROUPS, p_group * p1 / tot, p_group * p2 / tot


def _post_mixer_kernel(oa_ref, ob_ref, ga_ref, gb_ref, x_ref, gt_ref, sc_ref, sh_ref,
                       wa_ref, wb_ref, wo_ref, gpost_ref, gpre_ref, wr_ref, br_ref, cnt0_ref,
                       x1_ref, h2_ref, rt_ref, cnt_out_ref, meta_ref, *rest):
    cnt_ref = rest[-1]
    if len(rest) == 2:
        rest[0][...] = jnp.zeros_like(rest[0])
    step = pl.program_id(0)

    @pl.when(step == 0)
    def _():
        cnt_ref[...] = cnt0_ref[...]

    merged = (ga_ref[...].astype(F32) * jnp.dot(oa_ref[...], wa_ref[...], preferred_element_type=F32)
              + gb_ref[...].astype(F32) * jnp.dot(ob_ref[...], wb_ref[...], preferred_element_type=F32))
    mix = _bdot(merged, wo_ref[...])
    x1 = x_ref[...] + gt_ref[0] * _rms(mix, gpost_ref[...])
    x1_ref[...] = x1
    h2 = _rms(x1, gpre_ref[...]) * (1.0 + sc_ref[0]) + sh_ref[0]
    _rows_to_tiles(h2_ref, h2)
    h_hi = h2.astype(BF16)
    h_lo = (h2 - h_hi.astype(F32)).astype(BF16)
    part = jnp.dot(h_hi, wr_ref[...], preferred_element_type=F32)
    logits = (part[:, :LANES] + part[:, LANES:]
              + jnp.dot(h_lo, wr_ref[:, :LANES], preferred_element_type=F32) + br_ref[...])
    ia, ib, wa, wb = _route(logits)
    lane = _iota2(logits.shape, 1)
    tm = logits.shape[0]
    lane_f = lane.astype(F32)
    hot_a = (lane_f == ia).astype(F32)
    hot_b = (lane_f == ib).astype(F32)
    hot = hot_a + hot_b
    earlier = (_iota2((tm, tm), 0) > _iota2((tm, tm), 1)).astype(BF16)
    before = jnp.dot(earlier, hot.astype(BF16), preferred_element_type=F32) + cnt_ref[...]
    rank_a = jnp.sum(hot_a * before, axis=-1, keepdims=True)
    rank_b = jnp.sum(hot_b * before, axis=-1, keepdims=True)
    cnt_ref[...] = cnt_ref[...] + jnp.sum(hot, axis=0, keepdims=True)
    cnt_out_ref[...] = cnt_ref[...]
    rt_ref[...] = jnp.where(lane == 0, ia, jnp.where(lane == 1, ib, jnp.where(lane == 2, wa, jnp.where(
        lane == 3, wb, 0.0))))
    packed = jnp.where(lane == 0, rank_a * N_EXPERTS + ia, jnp.where(lane == 1, rank_b * N_EXPERTS + ib, 0.0))
    meta_ref[0] = packed.T[0:TOP_K, :].astype(I32)


def _post_mixer(oa, ob, ga, gb, x2d, gt, sc, sh, w_a, w_b, w_o, g_post, g_pre, w_rt, b_rt, cnt0, tm,
                zero_rows=0):
    rows = x2d.shape[0]
    n_tiles = rows // tm
    row_blk = lambda w: pl.BlockSpec((tm, w), lambda i: (i, 0))
    full = lambda r, c: pl.BlockSpec((r, c), lambda i: (0, 0))
    out_specs = [row_blk(D_MODEL), pl.BlockSpec((tm * TILE_ROWS, LANES), lambda i: (i, 0)), row_blk(LANES),
                 full(1, LANES), pl.BlockSpec((1, TOP_K, tm), lambda i: (i, 0, 0))]
    out_shape = [jax.ShapeDtypeStruct((rows, D_MODEL), F32),
                 jax.ShapeDtypeStruct((rows * TILE_ROWS, LANES), F32),
                 jax.ShapeDtypeStruct((rows, LANES), F32),
                 jax.ShapeDtypeStruct((1, LANES), F32),
                 jax.ShapeDtypeStruct((n_tiles, TOP_K, tm), I32)]
    if zero_rows:
        assert zero_rows % (n_tiles * TILE_ROWS) == 0
        out_specs.append(pl.BlockSpec((zero_rows // n_tiles, LANES), lambda i: (i, 0)))
        out_shape.append(jax.ShapeDtypeStruct((zero_rows, LANES), F32))
    return pl.pallas_call(
        _post_mixer_kernel,
        grid=(n_tiles,),
        in_specs=[row_blk(GDN_WIDTH), row_blk(SWA_WIDTH), row_blk(D_MODEL), row_blk(D_MODEL), row_blk(D_MODEL),
                  _mod_spec(gt, n_tiles), _mod_spec(sc, n_tiles), _mod_spec(sh, n_tiles),
                  full(GDN_WIDTH, D_MODEL), full(SWA_WIDTH, D_MODEL), full(D_MODEL, D_MODEL),
                  full(1, D_MODEL), full(1, D_MODEL), full(D_MODEL, 2 * LANES), full(1, LANES), full(1, LANES)],
        out_specs=out_specs,
        out_shape=out_shape,
        scratch_shapes=[pltpu.VMEM((1, LANES), F32)],
        compiler_params=_cparams("arbitrary"),
        name="post_mixer",
    )(oa, ob, ga, gb, x2d, gt, sc, sh, w_a, w_b, w_o,
      g_post.reshape(1, D_MODEL), g_pre.reshape(1, D_MODEL), w_rt, b_rt, cnt0)


TILE_ROWS = D_MODEL // LANES


def _tiles_to_rows(ref, first, rows):
    base = first * TILE_ROWS
    return jnp.concatenate([ref[pl.ds(base + c, rows, stride=TILE_ROWS), :] for c in range(TILE_ROWS)], axis=1)


def _rows_to_tiles(ref, mat):
    rows = mat.shape[0]
    for c in range(TILE_ROWS):
        ref[pl.ds(c, rows, stride=TILE_ROWS), :] = mat[:, c * LANES:(c + 1) * LANES]


def _tile_copy_loop(n, copies, start):
    def body(t, carry):
        for j, cp in enumerate(copies(t)):
            if start:
                cp.start(priority=j % 2)
            else:
                cp.wait()
        return carry

    lax.fori_loop(0, n, body, 0, unroll=8)


def _slots_kernel(pstart_ref, meta_ref, o_ref):
    packed = meta_ref[...]
    expert = packed & (N_EXPERTS - 1)
    first = jnp.zeros_like(packed)
    for e in range(N_EXPERTS):
        first = jnp.where(expert == e, pstart_ref[e], first)
    o_ref[...] = first + lax.shift_right_logical(packed, N_EXPERTS.bit_length() - 1)


def _slots(meta, pstarts):
    return pl.pallas_call(
        _slots_kernel,
        grid_spec=pltpu.PrefetchScalarGridSpec(
            num_scalar_prefetch=1,
            grid=(1,),
            in_specs=[pl.BlockSpec(meta.shape, lambda i, p: (0, 0, 0))],
            out_specs=pl.BlockSpec(meta.shape, lambda i, p: (0, 0, 0))),
        out_shape=jax.ShapeDtypeStruct(meta.shape, I32),
        compiler_params=_cparams("arbitrary"),
        name="moe_slots",
    )(pstarts, meta)


def _meta_index(token, k):
    tile = lax.shift_right_logical(token, ROW_TILE.bit_length() - 1)
    return (tile * TOP_K + k) * ROW_TILE + (token & (ROW_TILE - 1))


def _dispatch_kernel(slot_ref, h_ref, xs_in_hbm, xs_hbm, sem):
    del xs_in_hbm
    i = pl.program_id(0)
    tm = h_ref.shape[0] // TILE_ROWS

    def copies(t):
        src = h_ref.at[pl.ds(pl.multiple_of(t * TILE_ROWS, TILE_ROWS), TILE_ROWS)]
        return [pltpu.make_async_copy(src, xs_hbm.at[slot_ref[_meta_index(i * tm + t, k)]], sem)
                for k in range(TOP_K)]

    _tile_copy_loop(tm, copies, True)
    _tile_copy_loop(tm, copies, False)


def _dispatch(slots, h_tiles, xs, tm):
    n_tiles = h_tiles.shape[0] // (tm * TILE_ROWS)
    return pl.pallas_call(
        _dispatch_kernel,
        grid_spec=pltpu.PrefetchScalarGridSpec(
            num_scalar_prefetch=1,
            grid=(n_tiles,),
            in_specs=[pl.BlockSpec((tm * TILE_ROWS, LANES), lambda i, s: (i, 0)),
                      pl.BlockSpec(memory_space=pl.ANY)],
            out_specs=pl.BlockSpec(memory_space=pl.ANY),
            scratch_shapes=[pltpu.SemaphoreType.DMA(())]),
        out_shape=jax.ShapeDtypeStruct(xs.shape, xs.dtype),
        input_output_aliases={2: 0},
        compiler_params=_cparams("arbitrary"),
        name="moe_dispatch",
    )(slots, h_tiles, xs)


def _moe_kernel(blk_e_ref, n_used_ref, x_ref, wg_ref, wu_ref, wd_ref, y_ref, wgb, wub, wdb):
    b = pl.program_id(0)
    rows = x_ref.shape[0] // TILE_ROWS
    changed = jnp.logical_or(b == 0, blk_e_ref[b] != blk_e_ref[jnp.maximum(b - 1, 0)])

    @pl.when(changed)
    def _():
        wgb[...] = wg_ref[0].astype(BF16)
        wub[...] = wu_ref[0].astype(BF16)
        wdb[...] = wd_ref[0].astype(BF16)

    @pl.when(b < n_used_ref[0])
    def _():
        x = _tiles_to_rows(x_ref, 0, rows).astype(BF16)
        gate = jnp.dot(x, wgb[...], preferred_element_type=F32)
        up = jnp.dot(x, wub[...], preferred_element_type=F32)
        _rows_to_tiles(y_ref, _bdot(_silu(gate) * up, wdb[...]))

    @pl.when(b >= n_used_ref[0])
    def _():
        y_ref[...] = jnp.zeros_like(y_ref)


def _moe(xs_tiles, blk_e, n_used, w_gate, w_up, w_down):
    n_blocks = blk_e.shape[0]
    rows = MOE_ROWS
    wspec = lambda r, c: pl.BlockSpec((1, r, c), lambda b, be, nu: (be[b], 0, 0))
    xspec = pl.BlockSpec((rows * TILE_ROWS, LANES), lambda b, be, nu: (b, 0))
    return pl.pallas_call(
        _moe_kernel,
        grid_spec=pltpu.PrefetchScalarGridSpec(
            num_scalar_prefetch=2,
            grid=(n_blocks,),
            in_specs=[xspec, wspec(D_MODEL, EXPERT_FF), wspec(D_MODEL, EXPERT_FF), wspec(EXPERT_FF, D_MODEL)],
            out_specs=xspec,
            scratch_shapes=[pltpu.VMEM((D_MODEL, EXPERT_FF), BF16),
                            pltpu.VMEM((D_MODEL, EXPERT_FF), BF16),
                            pltpu.VMEM((EXPERT_FF, D_MODEL), BF16)]),
        out_shape=jax.ShapeDtypeStruct(xs_tiles.shape, F32),
        compiler_params=_cparams("arbitrary"),
        name="moe_experts",
    )(blk_e, n_used, xs_tiles, w_gate, w_up, w_down)


def _combine_kernel(slot_ref, y_hbm, x1_ref, rt_ref, gt_ref, gpost_ref, o_ref, ybuf, sems):
    i = pl.program_id(0)
    n = pl.num_programs(0)
    rows = ybuf.shape[1] // TILE_ROWS
    slot = i % 2

    def gather(step, buf_slot, start):
        def copy(r):
            dst = ybuf.at[buf_slot, pl.ds(pl.multiple_of(r * TILE_ROWS, TILE_ROWS), TILE_ROWS)]
            return pltpu.make_async_copy(y_hbm.at[slot_ref[step * rows + r]], dst, sems.at[buf_slot])

        _tile_copy_loop(rows // 2, lambda t: [copy(2 * t), copy(2 * t + 1)], start)

    @pl.when(i == 0)
    def _():
        gather(0, 0, True)

    @pl.when(i + 1 < n)
    def _():
        gather(i + 1, 1 - slot, True)

    gather(i, slot, False)
    half = rows // 2
    rt = rt_ref[...]
    buf = ybuf.at[slot]
    f = rt[:, 2:3] * _tiles_to_rows(buf, 0, half) + rt[:, 3:4] * _tiles_to_rows(buf, half, half)
    o_ref[...] = x1_ref[...] + gt_ref[0] * _rms(f, gpost_ref[...])


def _combine(slots, yb, x1, rt, gt, g_post):
    rows = x1.shape[0]
    tm = ROW_TILE
    n_tiles = rows // tm
    tiles_per_mod = n_tiles // gt.shape[0]
    return pl.pallas_call(
        _combine_kernel,
        grid_spec=pltpu.PrefetchScalarGridSpec(
            num_scalar_prefetch=1,
            grid=(n_tiles,),
            in_specs=[pl.BlockSpec(memory_space=pl.ANY),
                      pl.BlockSpec((tm, D_MODEL), lambda i, s: (i, 0)),
                      pl.BlockSpec((tm, LANES), lambda i, s: (i, 0)),
                      pl.BlockSpec((1, gt.shape[1], D_MODEL), lambda i, s: (i // tiles_per_mod, 0, 0)),
                      pl.BlockSpec((1, D_MODEL), lambda i, s: (0, 0))],
            out_specs=pl.BlockSpec((tm, D_MODEL), lambda i, s: (i, 0)),
            scratch_shapes=[pltpu.VMEM((2, TOP_K * tm * TILE_ROWS, LANES), F32),
                            pltpu.SemaphoreType.DMA((2,))]),
        out_shape=jax.ShapeDtypeStruct((rows, D_MODEL), F32),
        compiler_params=_cparams("arbitrary"),
        name="moe_combine",
    )(slots, yb, x1, rt, gt, g_post.reshape(1, D_MODEL))


def _dispatch_plan(counts, n_tok):
    pcounts = (counts + MOE_ROWS - 1) // MOE_ROWS * MOE_ROWS
    pends = jnp.cumsum(pcounts)
    pstarts = (pends - pcounts).astype(I32)
    n_blocks = n_tok * TOP_K // MOE_ROWS + N_EXPERTS
    blk_start = jnp.arange(n_blocks, dtype=I32) * MOE_ROWS
    blk_e = jnp.minimum(jnp.sum(blk_start[:, None] >= pends[None, :], axis=1), N_EXPERTS - 1).astype(I32)
    n_used = (pends[-1] // MOE_ROWS).astype(I32).reshape(1)
    return blk_e, n_used, pstarts


def _prep_in_weight_kernel(w_ref, o_ref):
    a0 = 4 * GDN_QK_WIDTH
    a1 = a0 + 2 * GDN_HEADS
    rows = w_ref.shape[0]
    for c in range(0, a0, PROJ_TILE):
        o_ref[:, c:c + PROJ_TILE] = w_ref[:, c:c + PROJ_TILE].astype(BF16)
    for c in range(a0, _C_AB, PROJ_TILE):
        o_ref[:, c:c + PROJ_TILE] = w_ref[:, c + a1 - a0:c + a1 - a0 + PROJ_TILE].astype(BF16)
    ab = jnp.concatenate([w_ref[:, a0:a1], jnp.zeros((rows, LANES - (a1 - a0)), F32)], axis=1)
    o_ref[:, _C_AB:IN_COLS] = ab.astype(BF16)


def _prep_in_weight(w_in):
    rows = 128
    return pl.pallas_call(
        _prep_in_weight_kernel,
        grid=(D_MODEL // rows,),
        in_specs=[pl.BlockSpec((rows, w_in.shape[1]), lambda i: (i, 0))],
        out_specs=pl.BlockSpec((rows, IN_COLS), lambda i: (i, 0)),
        out_shape=jax.ShapeDtypeStruct((D_MODEL, IN_COLS), BF16),
        compiler_params=_cparams("parallel"),
        name="prep_in_weight",
    )(w_in)


def _head_param_tile(a_log, dt_bias):
    tile = jnp.zeros((8, LANES), F32)
    return tile.at[0, :GDN_HEADS].set(a_log.astype(F32)).at[1, :GDN_HEADS].set(dt_bias.astype(F32))


def _router_weight(w_group, b_group, w_router, b_router):
    w = jnp.zeros((D_MODEL, LANES), F32)
    w = w.at[:, :N_GROUPS].set(w_group).at[:, N_GROUPS:N_GROUPS + N_EXPERTS].set(w_router)
    b = jnp.zeros((1, LANES), F32)
    b = b.at[0, :N_GROUPS].set(b_group).at[0, N_GROUPS:N_GROUPS + N_EXPERTS].set(b_router)
    w_hi = w.astype(BF16)
    w_lo = (w - w_hi.astype(F32)).astype(BF16)
    return jnp.concatenate([w_hi, w_lo], axis=1), b


def kernel(x_prompt, x_sample, state_gdn, state_conv, cache_k_win, cache_v_win, c_prompt, c_sample, w_ada, b_ada, g_mix_pre, g_mix_post, g_ffn_pre, g_ffn_post, w_in, conv_w, a_log, dt_bias, gdn_norm, sinks, w_br_gdn, w_br_swa, w_out, w_group, b_group, w_router, b_router, w_gate, w_up, w_down):
    depth = w_ada.shape[0]
    assert depth == 1, "single-layer trunk"
    bp, tp, _ = x_prompt.shape
    bs, ts, _ = x_sample.shape
    n_p = bp * tp
    n_s = bs * ts
    tm = ROW_TILE
    assert tp % tm == 0 and n_s % tm == 0 and ts >= GDN_CONV - 1 and ts + GDN_CONV - 1 <= GDN_SAMPLE_CHUNK
    assert ts & (ts - 1) == 0 and ts <= BF16_ROWS

    c_all = jnp.concatenate([c_prompt, c_sample], axis=0)
    c_rows = -(-c_all.shape[0] // 8) * 8
    c_all = jnp.pad(c_all, ((0, c_rows - c_all.shape[0]), (0, 0)))
    mod = _adaln(c_all, w_ada[0], b_ada[0])
    mods_p = [m[:bp].reshape(bp, 1, D_MODEL) for m in jnp.split(mod, 6, axis=-1)]
    mods_s = [jnp.repeat(m[bp:bp + bs], ts, axis=0).reshape(n_s // tm, tm, D_MODEL)
              for m in jnp.split(mod, 6, axis=-1)]

    w_prep = _prep_in_weight(w_in[0])
    head_params = _head_param_tile(a_log[0], dt_bias[0])
    w_a, w_b, w_o = w_br_gdn[0].astype(BF16), w_br_swa[0].astype(BF16), w_out[0].astype(BF16)
    w_rt, b_rt = _router_weight(w_group[0], b_group[0], w_router[0], b_router[0])
    sinks0 = sinks[0].astype(F32)

    xp2d = x_prompt.reshape(n_p, D_MODEL)
    sh1, sc1, gt1, sh2, sc2, gt2 = mods_p
    qkv_p, zs_p, qb_p, kvb_p, ga_p, gb_p, ab_p, conv_tail_p = _inproj(
        xp2d, g_mix_pre[0], sc1, sh1, w_prep, INPROJ_ROWS, conv_w=conv_w[0].astype(F32), n_seq=bp)
    qkv_p3 = qkv_p.reshape(bp, tp, GDN_CONV_DIM)
    u, w, qd, kd, qk, ge = _gdn_prep(qkv_p3, ab_p.reshape(bp, tp, LANES), head_params)
    oa_p, s_prompt = _gdn_scan(u, w, qd, kd, qk, ge, zs_p.reshape(bp, tp, GDN_WIDTH), gdn_norm[0])
    kvb_p3 = kvb_p.reshape(bp, tp, 2 * SWA_KV_WIDTH)
    ob_p = _swa_prompt(qb_p.reshape(bp, tp, SWA_WIDTH), kvb_p3, sinks0)
    n_slots = ((n_p + n_s) * TOP_K // MOE_ROWS + N_EXPERTS) * MOE_ROWS
    x1_p, h2_p, rt_p, cnt_p, meta_p, xs = _post_mixer(
        oa_p.reshape(n_p, GDN_WIDTH), ob_p.reshape(n_p, SWA_WIDTH), ga_p, gb_p, xp2d, gt1, sc2, sh2,
        w_a, w_b, w_o, g_mix_post[0], g_ffn_pre[0], w_rt, b_rt, jnp.zeros((1, LANES), F32), tm,
        zero_rows=n_slots * TILE_ROWS)

    xs2d = x_sample.reshape(n_s, D_MODEL)
    sh1s, sc1s, gt1s, sh2s, sc2s, gt2s = mods_s
    qkv_s, zs_s, qb_s, kvb_s, ga_s, gb_s, ab_s = _inproj(xs2d, g_mix_pre[0], sc1s, sh1s, w_prep, tm)
    cc = GDN_SAMPLE_CHUNK
    pad_rows = cc - ts - (GDN_CONV - 1)
    qkv_s3 = qkv_s.reshape(bs, ts, GDN_CONV_DIM)
    xp_s = jnp.concatenate([jnp.zeros((bs, pad_rows, GDN_CONV_DIM), BF16), state_conv[0].astype(BF16), qkv_s3],
                           axis=1)
    front = lambda a: jnp.pad(a, ((0, 0), (cc - ts, 0), (0, 0)))
    oa_s16, s_sample = _gdn_sample(xp_s, front(ab_s.reshape(bs, ts, LANES)), front(zs_s.reshape(bs, ts, GDN_WIDTH)),
                                   state_gdn[0].astype(F32), conv_w[0], head_params, gdn_norm[0], ts)
    oa_s = oa_s16[:, cc - ts:, :].reshape(n_s, GDN_WIDTH)
    ob_s, k_new_s, v_new_s = _swa_sample(
        qb_s.reshape(bs, ts, SWA_WIDTH), kvb_s.reshape(bs, ts, 2 * SWA_KV_WIDTH),
        cache_k_win[0].reshape(bs, WINDOW, SWA_KV_WIDTH).astype(F32),
        cache_v_win[0].reshape(bs, WINDOW, SWA_KV_WIDTH).astype(F32), sinks0)
    x1_s, h2_s, rt_s, cnt_all, meta_s = _post_mixer(
        oa_s, ob_s.reshape(n_s, SWA_WIDTH), ga_s, gb_s, xs2d, gt1s, sc2s, sh2s,
        w_a, w_b, w_o, g_mix_post[0], g_ffn_pre[0], w_rt, b_rt, cnt_p, tm)

    blk_e, n_used, pstarts = _dispatch_plan(cnt_all[0, :N_EXPERTS].astype(I32), n_p + n_s)
    assert n_slots == blk_e.shape[0] * MOE_ROWS
    slots_p = _slots(meta_p, pstarts).reshape(-1)
    slots_s = _slots(meta_s, pstarts).reshape(-1)
    xs = _dispatch(slots_p, h2_p, xs.reshape(n_slots, TILE_ROWS, LANES), min(DISPATCH_ROWS, n_p))
    xs = _dispatch(slots_s, h2_s, xs, min(DISPATCH_ROWS, n_s))
    yb = _moe(xs.reshape(n_slots * TILE_ROWS, LANES), blk_e, n_used, w_gate[0], w_up[0], w_down[0])
    yb = yb.reshape(n_slots, TILE_ROWS, LANES)
    y_p = _combine(slots_p, yb, x1_p, rt_p, gt2, g_ffn_post[0])
    y_s = _combine(slots_s, yb, x1_s, rt_s, gt2s, g_ffn_post[0])

    f32 = lambda a: a.astype(F32)
    kv_tail = kvb_p3[:, tp - WINDOW:, :]
    kv_heads = lambda a: f32(a).reshape(a.shape[0], WINDOW, SWA_KV_HEADS, SWA_HEAD_DIM)[None]
    return (y_p.reshape(bp, tp, D_MODEL), y_s.reshape(bs, ts, D_MODEL),
            s_prompt[None], conv_tail_p[:, 8 - (GDN_CONV - 1):, :][None],
            kv_heads(kv_tail[:, :, :SWA_KV_WIDTH]), kv_heads(kv_tail[:, :, SWA_KV_WIDTH:]),
            s_sample[None], f32(qkv_s3[:, ts - (GDN_CONV - 1):, :])[None],
            kv_heads(k_new_s), kv_heads(v_new_s))
```

```python
import functools

import jax
import jax.numpy as jnp
from jax import lax
from jax.experimental import pallas as pl
from jax.experimental.pallas import tpu as pltpu

F32 = jnp.float32
BF16 = jnp.bfloat16
I32 = jnp.int32

D_MODEL = 1024
NORM_EPS = 1e-6
GDN_HEADS = 8
GDN_DK = 128
GDN_DV = 128
GDN_CONV = 4
GDN_CHUNK = 64
GDN_QK_WIDTH = GDN_HEADS * GDN_DK
GDN_WIDTH = GDN_HEADS * GDN_DV
GDN_CONV_DIM = 2 * GDN_QK_WIDTH + GDN_WIDTH
SWA_Q_HEADS = 16
SWA_KV_HEADS = 4
SWA_HEAD_DIM = 64
SWA_GROUP = SWA_Q_HEADS // SWA_KV_HEADS
SWA_WIDTH = SWA_Q_HEADS * SWA_HEAD_DIM
SWA_KV_WIDTH = SWA_KV_HEADS * SWA_HEAD_DIM
WINDOW = 128
N_GROUPS = 4
EXPERTS_PER_GROUP = 8
N_EXPERTS = N_GROUPS * EXPERTS_PER_GROUP
TOP_K = 2
EXPERT_FF = 512

LANES = 128
BF16_ROWS = 16
VMEM_LIMIT = 56 * 1024 * 1024

_C_QKV = 0
_C_Z = _C_QKV + GDN_CONV_DIM
_C_QB = _C_Z + GDN_WIDTH
_C_KVB = _C_QB + SWA_WIDTH
_C_GA = _C_KVB + 2 * SWA_KV_WIDTH
_C_GB = _C_GA + D_MODEL
_C_AB = _C_GB + D_MODEL
IN_COLS = _C_AB + LANES
PROJ_TILE = 512

ROW_TILE = 512
INPROJ_ROWS = 512
GDN_PREP_ROWS = 256
GDN_SCAN_ROWS = 512
GDN_SAMPLE_CHUNK = 16
GDN_SAMPLE_GROUP = 8
SWA_SAMPLE_GROUP = 8
SWA_BLOCKS_PER_STEP = 2
MOE_ROWS = 256
DISPATCH_ROWS = 1024


def _cparams(*sem):
    return pltpu.CompilerParams(dimension_semantics=sem, vmem_limit_bytes=VMEM_LIMIT)


def _bdot(a, b):
    return jnp.dot(a.astype(BF16), b.astype(BF16), preferred_element_type=F32)


def _bdot_nt(a, b):
    return lax.dot_general(a.astype(BF16), b.astype(BF16), (((1,), (1,)), ((), ())),
                           preferred_element_type=F32)


def _bdot_tn(a, b):
    return lax.dot_general(a.astype(BF16), b.astype(BF16), (((0,), (0,)), ((), ())),
                           preferred_element_type=F32)


def _sigmoid(x):
    return 1.0 / (1.0 + jnp.exp(-x))


def _silu(x):
    return x * _sigmoid(x)


def _rms(x, gain):
    return x * lax.rsqrt(jnp.mean(x * x, axis=-1, keepdims=True) + NORM_EPS) * gain


def _iota2(shape, dim):
    return lax.broadcasted_iota(I32, shape, dim)


def _adaln_kernel(c_ref, w_ref, b_ref, o_ref):
    o_ref[...] = _bdot(_silu(c_ref[...]), w_ref[...]) + b_ref[...]


def _adaln(c_all, w_ada, b_ada):
    rows = c_all.shape[0]
    n_out = w_ada.shape[1]
    tn = D_MODEL
    return pl.pallas_call(
        _adaln_kernel,
        grid=(n_out // tn,),
        in_specs=[pl.BlockSpec((rows, D_MODEL), lambda j: (0, 0)),
                  pl.BlockSpec((D_MODEL, tn), lambda j: (0, j)),
                  pl.BlockSpec((1, tn), lambda j: (0, j))],
        out_specs=pl.BlockSpec((rows, tn), lambda j: (0, j)),
        out_shape=jax.ShapeDtypeStruct((rows, n_out), F32),
        compiler_params=_cparams("arbitrary"),
        name="adaln",
    )(c_all, w_ada, b_ada.reshape(1, n_out))


def _inproj_kernel(tiles_per_seq, x_ref, g_ref, sc_ref, sh_ref, w_ref, *rest):
    if tiles_per_seq:
        cw_ref, qkv_ref, z_ref, qb_ref, kvb_ref, ga_ref, gb_ref, ab_ref, tail_ref, carry_ref = rest
    else:
        qkv_ref, z_ref, qb_ref, kvb_ref, ga_ref, gb_ref, ab_ref = rest
    tm = x_ref.shape[0]
    h = (_rms(x_ref[...], g_ref[...]) * (1.0 + sc_ref[0]) + sh_ref[0]).astype(BF16)

    def fill(ref, c0, width, fn):
        step = min(PROJ_TILE, width)
        for c in range(0, width, step):
            acc = jnp.dot(h, w_ref[:, c0 + c:c0 + c + step], preferred_element_type=F32)
            ref[:, c:c + step] = fn(acc, c, step).astype(ref.dtype)

    def conv_act(acc, c, step):
        cols = slice(c, c + step)
        seq_start = pl.program_id(0) % tiles_per_seq == 0
        prev = jnp.where(seq_start, 0.0, carry_ref[:, cols])
        last = acc[tm - 8:tm]
        carry_ref[:, cols] = last
        tail_ref[0, :, cols] = last
        ext = jnp.concatenate([prev, acc], axis=0)
        y = cw_ref[GDN_CONV - 1:GDN_CONV, cols] * acc
        for j in range(GDN_CONV - 1):
            y = y + cw_ref[j:j + 1, cols] * ext[8 - (GDN_CONV - 1) + j:8 - (GDN_CONV - 1) + j + tm]
        y = _silu(y)
        if c >= 2 * GDN_QK_WIDTH:
            return y
        scale = GDN_DK ** -0.5 if c < GDN_QK_WIDTH else 1.0
        heads = [_l2n(y[:, d:d + GDN_DK]) * scale for d in range(0, step, GDN_DK)]
        return jnp.concatenate(heads, axis=1)

    ident = lambda v, c, step: v
    silu = lambda v, c, step: _silu(v)
    sigmoid = lambda v, c, step: _sigmoid(v)
    fill(qkv_ref, _C_QKV, GDN_CONV_DIM, conv_act if tiles_per_seq else ident)
    fill(z_ref, _C_Z, GDN_WIDTH, silu)
    fill(qb_ref, _C_QB, SWA_WIDTH, ident)
    fill(kvb_ref, _C_KVB, 2 * SWA_KV_WIDTH, ident)
    fill(ga_ref, _C_GA, D_MODEL, sigmoid)
    fill(gb_ref, _C_GB, D_MODEL, sigmoid)
    fill(ab_ref, _C_AB, LANES, ident)


def _mod_spec(mod, n_tiles):
    tiles_per_mod = n_tiles // mod.shape[0]
    return pl.BlockSpec((1, mod.shape[1], D_MODEL), lambda i: (i // tiles_per_mod, 0, 0))


def _inproj(x2d, gain, sc, sh, w_prep, tm, conv_w=None, n_seq=0):
    rows = x2d.shape[0]
    n_tiles = rows // tm
    widths = (GDN_CONV_DIM, GDN_WIDTH, SWA_WIDTH, 2 * SWA_KV_WIDTH, D_MODEL, D_MODEL, LANES)
    dtypes = (BF16, BF16, BF16, BF16, BF16, BF16, F32)
    in_specs = [pl.BlockSpec((tm, D_MODEL), lambda i: (i, 0)),
                pl.BlockSpec((1, D_MODEL), lambda i: (0, 0)),
                _mod_spec(sc, n_tiles), _mod_spec(sh, n_tiles),
                pl.BlockSpec((D_MODEL, IN_COLS), lambda i: (0, 0))]
    out_specs = [pl.BlockSpec((tm, w), lambda i: (i, 0)) for w in widths]
    out_shape = [jax.ShapeDtypeStruct((rows, w), dt) for w, dt in zip(widths, dtypes)]
    args = [x2d, gain.reshape(1, D_MODEL), sc, sh, w_prep]
    scratch = []
    tiles_per_seq = 0
    if conv_w is not None:
        tiles_per_seq = n_tiles // n_seq
        in_specs.append(pl.BlockSpec((GDN_CONV, GDN_CONV_DIM), lambda i: (0, 0)))
        out_specs.append(pl.BlockSpec((1, 8, GDN_CONV_DIM), lambda i: (i // tiles_per_seq, 0, 0)))
        out_shape.append(jax.ShapeDtypeStruct((n_seq, 8, GDN_CONV_DIM), F32))
        args.append(conv_w)
        scratch.append(pltpu.VMEM((8, GDN_CONV_DIM), F32))
    return pl.pallas_call(
        functools.partial(_inproj_kernel, tiles_per_seq),
        grid=(n_tiles,),
        in_specs=in_specs,
        out_specs=out_specs,
        out_shape=out_shape,
        scratch_shapes=scratch,
        compiler_params=_cparams("arbitrary"),
        name="inproj",
    )(*args)


def _cumsum_rows(g):
    c = g.shape[0]
    tril = (_iota2((c, c), 0) >= _iota2((c, c), 1)).astype(BF16)
    hi = g.astype(BF16)
    r1 = g - hi.astype(F32)
    mid = r1.astype(BF16)
    lo = (r1 - mid.astype(F32)).astype(BF16)
    dot = lambda p: jnp.dot(tril, p, preferred_element_type=F32)
    return dot(hi) + dot(mid) + dot(lo)


def _each(fn, *lists):
    return [fn(*args) for args in zip(*lists)]


def _pair_blockdiag(m, left):
    return jnp.concatenate([jnp.where(left, m, 0.0), jnp.where(left, 0.0, m)], axis=0)


def _unit_lower_inverse_offset(a_list, ii, jj, left):
    c = a_list[0].shape[0]

    def same_block(shift):
        return lax.shift_right_logical(ii, shift) == lax.shift_right_logical(jj, shift)

    base = same_block(1)
    n_list = _each(lambda a: jnp.where(base, -a, 0.0), a_list)
    shift = 1
    while (1 << shift) < c:
        outer, inner = same_block(shift + 1), same_block(shift)
        off_list = _each(lambda a: jnp.where(outer, jnp.where(inner, 0.0, a), 0.0), a_list)
        x_list = _each(lambda off, n: off + _bdot(off, _pair_blockdiag(n, left)), off_list, n_list)
        n_list = _each(lambda n, x: n - x - _bdot(n, _pair_blockdiag(x, left)), n_list, x_list)
        shift += 1
    return n_list


def _chunk_prep(q, k, v, gcol, grow, bcol):
    c = q[0].shape[0]
    assert len(q) % 2 == 0
    ii = _iota2((c, 2 * c), 0)
    lane = _iota2((c, 2 * c), 1)
    left = lane < c
    jj = lane & (c - 1)
    causal = ii >= jj
    strict = ii > jj
    first, second = slice(0, None, 2), slice(1, None, 2)
    kb = _each(lambda kk, b: kk * b, k, bcol)
    both = _each(lambda qa, ka, qb, kbb, x, y: _bdot_nt(jnp.concatenate([qa, ka, qb, kbb], axis=0),
                                                       jnp.concatenate([x, y], axis=0)),
                 q[first], kb[first], q[second], kb[second], k[first], k[second])
    decay = _each(lambda ga, gb, ra, rb: jnp.where(causal, jnp.exp(jnp.where(
        causal, jnp.where(left, ga, gb) - jnp.concatenate([ra, rb], axis=1), 0.0)), 0.0),
        gcol[first], gcol[second], grow[first], grow[second])
    qk = _each(lambda bo, d: jnp.where(left, bo[0:c], bo[2 * c:3 * c]) * d, both, decay)
    a = _each(lambda bo, d: jnp.where(strict, jnp.where(left, bo[c:2 * c], bo[3 * c:4 * c]) * d, 0.0), both, decay)
    n = _unit_lower_inverse_offset(a, ii, jj, left)
    eg = _each(jnp.exp, gcol)
    rhs = _each(lambda vv, b, kbb, e: jnp.concatenate([vv * b, kbb * e], axis=1), v, bcol, kb, eg)
    uw = _each(lambda ra, rb, nn: (lambda r: r + _bdot(_pair_blockdiag(nn, left), r))(
        jnp.concatenate([ra, rb], axis=0)), rhs[first], rhs[second], n)
    uw = [x[half] for x in uw for half in (slice(0, c), slice(c, 2 * c))]
    u = [x[:, :GDN_DV] for x in uw]
    w = [x[:, GDN_DV:] for x in uw]
    qd = _each(lambda qq, e: qq * e, q, eg)
    kd = _each(lambda kk, gc: kk * jnp.exp(gc[c - 1:c, :] - gc), k, gcol)
    return u, w, qd, kd, qk


def _chunk_step(s, u, w, qd, kd, qk, ge):
    c = u[0].shape[0]
    both = _each(lambda ww, qq, ss: _bdot(jnp.concatenate([ww, qq], axis=0), ss), w, qd, s)
    v_new = _each(lambda uu, bo: uu.astype(F32) - bo[:c], u, both)
    o = _each(lambda bo, m, vn: bo[c:] + _bdot(m, vn), both, qk, v_new)
    s_new = _each(lambda ss, g, kk, vn: ss * g + _bdot_tn(kk, vn), s, ge, kd, v_new)
    return o, s_new


def _conv_act(xp_ref, cw_ref, r0, rows, c0):
    cols = slice(c0, c0 + LANES)
    acc = cw_ref[3:4, cols] * xp_ref[r0:r0 + rows, cols]
    for j in range(GDN_CONV - 1):
        acc = acc + cw_ref[j:j + 1, cols] * xp_ref[r0 - 3 + j:r0 - 3 + j + rows, cols]
    return _silu(acc)


def _l2n(x):
    return x * lax.rsqrt(jnp.sum(x * x, axis=-1, keepdims=True) + NORM_EPS)


def _softplus(x):
    return jnp.maximum(x, 0.0) + jnp.log1p(jnp.exp(-jnp.abs(x)))


def _head_cols(hd):
    return (hd * GDN_DK, GDN_QK_WIDTH + hd * GDN_DK, 2 * GDN_QK_WIDTH + hd * GDN_DV)


def _activate_qkv(xp_ref, cw_ref, act_ref, r0, rows):
    for hd in range(GDN_HEADS):
        cq, ck, cv = _head_cols(hd)
        act_ref[0:rows, cq:cq + LANES] = _l2n(_conv_act(xp_ref, cw_ref, r0, rows, cq)) * (GDN_DK ** -0.5)
        act_ref[0:rows, ck:ck + LANES] = _l2n(_conv_act(xp_ref, cw_ref, r0, rows, ck))
        act_ref[0:rows, cv:cv + LANES] = _conv_act(xp_ref, cw_ref, r0, rows, cv)


def _decay_beta(ab, hp_ref):
    g = -jnp.exp(hp_ref[0:1, :]) * _softplus(ab + hp_ref[1:2, :])
    return g, _sigmoid(ab)


def _gdn_prep_kernel(act_ref, ab_ref, hp_ref, u_ref, w_ref, qd_ref, kd_ref, qk_ref, ge_ref):
    tb = act_ref.shape[1]
    cc = GDN_CHUNK
    g_all, beta_all = _decay_beta(ab_ref[0], hp_ref)

    chunks = [slice(ci * cc, (ci + 1) * cc) for ci in range(tb // cc)]
    gcs = _each(lambda rows: _cumsum_rows(g_all[rows, :]), chunks)
    gcts = _each(lambda gc: gc.T, gcs)
    for ci, gc in enumerate(gcs):
        ge_ref[0, ci] = jnp.exp(gc[cc - 1:cc, :])
    items = [(ci, hd) for ci in range(len(chunks)) for hd in range(GDN_HEADS)]
    col = lambda which: [act_ref[0, chunks[ci], _head_cols(hd)[which]:_head_cols(hd)[which] + LANES].astype(F32)
                         for ci, hd in items]
    u, w, qd, kd, qk = _chunk_prep(
        col(0), col(1), col(2),
        [gcs[ci][:, hd:hd + 1] for ci, hd in items], [gcts[ci][hd:hd + 1, :] for ci, hd in items],
        [beta_all[chunks[ci], GDN_HEADS + hd:GDN_HEADS + hd + 1] for ci, hd in items])
    for idx, (ci, hd) in enumerate(items):
        rows = chunks[ci]
        oc = slice(hd * GDN_DV, (hd + 1) * GDN_DV)
        u_ref[0, rows, oc] = u[idx].astype(BF16)
        w_ref[0, rows, oc] = w[idx].astype(BF16)
        qd_ref[0, rows, oc] = qd[idx].astype(BF16)
        kd_ref[0, rows, oc] = kd[idx].astype(BF16)
        if hd % 2 == 0:
            qk_ref[0, rows, hd * cc:(hd + 2) * cc] = qk[idx // 2].astype(BF16)


def _gdn_prep(qkv, ab, head_params):
    b, t, _ = qkv.shape
    tb = min(GDN_PREP_ROWS, t)
    nch = tb // GDN_CHUNK
    blk = lambda w: pl.BlockSpec((1, tb, w), lambda bi, i: (bi, i, 0))
    out_shapes = [jax.ShapeDtypeStruct((b, t, GDN_WIDTH), BF16)] * 4 + [
        jax.ShapeDtypeStruct((b, t, GDN_HEADS * GDN_CHUNK), BF16),
        jax.ShapeDtypeStruct((b, t // GDN_CHUNK, 1, LANES), F32)]
    return pl.pallas_call(
        _gdn_prep_kernel,
        grid=(b, t // tb),
        in_specs=[blk(GDN_CONV_DIM), blk(LANES), pl.BlockSpec((8, LANES), lambda bi, i: (0, 0))],
        out_specs=[blk(GDN_WIDTH)] * 4 + [
            blk(GDN_HEADS * GDN_CHUNK),
            pl.BlockSpec((1, nch, 1, LANES), lambda bi, i: (bi, i, 0, 0))],
        out_shape=out_shapes,
        compiler_params=_cparams("parallel", "parallel"),
        name="gdn_prep",
    )(qkv, ab, head_params)


def _gated_norm_store(o_ref, idx, o, gain, zs):
    o_ref[idx] = (_rms(o, gain) * zs.astype(F32)).astype(o_ref.dtype)


def _gdn_scan_kernel(u_ref, w_ref, qd_ref, kd_ref, qk_ref, ge_ref, zs_ref, gain_ref,
                     o_ref, s_out_ref, s_ref):
    nb, tb, _ = u_ref.shape
    cc = GDN_CHUNK
    step = pl.program_id(0)

    @pl.when(step == 0)
    def _():
        s_ref[...] = jnp.zeros_like(s_ref)

    gain = gain_ref[...]

    def chunk_body(ci, carry):
        rows = pl.ds(pl.multiple_of(ci * cc, cc), cc)
        items = [(bi, hd) for bi in range(nb) for hd in range(GDN_HEADS)]
        oc = lambda hd: slice(hd * GDN_DV, (hd + 1) * GDN_DV)
        ge_rows = [ge_ref[bi, ci] for bi in range(nb)]
        o, s_new = _chunk_step(
            [s_ref[bi * GDN_HEADS + hd] for bi, hd in items],
            [u_ref[bi, rows, oc(hd)] for bi, hd in items], [w_ref[bi, rows, oc(hd)] for bi, hd in items],
            [qd_ref[bi, rows, oc(hd)] for bi, hd in items], [kd_ref[bi, rows, oc(hd)] for bi, hd in items],
            [qk_ref[bi, rows, hd * cc:(hd + 1) * cc] for bi, hd in items],
            [ge_rows[bi][:, hd:hd + 1] for bi, hd in items])
        for idx, (bi, hd) in enumerate(items):
            s_ref[bi * GDN_HEADS + hd] = s_new[idx]
            _gated_norm_store(o_ref, (bi, rows, oc(hd)), o[idx], gain, zs_ref[bi, rows, oc(hd)])
        return carry

    lax.fori_loop(0, tb // cc, chunk_body, 0)

    @pl.when(step == pl.num_programs(0) - 1)
    def _():
        s_out_ref[...] = s_ref[...]


def _gdn_scan(u, w, qd, kd, qk, ge, zs, gain):
    b, t, _ = u.shape
    tb = min(GDN_SCAN_ROWS, t)
    nch = tb // GDN_CHUNK
    blk = lambda wd: pl.BlockSpec((b, tb, wd), lambda i: (0, i, 0))
    o, s = pl.pallas_call(
        _gdn_scan_kernel,
        grid=(t // tb,),
        in_specs=[blk(GDN_WIDTH)] * 4 + [
            blk(GDN_HEADS * GDN_CHUNK),
            pl.BlockSpec((b, nch, 1, LANES), lambda i: (0, i, 0, 0)),
            blk(GDN_WIDTH),
            pl.BlockSpec((1, GDN_DV), lambda i: (0, 0))],
        out_specs=[blk(GDN_WIDTH),
                   pl.BlockSpec((b * GDN_HEADS, GDN_DK, GDN_DV), lambda i: (0, 0, 0))],
        out_shape=[jax.ShapeDtypeStruct((b, t, GDN_WIDTH), BF16),
                   jax.ShapeDtypeStruct((b * GDN_HEADS, GDN_DK, GDN_DV), F32)],
        scratch_shapes=[pltpu.VMEM((b * GDN_HEADS, GDN_DK, GDN_DV), F32)],
        compiler_params=_cparams("arbitrary"),
        name="gdn_scan",
    )(u, w, qd, kd, qk, ge, zs, gain.reshape(1, GDN_DV))
    return o, s.reshape(b, GDN_HEADS, GDN_DK, GDN_DV)


def _gdn_sample_kernel(new_rows, xp_ref, ab_ref, zs_ref, s0_ref, cw_ref, hp_ref, gain_ref,
                       o_ref, s_out_ref, xs_ref, act_ref):
    grp = xp_ref.shape[0]
    cc = GDN_SAMPLE_CHUNK
    gain = gain_ref[...]
    rowmask = (_iota2((cc, 1), 0) >= cc - new_rows).astype(F32)
    seqs = list(range(grp))
    for bi in seqs:
        xs = xs_ref.at[bi]
        xs[0:8, :] = jnp.zeros((8, GDN_CONV_DIM), F32)
        xs[8:8 + cc, :] = xp_ref[bi].astype(F32)
        _activate_qkv(xs, cw_ref, act_ref.at[bi], 8, cc)
    gb = _each(lambda bi: _decay_beta(ab_ref[bi], hp_ref), seqs)
    gcs = _each(lambda x: _cumsum_rows(x[0] * rowmask), gb)
    gcts = _each(lambda gc: gc.T, gcs)
    ge_rows = _each(lambda gc: jnp.exp(gc[cc - 1:cc, :]), gcs)
    betas = _each(lambda x: x[1] * rowmask, gb)
    items = [(bi, hd) for bi in seqs for hd in range(GDN_HEADS)]
    col = lambda which: [act_ref[bi, :, _head_cols(hd)[which]:_head_cols(hd)[which] + LANES] * rowmask
                         for bi, hd in items]
    u, w, qd, kd, qk = _chunk_prep(
        col(0), col(1), col(2),
        [gcs[bi][:, hd:hd + 1] for bi, hd in items], [gcts[bi][hd:hd + 1, :] for bi, hd in items],
        [betas[bi][:, GDN_HEADS + hd:GDN_HEADS + hd + 1] for bi, hd in items])
    qk = [pair[:, half] for pair in qk for half in (slice(0, cc), slice(cc, 2 * cc))]
    o, s_new = _chunk_step([s0_ref[bi, hd] for bi, hd in items], u, w, qd, kd, qk,
                           [ge_rows[bi][:, hd:hd + 1] for bi, hd in items])
    for idx, (bi, hd) in enumerate(items):
        s_out_ref[bi, hd] = s_new[idx]
        oc = slice(hd * GDN_DV, (hd + 1) * GDN_DV)
        _gated_norm_store(o_ref, (bi, slice(None), oc), o[idx], gain, zs_ref[bi, :, oc])


def _gdn_sample(xp, ab, zs, s0, conv_w, head_params, gain, new_rows):
    b = xp.shape[0]
    cc = GDN_SAMPLE_CHUNK
    grp = GDN_SAMPLE_GROUP
    blk3 = lambda w: pl.BlockSpec((grp, cc, w), lambda i: (i, 0, 0))
    sblk = pl.BlockSpec((grp, GDN_HEADS, GDN_DK, GDN_DV), lambda i: (i, 0, 0, 0))
    return pl.pallas_call(
        functools.partial(_gdn_sample_kernel, new_rows),
        grid=(b // grp,),
        in_specs=[blk3(GDN_CONV_DIM), blk3(LANES), blk3(GDN_WIDTH), sblk,
                  pl.BlockSpec((GDN_CONV, GDN_CONV_DIM), lambda i: (0, 0)),
                  pl.BlockSpec((8, LANES), lambda i: (0, 0)),
                  pl.BlockSpec((1, GDN_DV), lambda i: (0, 0))],
        out_specs=[blk3(GDN_WIDTH), sblk],
        out_shape=[jax.ShapeDtypeStruct((b, cc, GDN_WIDTH), BF16),
                   jax.ShapeDtypeStruct((b, GDN_HEADS, GDN_DK, GDN_DV), F32)],
        scratch_shapes=[pltpu.VMEM((grp, cc + 8, GDN_CONV_DIM), F32),
                        pltpu.VMEM((grp, cc, GDN_CONV_DIM), F32)],
        compiler_params=_cparams("parallel"),
        name="gdn_sample",
    )(xp, ab, zs, s0, conv_w, head_params, gain.reshape(1, GDN_DV))


def _sink_attention(q, k, v, mask, sink_col):
    s = _each(lambda qq, kk: jnp.where(mask, _bdot_nt(qq, kk) * (SWA_HEAD_DIM ** -0.5), -jnp.inf), q, k)
    m = _each(lambda ss, sk: jnp.maximum(jnp.max(ss, axis=-1, keepdims=True), sk), s, sink_col)
    p = _each(lambda ss, mm: jnp.exp(ss - mm), s, m)
    denom = _each(lambda pp, sk, mm: jnp.sum(pp, axis=-1, keepdims=True) + jnp.exp(sk - mm), p, sink_col, m)
    return _each(lambda pp, vv, dd: _bdot(pp, vv) / dd, p, v, denom)


def _sink_column(sinks_ref, kv_head, rows_per_head):
    parts = [jnp.full((rows_per_head, 1), sinks_ref[kv_head * SWA_GROUP + g], F32) for g in range(SWA_GROUP)]
    return jnp.concatenate(parts, axis=0)


def _swa_prompt_kernel(sinks_ref, q_ref, kvp_ref, kvc_ref, o_ref):
    wnd = WINDOW
    nblk = q_ref.shape[1] // wnd
    step = pl.program_id(1)
    kv = jnp.concatenate([kvp_ref[0], kvc_ref[0]], axis=0)
    cols = SWA_GROUP * wnd
    kj = _iota2((2 * wnd, cols), 0)
    qi = _iota2((2 * wnd, cols), 1) & (wnd - 1)
    dist = qi + wnd - kj
    band = (dist >= 0) & (dist <= wnd)
    first_key = jnp.where(step > 0, 0, wnd)
    masks = [band & (kj >= first_key)] + [band] * (nblk - 1)
    head_cols = lambda h: slice(h * SWA_HEAD_DIM, (h + 1) * SWA_HEAD_DIM)
    q_heads = lambda hk: [hk * SWA_GROUP + g for g in range(SWA_GROUP)]
    scale = SWA_HEAD_DIM ** -0.5
    items = [(j, hk) for j in range(nblk) for hk in range(SWA_KV_HEADS)]
    q_rows = lambda j: slice(j * wnd, (j + 1) * wnd)
    k_rows = lambda j: slice(j * wnd, (j + 2) * wnd)
    q = [jnp.concatenate([q_ref[0, q_rows(j), head_cols(h)] for h in q_heads(hk)], axis=0) for j, hk in items]
    k = [kv[k_rows(j), head_cols(hk)] for j, hk in items]
    v = [kv[k_rows(j), SWA_KV_WIDTH + hk * SWA_HEAD_DIM:SWA_KV_WIDTH + (hk + 1) * SWA_HEAD_DIM] for j, hk in items]
    sinks_kv = [jnp.concatenate([jnp.full((1, wnd), sinks_ref[h], F32) for h in q_heads(hk)], axis=1)
                for hk in range(SWA_KV_HEADS)]
    sink = [sinks_kv[hk] for j, hk in items]
    mask = [masks[j] for j, hk in items]
    s = _each(lambda kk, qq, mk: jnp.where(mk, _bdot_nt(kk, qq) * scale, -jnp.inf), k, q, mask)
    m = _each(lambda ss, sk: jnp.maximum(jnp.max(ss, axis=0, keepdims=True), sk), s, sink)
    p = _each(lambda ss, mm: jnp.exp(ss - mm), s, m)
    denom = _each(lambda pp, sk, mm: jnp.sum(pp, axis=0, keepdims=True) + jnp.exp(sk - mm), p, sink, m)
    ot = _each(lambda vv, pp, dd: _bdot_tn(vv, pp) / dd, v, p, denom)
    for idx, (j, hk) in enumerate(items):
        for g in range(0, SWA_GROUP, 2):
            pair = jnp.concatenate([ot[idx][:, g * wnd:(g + 1) * wnd], ot[idx][:, (g + 1) * wnd:(g + 2) * wnd]],
                                   axis=0)
            h0 = hk * SWA_GROUP + g
            o_ref[0, q_rows(j), h0 * SWA_HEAD_DIM:(h0 + 2) * SWA_HEAD_DIM] = pair.T.astype(o_ref.dtype)


def _swa_prompt(q, kv, sinks):
    b, t, _ = q.shape
    nblk = SWA_BLOCKS_PER_STEP
    rows = nblk * WINDOW
    return pl.pallas_call(
        _swa_prompt_kernel,
        grid=(b, t // rows),
        in_specs=[pl.BlockSpec(memory_space=pltpu.SMEM),
                  pl.BlockSpec((1, rows, SWA_WIDTH), lambda bi, i: (bi, i, 0)),
                  pl.BlockSpec((1, WINDOW, 2 * SWA_KV_WIDTH), lambda bi, i: (bi, jnp.maximum(i * nblk - 1, 0), 0)),
                  pl.BlockSpec((1, rows, 2 * SWA_KV_WIDTH), lambda bi, i: (bi, i, 0))],
        out_specs=pl.BlockSpec((1, rows, SWA_WIDTH), lambda bi, i: (bi, i, 0)),
        out_shape=jax.ShapeDtypeStruct((b, t, SWA_WIDTH), BF16),
        compiler_params=_cparams("parallel", "parallel"),
        name="swa_prompt",
    )(sinks, q, kv, kv)


def _swa_sample_kernel(sinks_ref, q_ref, kvn_ref, kc_ref, vc_ref, o_ref, ko_ref, vo_ref):
    grp, t, _ = q_ref.shape
    wnd = WINDOW
    nk = wnd + BF16_ROWS
    rows = SWA_GROUP * t
    tq = _iota2((rows, nk), 0) & (t - 1)
    kj = _iota2((rows, nk), 1)
    dist = tq + wnd - kj
    mask = (dist >= 0) & (dist <= wnd)
    zpad = jnp.zeros((BF16_ROWS - t, SWA_KV_WIDTH), F32)
    kks, vvs = [], []
    for bi in range(grp):
        kvn = kvn_ref[bi].astype(F32)
        kk = jnp.concatenate([kc_ref[bi], kvn[:, :SWA_KV_WIDTH], zpad], axis=0)
        vv = jnp.concatenate([vc_ref[bi], kvn[:, SWA_KV_WIDTH:], zpad], axis=0)
        ko_ref[bi] = kk[t:t + wnd, :]
        vo_ref[bi] = vv[t:t + wnd, :]
        kks.append(kk)
        vvs.append(vv)
    items = [(bi, hk) for bi in range(grp) for hk in range(SWA_KV_HEADS)]
    head_cols = lambda h: slice(h * SWA_HEAD_DIM, (h + 1) * SWA_HEAD_DIM)
    q_heads = lambda hk: [hk * SWA_GROUP + g for g in range(SWA_GROUP)]
    sink_cols = [_sink_column(sinks_ref, hk, t) for hk in range(SWA_KV_HEADS)]
    o = _sink_attention(
        [jnp.concatenate([q_ref[bi, :, head_cols(h)] for h in q_heads(hk)], axis=0) for bi, hk in items],
        [kks[bi][:, head_cols(hk)] for bi, hk in items], [vvs[bi][:, head_cols(hk)] for bi, hk in items],
        mask, [sink_cols[hk] for bi, hk in items])
    for idx, (bi, hk) in enumerate(items):
        for g, h in enumerate(q_heads(hk)):
            o_ref[bi, :, head_cols(h)] = o[idx][g * t:(g + 1) * t].astype(o_ref.dtype)


def _swa_sample(q, kv_new, k_cache, v_cache, sinks):
    b, t, _ = q.shape
    grp = SWA_SAMPLE_GROUP
    blk = lambda r, w: pl.BlockSpec((grp, r, w), lambda i: (i, 0, 0))
    return pl.pallas_call(
        _swa_sample_kernel,
        grid=(b // grp,),
        in_specs=[pl.BlockSpec(memory_space=pltpu.SMEM),
                  blk(t, SWA_WIDTH), blk(t, 2 * SWA_KV_WIDTH), blk(WINDOW, SWA_KV_WIDTH), blk(WINDOW, SWA_KV_WIDTH)],
        out_specs=[blk(t, SWA_WIDTH), blk(WINDOW, SWA_KV_WIDTH), blk(WINDOW, SWA_KV_WIDTH)],
        out_shape=[jax.ShapeDtypeStruct((b, t, SWA_WIDTH), BF16),
                   jax.ShapeDtypeStruct((b, WINDOW, SWA_KV_WIDTH), F32),
                   jax.ShapeDtypeStruct((b, WINDOW, SWA_KV_WIDTH), F32)],
        compiler_params=_cparams("parallel"),
        name="swa_sample",
    )(sinks, q, kv_new, k_cache, v_cache)


def _route(logits):
    lane = _iota2(logits.shape, 1).astype(F32)
    neg = -jnp.inf

    def first_argmax(vals, valid):
        v = jnp.where(valid, vals, neg)
        m = jnp.max(v, axis=-1, keepdims=True)
        idx = jnp.min(jnp.where(jnp.logical_and(valid, v == m), lane, float(LANES)), axis=-1, keepdims=True)
        return m, idx

    is_group = lane < N_GROUPS
    gmax, gidx = first_argmax(logits, is_group)
    p_group = 1.0 / jnp.sum(jnp.where(is_group, jnp.exp(logits - gmax), 0.0), axis=-1, keepdims=True)
    lo = N_GROUPS + gidx * EXPERTS_PER_GROUP
    in_group = jnp.logical_and(lane >= lo, lane < lo + EXPERTS_PER_GROUP)
    m1, i1 = first_argmax(logits, in_group)
    esum = jnp.sum(jnp.where(in_group, jnp.exp(logits - m1), 0.0), axis=-1, keepdims=True)
    m2, i2 = first_argmax(logits, jnp.logical_and(in_group, lane != i1))
    p1 = 1.0 / esum
    p2 = jnp.exp(m2 - m1) / esum
    tot = p1 + p2
    return i1 - N_GROUPS, i2 - N_GROUPS, p_group * p1 / tot, p_group * p2 / tot


def _post_mixer_kernel(oa_ref, ob_ref, ga_ref, gb_ref, x_ref, gt_ref, sc_ref, sh_ref,
                       wa_ref, wb_ref, wo_ref, gpost_ref, gpre_ref, wr_ref, br_ref, cnt0_ref,
                       x1_ref, h2_ref, rt_ref, cnt_out_ref, meta_ref, *rest):
    cnt_ref = rest[-1]
    if len(rest) == 2:
        rest[0][...] = jnp.zeros_like(rest[0])
    step = pl.program_id(0)

    @pl.when(step == 0)
    def _():
        cnt_ref[...] = cnt0_ref[...]

    merged = (ga_ref[...].astype(F32) * jnp.dot(oa_ref[...], wa_ref[...], preferred_element_type=F32)
              + gb_ref[...].astype(F32) * jnp.dot(ob_ref[...], wb_ref[...], preferred_element_type=F32))
    mix = _bdot(merged, wo_ref[...])
    x1 = x_ref[...] + gt_ref[0] * _rms(mix, gpost_ref[...])
    x1_ref[...] = x1
    h2 = _rms(x1, gpre_ref[...]) * (1.0 + sc_ref[0]) + sh_ref[0]
    _rows_to_tiles(h2_ref, h2)
    h_hi = h2.astype(BF16)
    h_lo = (h2 - h_hi.astype(F32)).astype(BF16)
    part = jnp.dot(h_hi, wr_ref[...], preferred_element_type=F32)
    logits = (part[:, :LANES] + part[:, LANES:]
              + jnp.dot(h_lo, wr_ref[:, :LANES], preferred_element_type=F32) + br_ref[...])
    ia, ib, wa, wb = _route(logits)
    lane = _iota2(logits.shape, 1)
    tm = logits.shape[0]
    lane_f = lane.astype(F32)
    hot_a = (lane_f == ia).astype(F32)
    hot_b = (lane_f == ib).astype(F32)
    hot = hot_a + hot_b
    earlier = (_iota2((tm, tm), 0) > _iota2((tm, tm), 1)).astype(BF16)
    before = jnp.dot(earlier, hot.astype(BF16), preferred_element_type=F32) + cnt_ref[...]
    rank_a = jnp.sum(hot_a * before, axis=-1, keepdims=True)
    rank_b = jnp.sum(hot_b * before, axis=-1, keepdims=True)
    cnt_ref[...] = cnt_ref[...] + jnp.sum(hot, axis=0, keepdims=True)
    cnt_out_ref[...] = cnt_ref[...]
    rt_ref[...] = jnp.where(lane == 0, ia, jnp.where(lane == 1, ib, jnp.where(lane == 2, wa, jnp.where(
        lane == 3, wb, 0.0))))
    packed = jnp.where(lane == 0, rank_a * N_EXPERTS + ia, jnp.where(lane == 1, rank_b * N_EXPERTS + ib, 0.0))
    meta_ref[0] = packed.T[0:TOP_K, :].astype(I32)


def _post_mixer(oa, ob, ga, gb, x2d, gt, sc, sh, w_a, w_b, w_o, g_post, g_pre, w_rt, b_rt, cnt0, tm,
                zero_rows=0):
    rows = x2d.shape[0]
    n_tiles = rows // tm
    row_blk = lambda w: pl.BlockSpec((tm, w), lambda i: (i, 0))
    full = lambda r, c: pl.BlockSpec((r, c), lambda i: (0, 0))
    out_specs = [row_blk(D_MODEL), pl.BlockSpec((tm * TILE_ROWS, LANES), lambda i: (i, 0)), row_blk(LANES),
                 full(1, LANES), pl.BlockSpec((1, TOP_K, tm), lambda i: (i, 0, 0))]
    out_shape = [jax.ShapeDtypeStruct((rows, D_MODEL), F32),
                 jax.ShapeDtypeStruct((rows * TILE_ROWS, LANES), F32),
                 jax.ShapeDtypeStruct((rows, LANES), F32),
                 jax.ShapeDtypeStruct((1, LANES), F32),
                 jax.ShapeDtypeStruct((n_tiles, TOP_K, tm), I32)]
    if zero_rows:
        assert zero_rows % (n_tiles * TILE_ROWS) == 0
        out_specs.append(pl.BlockSpec((zero_rows // n_tiles, LANES), lambda i: (i, 0)))
        out_shape.append(jax.ShapeDtypeStruct((zero_rows, LANES), F32))
    return pl.pallas_call(
        _post_mixer_kernel,
        grid=(n_tiles,),
        in_specs=[row_blk(GDN_WIDTH), row_blk(SWA_WIDTH), row_blk(D_MODEL), row_blk(D_MODEL), row_blk(D_MODEL),
                  _mod_spec(gt, n_tiles), _mod_spec(sc, n_tiles), _mod_spec(sh, n_tiles),
                  full(GDN_WIDTH, D_MODEL), full(SWA_WIDTH, D_MODEL), full(D_MODEL, D_MODEL),
                  full(1, D_MODEL), full(1, D_MODEL), full(D_MODEL, 2 * LANES), full(1, LANES), full(1, LANES)],
        out_specs=out_specs,
        out_shape=out_shape,
        scratch_shapes=[pltpu.VMEM((1, LANES), F32)],
        compiler_params=_cparams("arbitrary"),
        name="post_mixer",
    )(oa, ob, ga, gb, x2d, gt, sc, sh, w_a, w_b, w_o,
      g_post.reshape(1, D_MODEL), g_pre.reshape(1, D_MODEL), w_rt, b_rt, cnt0)


TILE_ROWS = D_MODEL // LANES


def _tiles_to_rows(ref, first, rows):
    base = first * TILE_ROWS
    return jnp.concatenate([ref[pl.ds(base + c, rows, stride=TILE_ROWS), :] for c in range(TILE_ROWS)], axis=1)


def _rows_to_tiles(ref, mat):
    rows = mat.shape[0]
    for c in range(TILE_ROWS):
        ref[pl.ds(c, rows, stride=TILE_ROWS), :] = mat[:, c * LANES:(c + 1) * LANES]


def _tile_copy_loop(n, copies, start):
    def body(t, carry):
        for j, cp in enumerate(copies(t)):
            if start:
                cp.start(priority=j % 2)
            else:
                cp.wait()
        return carry

    lax.fori_loop(0, n, body, 0, unroll=8)


def _slots_kernel(pstart_ref, meta_ref, o_ref):
    packed = meta_ref[...]
    expert = packed & (N_EXPERTS - 1)
    first = jnp.zeros_like(packed)
    for e in range(N_EXPERTS):
        first = jnp.where(expert == e, pstart_ref[e], first)
    o_ref[...] = first + lax.shift_right_logical(packed, N_EXPERTS.bit_length() - 1)


def _slots(meta, pstarts):
    return pl.pallas_call(
        _slots_kernel,
        grid_spec=pltpu.PrefetchScalarGridSpec(
            num_scalar_prefetch=1,
            grid=(1,),
            in_specs=[pl.BlockSpec(meta.shape, lambda i, p: (0, 0, 0))],
            out_specs=pl.BlockSpec(meta.shape, lambda i, p: (0, 0, 0))),
        out_shape=jax.ShapeDtypeStruct(meta.shape, I32),
        compiler_params=_cparams("arbitrary"),
        name="moe_slots",
    )(pstarts, meta)


def _dispatch_kernel(slot_ref, h_ref, xs_in_hbm, xs_hbm, sem):
    del xs_in_hbm
    i = pl.program_id(0)
    tm = h_ref.shape[0] // TILE_ROWS
    sub_tiles = tm // ROW_TILE

    for start in (True, False):
        for j in range(sub_tiles):
            base = (i * sub_tiles + j) * TOP_K * ROW_TILE

            def copies(t, j=j, base=base):
                row = pl.multiple_of((j * ROW_TILE + t) * TILE_ROWS, TILE_ROWS)
                return [pltpu.make_async_copy(h_ref.at[pl.ds(row, TILE_ROWS)],
                                              xs_hbm.at[slot_ref[base + k * ROW_TILE + t]], sem)
                        for k in range(TOP_K)]

            _tile_copy_loop(ROW_TILE, copies, start)


def _dispatch(slots, h_tiles, xs, tm):
    n_tiles = h_tiles.shape[0] // (tm * TILE_ROWS)
    return pl.pallas_call(
        _dispatch_kernel,
        grid_spec=pltpu.PrefetchScalarGridSpec(
            num_scalar_prefetch=1,
            grid=(n_tiles,),
            in_specs=[pl.BlockSpec((tm * TILE_ROWS, LANES), lambda i, s: (i, 0)),
                      pl.BlockSpec(memory_space=pl.ANY)],
            out_specs=pl.BlockSpec(memory_space=pl.ANY),
            scratch_shapes=[pltpu.SemaphoreType.DMA(())]),
        out_shape=jax.ShapeDtypeStruct(xs.shape, xs.dtype),
        input_output_aliases={2: 0},
        compiler_params=_cparams("arbitrary"),
        name="moe_dispatch",
    )(slots, h_tiles, xs)


def _moe_kernel(first_ref, nblk_ref, x_hbm, wg_ref, wu_ref, wd_ref, y_hbm,
                xbuf, ybuf, wgb, wub, wdb, in_sems, out_sems):
    e = pl.program_id(0)
    tile_rows = xbuf.shape[1]
    nblk = nblk_ref[e]
    first = first_ref[e]

    def rows_of(j):
        return pl.ds(pl.multiple_of((first + j) * tile_rows, tile_rows), tile_rows)

    def in_copy(j, slot):
        return pltpu.make_async_copy(x_hbm.at[rows_of(j)], xbuf.at[slot], in_sems.at[slot])

    def out_copy(j, slot):
        return pltpu.make_async_copy(ybuf.at[slot], y_hbm.at[rows_of(j)], out_sems.at[slot])

    @pl.when(nblk > 0)
    def _():
        in_copy(0, 0).start()
        wgb[...] = wg_ref[0].astype(BF16)
        wub[...] = wu_ref[0].astype(BF16)
        wdb[...] = wd_ref[0].astype(BF16)

    def body(j, carry):
        slot = j % 2
        in_copy(j, slot).wait()

        @pl.when(j + 1 < nblk)
        def _():
            in_copy(j + 1, 1 - slot).start()

        @pl.when(j >= 2)
        def _():
            out_copy(j - 2, slot).wait()

        x = _tiles_to_rows(xbuf.at[slot], 0, tile_rows // TILE_ROWS).astype(BF16)
        gate = jnp.dot(x, wgb[...], preferred_element_type=F32)
        up = jnp.dot(x, wub[...], preferred_element_type=F32)
        _rows_to_tiles(ybuf.at[slot], _bdot(_silu(gate) * up, wdb[...]))
        out_copy(j, slot).start()
        return carry

    lax.fori_loop(0, nblk, body, 0)

    @pl.when(nblk >= 2)
    def _():
        out_copy(nblk - 2, nblk % 2).wait()

    @pl.when(nblk >= 1)
    def _():
        out_copy(nblk - 1, (nblk - 1) % 2).wait()

    @pl.when(e == pl.num_programs(0) - 1)
    def _():
        n_tail = nblk_ref[N_EXPERTS]
        tail_first = first_ref[N_EXPERTS]
        ybuf[0] = jnp.zeros(ybuf.shape[1:], F32)

        def tail_copy(t):
            dst = y_hbm.at[pl.ds(pl.multiple_of((tail_first + t) * tile_rows, tile_rows), tile_rows)]
            return pltpu.make_async_copy(ybuf.at[0], dst, out_sems.at[0])

        lax.fori_loop(0, n_tail, lambda t, c: (tail_copy(t).start(), c)[1], 0)
        lax.fori_loop(0, n_tail, lambda t, c: (tail_copy(t).wait(), c)[1], 0)


def _moe(xs_tiles, first_blk, n_blk, w_gate, w_up, w_down):
    tile_rows = MOE_ROWS * TILE_ROWS
    wspec = lambda r, c: pl.BlockSpec((1, r, c), lambda e, fb, nb: (e, 0, 0))
    return pl.pallas_call(
        _moe_kernel,
        grid_spec=pltpu.PrefetchScalarGridSpec(
            num_scalar_prefetch=2,
            grid=(N_EXPERTS,),
            in_specs=[pl.BlockSpec(memory_space=pl.ANY),
                      wspec(D_MODEL, EXPERT_FF), wspec(D_MODEL, EXPERT_FF), wspec(EXPERT_FF, D_MODEL)],
            out_specs=pl.BlockSpec(memory_space=pl.ANY),
            scratch_shapes=[pltpu.VMEM((2, tile_rows, LANES), F32),
                            pltpu.VMEM((2, tile_rows, LANES), F32),
                            pltpu.VMEM((D_MODEL, EXPERT_FF), BF16),
                            pltpu.VMEM((D_MODEL, EXPERT_FF), BF16),
                            pltpu.VMEM((EXPERT_FF, D_MODEL), BF16),
                            pltpu.SemaphoreType.DMA((2,)),
                            pltpu.SemaphoreType.DMA((2,))]),
        out_shape=jax.ShapeDtypeStruct(xs_tiles.shape, F32),
        compiler_params=_cparams("arbitrary"),
        name="moe_experts",
    )(first_blk, n_blk, xs_tiles, w_gate, w_up, w_down)


def _combine_kernel(slot_ref, y_hbm, x1_ref, rt_ref, gt_ref, gpost_ref, o_ref, ybuf, sems):
    i = pl.program_id(0)
    n = pl.num_programs(0)
    rows = ybuf.shape[1] // TILE_ROWS
    slot = i % 2

    def gather(step, buf_slot, start):
        def copy(r):
            dst = ybuf.at[buf_slot, pl.ds(pl.multiple_of(r * TILE_ROWS, TILE_ROWS), TILE_ROWS)]
            return pltpu.make_async_copy(y_hbm.at[slot_ref[step * rows + r]], dst, sems.at[buf_slot])

        _tile_copy_loop(rows // 2, lambda t: [copy(2 * t), copy(2 * t + 1)], start)

    @pl.when(i == 0)
    def _():
        gather(0, 0, True)

    @pl.when(i + 1 < n)
    def _():
        gather(i + 1, 1 - slot, True)

    gather(i, slot, False)
    half = rows // 2
    rt = rt_ref[...]
    buf = ybuf.at[slot]
    f = rt[:, 2:3] * _tiles_to_rows(buf, 0, half) + rt[:, 3:4] * _tiles_to_rows(buf, half, half)
    o_ref[...] = x1_ref[...] + gt_ref[0] * _rms(f, gpost_ref[...])


def _combine(slots, yb, x1, rt, gt, g_post):
    rows = x1.shape[0]
    tm = ROW_TILE
    n_tiles = rows // tm
    tiles_per_mod = n_tiles // gt.shape[0]
    return pl.pallas_call(
        _combine_kernel,
        grid_spec=pltpu.PrefetchScalarGridSpec(
            num_scalar_prefetch=1,
            grid=(n_tiles,),
            in_specs=[pl.BlockSpec(memory_space=pl.ANY),
                      pl.BlockSpec((tm, D_MODEL), lambda i, s: (i, 0)),
                      pl.BlockSpec((tm, LANES), lambda i, s: (i, 0)),
                      pl.BlockSpec((1, gt.shape[1], D_MODEL), lambda i, s: (i // tiles_per_mod, 0, 0)),
                      pl.BlockSpec((1, D_MODEL), lambda i, s: (0, 0))],
            out_specs=pl.BlockSpec((tm, D_MODEL), lambda i, s: (i, 0)),
            scratch_shapes=[pltpu.VMEM((2, TOP_K * tm * TILE_ROWS, LANES), F32),
                            pltpu.SemaphoreType.DMA((2,))]),
        out_shape=jax.ShapeDtypeStruct((rows, D_MODEL), F32),
        compiler_params=_cparams("arbitrary"),
        name="moe_combine",
    )(slots, yb, x1, rt, gt, g_post.reshape(1, D_MODEL))


def _dispatch_plan(counts, n_tok):
    nblk = (counts + MOE_ROWS - 1) // MOE_ROWS
    ends = jnp.cumsum(nblk)
    first = ends - nblk
    n_blocks = n_tok * TOP_K // MOE_ROWS + N_EXPERTS
    first_blk = jnp.concatenate([first, ends[-1:]]).astype(I32)
    n_blk = jnp.concatenate([nblk, n_blocks - ends[-1:]]).astype(I32)
    return first_blk, n_blk, (first * MOE_ROWS).astype(I32)


def _prep_in_weight_kernel(w_ref, o_ref):
    a0 = 4 * GDN_QK_WIDTH
    a1 = a0 + 2 * GDN_HEADS
    rows = w_ref.shape[0]
    for c in range(0, a0, PROJ_TILE):
        o_ref[:, c:c + PROJ_TILE] = w_ref[:, c:c + PROJ_TILE].astype(BF16)
    for c in range(a0, _C_AB, PROJ_TILE):
        o_ref[:, c:c + PROJ_TILE] = w_ref[:, c + a1 - a0:c + a1 - a0 + PROJ_TILE].astype(BF16)
    ab = jnp.concatenate([w_ref[:, a0:a1], jnp.zeros((rows, LANES - (a1 - a0)), F32)], axis=1)
    o_ref[:, _C_AB:IN_COLS] = ab.astype(BF16)


def _prep_in_weight(w_in):
    rows = 128
    return pl.pallas_call(
        _prep_in_weight_kernel,
        grid=(D_MODEL // rows,),
        in_specs=[pl.BlockSpec((rows, w_in.shape[1]), lambda i: (i, 0))],
        out_specs=pl.BlockSpec((rows, IN_COLS), lambda i: (i, 0)),
        out_shape=jax.ShapeDtypeStruct((D_MODEL, IN_COLS), BF16),
        compiler_params=_cparams("parallel"),
        name="prep_in_weight",
    )(w_in)


def _head_param_tile(a_log, dt_bias):
    tile = jnp.zeros((8, LANES), F32)
    return tile.at[0, :GDN_HEADS].set(a_log.astype(F32)).at[1, :GDN_HEADS].set(dt_bias.astype(F32))


def _router_weight(w_group, b_group, w_router, b_router):
    w = jnp.zeros((D_MODEL, LANES), F32)
    w = w.at[:, :N_GROUPS].set(w_group).at[:, N_GROUPS:N_GROUPS + N_EXPERTS].set(w_router)
    b = jnp.zeros((1, LANES), F32)
    b = b.at[0, :N_GROUPS].set(b_group).at[0, N_GROUPS:N_GROUPS + N_EXPERTS].set(b_router)
    w_hi = w.astype(BF16)
    w_lo = (w - w_hi.astype(F32)).astype(BF16)
    return jnp.concatenate([w_hi, w_lo], axis=1), b


def kernel(x_prompt, x_sample, state_gdn, state_conv, cache_k_win, cache_v_win, c_prompt, c_sample, w_ada, b_ada, g_mix_pre, g_mix_post, g_ffn_pre, g_ffn_post, w_in, conv_w, a_log, dt_bias, gdn_norm, sinks, w_br_gdn, w_br_swa, w_out, w_group, b_group, w_router, b_router, w_gate, w_up, w_down):
    depth = w_ada.shape[0]
    assert depth == 1, "single-layer trunk"
    bp, tp, _ = x_prompt.shape
    bs, ts, _ = x_sample.shape
    n_p = bp * tp
    n_s = bs * ts
    tm = ROW_TILE
    assert tp % tm == 0 and n_s % tm == 0 and ts >= GDN_CONV - 1 and ts + GDN_CONV - 1 <= GDN_SAMPLE_CHUNK
    assert ts & (ts - 1) == 0 and ts <= BF16_ROWS

    c_all = jnp.concatenate([c_prompt, c_sample], axis=0)
    c_rows = -(-c_all.shape[0] // 8) * 8
    c_all = jnp.pad(c_all, ((0, c_rows - c_all.shape[0]), (0, 0)))
    mod = _adaln(c_all, w_ada[0], b_ada[0])
    mods_p = [m[:bp].reshape(bp, 1, D_MODEL) for m in jnp.split(mod, 6, axis=-1)]
    mods_s = [jnp.repeat(m[bp:bp + bs], ts, axis=0).reshape(n_s // tm, tm, D_MODEL)
              for m in jnp.split(mod, 6, axis=-1)]

    w_prep = _prep_in_weight(w_in[0])
    head_params = _head_param_tile(a_log[0], dt_bias[0])
    w_a, w_b, w_o = w_br_gdn[0].astype(BF16), w_br_swa[0].astype(BF16), w_out[0].astype(BF16)
    w_rt, b_rt = _router_weight(w_group[0], b_group[0], w_router[0], b_router[0])
    sinks0 = sinks[0].astype(F32)

    xp2d = x_prompt.reshape(n_p, D_MODEL)
    sh1, sc1, gt1, sh2, sc2, gt2 = mods_p
    qkv_p, zs_p, qb_p, kvb_p, ga_p, gb_p, ab_p, conv_tail_p = _inproj(
        xp2d, g_mix_pre[0], sc1, sh1, w_prep, INPROJ_ROWS, conv_w=conv_w[0].astype(F32), n_seq=bp)
    qkv_p3 = qkv_p.reshape(bp, tp, GDN_CONV_DIM)
    u, w, qd, kd, qk, ge = _gdn_prep(qkv_p3, ab_p.reshape(bp, tp, LANES), head_params)
    oa_p, s_prompt = _gdn_scan(u, w, qd, kd, qk, ge, zs_p.reshape(bp, tp, GDN_WIDTH), gdn_norm[0])
    kvb_p3 = kvb_p.reshape(bp, tp, 2 * SWA_KV_WIDTH)
    ob_p = _swa_prompt(qb_p.reshape(bp, tp, SWA_WIDTH), kvb_p3, sinks0)
    n_slots = ((n_p + n_s) * TOP_K // MOE_ROWS + N_EXPERTS) * MOE_ROWS
    x1_p, h2_p, rt_p, cnt_p, meta_p, xs = _post_mixer(
        oa_p.reshape(n_p, GDN_WIDTH), ob_p.reshape(n_p, SWA_WIDTH), ga_p, gb_p, xp2d, gt1, sc2, sh2,
        w_a, w_b, w_o, g_mix_post[0], g_ffn_pre[0], w_rt, b_rt, jnp.zeros((1, LANES), F32), tm,
        zero_rows=n_slots * TILE_ROWS)

    xs2d = x_sample.reshape(n_s, D_MODEL)
    sh1s, sc1s, gt1s, sh2s, sc2s, gt2s = mods_s
    qkv_s, zs_s, qb_s, kvb_s, ga_s, gb_s, ab_s = _inproj(xs2d, g_mix_pre[0], sc1s, sh1s, w_prep, tm)
    cc = GDN_SAMPLE_CHUNK
    pad_rows = cc - ts - (GDN_CONV - 1)
    qkv_s3 = qkv_s.reshape(bs, ts, GDN_CONV_DIM)
    xp_s = jnp.concatenate([jnp.zeros((bs, pad_rows, GDN_CONV_DIM), BF16), state_conv[0].astype(BF16), qkv_s3],
                           axis=1)
    front = lambda a: jnp.pad(a, ((0, 0), (cc - ts, 0), (0, 0)))
    oa_s16, s_sample = _gdn_sample(xp_s, front(ab_s.reshape(bs, ts, LANES)), front(zs_s.reshape(bs, ts, GDN_WIDTH)),
                                   state_gdn[0].astype(F32), conv_w[0], head_params, gdn_norm[0], ts)
    oa_s = oa_s16[:, cc - ts:, :].reshape(n_s, GDN_WIDTH)
    ob_s, k_new_s, v_new_s = _swa_sample(
        qb_s.reshape(bs, ts, SWA_WIDTH), kvb_s.reshape(bs, ts, 2 * SWA_KV_WIDTH),
        cache_k_win[0].reshape(bs, WINDOW, SWA_KV_WIDTH).astype(F32),
        cache_v_win[0].reshape(bs, WINDOW, SWA_KV_WIDTH).astype(F32), sinks0)
    x1_s, h2_s, rt_s, cnt_all, meta_s = _post_mixer(
        oa_s, ob_s.reshape(n_s, SWA_WIDTH), ga_s, gb_s, xs2d, gt1s, sc2s, sh2s,
        w_a, w_b, w_o, g_mix_post[0], g_ffn_pre[0], w_rt, b_rt, cnt_p, tm)

    first_blk, n_blk, pstarts = _dispatch_plan(cnt_all[0, :N_EXPERTS].astype(I32), n_p + n_s)
    slots_p = _slots(meta_p, pstarts).reshape(-1)
    slots_s = _slots(meta_s, pstarts).reshape(-1)
    xs = _dispatch(slots_p, h2_p, xs.reshape(n_slots, TILE_ROWS, LANES), min(DISPATCH_ROWS, n_p))
    xs = _dispatch(slots_s, h2_s, xs, min(DISPATCH_ROWS, n_s))
    yb = _moe(xs.reshape(n_slots * TILE_ROWS, LANES), first_blk, n_blk, w_gate[0], w_up[0], w_down[0])
    yb = yb.reshape(n_slots, TILE_ROWS, LANES)
    y_p = _combine(slots_p, yb, x1_p, rt_p, gt2, g_ffn_post[0])
    y_s = _combine(slots_s, yb, x1_s, rt_s, gt2s, g_ffn_post[0])

    f32 = lambda a: a.astype(F32)
    kv_tail = kvb_p3[:, tp - WINDOW:, :]
    kv_heads = lambda a: f32(a).reshape(a.shape[0], WINDOW, SWA_KV_HEADS, SWA_HEAD_DIM)[None]
    return (y_p.reshape(bp, tp, D_MODEL), y_s.reshape(bs, ts, D_MODEL),
            s_prompt[None], conv_tail_p[:, 8 - (GDN_CONV - 1):, :][None],
            kv_heads(kv_tail[:, :, :SWA_KV_WIDTH]), kv_heads(kv_tail[:, :, SWA_KV_WIDTH:]),
            s_sample[None], f32(qkv_s3[:, ts - (GDN_CONV - 1):, :])[None],
            kv_heads(k_new_s), kv_heads(v_new_s))
```

```python
import functools

import jax
import jax.numpy as jnp
from jax import lax
from jax.experimental import pallas as pl
from jax.experimental.pallas import tpu as pltpu

F32 = jnp.float32
BF16 = jnp.bfloat16
I32 = jnp.int32

D_MODEL = 1024
NORM_EPS = 1e-6
GDN_HEADS = 8
GDN_DK = 128
GDN_DV = 128
GDN_CONV = 4
GDN_CHUNK = 64
GDN_QK_WIDTH = GDN_HEADS * GDN_DK
GDN_WIDTH = GDN_HEADS * GDN_DV
GDN_CONV_DIM = 2 * GDN_QK_WIDTH + GDN_WIDTH
SWA_Q_HEADS = 16
SWA_KV_HEADS = 4
SWA_HEAD_DIM = 64
SWA_GROUP = SWA_Q_HEADS // SWA_KV_HEADS
SWA_WIDTH = SWA_Q_HEADS * SWA_HEAD_DIM
SWA_KV_WIDTH = SWA_KV_HEADS * SWA_HEAD_DIM
WINDOW = 128
N_GROUPS = 4
EXPERTS_PER_GROUP = 8
N_EXPERTS = N_GROUPS * EXPERTS_PER_GROUP
TOP_K = 2
EXPERT_FF = 512

LANES = 128
BF16_ROWS = 16
VMEM_LIMIT = 56 * 1024 * 1024

_C_QKV = 0
_C_Z = _C_QKV + GDN_CONV_DIM
_C_QB = _C_Z + GDN_WIDTH
_C_KVB = _C_QB + SWA_WIDTH
_C_GA = _C_KVB + 2 * SWA_KV_WIDTH
_C_GB = _C_GA + D_MODEL
_C_AB = _C_GB + D_MODEL
IN_COLS = _C_AB + LANES
PROJ_TILE = 512

ROW_TILE = 512
INPROJ_ROWS = 512
GDN_PREP_ROWS = 256
GDN_SCAN_ROWS = 512
GDN_SAMPLE_CHUNK = 16
GDN_SAMPLE_GROUP = 8
SWA_SAMPLE_GROUP = 8
SWA_BLOCKS_PER_STEP = 2
MOE_ROWS = 512
DISPATCH_ROWS = 1024


def _cparams(*sem):
    return pltpu.CompilerParams(dimension_semantics=sem, vmem_limit_bytes=VMEM_LIMIT)


def _bdot(a, b):
    return jnp.dot(a.astype(BF16), b.astype(BF16), preferred_element_type=F32)


def _bdot_nt(a, b):
    return lax.dot_general(a.astype(BF16), b.astype(BF16), (((1,), (1,)), ((), ())),
                           preferred_element_type=F32)


def _bdot_tn(a, b):
    return lax.dot_general(a.astype(BF16), b.astype(BF16), (((0,), (0,)), ((), ())),
                           preferred_element_type=F32)


def _sigmoid(x):
    return 1.0 / (1.0 + jnp.exp(-x))


def _silu(x):
    return x * _sigmoid(x)


def _rms(x, gain):
    return x * lax.rsqrt(jnp.mean(x * x, axis=-1, keepdims=True) + NORM_EPS) * gain


def _iota2(shape, dim):
    return lax.broadcasted_iota(I32, shape, dim)


def _adaln_kernel(c_ref, w_ref, b_ref, o_ref):
    o_ref[...] = _bdot(_silu(c_ref[...]), w_ref[...]) + b_ref[...]


def _adaln(c_all, w_ada, b_ada):
    rows = c_all.shape[0]
    n_out = w_ada.shape[1]
    tn = D_MODEL
    return pl.pallas_call(
        _adaln_kernel,
        grid=(n_out // tn,),
        in_specs=[pl.BlockSpec((rows, D_MODEL), lambda j: (0, 0)),
                  pl.BlockSpec((D_MODEL, tn), lambda j: (0, j)),
                  pl.BlockSpec((1, tn), lambda j: (0, j))],
        out_specs=pl.BlockSpec((rows, tn), lambda j: (0, j)),
        out_shape=jax.ShapeDtypeStruct((rows, n_out), F32),
        compiler_params=_cparams("arbitrary"),
        name="adaln",
    )(c_all, w_ada, b_ada.reshape(1, n_out))


def _inproj_kernel(tiles_per_seq, x_ref, g_ref, sc_ref, sh_ref, w_ref, *rest):
    if tiles_per_seq:
        cw_ref, qkv_ref, z_ref, qb_ref, kvb_ref, ga_ref, gb_ref, ab_ref, tail_ref, carry_ref = rest
    else:
        qkv_ref, z_ref, qb_ref, kvb_ref, ga_ref, gb_ref, ab_ref = rest
    tm = x_ref.shape[0]
    h = (_rms(x_ref[...], g_ref[...]) * (1.0 + sc_ref[0]) + sh_ref[0]).astype(BF16)

    tasks = []

    def fill(ref, c0, width, fn):
        step = min(PROJ_TILE, width)
        tasks.append([(ref, c0, c, step, fn) for c in range(0, width, step)])

    def run(ref, c0, c, step, fn):
        acc = jnp.dot(h, w_ref[:, c0 + c:c0 + c + step], preferred_element_type=F32)
        ref[:, c:c + step] = fn(acc, c, step).astype(ref.dtype)

    def conv_act(acc, c, step):
        cols = slice(c, c + step)
        seq_start = pl.program_id(0) % tiles_per_seq == 0
        prev = jnp.where(seq_start, 0.0, carry_ref[:, cols])
        last = acc[tm - 8:tm]
        carry_ref[:, cols] = last
        tail_ref[0, :, cols] = last
        ext = jnp.concatenate([prev, acc], axis=0)
        y = cw_ref[GDN_CONV - 1:GDN_CONV, cols] * acc
        for j in range(GDN_CONV - 1):
            y = y + cw_ref[j:j + 1, cols] * ext[8 - (GDN_CONV - 1) + j:8 - (GDN_CONV - 1) + j + tm]
        y = _silu(y)
        if c >= 2 * GDN_QK_WIDTH:
            return y
        scale = GDN_DK ** -0.5 if c < GDN_QK_WIDTH else 1.0
        heads = [_l2n(y[:, d:d + GDN_DK]) * scale for d in range(0, step, GDN_DK)]
        return jnp.concatenate(heads, axis=1)

    ident = lambda v, c, step: v
    silu = lambda v, c, step: _silu(v)
    sigmoid = lambda v, c, step: _sigmoid(v)
    fill(qkv_ref, _C_QKV, GDN_CONV_DIM, conv_act if tiles_per_seq else ident)
    fill(z_ref, _C_Z, GDN_WIDTH, silu)
    fill(qb_ref, _C_QB, SWA_WIDTH, ident)
    fill(kvb_ref, _C_KVB, 2 * SWA_KV_WIDTH, ident)
    fill(ga_ref, _C_GA, D_MODEL, sigmoid)
    fill(gb_ref, _C_GB, D_MODEL, sigmoid)
    fill(ab_ref, _C_AB, LANES, ident)
    heavy, light = tasks[0], [t for seg in tasks[1:] for t in seg]
    while heavy or light:
        for queue in (light, heavy):
            if queue:
                run(*queue.pop(0))


def _mod_spec(mod, n_tiles):
    tiles_per_mod = n_tiles // mod.shape[0]
    return pl.BlockSpec((1, mod.shape[1], D_MODEL), lambda i: (i // tiles_per_mod, 0, 0))


def _inproj(x2d, gain, sc, sh, w_prep, tm, conv_w=None, n_seq=0):
    rows = x2d.shape[0]
    n_tiles = rows // tm
    widths = (GDN_CONV_DIM, GDN_WIDTH, SWA_WIDTH, 2 * SWA_KV_WIDTH, D_MODEL, D_MODEL, LANES)
    dtypes = (BF16, BF16, BF16, BF16, BF16, BF16, F32)
    in_specs = [pl.BlockSpec((tm, D_MODEL), lambda i: (i, 0)),
                pl.BlockSpec((1, D_MODEL), lambda i: (0, 0)),
                _mod_spec(sc, n_tiles), _mod_spec(sh, n_tiles),
                pl.BlockSpec((D_MODEL, IN_COLS), lambda i: (0, 0))]
    out_specs = [pl.BlockSpec((tm, w), lambda i: (i, 0)) for w in widths]
    out_shape = [jax.ShapeDtypeStruct((rows, w), dt) for w, dt in zip(widths, dtypes)]
    args = [x2d, gain.reshape(1, D_MODEL), sc, sh, w_prep]
    scratch = []
    tiles_per_seq = 0
    if conv_w is not None:
        tiles_per_seq = n_tiles // n_seq
        in_specs.append(pl.BlockSpec((GDN_CONV, GDN_CONV_DIM), lambda i: (0, 0)))
        out_specs.append(pl.BlockSpec((1, 8, GDN_CONV_DIM), lambda i: (i // tiles_per_seq, 0, 0)))
        out_shape.append(jax.ShapeDtypeStruct((n_seq, 8, GDN_CONV_DIM), F32))
        args.append(conv_w)
        scratch.append(pltpu.VMEM((8, GDN_CONV_DIM), F32))
    return pl.pallas_call(
        functools.partial(_inproj_kernel, tiles_per_seq),
        grid=(n_tiles,),
        in_specs=in_specs,
        out_specs=out_specs,
        out_shape=out_shape,
        scratch_shapes=scratch,
        compiler_params=_cparams("arbitrary"),
        name="inproj",
    )(*args)


def _cumsum_rows(g):
    c = g.shape[0]
    tril = (_iota2((c, c), 0) >= _iota2((c, c), 1)).astype(BF16)
    hi = g.astype(BF16)
    r1 = g - hi.astype(F32)
    mid = r1.astype(BF16)
    lo = (r1 - mid.astype(F32)).astype(BF16)
    dot = lambda p: jnp.dot(tril, p, preferred_element_type=F32)
    return dot(hi) + dot(mid) + dot(lo)


def _each(fn, *lists):
    return [fn(*args) for args in zip(*lists)]


def _pair_blockdiag(m, left):
    return jnp.concatenate([jnp.where(left, m, 0.0), jnp.where(left, 0.0, m)], axis=0)


def _unit_lower_inverse_offset(a_list, ii, jj, left):
    c = a_list[0].shape[0]

    def same_block(shift):
        return lax.shift_right_logical(ii, shift) == lax.shift_right_logical(jj, shift)

    base = same_block(1)
    n_list = _each(lambda a: jnp.where(base, -a, 0.0), a_list)
    shift = 1
    while (1 << shift) < c:
        outer, inner = same_block(shift + 1), same_block(shift)
        off_list = _each(lambda a: jnp.where(outer, jnp.where(inner, 0.0, a), 0.0), a_list)
        x_list = _each(lambda off, n: off + _bdot(off, _pair_blockdiag(n, left)), off_list, n_list)
        n_list = _each(lambda n, x: n - x - _bdot(n, _pair_blockdiag(x, left)), n_list, x_list)
        shift += 1
    return n_list


def _chunk_prep(q, k, v, gcol, grow, bcol):
    c = q[0].shape[0]
    assert len(q) % 2 == 0
    ii = _iota2((c, 2 * c), 0)
    lane = _iota2((c, 2 * c), 1)
    left = lane < c
    jj = lane & (c - 1)
    causal = ii >= jj
    strict = ii > jj
    first, second = slice(0, None, 2), slice(1, None, 2)
    kb = _each(lambda kk, b: kk * b, k, bcol)
    both = _each(lambda qa, ka, qb, kbb, x, y: _bdot_nt(jnp.concatenate([qa, ka, qb, kbb], axis=0),
                                                       jnp.concatenate([x, y], axis=0)),
                 q[first], kb[first], q[second], kb[second], k[first], k[second])
    decay = _each(lambda ga, gb, ra, rb: jnp.where(causal, jnp.exp(jnp.where(
        causal, jnp.where(left, ga, gb) - jnp.concatenate([ra, rb], axis=1), 0.0)), 0.0),
        gcol[first], gcol[second], grow[first], grow[second])
    qk = _each(lambda bo, d: jnp.where(left, bo[0:c], bo[2 * c:3 * c]) * d, both, decay)
    a = _each(lambda bo, d: jnp.where(strict, jnp.where(left, bo[c:2 * c], bo[3 * c:4 * c]) * d, 0.0), both, decay)
    n = _unit_lower_inverse_offset(a, ii, jj, left)
    eg = _each(jnp.exp, gcol)
    rhs = _each(lambda vv, b, kbb, e: jnp.concatenate([vv * b, kbb * e], axis=1), v, bcol, kb, eg)
    uw = _each(lambda ra, rb, nn: (lambda r: r + _bdot(_pair_blockdiag(nn, left), r))(
        jnp.concatenate([ra, rb], axis=0)), rhs[first], rhs[second], n)
    uw = [x[half] for x in uw for half in (slice(0, c), slice(c, 2 * c))]
    u = [x[:, :GDN_DV] for x in uw]
    w = [x[:, GDN_DV:] for x in uw]
    qd = _each(lambda qq, e: qq * e, q, eg)
    kd = _each(lambda kk, gc: kk * jnp.exp(gc[c - 1:c, :] - gc), k, gcol)
    return u, w, qd, kd, qk


def _chunk_step(s, u, w, qd, kd, qk, ge):
    c = u[0].shape[0]
    both = _each(lambda ww, qq, ss: _bdot(jnp.concatenate([ww, qq], axis=0), ss), w, qd, s)
    v_new = _each(lambda uu, bo: uu.astype(F32) - bo[:c], u, both)
    o = _each(lambda bo, m, vn: bo[c:] + _bdot(m, vn), both, qk, v_new)
    s_new = _each(lambda ss, g, kk, vn: ss * g + _bdot_tn(kk, vn), s, ge, kd, v_new)
    return o, s_new


def _conv_act(xp_ref, cw_ref, r0, rows, c0):
    cols = slice(c0, c0 + LANES)
    acc = cw_ref[3:4, cols] * xp_ref[r0:r0 + rows, cols]
    for j in range(GDN_CONV - 1):
        acc = acc + cw_ref[j:j + 1, cols] * xp_ref[r0 - 3 + j:r0 - 3 + j + rows, cols]
    return _silu(acc)


def _l2n(x):
    return x * lax.rsqrt(jnp.sum(x * x, axis=-1, keepdims=True) + NORM_EPS)


def _softplus(x):
    return jnp.maximum(x, 0.0) + jnp.log1p(jnp.exp(-jnp.abs(x)))


def _head_cols(hd):
    return (hd * GDN_DK, GDN_QK_WIDTH + hd * GDN_DK, 2 * GDN_QK_WIDTH + hd * GDN_DV)


def _activate_qkv(xp_ref, cw_ref, act_ref, r0, rows):
    for hd in range(GDN_HEADS):
        cq, ck, cv = _head_cols(hd)
        act_ref[0:rows, cq:cq + LANES] = _l2n(_conv_act(xp_ref, cw_ref, r0, rows, cq)) * (GDN_DK ** -0.5)
        act_ref[0:rows, ck:ck + LANES] = _l2n(_conv_act(xp_ref, cw_ref, r0, rows, ck))
        act_ref[0:rows, cv:cv + LANES] = _conv_act(xp_ref, cw_ref, r0, rows, cv)


def _decay_beta(ab, hp_ref):
    g = -jnp.exp(hp_ref[0:1, :]) * _softplus(ab + hp_ref[1:2, :])
    return g, _sigmoid(ab)


def _gdn_prep_kernel(act_ref, ab_ref, hp_ref, u_ref, w_ref, qd_ref, kd_ref, qk_ref, ge_ref):
    tb = act_ref.shape[1]
    cc = GDN_CHUNK
    g_all, beta_all = _decay_beta(ab_ref[0], hp_ref)

    chunks = [slice(ci * cc, (ci + 1) * cc) for ci in range(tb // cc)]
    gcs = _each(lambda rows: _cumsum_rows(g_all[rows, :]), chunks)
    gcts = _each(lambda gc: gc.T, gcs)
    for ci, gc in enumerate(gcs):
        ge_ref[0, ci] = jnp.exp(gc[cc - 1:cc, :])
    items = [(ci, hd) for ci in range(len(chunks)) for hd in range(GDN_HEADS)]
    col = lambda which: [act_ref[0, chunks[ci], _head_cols(hd)[which]:_head_cols(hd)[which] + LANES].astype(F32)
                         for ci, hd in items]
    u, w, qd, kd, qk = _chunk_prep(
        col(0), col(1), col(2),
        [gcs[ci][:, hd:hd + 1] for ci, hd in items], [gcts[ci][hd:hd + 1, :] for ci, hd in items],
        [beta_all[chunks[ci], GDN_HEADS + hd:GDN_HEADS + hd + 1] for ci, hd in items])
    for idx, (ci, hd) in enumerate(items):
        rows = chunks[ci]
        oc = slice(hd * GDN_DV, (hd + 1) * GDN_DV)
        u_ref[0, rows, oc] = u[idx].astype(BF16)
        w_ref[0, rows, oc] = w[idx].astype(BF16)
        qd_ref[0, rows, oc] = qd[idx].astype(BF16)
        kd_ref[0, rows, oc] = kd[idx].astype(BF16)
        if hd % 2 == 0:
            qk_ref[0, rows, hd * cc:(hd + 2) * cc] = qk[idx // 2].astype(BF16)


def _gdn_prep(qkv, ab, head_params):
    b, t, _ = qkv.shape
    tb = min(GDN_PREP_ROWS, t)
    nch = tb // GDN_CHUNK
    blk = lambda w: pl.BlockSpec((1, tb, w), lambda bi, i: (bi, i, 0))
    out_shapes = [jax.ShapeDtypeStruct((b, t, GDN_WIDTH), BF16)] * 4 + [
        jax.ShapeDtypeStruct((b, t, GDN_HEADS * GDN_CHUNK), BF16),
        jax.ShapeDtypeStruct((b, t // GDN_CHUNK, 1, LANES), F32)]
    return pl.pallas_call(
        _gdn_prep_kernel,
        grid=(b, t // tb),
        in_specs=[blk(GDN_CONV_DIM), blk(LANES), pl.BlockSpec((8, LANES), lambda bi, i: (0, 0))],
        out_specs=[blk(GDN_WIDTH)] * 4 + [
            blk(GDN_HEADS * GDN_CHUNK),
            pl.BlockSpec((1, nch, 1, LANES), lambda bi, i: (bi, i, 0, 0))],
        out_shape=out_shapes,
        compiler_params=_cparams("parallel", "parallel"),
        name="gdn_prep",
    )(qkv, ab, head_params)


def _gated_norm_store(o_ref, idx, o, gain, zs):
    o_ref[idx] = (_rms(o, gain) * zs.astype(F32)).astype(o_ref.dtype)


def _gdn_scan_kernel(u_ref, w_ref, qd_ref, kd_ref, qk_ref, ge_ref, zs_ref, gain_ref,
                     o_ref, s_out_ref, s_ref):
    nb, tb, _ = u_ref.shape
    cc = GDN_CHUNK
    step = pl.program_id(0)

    @pl.when(step == 0)
    def _():
        s_ref[...] = jnp.zeros_like(s_ref)

    gain = gain_ref[...]

    def chunk_body(ci, carry):
        rows = pl.ds(pl.multiple_of(ci * cc, cc), cc)
        items = [(bi, hd) for bi in range(nb) for hd in range(GDN_HEADS)]
        oc = lambda hd: slice(hd * GDN_DV, (hd + 1) * GDN_DV)
        ge_rows = [ge_ref[bi, ci] for bi in range(nb)]
        o, s_new = _chunk_step(
            [s_ref[bi * GDN_HEADS + hd] for bi, hd in items],
            [u_ref[bi, rows, oc(hd)] for bi, hd in items], [w_ref[bi, rows, oc(hd)] for bi, hd in items],
            [qd_ref[bi, rows, oc(hd)] for bi, hd in items], [kd_ref[bi, rows, oc(hd)] for bi, hd in items],
            [qk_ref[bi, rows, hd * cc:(hd + 1) * cc] for bi, hd in items],
            [ge_rows[bi][:, hd:hd + 1] for bi, hd in items])
        for idx, (bi, hd) in enumerate(items):
            s_ref[bi * GDN_HEADS + hd] = s_new[idx]
            _gated_norm_store(o_ref, (bi, rows, oc(hd)), o[idx], gain, zs_ref[bi, rows, oc(hd)])
        return carry

    lax.fori_loop(0, tb // cc, chunk_body, 0)

    @pl.when(step == pl.num_programs(0) - 1)
    def _():
        s_out_ref[...] = s_ref[...]


def _gdn_scan(u, w, qd, kd, qk, ge, zs, gain):
    b, t, _ = u.shape
    tb = min(GDN_SCAN_ROWS, t)
    nch = tb // GDN_CHUNK
    blk = lambda wd: pl.BlockSpec((b, tb, wd), lambda i: (0, i, 0))
    o, s = pl.pallas_call(
        _gdn_scan_kernel,
        grid=(t // tb,),
        in_specs=[blk(GDN_WIDTH)] * 4 + [
            blk(GDN_HEADS * GDN_CHUNK),
            pl.BlockSpec((b, nch, 1, LANES), lambda i: (0, i, 0, 0)),
            blk(GDN_WIDTH),
            pl.BlockSpec((1, GDN_DV), lambda i: (0, 0))],
        out_specs=[blk(GDN_WIDTH),
                   pl.BlockSpec((b * GDN_HEADS, GDN_DK, GDN_DV), lambda i: (0, 0, 0))],
        out_shape=[jax.ShapeDtypeStruct((b, t, GDN_WIDTH), BF16),
                   jax.ShapeDtypeStruct((b * GDN_HEADS, GDN_DK, GDN_DV), F32)],
        scratch_shapes=[pltpu.VMEM((b * GDN_HEADS, GDN_DK, GDN_DV), F32)],
        compiler_params=_cparams("arbitrary"),
        name="gdn_scan",
    )(u, w, qd, kd, qk, ge, zs, gain.reshape(1, GDN_DV))
    return o, s.reshape(b, GDN_HEADS, GDN_DK, GDN_DV)


def _gdn_sample_kernel(new_rows, xp_ref, ab_ref, zs_ref, s0_ref, cw_ref, hp_ref, gain_ref,
                       o_ref, s_out_ref, xs_ref, act_ref):
    grp = xp_ref.shape[0]
    cc = GDN_SAMPLE_CHUNK
    gain = gain_ref[...]
    rowmask = (_iota2((cc, 1), 0) >= cc - new_rows).astype(F32)
    seqs = list(range(grp))
    for bi in seqs:
        xs = xs_ref.at[bi]
        xs[0:8, :] = jnp.zeros((8, GDN_CONV_DIM), F32)
        xs[8:8 + cc, :] = xp_ref[bi].astype(F32)
        _activate_qkv(xs, cw_ref, act_ref.at[bi], 8, cc)
    gb = _each(lambda bi: _decay_beta(ab_ref[bi], hp_ref), seqs)
    gcs = _each(lambda x: _cumsum_rows(x[0] * rowmask), gb)
    gcts = _each(lambda gc: gc.T, gcs)
    ge_rows = _each(lambda gc: jnp.exp(gc[cc - 1:cc, :]), gcs)
    betas = _each(lambda x: x[1] * rowmask, gb)
    items = [(bi, hd) for bi in seqs for hd in range(GDN_HEADS)]
    col = lambda which: [act_ref[bi, :, _head_cols(hd)[which]:_head_cols(hd)[which] + LANES] * rowmask
                         for bi, hd in items]
    u, w, qd, kd, qk = _chunk_prep(
        col(0), col(1), col(2),
        [gcs[bi][:, hd:hd + 1] for bi, hd in items], [gcts[bi][hd:hd + 1, :] for bi, hd in items],
        [betas[bi][:, GDN_HEADS + hd:GDN_HEADS + hd + 1] for bi, hd in items])
    qk = [pair[:, half] for pair in qk for half in (slice(0, cc), slice(cc, 2 * cc))]
    o, s_new = _chunk_step([s0_ref[bi, hd] for bi, hd in items], u, w, qd, kd, qk,
                           [ge_rows[bi][:, hd:hd + 1] for bi, hd in items])
    for idx, (bi, hd) in enumerate(items):
        s_out_ref[bi, hd] = s_new[idx]
        oc = slice(hd * GDN_DV, (hd + 1) * GDN_DV)
        _gated_norm_store(o_ref, (bi, slice(None), oc), o[idx], gain, zs_ref[bi, :, oc])


def _gdn_sample(xp, ab, zs, s0, conv_w, head_params, gain, new_rows):
    b = xp.shape[0]
    cc = GDN_SAMPLE_CHUNK
    grp = GDN_SAMPLE_GROUP
    blk3 = lambda w: pl.BlockSpec((grp, cc, w), lambda i: (i, 0, 0))
    sblk = pl.BlockSpec((grp, GDN_HEADS, GDN_DK, GDN_DV), lambda i: (i, 0, 0, 0))
    return pl.pallas_call(
        functools.partial(_gdn_sample_kernel, new_rows),
        grid=(b // grp,),
        in_specs=[blk3(GDN_CONV_DIM), blk3(LANES), blk3(GDN_WIDTH), sblk,
                  pl.BlockSpec((GDN_CONV, GDN_CONV_DIM), lambda i: (0, 0)),
                  pl.BlockSpec((8, LANES), lambda i: (0, 0)),
                  pl.BlockSpec((1, GDN_DV), lambda i: (0, 0))],
        out_specs=[blk3(GDN_WIDTH), sblk],
        out_shape=[jax.ShapeDtypeStruct((b, cc, GDN_WIDTH), BF16),
                   jax.ShapeDtypeStruct((b, GDN_HEADS, GDN_DK, GDN_DV), F32)],
        scratch_shapes=[pltpu.VMEM((grp, cc + 8, GDN_CONV_DIM), F32),
                        pltpu.VMEM((grp, cc, GDN_CONV_DIM), F32)],
        compiler_params=_cparams("parallel"),
        name="gdn_sample",
    )(xp, ab, zs, s0, conv_w, head_params, gain.reshape(1, GDN_DV))


def _sink_attention(q, k, v, mask, sink_col):
    s = _each(lambda qq, kk: jnp.where(mask, _bdot_nt(qq, kk) * (SWA_HEAD_DIM ** -0.5), -jnp.inf), q, k)
    m = _each(lambda ss, sk: jnp.maximum(jnp.max(ss, axis=-1, keepdims=True), sk), s, sink_col)
    p = _each(lambda ss, mm: jnp.exp(ss - mm), s, m)
    denom = _each(lambda pp, sk, mm: jnp.sum(pp, axis=-1, keepdims=True) + jnp.exp(sk - mm), p, sink_col, m)
    return _each(lambda pp, vv, dd: _bdot(pp, vv) / dd, p, v, denom)


def _sink_column(sinks_ref, kv_head, rows_per_head):
    parts = [jnp.full((rows_per_head, 1), sinks_ref[kv_head * SWA_GROUP + g], F32) for g in range(SWA_GROUP)]
    return jnp.concatenate(parts, axis=0)


def _swa_prompt_kernel(sinks_ref, q_ref, kvp_ref, kvc_ref, o_ref):
    wnd = WINDOW
    nblk = q_ref.shape[1] // wnd
    step = pl.program_id(1)
    kv = jnp.concatenate([kvp_ref[0], kvc_ref[0]], axis=0)
    cols = SWA_GROUP * wnd
    kj = _iota2((2 * wnd, cols), 0)
    qi = _iota2((2 * wnd, cols), 1) & (wnd - 1)
    dist = qi + wnd - kj
    band = (dist >= 0) & (dist <= wnd)
    first_key = jnp.where(step > 0, 0, wnd)
    masks = [band & (kj >= first_key)] + [band] * (nblk - 1)
    head_cols = lambda h: slice(h * SWA_HEAD_DIM, (h + 1) * SWA_HEAD_DIM)
    q_heads = lambda hk: [hk * SWA_GROUP + g for g in range(SWA_GROUP)]
    scale = SWA_HEAD_DIM ** -0.5
    items = [(j, hk) for j in range(nblk) for hk in range(SWA_KV_HEADS)]
    q_rows = lambda j: slice(j * wnd, (j + 1) * wnd)
    k_rows = lambda j: slice(j * wnd, (j + 2) * wnd)
    q = [jnp.concatenate([q_ref[0, q_rows(j), head_cols(h)] for h in q_heads(hk)], axis=0) for j, hk in items]
    k = [kv[k_rows(j), head_cols(hk)] for j, hk in items]
    v = [kv[k_rows(j), SWA_KV_WIDTH + hk * SWA_HEAD_DIM:SWA_KV_WIDTH + (hk + 1) * SWA_HEAD_DIM] for j, hk in items]
    sinks_kv = [jnp.concatenate([jnp.full((1, wnd), sinks_ref[h], F32) for h in q_heads(hk)], axis=1)
                for hk in range(SWA_KV_HEADS)]
    sink = [sinks_kv[hk] for j, hk in items]
    mask = [masks[j] for j, hk in items]
    s = _each(lambda kk, qq, mk: jnp.where(mk, _bdot_nt(kk, qq) * scale, -jnp.inf), k, q, mask)
    m = _each(lambda ss, sk: jnp.maximum(jnp.max(ss, axis=0, keepdims=True), sk), s, sink)
    p = _each(lambda ss, mm: jnp.exp(ss - mm), s, m)
    denom = _each(lambda pp, sk, mm: jnp.sum(pp, axis=0, keepdims=True) + jnp.exp(sk - mm), p, sink, m)
    ot = _each(lambda vv, pp, dd: _bdot_tn(vv, pp) / dd, v, p, denom)
    for idx, (j, hk) in enumerate(items):
        for g in range(0, SWA_GROUP, 2):
            pair = jnp.concatenate([ot[idx][:, g * wnd:(g + 1) * wnd], ot[idx][:, (g + 1) * wnd:(g + 2) * wnd]],
                                   axis=0)
            h0 = hk * SWA_GROUP + g
            o_ref[0, q_rows(j), h0 * SWA_HEAD_DIM:(h0 + 2) * SWA_HEAD_DIM] = pair.T.astype(o_ref.dtype)


def _swa_prompt(q, kv, sinks):
    b, t, _ = q.shape
    nblk = SWA_BLOCKS_PER_STEP
    rows = nblk * WINDOW
    return pl.pallas_call(
        _swa_prompt_kernel,
        grid=(b, t // rows),
        in_specs=[pl.BlockSpec(memory_space=pltpu.SMEM),
                  pl.BlockSpec((1, rows, SWA_WIDTH), lambda bi, i: (bi, i, 0)),
                  pl.BlockSpec((1, WINDOW, 2 * SWA_KV_WIDTH), lambda bi, i: (bi, jnp.maximum(i * nblk - 1, 0), 0)),
                  pl.BlockSpec((1, rows, 2 * SWA_KV_WIDTH), lambda bi, i: (bi, i, 0))],
        out_specs=pl.BlockSpec((1, rows, SWA_WIDTH), lambda bi, i: (bi, i, 0)),
        out_shape=jax.ShapeDtypeStruct((b, t, SWA_WIDTH), BF16),
        compiler_params=_cparams("parallel", "parallel"),
        name="swa_prompt",
    )(sinks, q, kv, kv)


def _swa_sample_kernel(sinks_ref, q_ref, kvn_ref, kc_ref, vc_ref, o_ref, ko_ref, vo_ref):
    grp, t, _ = q_ref.shape
    wnd = WINDOW
    nk = wnd + BF16_ROWS
    rows = SWA_GROUP * t
    tq = _iota2((rows, nk), 0) & (t - 1)
    kj = _iota2((rows, nk), 1)
    dist = tq + wnd - kj
    mask = (dist >= 0) & (dist <= wnd)
    zpad = jnp.zeros((BF16_ROWS - t, SWA_KV_WIDTH), F32)
    kks, vvs = [], []
    for bi in range(grp):
        kvn = kvn_ref[bi].astype(F32)
        kk = jnp.concatenate([kc_ref[bi], kvn[:, :SWA_KV_WIDTH], zpad], axis=0)
        vv = jnp.concatenate([vc_ref[bi], kvn[:, SWA_KV_WIDTH:], zpad], axis=0)
        ko_ref[bi] = kk[t:t + wnd, :]
        vo_ref[bi] = vv[t:t + wnd, :]
        kks.append(kk)
        vvs.append(vv)
    items = [(bi, hk) for bi in range(grp) for hk in range(SWA_KV_HEADS)]
    head_cols = lambda h: slice(h * SWA_HEAD_DIM, (h + 1) * SWA_HEAD_DIM)
    q_heads = lambda hk: [hk * SWA_GROUP + g for g in range(SWA_GROUP)]
    sink_cols = [_sink_column(sinks_ref, hk, t) for hk in range(SWA_KV_HEADS)]
    o = _sink_attention(
        [jnp.concatenate([q_ref[bi, :, head_cols(h)] for h in q_heads(hk)], axis=0) for bi, hk in items],
        [kks[bi][:, head_cols(hk)] for bi, hk in items], [vvs[bi][:, head_cols(hk)] for bi, hk in items],
        mask, [sink_cols[hk] for bi, hk in items])
    for idx, (bi, hk) in enumerate(items):
        for g, h in enumerate(q_heads(hk)):
            o_ref[bi, :, head_cols(h)] = o[idx][g * t:(g + 1) * t].astype(o_ref.dtype)


def _swa_sample(q, kv_new, k_cache, v_cache, sinks):
    b, t, _ = q.shape
    grp = SWA_SAMPLE_GROUP
    blk = lambda r, w: pl.BlockSpec((grp, r, w), lambda i: (i, 0, 0))
    return pl.pallas_call(
        _swa_sample_kernel,
        grid=(b // grp,),
        in_specs=[pl.BlockSpec(memory_space=pltpu.SMEM),
                  blk(t, SWA_WIDTH), blk(t, 2 * SWA_KV_WIDTH), blk(WINDOW, SWA_KV_WIDTH), blk(WINDOW, SWA_KV_WIDTH)],
        out_specs=[blk(t, SWA_WIDTH), blk(WINDOW, SWA_KV_WIDTH), blk(WINDOW, SWA_KV_WIDTH)],
        out_shape=[jax.ShapeDtypeStruct((b, t, SWA_WIDTH), BF16),
                   jax.ShapeDtypeStruct((b, WINDOW, SWA_KV_WIDTH), F32),
                   jax.ShapeDtypeStruct((b, WINDOW, SWA_KV_WIDTH), F32)],
        compiler_params=_cparams("parallel"),
        name="swa_sample",
    )(sinks, q, kv_new, k_cache, v_cache)


def _route(logits):
    lane = _iota2(logits.shape, 1).astype(F32)
    neg = -jnp.inf

    def first_argmax(vals, valid):
        v = jnp.where(valid, vals, neg)
        m = jnp.max(v, axis=-1, keepdims=True)
        idx = jnp.min(jnp.where(jnp.logical_and(valid, v == m), lane, float(LANES)), axis=-1, keepdims=True)
        return m, idx

    is_group = lane < N_GROUPS
    gmax, gidx = first_argmax(logits, is_group)
    p_group = 1.0 / jnp.sum(jnp.where(is_group, jnp.exp(logits - gmax), 0.0), axis=-1, keepdims=True)
    lo = N_GROUPS + gidx * EXPERTS_PER_GROUP
    in_group = jnp.logical_and(lane >= lo, lane < lo + EXPERTS_PER_GROUP)
    m1, i1 = first_argmax(logits, in_group)
    esum = jnp.sum(jnp.where(in_group, jnp.exp(logits - m1), 0.0), axis=-1, keepdims=True)
    m2, i2 = first_argmax(logits, jnp.logical_and(in_group, lane != i1))
    p1 = 1.0 / esum
    p2 = jnp.exp(m2 - m1) / esum
    tot = p1 + p2
    return i1 - N_GROUPS, i2 - N_GROUPS, p_group * p1 / tot, p_group * p2 / tot


def _post_mixer_kernel(oa_ref, ob_ref, ga_ref, gb_ref, x_ref, gt_ref, sc_ref, sh_ref,
                       wa_ref, wb_ref, wo_ref, gpost_ref, gpre_ref, wr_ref, br_ref, cnt0_ref,
                       x1_ref, h2_ref, rt_ref, cnt_out_ref, meta_ref, *rest):
    cnt_ref = rest[-1]
    if len(rest) == 2:
        rest[0][...] = jnp.zeros_like(rest[0])
    step = pl.program_id(0)

    @pl.when(step == 0)
    def _():
        cnt_ref[...] = cnt0_ref[...]

    merged = (ga_ref[...].astype(F32) * jnp.dot(oa_ref[...], wa_ref[...], preferred_element_type=F32)
              + gb_ref[...].astype(F32) * jnp.dot(ob_ref[...], wb_ref[...], preferred_element_type=F32))
    mix = _bdot(merged, wo_ref[...])
    x1 = x_ref[...] + gt_ref[0] * _rms(mix, gpost_ref[...])
    x1_ref[...] = x1
    h2 = _rms(x1, gpre_ref[...]) * (1.0 + sc_ref[0]) + sh_ref[0]
    _rows_to_tiles(h2_ref, h2)
    h_hi = h2.astype(BF16)
    h_lo = (h2 - h_hi.astype(F32)).astype(BF16)
    part = jnp.dot(h_hi, wr_ref[...], preferred_element_type=F32)
    logits = (part[:, :LANES] + part[:, LANES:]
              + jnp.dot(h_lo, wr_ref[:, :LANES], preferred_element_type=F32) + br_ref[...])
    ia, ib, wa, wb = _route(logits)
    lane = _iota2(logits.shape, 1)
    tm = logits.shape[0]
    lane_f = lane.astype(F32)
    hot_a = (lane_f == ia).astype(F32)
    hot_b = (lane_f == ib).astype(F32)
    hot = hot_a + hot_b
    earlier = (_iota2((tm, tm), 0) > _iota2((tm, tm), 1)).astype(BF16)
    before = jnp.dot(earlier, hot.astype(BF16), preferred_element_type=F32) + cnt_ref[...]
    rank_a = jnp.sum(hot_a * before, axis=-1, keepdims=True)
    rank_b = jnp.sum(hot_b * before, axis=-1, keepdims=True)
    cnt_ref[...] = cnt_ref[...] + jnp.sum(hot, axis=0, keepdims=True)
    cnt_out_ref[...] = cnt_ref[...]
    rt_ref[...] = jnp.where(lane == 0, ia, jnp.where(lane == 1, ib, jnp.where(lane == 2, wa, jnp.where(
        lane == 3, wb, 0.0))))
    packed = jnp.where(lane == 0, rank_a * N_EXPERTS + ia, jnp.where(lane == 1, rank_b * N_EXPERTS + ib, 0.0))
    meta_ref[0] = packed.T[0:TOP_K, :].astype(I32)


def _post_mixer(oa, ob, ga, gb, x2d, gt, sc, sh, w_a, w_b, w_o, g_post, g_pre, w_rt, b_rt, cnt0, tm,
                zero_rows=0):
    rows = x2d.shape[0]
    n_tiles = rows // tm
    row_blk = lambda w: pl.BlockSpec((tm, w), lambda i: (i, 0))
    full = lambda r, c: pl.BlockSpec((r, c), lambda i: (0, 0))
    out_specs = [row_blk(D_MODEL), pl.BlockSpec((tm * TILE_ROWS, LANES), lambda i: (i, 0)), row_blk(LANES),
                 full(1, LANES), pl.BlockSpec((1, TOP_K, tm), lambda i: (i, 0, 0))]
    out_shape = [jax.ShapeDtypeStruct((rows, D_MODEL), F32),
                 jax.ShapeDtypeStruct((rows * TILE_ROWS, LANES), F32),
                 jax.ShapeDtypeStruct((rows, LANES), F32),
                 jax.ShapeDtypeStruct((1, LANES), F32),
                 jax.ShapeDtypeStruct((n_tiles, TOP_K, tm), I32)]
    if zero_rows:
        assert zero_rows % (n_tiles * TILE_ROWS) == 0
        out_specs.append(pl.BlockSpec((zero_rows // n_tiles, LANES), lambda i: (i, 0)))
        out_shape.append(jax.ShapeDtypeStruct((zero_rows, LANES), F32))
    return pl.pallas_call(
        _post_mixer_kernel,
        grid=(n_tiles,),
        in_specs=[row_blk(GDN_WIDTH), row_blk(SWA_WIDTH), row_blk(D_MODEL), row_blk(D_MODEL), row_blk(D_MODEL),
                  _mod_spec(gt, n_tiles), _mod_spec(sc, n_tiles), _mod_spec(sh, n_tiles),
                  full(GDN_WIDTH, D_MODEL), full(SWA_WIDTH, D_MODEL), full(D_MODEL, D_MODEL),
                  full(1, D_MODEL), full(1, D_MODEL), full(D_MODEL, 2 * LANES), full(1, LANES), full(1, LANES)],
        out_specs=out_specs,
        out_shape=out_shape,
        scratch_shapes=[pltpu.VMEM((1, LANES), F32)],
        compiler_params=_cparams("arbitrary"),
        name="post_mixer",
    )(oa, ob, ga, gb, x2d, gt, sc, sh, w_a, w_b, w_o,
      g_post.reshape(1, D_MODEL), g_pre.reshape(1, D_MODEL), w_rt, b_rt, cnt0)


TILE_ROWS = D_MODEL // LANES


def _tiles_to_rows(ref, first, rows):
    base = first * TILE_ROWS
    return jnp.concatenate([ref[pl.ds(base + c, rows, stride=TILE_ROWS), :] for c in range(TILE_ROWS)], axis=1)


def _rows_to_tiles(ref, mat):
    rows = mat.shape[0]
    for c in range(TILE_ROWS):
        ref[pl.ds(c, rows, stride=TILE_ROWS), :] = mat[:, c * LANES:(c + 1) * LANES]


def _tile_copy_loop(n, copies, start):
    def body(t, carry):
        for j, cp in enumerate(copies(t)):
            if start:
                cp.start(priority=j % 2)
            else:
                cp.wait()
        return carry

    lax.fori_loop(0, n, body, 0, unroll=8)


def _slots_kernel(pstart_ref, meta_ref, o_ref):
    packed = meta_ref[...]
    expert = packed & (N_EXPERTS - 1)
    first = jnp.zeros_like(packed)
    for e in range(N_EXPERTS):
        first = jnp.where(expert == e, pstart_ref[e], first)
    o_ref[...] = first + lax.shift_right_logical(packed, N_EXPERTS.bit_length() - 1)


def _slots(meta, pstarts):
    return pl.pallas_call(
        _slots_kernel,
        grid_spec=pltpu.PrefetchScalarGridSpec(
            num_scalar_prefetch=1,
            grid=(1,),
            in_specs=[pl.BlockSpec(meta.shape, lambda i, p: (0, 0, 0))],
            out_specs=pl.BlockSpec(meta.shape, lambda i, p: (0, 0, 0))),
        out_shape=jax.ShapeDtypeStruct(meta.shape, I32),
        compiler_params=_cparams("arbitrary"),
        name="moe_slots",
    )(pstarts, meta)


def _dispatch_kernel(slot_ref, h_ref, xs_in_hbm, xs_hbm, sem):
    del xs_in_hbm
    i = pl.program_id(0)
    tm = h_ref.shape[0] // TILE_ROWS
    sub_tiles = tm // ROW_TILE

    for start in (True, False):
        for j in range(sub_tiles):
            base = (i * sub_tiles + j) * TOP_K * ROW_TILE

            def copies(t, j=j, base=base):
                row = pl.multiple_of((j * ROW_TILE + t) * TILE_ROWS, TILE_ROWS)
                return [pltpu.make_async_copy(h_ref.at[pl.ds(row, TILE_ROWS)],
                                              xs_hbm.at[slot_ref[base + k * ROW_TILE + t]], sem)
                        for k in range(TOP_K)]

            _tile_copy_loop(ROW_TILE, copies, start)


def _dispatch(slots, h_tiles, xs, tm):
    n_tiles = h_tiles.shape[0] // (tm * TILE_ROWS)
    return pl.pallas_call(
        _dispatch_kernel,
        grid_spec=pltpu.PrefetchScalarGridSpec(
            num_scalar_prefetch=1,
            grid=(n_tiles,),
            in_specs=[pl.BlockSpec((tm * TILE_ROWS, LANES), lambda i, s: (i, 0)),
                      pl.BlockSpec(memory_space=pl.ANY)],
            out_specs=pl.BlockSpec(memory_space=pl.ANY),
            scratch_shapes=[pltpu.SemaphoreType.DMA(())]),
        out_shape=jax.ShapeDtypeStruct(xs.shape, xs.dtype),
        input_output_aliases={2: 0},
        compiler_params=_cparams("arbitrary"),
        name="moe_dispatch",
    )(slots, h_tiles, xs)


def _moe_kernel(blk_e_ref, n_used_ref, x_ref, wg_ref, wu_ref, wd_ref, y_ref, wgb, wub, wdb):
    b = pl.program_id(0)
    rows = x_ref.shape[0] // TILE_ROWS
    changed = jnp.logical_or(b == 0, blk_e_ref[b] != blk_e_ref[jnp.maximum(b - 1, 0)])

    @pl.when(changed)
    def _():
        wgb[...] = wg_ref[0].astype(BF16)
        wub[...] = wu_ref[0].astype(BF16)
        wdb[...] = wd_ref[0].astype(BF16)

    @pl.when(b < n_used_ref[0])
    def _():
        x = _tiles_to_rows(x_ref, 0, rows).astype(BF16)
        gate = jnp.dot(x, wgb[...], preferred_element_type=F32)
        up = jnp.dot(x, wub[...], preferred_element_type=F32)
        _rows_to_tiles(y_ref, _bdot(_silu(gate) * up, wdb[...]))

    @pl.when(b >= n_used_ref[0])
    def _():
        y_ref[...] = jnp.zeros_like(y_ref)


def _moe(xs_tiles, blk_e, n_used, w_gate, w_up, w_down):
    n_blocks = blk_e.shape[0]
    rows = MOE_ROWS
    wspec = lambda r, c: pl.BlockSpec((1, r, c), lambda b, be, nu: (be[b], 0, 0))
    xspec = pl.BlockSpec((rows * TILE_ROWS, LANES), lambda b, be, nu: (b, 0))
    return pl.pallas_call(
        _moe_kernel,
        grid_spec=pltpu.PrefetchScalarGridSpec(
            num_scalar_prefetch=2,
            grid=(n_blocks,),
            in_specs=[xspec, wspec(D_MODEL, EXPERT_FF), wspec(D_MODEL, EXPERT_FF), wspec(EXPERT_FF, D_MODEL)],
            out_specs=xspec,
            scratch_shapes=[pltpu.VMEM((D_MODEL, EXPERT_FF), BF16),
                            pltpu.VMEM((D_MODEL, EXPERT_FF), BF16),
                            pltpu.VMEM((EXPERT_FF, D_MODEL), BF16)]),
        out_shape=jax.ShapeDtypeStruct(xs_tiles.shape, F32),
        compiler_params=_cparams("arbitrary"),
        name="moe_experts",
    )(blk_e, n_used, xs_tiles, w_gate, w_up, w_down)


def _combine_kernel(slot_ref, y_hbm, x1_ref, rt_ref, gt_ref, gpost_ref, o_ref, ybuf, sems):
    i = pl.program_id(0)
    n = pl.num_programs(0)
    rows = ybuf.shape[1] // TILE_ROWS
    slot = i % 2

    def gather(step, buf_slot, start):
        def copy(r):
            dst = ybuf.at[buf_slot, pl.ds(pl.multiple_of(r * TILE_ROWS, TILE_ROWS), TILE_ROWS)]
            return pltpu.make_async_copy(y_hbm.at[slot_ref[step * rows + r]], dst, sems.at[buf_slot])

        _tile_copy_loop(rows // 2, lambda t: [copy(2 * t), copy(2 * t + 1)], start)

    @pl.when(i == 0)
    def _():
        gather(0, 0, True)

    @pl.when(i + 1 < n)
    def _():
        gather(i + 1, 1 - slot, True)

    gather(i, slot, False)
    half = rows // 2
    rt = rt_ref[...]
    buf = ybuf.at[slot]
    f = rt[:, 2:3] * _tiles_to_rows(buf, 0, half) + rt[:, 3:4] * _tiles_to_rows(buf, half, half)
    o_ref[...] = x1_ref[...] + gt_ref[0] * _rms(f, gpost_ref[...])


def _combine(slots, yb, x1, rt, gt, g_post):
    rows = x1.shape[0]
    tm = ROW_TILE
    n_tiles = rows // tm
    tiles_per_mod = n_tiles // gt.shape[0]
    return pl.pallas_call(
        _combine_kernel,
        grid_spec=pltpu.PrefetchScalarGridSpec(
            num_scalar_prefetch=1,
            grid=(n_tiles,),
            in_specs=[pl.BlockSpec(memory_space=pl.ANY),
                      pl.BlockSpec((tm, D_MODEL), lambda i, s: (i, 0)),
                      pl.BlockSpec((tm, LANES), lambda i, s: (i, 0)),
                      pl.BlockSpec((1, gt.shape[1], D_MODEL), lambda i, s: (i // tiles_per_mod, 0, 0)),
                      pl.BlockSpec((1, D_MODEL), lambda i, s: (0, 0))],
            out_specs=pl.BlockSpec((tm, D_MODEL), lambda i, s: (i, 0)),
            scratch_shapes=[pltpu.VMEM((2, TOP_K * tm * TILE_ROWS, LANES), F32),
                            pltpu.SemaphoreType.DMA((2,))]),
        out_shape=jax.ShapeDtypeStruct((rows, D_MODEL), F32),
        compiler_params=_cparams("arbitrary"),
        name="moe_combine",
    )(slots, yb, x1, rt, gt, g_post.reshape(1, D_MODEL))


def _dispatch_plan(counts, n_tok):
    pcounts = (counts + MOE_ROWS - 1) // MOE_ROWS * MOE_ROWS
    pends = jnp.cumsum(pcounts)
    pstarts = (pends - pcounts).astype(I32)
    n_blocks = n_tok * TOP_K // MOE_ROWS + N_EXPERTS
    blk_start = jnp.arange(n_blocks, dtype=I32) * MOE_ROWS
    blk_e = jnp.minimum(jnp.sum(blk_start[:, None] >= pends[None, :], axis=1), N_EXPERTS - 1).astype(I32)
    n_used = (pends[-1] // MOE_ROWS).astype(I32).reshape(1)
    return blk_e, n_used, pstarts


def _prep_in_weight_kernel(w_ref, o_ref):
    a0 = 4 * GDN_QK_WIDTH
    a1 = a0 + 2 * GDN_HEADS
    rows = w_ref.shape[0]
    for c in range(0, a0, PROJ_TILE):
        o_ref[:, c:c + PROJ_TILE] = w_ref[:, c:c + PROJ_TILE].astype(BF16)
    for c in range(a0, _C_AB, PROJ_TILE):
        o_ref[:, c:c + PROJ_TILE] = w_ref[:, c + a1 - a0:c + a1 - a0 + PROJ_TILE].astype(BF16)
    ab = jnp.concatenate([w_ref[:, a0:a1], jnp.zeros((rows, LANES - (a1 - a0)), F32)], axis=1)
    o_ref[:, _C_AB:IN_COLS] = ab.astype(BF16)


def _prep_in_weight(w_in):
    rows = 128
    return pl.pallas_call(
        _prep_in_weight_kernel,
        grid=(D_MODEL // rows,),
        in_specs=[pl.BlockSpec((rows, w_in.shape[1]), lambda i: (i, 0))],
        out_specs=pl.BlockSpec((rows, IN_COLS), lambda i: (i, 0)),
        out_shape=jax.ShapeDtypeStruct((D_MODEL, IN_COLS), BF16),
        compiler_params=_cparams("parallel"),
        name="prep_in_weight",
    )(w_in)


def _head_param_tile(a_log, dt_bias):
    tile = jnp.zeros((8, LANES), F32)
    return tile.at[0, :GDN_HEADS].set(a_log.astype(F32)).at[1, :GDN_HEADS].set(dt_bias.astype(F32))


def _router_weight(w_group, b_group, w_router, b_router):
    w = jnp.zeros((D_MODEL, LANES), F32)
    w = w.at[:, :N_GROUPS].set(w_group).at[:, N_GROUPS:N_GROUPS + N_EXPERTS].set(w_router)
    b = jnp.zeros((1, LANES), F32)
    b = b.at[0, :N_GROUPS].set(b_group).at[0, N_GROUPS:N_GROUPS + N_EXPERTS].set(b_router)
    w_hi = w.astype(BF16)
    w_lo = (w - w_hi.astype(F32)).astype(BF16)
    return jnp.concatenate([w_hi, w_lo], axis=1), b


def kernel(x_prompt, x_sample, state_gdn, state_conv, cache_k_win, cache_v_win, c_prompt, c_sample, w_ada, b_ada, g_mix_pre, g_mix_post, g_ffn_pre, g_ffn_post, w_in, conv_w, a_log, dt_bias, gdn_norm, sinks, w_br_gdn, w_br_swa, w_out, w_group, b_group, w_router, b_router, w_gate, w_up, w_down):
    depth = w_ada.shape[0]
    assert depth == 1, "single-layer trunk"
    bp, tp, _ = x_prompt.shape
    bs, ts, _ = x_sample.shape
    n_p = bp * tp
    n_s = bs * ts
    tm = ROW_TILE
    assert tp % tm == 0 and n_s % tm == 0 and ts >= GDN_CONV - 1 and ts + GDN_CONV - 1 <= GDN_SAMPLE_CHUNK
    assert ts & (ts - 1) == 0 and ts <= BF16_ROWS

    c_all = jnp.concatenate([c_prompt, c_sample], axis=0)
    c_rows = -(-c_all.shape[0] // 8) * 8
    c_all = jnp.pad(c_all, ((0, c_rows - c_all.shape[0]), (0, 0)))
    mod = _adaln(c_all, w_ada[0], b_ada[0])
    mods_p = [m[:bp].reshape(bp, 1, D_MODEL) for m in jnp.split(mod, 6, axis=-1)]
    mods_s = [jnp.repeat(m[bp:bp + bs], ts, axis=0).reshape(n_s // tm, tm, D_MODEL)
              for m in jnp.split(mod, 6, axis=-1)]

    w_prep = _prep_in_weight(w_in[0])
    head_params = _head_param_tile(a_log[0], dt_bias[0])
    w_a, w_b, w_o = w_br_gdn[0].astype(BF16), w_br_swa[0].astype(BF16), w_out[0].astype(BF16)
    w_rt, b_rt = _router_weight(w_group[0], b_group[0], w_router[0], b_router[0])
    sinks0 = sinks[0].astype(F32)

    xp2d = x_prompt.reshape(n_p, D_MODEL)
    sh1, sc1, gt1, sh2, sc2, gt2 = mods_p
    qkv_p, zs_p, qb_p, kvb_p, ga_p, gb_p, ab_p, conv_tail_p = _inproj(
        xp2d, g_mix_pre[0], sc1, sh1, w_prep, INPROJ_ROWS, conv_w=conv_w[0].astype(F32), n_seq=bp)
    qkv_p3 = qkv_p.reshape(bp, tp, GDN_CONV_DIM)
    u, w, qd, kd, qk, ge = _gdn_prep(qkv_p3, ab_p.reshape(bp, tp, LANES), head_params)
    oa_p, s_prompt = _gdn_scan(u, w, qd, kd, qk, ge, zs_p.reshape(bp, tp, GDN_WIDTH), gdn_norm[0])
    kvb_p3 = kvb_p.reshape(bp, tp, 2 * SWA_KV_WIDTH)
    ob_p = _swa_prompt(qb_p.reshape(bp, tp, SWA_WIDTH), kvb_p3, sinks0)
    n_slots = ((n_p + n_s) * TOP_K // MOE_ROWS + N_EXPERTS) * MOE_ROWS
    x1_p, h2_p, rt_p, cnt_p, meta_p, xs = _post_mixer(
        oa_p.reshape(n_p, GDN_WIDTH), ob_p.reshape(n_p, SWA_WIDTH), ga_p, gb_p, xp2d, gt1, sc2, sh2,
        w_a, w_b, w_o, g_mix_post[0], g_ffn_pre[0], w_rt, b_rt, jnp.zeros((1, LANES), F32), tm,
        zero_rows=n_slots * TILE_ROWS)

    xs2d = x_sample.reshape(n_s, D_MODEL)
    sh1s, sc1s, gt1s, sh2s, sc2s, gt2s = mods_s
    qkv_s, zs_s, qb_s, kvb_s, ga_s, gb_s, ab_s = _inproj(xs2d, g_mix_pre[0], sc1s, sh1s, w_prep, tm)
    cc = GDN_SAMPLE_CHUNK
    pad_rows = cc - ts - (GDN_CONV - 1)
    qkv_s3 = qkv_s.reshape(bs, ts, GDN_CONV_DIM)
    xp_s = jnp.concatenate([jnp.zeros((bs, pad_rows, GDN_CONV_DIM), BF16), state_conv[0].astype(BF16), qkv_s3],
                           axis=1)
    front = lambda a: jnp.pad(a, ((0, 0), (cc - ts, 0), (0, 0)))
    oa_s16, s_sample = _gdn_sample(xp_s, front(ab_s.reshape(bs, ts, LANES)), front(zs_s.reshape(bs, ts, GDN_WIDTH)),
                                   state_gdn[0].astype(F32), conv_w[0], head_params, gdn_norm[0], ts)
    oa_s = oa_s16[:, cc - ts:, :].reshape(n_s, GDN_WIDTH)
    ob_s, k_new_s, v_new_s = _swa_sample(
        qb_s.reshape(bs, ts, SWA_WIDTH), kvb_s.reshape(bs, ts, 2 * SWA_KV_WIDTH),
        cache_k_win[0].reshape(bs, WINDOW, SWA_KV_WIDTH).astype(F32),
        cache_v_win[0].reshape(bs, WINDOW, SWA_KV_WIDTH).astype(F32), sinks0)
    x1_s, h2_s, rt_s, cnt_all, meta_s = _post_mixer(
        oa_s, ob_s.reshape(n_s, SWA_WIDTH), ga_s, gb_s, xs2d, gt1s, sc2s, sh2s,
        w_a, w_b, w_o, g_mix_post[0], g_ffn_pre[0], w_rt, b_rt, cnt_p, tm)

    blk_e, n_used, pstarts = _dispatch_plan(cnt_all[0, :N_EXPERTS].astype(I32), n_p + n_s)
    slots_p = _slots(meta_p, pstarts).reshape(-1)
    slots_s = _slots(meta_s, pstarts).reshape(-1)
    xs = _dispatch(slots_p, h2_p, xs.reshape(n_slots, TILE_ROWS, LANES), min(DISPATCH_ROWS, n_p))
    xs = _dispatch(slots_s, h2_s, xs, min(DISPATCH_ROWS, n_s))
    yb = _moe(xs.reshape(n_slots * TILE_ROWS, LANES), blk_e, n_used, w_gate[0], w_up[0], w_down[0])
    yb = yb.reshape(n_slots, TILE_ROWS, LANES)
    y_p = _combine(slots_p, yb, x1_p, rt_p, gt2, g_ffn_post[0])
    y_s = _combine(slots_s, yb, x1_s, rt_s, gt2s, g_ffn_post[0])

    f32 = lambda a: a.astype(F32)
    kv_tail = kvb_p3[:, tp - WINDOW:, :]
    kv_heads = lambda a: f32(a).reshape(a.shape[0], WINDOW, SWA_KV_HEADS, SWA_HEAD_DIM)[None]
    return (y_p.reshape(bp, tp, D_MODEL), y_s.reshape(bs, ts, D_MODEL),
            s_prompt[None], conv_tail_p[:, 8 - (GDN_CONV - 1):, :][None],
            kv_heads(kv_tail[:, :, :SWA_KV_WIDTH]), kv_heads(kv_tail[:, :, SWA_KV_WIDTH:]),
            s_sample[None], f32(qkv_s3[:, ts - (GDN_CONV - 1):, :])[None],
            kv_heads(k_new_s), kv_heads(v_new_s))
```

```python
import functools

import jax
import jax.numpy as jnp
from jax import lax
from jax.experimental import pallas as pl
from jax.experimental.pallas import tpu as pltpu

F32 = jnp.float32
BF16 = jnp.bfloat16
I32 = jnp.int32

D_MODEL = 1024
NORM_EPS = 1e-6
GDN_HEADS = 8
GDN_DK = 128
GDN_DV = 128
GDN_CONV = 4
GDN_CHUNK = 64
GDN_QK_WIDTH = GDN_HEADS * GDN_DK
GDN_WIDTH = GDN_HEADS * GDN_DV
GDN_CONV_DIM = 2 * GDN_QK_WIDTH + GDN_WIDTH
SWA_Q_HEADS = 16
SWA_KV_HEADS = 4
SWA_HEAD_DIM = 64
SWA_GROUP = SWA_Q_HEADS // SWA_KV_HEADS
SWA_WIDTH = SWA_Q_HEADS * SWA_HEAD_DIM
SWA_KV_WIDTH = SWA_KV_HEADS * SWA_HEAD_DIM
WINDOW = 128
N_GROUPS = 4
EXPERTS_PER_GROUP = 8
N_EXPERTS = N_GROUPS * EXPERTS_PER_GROUP
TOP_K = 2
EXPERT_FF = 512

LANES = 128
BF16_ROWS = 16
VMEM_LIMIT = 56 * 1024 * 1024

_C_QKV = 0
_C_Z = _C_QKV + GDN_CONV_DIM
_C_QB = _C_Z + GDN_WIDTH
_C_KVB = _C_QB + SWA_WIDTH
_C_GA = _C_KVB + 2 * SWA_KV_WIDTH
_C_GB = _C_GA + D_MODEL
_C_AB = _C_GB + D_MODEL
IN_COLS = _C_AB + LANES
PROJ_TILE = 512

ROW_TILE = 512
INPROJ_ROWS = 512
GDN_PREP_ROWS = 256
GDN_SCAN_ROWS = 512
GDN_SAMPLE_CHUNK = 16
GDN_SAMPLE_GROUP = 8
SWA_SAMPLE_GROUP = 8
SWA_BLOCKS_PER_STEP = 4
MOE_ROWS = 512
DISPATCH_ROWS = 1024


def _cparams(*sem):
    return pltpu.CompilerParams(dimension_semantics=sem, vmem_limit_bytes=VMEM_LIMIT)


def _bdot(a, b):
    return jnp.dot(a.astype(BF16), b.astype(BF16), preferred_element_type=F32)


def _bdot_nt(a, b):
    return lax.dot_general(a.astype(BF16), b.astype(BF16), (((1,), (1,)), ((), ())),
                           preferred_element_type=F32)


def _bdot_tn(a, b):
    return lax.dot_general(a.astype(BF16), b.astype(BF16), (((0,), (0,)), ((), ())),
                           preferred_element_type=F32)


def _sigmoid(x):
    return 1.0 / (1.0 + jnp.exp(-x))


def _silu(x):
    return x * _sigmoid(x)


def _rms(x, gain):
    return x * lax.rsqrt(jnp.mean(x * x, axis=-1, keepdims=True) + NORM_EPS) * gain


def _iota2(shape, dim):
    return lax.broadcasted_iota(I32, shape, dim)


def _adaln_kernel(c_ref, w_ref, b_ref, o_ref):
    o_ref[...] = _bdot(_silu(c_ref[...]), w_ref[...]) + b_ref[...]


def _adaln(c_all, w_ada, b_ada):
    rows = c_all.shape[0]
    n_out = w_ada.shape[1]
    tn = D_MODEL
    return pl.pallas_call(
        _adaln_kernel,
        grid=(n_out // tn,),
        in_specs=[pl.BlockSpec((rows, D_MODEL), lambda j: (0, 0)),
                  pl.BlockSpec((D_MODEL, tn), lambda j: (0, j)),
                  pl.BlockSpec((1, tn), lambda j: (0, j))],
        out_specs=pl.BlockSpec((rows, tn), lambda j: (0, j)),
        out_shape=jax.ShapeDtypeStruct((rows, n_out), F32),
        compiler_params=_cparams("arbitrary"),
        name="adaln",
    )(c_all, w_ada, b_ada.reshape(1, n_out))


def _inproj_kernel(tiles_per_seq, x_ref, g_ref, sc_ref, sh_ref, w_ref, *rest):
    if tiles_per_seq:
        cw_ref, qkv_ref, z_ref, qb_ref, kvb_ref, ga_ref, gb_ref, ab_ref, tail_ref, carry_ref = rest
    else:
        qkv_ref, z_ref, qb_ref, kvb_ref, ga_ref, gb_ref, ab_ref = rest
    tm = x_ref.shape[0]
    h = (_rms(x_ref[...], g_ref[...]) * (1.0 + sc_ref[0]) + sh_ref[0]).astype(BF16)

    tasks = []

    def fill(ref, c0, width, fn):
        step = min(PROJ_TILE, width)
        tasks.append([(ref, c0, c, step, fn) for c in range(0, width, step)])

    def run(ref, c0, c, step, fn):
        acc = jnp.dot(h, w_ref[:, c0 + c:c0 + c + step], preferred_element_type=F32)
        ref[:, c:c + step] = fn(acc, c, step).astype(ref.dtype)

    def conv_act(acc, c, step):
        cols = slice(c, c + step)
        seq_start = pl.program_id(0) % tiles_per_seq == 0
        prev = jnp.where(seq_start, 0.0, carry_ref[:, cols])
        last = acc[tm - 8:tm]
        carry_ref[:, cols] = last
        tail_ref[0, :, cols] = last
        ext = jnp.concatenate([prev, acc], axis=0)
        y = cw_ref[GDN_CONV - 1:GDN_CONV, cols] * acc
        for j in range(GDN_CONV - 1):
            y = y + cw_ref[j:j + 1, cols] * ext[8 - (GDN_CONV - 1) + j:8 - (GDN_CONV - 1) + j + tm]
        y = _silu(y)
        if c >= 2 * GDN_QK_WIDTH:
            return y
        scale = GDN_DK ** -0.5 if c < GDN_QK_WIDTH else 1.0
        heads = [_l2n(y[:, d:d + GDN_DK]) * scale for d in range(0, step, GDN_DK)]
        return jnp.concatenate(heads, axis=1)

    ident = lambda v, c, step: v
    silu = lambda v, c, step: _silu(v)
    sigmoid = lambda v, c, step: _sigmoid(v)
    fill(qkv_ref, _C_QKV, GDN_CONV_DIM, conv_act if tiles_per_seq else ident)
    fill(z_ref, _C_Z, GDN_WIDTH, silu)
    fill(qb_ref, _C_QB, SWA_WIDTH, ident)
    fill(kvb_ref, _C_KVB, 2 * SWA_KV_WIDTH, ident)
    fill(ga_ref, _C_GA, D_MODEL, sigmoid)
    fill(gb_ref, _C_GB, D_MODEL, sigmoid)
    fill(ab_ref, _C_AB, LANES, ident)
    heavy, light = tasks[0], [t for seg in tasks[1:] for t in seg]
    while heavy or light:
        for queue in (light, heavy):
            if queue:
                run(*queue.pop(0))


def _mod_spec(mod, n_tiles):
    tiles_per_mod = n_tiles // mod.shape[0]
    return pl.BlockSpec((1, mod.shape[1], D_MODEL), lambda i: (i // tiles_per_mod, 0, 0))


def _inproj(x2d, gain, sc, sh, w_prep, tm, conv_w=None, n_seq=0):
    rows = x2d.shape[0]
    n_tiles = rows // tm
    widths = (GDN_CONV_DIM, GDN_WIDTH, SWA_WIDTH, 2 * SWA_KV_WIDTH, D_MODEL, D_MODEL, LANES)
    dtypes = (BF16, BF16, BF16, BF16, BF16, BF16, F32)
    in_specs = [pl.BlockSpec((tm, D_MODEL), lambda i: (i, 0)),
                pl.BlockSpec((1, D_MODEL), lambda i: (0, 0)),
                _mod_spec(sc, n_tiles), _mod_spec(sh, n_tiles),
                pl.BlockSpec((D_MODEL, IN_COLS), lambda i: (0, 0))]
    out_specs = [pl.BlockSpec((tm, w), lambda i: (i, 0)) for w in widths]
    out_shape = [jax.ShapeDtypeStruct((rows, w), dt) for w, dt in zip(widths, dtypes)]
    args = [x2d, gain.reshape(1, D_MODEL), sc, sh, w_prep]
    scratch = []
    tiles_per_seq = 0
    if conv_w is not None:
        tiles_per_seq = n_tiles // n_seq
        in_specs.append(pl.BlockSpec((GDN_CONV, GDN_CONV_DIM), lambda i: (0, 0)))
        out_specs.append(pl.BlockSpec((1, 8, GDN_CONV_DIM), lambda i: (i // tiles_per_seq, 0, 0)))
        out_shape.append(jax.ShapeDtypeStruct((n_seq, 8, GDN_CONV_DIM), F32))
        args.append(conv_w)
        scratch.append(pltpu.VMEM((8, GDN_CONV_DIM), F32))
    return pl.pallas_call(
        functools.partial(_inproj_kernel, tiles_per_seq),
        grid=(n_tiles,),
        in_specs=in_specs,
        out_specs=out_specs,
        out_shape=out_shape,
        scratch_shapes=scratch,
        compiler_params=_cparams("arbitrary"),
        name="inproj",
    )(*args)


def _cumsum_rows(g):
    c = g.shape[0]
    tril = (_iota2((c, c), 0) >= _iota2((c, c), 1)).astype(BF16)
    hi = g.astype(BF16)
    r1 = g - hi.astype(F32)
    mid = r1.astype(BF16)
    lo = (r1 - mid.astype(F32)).astype(BF16)
    dot = lambda p: jnp.dot(tril, p, preferred_element_type=F32)
    return dot(hi) + dot(mid) + dot(lo)


def _each(fn, *lists):
    return [fn(*args) for args in zip(*lists)]


def _pair_blockdiag(m, left):
    return jnp.concatenate([jnp.where(left, m, 0.0), jnp.where(left, 0.0, m)], axis=0)


def _unit_lower_inverse_offset(a_list, ii, jj, left):
    c = a_list[0].shape[0]

    def same_block(shift):
        return lax.shift_right_logical(ii, shift) == lax.shift_right_logical(jj, shift)

    base = same_block(1)
    n_list = _each(lambda a: jnp.where(base, -a, 0.0), a_list)
    shift = 1
    while (1 << shift) < c:
        outer, inner = same_block(shift + 1), same_block(shift)
        off_list = _each(lambda a: jnp.where(outer, jnp.where(inner, 0.0, a), 0.0), a_list)
        x_list = _each(lambda off, n: off + _bdot(off, _pair_blockdiag(n, left)), off_list, n_list)
        n_list = _each(lambda n, x: n - x - _bdot(n, _pair_blockdiag(x, left)), n_list, x_list)
        shift += 1
    return n_list


def _chunk_prep(q, k, v, gcol, grow, bcol):
    c = q[0].shape[0]
    assert len(q) % 2 == 0
    ii = _iota2((c, 2 * c), 0)
    lane = _iota2((c, 2 * c), 1)
    left = lane < c
    jj = lane & (c - 1)
    causal = ii >= jj
    strict = ii > jj
    first, second = slice(0, None, 2), slice(1, None, 2)
    kb = _each(lambda kk, b: kk * b, k, bcol)
    both = _each(lambda qa, ka, qb, kbb, x, y: _bdot_nt(jnp.concatenate([qa, ka, qb, kbb], axis=0),
                                                       jnp.concatenate([x, y], axis=0)),
                 q[first], kb[first], q[second], kb[second], k[first], k[second])
    decay = _each(lambda ga, gb, ra, rb: jnp.where(causal, jnp.exp(jnp.where(
        causal, jnp.where(left, ga, gb) - jnp.concatenate([ra, rb], axis=1), 0.0)), 0.0),
        gcol[first], gcol[second], grow[first], grow[second])
    qk = _each(lambda bo, d: jnp.where(left, bo[0:c], bo[2 * c:3 * c]) * d, both, decay)
    a = _each(lambda bo, d: jnp.where(strict, jnp.where(left, bo[c:2 * c], bo[3 * c:4 * c]) * d, 0.0), both, decay)
    n = _unit_lower_inverse_offset(a, ii, jj, left)
    eg = _each(jnp.exp, gcol)
    rhs = _each(lambda vv, b, kbb, e: jnp.concatenate([vv * b, kbb * e], axis=1), v, bcol, kb, eg)
    uw = _each(lambda ra, rb, nn: (lambda r: r + _bdot(_pair_blockdiag(nn, left), r))(
        jnp.concatenate([ra, rb], axis=0)), rhs[first], rhs[second], n)
    uw = [x[half] for x in uw for half in (slice(0, c), slice(c, 2 * c))]
    u = [x[:, :GDN_DV] for x in uw]
    w = [x[:, GDN_DV:] for x in uw]
    qd = _each(lambda qq, e: qq * e, q, eg)
    kd = _each(lambda kk, gc: kk * jnp.exp(gc[c - 1:c, :] - gc), k, gcol)
    return u, w, qd, kd, qk


def _blockdiag2(a, b):
    z = jnp.zeros_like(a)
    return jnp.concatenate([jnp.concatenate([a, z], axis=1), jnp.concatenate([z, b], axis=1)], axis=0)


def _chunk_step(s, u, w, qd, kd, qk, ge):
    c = u[0].shape[0]
    first, second = slice(0, None, 2), slice(1, None, 2)
    bf = lambda x: x.astype(BF16)
    both = _each(lambda wa, qa, wb, qb, sa, sb: jnp.dot(
        jnp.concatenate([jnp.concatenate([bf(wa), bf(qa)], axis=0), jnp.concatenate([bf(wb), bf(qb)], axis=0)], axis=1),
        _blockdiag2(bf(sa), bf(sb)), preferred_element_type=F32),
        w[first], qd[first], w[second], qd[second], s[first], s[second])
    v_new = [uu.astype(F32) - bo[:c, half] for uu, bo, half in zip(
        u, [b for b in both for _ in range(2)], [slice(0, GDN_DV), slice(GDN_DV, 2 * GDN_DV)] * len(both))]
    o_pair = _each(lambda bo, m, va, vb: bo[c:] + jnp.dot(bf(m), _blockdiag2(bf(va), bf(vb)),
                                                         preferred_element_type=F32),
                   both, qk, v_new[first], v_new[second])
    o = [op[:, half] for op in o_pair for half in (slice(0, GDN_DV), slice(GDN_DV, 2 * GDN_DV))]
    s_new = _each(lambda ss, g, kk, vn: ss * g + _bdot_tn(kk, vn), s, ge, kd, v_new)
    return o, s_new


def _conv_act(xp_ref, cw_ref, r0, rows, c0):
    cols = slice(c0, c0 + LANES)
    acc = cw_ref[3:4, cols] * xp_ref[r0:r0 + rows, cols]
    for j in range(GDN_CONV - 1):
        acc = acc + cw_ref[j:j + 1, cols] * xp_ref[r0 - 3 + j:r0 - 3 + j + rows, cols]
    return _silu(acc)


def _l2n(x):
    return x * lax.rsqrt(jnp.sum(x * x, axis=-1, keepdims=True) + NORM_EPS)


def _softplus(x):
    return jnp.maximum(x, 0.0) + jnp.log1p(jnp.exp(-jnp.abs(x)))


def _head_cols(hd):
    return (hd * GDN_DK, GDN_QK_WIDTH + hd * GDN_DK, 2 * GDN_QK_WIDTH + hd * GDN_DV)


def _activate_qkv(xp_ref, cw_ref, act_ref, r0, rows):
    for hd in range(GDN_HEADS):
        cq, ck, cv = _head_cols(hd)
        act_ref[0:rows, cq:cq + LANES] = _l2n(_conv_act(xp_ref, cw_ref, r0, rows, cq)) * (GDN_DK ** -0.5)
        act_ref[0:rows, ck:ck + LANES] = _l2n(_conv_act(xp_ref, cw_ref, r0, rows, ck))
        act_ref[0:rows, cv:cv + LANES] = _conv_act(xp_ref, cw_ref, r0, rows, cv)


def _decay_beta(ab, hp_ref):
    g = -jnp.exp(hp_ref[0:1, :]) * _softplus(ab + hp_ref[1:2, :])
    return g, _sigmoid(ab)


def _gdn_prep_kernel(act_ref, ab_ref, hp_ref, u_ref, w_ref, qd_ref, kd_ref, qk_ref, ge_ref):
    tb = act_ref.shape[1]
    cc = GDN_CHUNK
    g_all, beta_all = _decay_beta(ab_ref[0], hp_ref)

    chunks = [slice(ci * cc, (ci + 1) * cc) for ci in range(tb // cc)]
    gcs = _each(lambda rows: _cumsum_rows(g_all[rows, :]), chunks)
    gcts = _each(lambda gc: gc.T, gcs)
    for ci, gc in enumerate(gcs):
        ge_ref[0, ci] = jnp.exp(gc[cc - 1:cc, :])
    items = [(ci, hd) for ci in range(len(chunks)) for hd in range(GDN_HEADS)]
    col = lambda which: [act_ref[0, chunks[ci], _head_cols(hd)[which]:_head_cols(hd)[which] + LANES].astype(F32)
                         for ci, hd in items]
    u, w, qd, kd, qk = _chunk_prep(
        col(0), col(1), col(2),
        [gcs[ci][:, hd:hd + 1] for ci, hd in items], [gcts[ci][hd:hd + 1, :] for ci, hd in items],
        [beta_all[chunks[ci], GDN_HEADS + hd:GDN_HEADS + hd + 1] for ci, hd in items])
    for idx, (ci, hd) in enumerate(items):
        rows = chunks[ci]
        oc = slice(hd * GDN_DV, (hd + 1) * GDN_DV)
        u_ref[0, rows, oc] = u[idx].astype(BF16)
        w_ref[0, rows, oc] = w[idx].astype(BF16)
        qd_ref[0, rows, oc] = qd[idx].astype(BF16)
        kd_ref[0, rows, oc] = kd[idx].astype(BF16)
        if hd % 2 == 0:
            qk_ref[0, rows, hd * cc:(hd + 2) * cc] = qk[idx // 2].astype(BF16)


def _gdn_prep(qkv, ab, head_params):
    b, t, _ = qkv.shape
    tb = min(GDN_PREP_ROWS, t)
    nch = tb // GDN_CHUNK
    blk = lambda w: pl.BlockSpec((1, tb, w), lambda bi, i: (bi, i, 0))
    out_shapes = [jax.ShapeDtypeStruct((b, t, GDN_WIDTH), BF16)] * 4 + [
        jax.ShapeDtypeStruct((b, t, GDN_HEADS * GDN_CHUNK), BF16),
        jax.ShapeDtypeStruct((b, t // GDN_CHUNK, 1, LANES), F32)]
    return pl.pallas_call(
        _gdn_prep_kernel,
        grid=(b, t // tb),
        in_specs=[blk(GDN_CONV_DIM), blk(LANES), pl.BlockSpec((8, LANES), lambda bi, i: (0, 0))],
        out_specs=[blk(GDN_WIDTH)] * 4 + [
            blk(GDN_HEADS * GDN_CHUNK),
            pl.BlockSpec((1, nch, 1, LANES), lambda bi, i: (bi, i, 0, 0))],
        out_shape=out_shapes,
        compiler_params=_cparams("parallel", "parallel"),
        name="gdn_prep",
    )(qkv, ab, head_params)


def _gated_norm_store(o_ref, idx, o, gain, zs):
    o_ref[idx] = (_rms(o, gain) * zs.astype(F32)).astype(o_ref.dtype)


def _gdn_scan_kernel(u_ref, w_ref, qd_ref, kd_ref, qk_ref, ge_ref, zs_ref, gain_ref,
                     o_ref, s_out_ref, s_ref):
    nb, tb, _ = u_ref.shape
    cc = GDN_CHUNK
    step = pl.program_id(0)

    @pl.when(step == 0)
    def _():
        s_ref[...] = jnp.zeros_like(s_ref)

    gain = gain_ref[...]

    def chunk_body(ci, carry):
        rows = pl.ds(pl.multiple_of(ci * cc, cc), cc)
        items = [(bi, hd) for bi in range(nb) for hd in range(GDN_HEADS)]
        oc = lambda hd: slice(hd * GDN_DV, (hd + 1) * GDN_DV)
        ge_rows = [ge_ref[bi, ci] for bi in range(nb)]
        o, s_new = _chunk_step(
            [s_ref[bi * GDN_HEADS + hd] for bi, hd in items],
            [u_ref[bi, rows, oc(hd)] for bi, hd in items], [w_ref[bi, rows, oc(hd)] for bi, hd in items],
            [qd_ref[bi, rows, oc(hd)] for bi, hd in items], [kd_ref[bi, rows, oc(hd)] for bi, hd in items],
            [qk_ref[bi, rows, hd * cc:(hd + 2) * cc] for bi, hd in items if hd % 2 == 0],
            [ge_rows[bi][:, hd:hd + 1] for bi, hd in items])
        for idx, (bi, hd) in enumerate(items):
            s_ref[bi * GDN_HEADS + hd] = s_new[idx]
            _gated_norm_store(o_ref, (bi, rows, oc(hd)), o[idx], gain, zs_ref[bi, rows, oc(hd)])
        return carry

    lax.fori_loop(0, tb // cc, chunk_body, 0, unroll=2)

    @pl.when(step == pl.num_programs(0) - 1)
    def _():
        s_out_ref[...] = s_ref[...]


def _gdn_scan(u, w, qd, kd, qk, ge, zs, gain):
    b, t, _ = u.shape
    tb = min(GDN_SCAN_ROWS, t)
    nch = tb // GDN_CHUNK
    blk = lambda wd: pl.BlockSpec((b, tb, wd), lambda i: (0, i, 0))
    o, s = pl.pallas_call(
        _gdn_scan_kernel,
        grid=(t // tb,),
        in_specs=[blk(GDN_WIDTH)] * 4 + [
            blk(GDN_HEADS * GDN_CHUNK),
            pl.BlockSpec((b, nch, 1, LANES), lambda i: (0, i, 0, 0)),
            blk(GDN_WIDTH),
            pl.BlockSpec((1, GDN_DV), lambda i: (0, 0))],
        out_specs=[blk(GDN_WIDTH),
                   pl.BlockSpec((b * GDN_HEADS, GDN_DK, GDN_DV), lambda i: (0, 0, 0))],
        out_shape=[jax.ShapeDtypeStruct((b, t, GDN_WIDTH), BF16),
                   jax.ShapeDtypeStruct((b * GDN_HEADS, GDN_DK, GDN_DV), F32)],
        scratch_shapes=[pltpu.VMEM((b * GDN_HEADS, GDN_DK, GDN_DV), F32)],
        compiler_params=_cparams("arbitrary"),
        name="gdn_scan",
    )(u, w, qd, kd, qk, ge, zs, gain.reshape(1, GDN_DV))
    return o, s.reshape(b, GDN_HEADS, GDN_DK, GDN_DV)


def _gdn_sample_kernel(new_rows, xp_ref, ab_ref, zs_ref, s0_ref, cw_ref, hp_ref, gain_ref,
                       o_ref, s_out_ref, xs_ref, act_ref):
    grp = xp_ref.shape[0]
    cc = GDN_SAMPLE_CHUNK
    gain = gain_ref[...]
    rowmask = (_iota2((cc, 1), 0) >= cc - new_rows).astype(F32)
    seqs = list(range(grp))
    for bi in seqs:
        xs = xs_ref.at[bi]
        xs[0:8, :] = jnp.zeros((8, GDN_CONV_DIM), F32)
        xs[8:8 + cc, :] = xp_ref[bi].astype(F32)
        _activate_qkv(xs, cw_ref, act_ref.at[bi], 8, cc)
    gb = _each(lambda bi: _decay_beta(ab_ref[bi], hp_ref), seqs)
    gcs = _each(lambda x: _cumsum_rows(x[0] * rowmask), gb)
    gcts = _each(lambda gc: gc.T, gcs)
    ge_rows = _each(lambda gc: jnp.exp(gc[cc - 1:cc, :]), gcs)
    betas = _each(lambda x: x[1] * rowmask, gb)
    items = [(bi, hd) for bi in seqs for hd in range(GDN_HEADS)]
    col = lambda which: [act_ref[bi, :, _head_cols(hd)[which]:_head_cols(hd)[which] + LANES] * rowmask
                         for bi, hd in items]
    u, w, qd, kd, qk = _chunk_prep(
        col(0), col(1), col(2),
        [gcs[bi][:, hd:hd + 1] for bi, hd in items], [gcts[bi][hd:hd + 1, :] for bi, hd in items],
        [betas[bi][:, GDN_HEADS + hd:GDN_HEADS + hd + 1] for bi, hd in items])
    o, s_new = _chunk_step([s0_ref[bi, hd] for bi, hd in items], u, w, qd, kd, qk,
                           [ge_rows[bi][:, hd:hd + 1] for bi, hd in items])
    for idx, (bi, hd) in enumerate(items):
        s_out_ref[bi, hd] = s_new[idx]
        oc = slice(hd * GDN_DV, (hd + 1) * GDN_DV)
        _gated_norm_store(o_ref, (bi, slice(None), oc), o[idx], gain, zs_ref[bi, :, oc])


def _gdn_sample(xp, ab, zs, s0, conv_w, head_params, gain, new_rows):
    b = xp.shape[0]
    cc = GDN_SAMPLE_CHUNK
    grp = GDN_SAMPLE_GROUP
    blk3 = lambda w: pl.BlockSpec((grp, cc, w), lambda i: (i, 0, 0))
    sblk = pl.BlockSpec((grp, GDN_HEADS, GDN_DK, GDN_DV), lambda i: (i, 0, 0, 0))
    return pl.pallas_call(
        functools.partial(_gdn_sample_kernel, new_rows),
        grid=(b // grp,),
        in_specs=[blk3(GDN_CONV_DIM), blk3(LANES), blk3(GDN_WIDTH), sblk,
                  pl.BlockSpec((GDN_CONV, GDN_CONV_DIM), lambda i: (0, 0)),
                  pl.BlockSpec((8, LANES), lambda i: (0, 0)),
                  pl.BlockSpec((1, GDN_DV), lambda i: (0, 0))],
        out_specs=[blk3(GDN_WIDTH), sblk],
        out_shape=[jax.ShapeDtypeStruct((b, cc, GDN_WIDTH), BF16),
                   jax.ShapeDtypeStruct((b, GDN_HEADS, GDN_DK, GDN_DV), F32)],
        scratch_shapes=[pltpu.VMEM((grp, cc + 8, GDN_CONV_DIM), F32),
                        pltpu.VMEM((grp, cc, GDN_CONV_DIM), F32)],
        compiler_params=_cparams("parallel"),
        name="gdn_sample",
    )(xp, ab, zs, s0, conv_w, head_params, gain.reshape(1, GDN_DV))


def _sink_attention(q, k, v, mask, sink_col):
    s = _each(lambda qq, kk: jnp.where(mask, _bdot_nt(qq, kk) * (SWA_HEAD_DIM ** -0.5), -jnp.inf), q, k)
    m = _each(lambda ss, sk: jnp.maximum(jnp.max(ss, axis=-1, keepdims=True), sk), s, sink_col)
    p = _each(lambda ss, mm: jnp.exp(ss - mm), s, m)
    denom = _each(lambda pp, sk, mm: jnp.sum(pp, axis=-1, keepdims=True) + jnp.exp(sk - mm), p, sink_col, m)
    return _each(lambda pp, vv, dd: _bdot(pp, vv) / dd, p, v, denom)


def _sink_column(sinks_ref, kv_head, rows_per_head):
    parts = [jnp.full((rows_per_head, 1), sinks_ref[kv_head * SWA_GROUP + g], F32) for g in range(SWA_GROUP)]
    return jnp.concatenate(parts, axis=0)


def _swa_prompt_kernel(sinks_ref, q_ref, kvp_ref, kvc_ref, o_ref):
    wnd = WINDOW
    nblk = q_ref.shape[1] // wnd
    step = pl.program_id(1)
    kv = jnp.concatenate([kvp_ref[0], kvc_ref[0]], axis=0)
    cols = SWA_GROUP * wnd
    kj = _iota2((2 * wnd, cols), 0)
    qi = _iota2((2 * wnd, cols), 1) & (wnd - 1)
    dist = qi + wnd - kj
    band = (dist >= 0) & (dist <= wnd)
    first_key = jnp.where(step > 0, 0, wnd)
    masks = [band & (kj >= first_key)] + [band] * (nblk - 1)
    head_cols = lambda h: slice(h * SWA_HEAD_DIM, (h + 1) * SWA_HEAD_DIM)
    q_heads = lambda hk: [hk * SWA_GROUP + g for g in range(SWA_GROUP)]
    scale = SWA_HEAD_DIM ** -0.5
    items = [(j, hk) for j in range(nblk) for hk in range(SWA_KV_HEADS)]
    q_rows = lambda j: slice(j * wnd, (j + 1) * wnd)
    k_rows = lambda j: slice(j * wnd, (j + 2) * wnd)
    q = [jnp.concatenate([q_ref[0, q_rows(j), head_cols(h)] for h in q_heads(hk)], axis=0) for j, hk in items]
    k = [kv[k_rows(j), head_cols(hk)] for j, hk in items]
    v = [kv[k_rows(j), SWA_KV_WIDTH + hk * SWA_HEAD_DIM:SWA_KV_WIDTH + (hk + 1) * SWA_HEAD_DIM] for j, hk in items]
    sinks_kv = [jnp.concatenate([jnp.full((1, wnd), sinks_ref[h], F32) for h in q_heads(hk)], axis=1)
                for hk in range(SWA_KV_HEADS)]
    sink = [sinks_kv[hk] for j, hk in items]
    mask = [masks[j] for j, hk in items]
    s = _each(lambda kk, qq, mk: jnp.where(mk, _bdot_nt(kk, qq) * scale, -jnp.inf), k, q, mask)
    m = _each(lambda ss, sk: jnp.maximum(jnp.max(ss, axis=0, keepdims=True), sk), s, sink)
    p = _each(lambda ss, mm: jnp.exp(ss - mm), s, m)
    denom = _each(lambda pp, sk, mm: jnp.sum(pp, axis=0, keepdims=True) + jnp.exp(sk - mm), p, sink, m)
    ot = _each(lambda vv, pp, dd: _bdot_tn(vv, pp) / dd, v, p, denom)
    for idx, (j, hk) in enumerate(items):
        for g in range(0, SWA_GROUP, 2):
            pair = jnp.concatenate([ot[idx][:, g * wnd:(g + 1) * wnd], ot[idx][:, (g + 1) * wnd:(g + 2) * wnd]],
                                   axis=0)
            h0 = hk * SWA_GROUP + g
            o_ref[0, q_rows(j), h0 * SWA_HEAD_DIM:(h0 + 2) * SWA_HEAD_DIM] = pair.T.astype(o_ref.dtype)


def _swa_prompt(q, kv, sinks):
    b, t, _ = q.shape
    nblk = SWA_BLOCKS_PER_STEP
    rows = nblk * WINDOW
    return pl.pallas_call(
        _swa_prompt_kernel,
        grid=(b, t // rows),
        in_specs=[pl.BlockSpec(memory_space=pltpu.SMEM),
                  pl.BlockSpec((1, rows, SWA_WIDTH), lambda bi, i: (bi, i, 0)),
                  pl.BlockSpec((1, WINDOW, 2 * SWA_KV_WIDTH), lambda bi, i: (bi, jnp.maximum(i * nblk - 1, 0), 0)),
                  pl.BlockSpec((1, rows, 2 * SWA_KV_WIDTH), lambda bi, i: (bi, i, 0))],
        out_specs=pl.BlockSpec((1, rows, SWA_WIDTH), lambda bi, i: (bi, i, 0)),
        out_shape=jax.ShapeDtypeStruct((b, t, SWA_WIDTH), BF16),
        compiler_params=_cparams("parallel", "parallel"),
        name="swa_prompt",
    )(sinks, q, kv, kv)


def _swa_sample_kernel(sinks_ref, q_ref, kvn_ref, kc_ref, vc_ref, o_ref, ko_ref, vo_ref):
    grp, t, _ = q_ref.shape
    wnd = WINDOW
    nk = wnd + BF16_ROWS
    rows = SWA_GROUP * t
    tq = _iota2((rows, nk), 0) & (t - 1)
    kj = _iota2((rows, nk), 1)
    dist = tq + wnd - kj
    mask = (dist >= 0) & (dist <= wnd)
    zpad = jnp.zeros((BF16_ROWS - t, SWA_KV_WIDTH), F32)
    kks, vvs = [], []
    for bi in range(grp):
        kvn = kvn_ref[bi].astype(F32)
        kk = jnp.concatenate([kc_ref[bi], kvn[:, :SWA_KV_WIDTH], zpad], axis=0)
        vv = jnp.concatenate([vc_ref[bi], kvn[:, SWA_KV_WIDTH:], zpad], axis=0)
        ko_ref[bi] = kk[t:t + wnd, :]
        vo_ref[bi] = vv[t:t + wnd, :]
        kks.append(kk)
        vvs.append(vv)
    items = [(bi, hk) for bi in range(grp) for hk in range(SWA_KV_HEADS)]
    head_cols = lambda h: slice(h * SWA_HEAD_DIM, (h + 1) * SWA_HEAD_DIM)
    q_heads = lambda hk: [hk * SWA_GROUP + g for g in range(SWA_GROUP)]
    sink_cols = [_sink_column(sinks_ref, hk, t) for hk in range(SWA_KV_HEADS)]
    o = _sink_attention(
        [jnp.concatenate([q_ref[bi, :, head_cols(h)] for h in q_heads(hk)], axis=0) for bi, hk in items],
        [kks[bi][:, head_cols(hk)] for bi, hk in items], [vvs[bi][:, head_cols(hk)] for bi, hk in items],
        mask, [sink_cols[hk] for bi, hk in items])
    for idx, (bi, hk) in enumerate(items):
        for g, h in enumerate(q_heads(hk)):
            o_ref[bi, :, head_cols(h)] = o[idx][g * t:(g + 1) * t].astype(o_ref.dtype)


def _swa_sample(q, kv_new, k_cache, v_cache, sinks):
    b, t, _ = q.shape
    grp = SWA_SAMPLE_GROUP
    blk = lambda r, w: pl.BlockSpec((grp, r, w), lambda i: (i, 0, 0))
    return pl.pallas_call(
        _swa_sample_kernel,
        grid=(b // grp,),
        in_specs=[pl.BlockSpec(memory_space=pltpu.SMEM),
                  blk(t, SWA_WIDTH), blk(t, 2 * SWA_KV_WIDTH), blk(WINDOW, SWA_KV_WIDTH), blk(WINDOW, SWA_KV_WIDTH)],
        out_specs=[blk(t, SWA_WIDTH), blk(WINDOW, SWA_KV_WIDTH), blk(WINDOW, SWA_KV_WIDTH)],
        out_shape=[jax.ShapeDtypeStruct((b, t, SWA_WIDTH), BF16),
                   jax.ShapeDtypeStruct((b, WINDOW, SWA_KV_WIDTH), F32),
                   jax.ShapeDtypeStruct((b, WINDOW, SWA_KV_WIDTH), F32)],
        compiler_params=_cparams("parallel"),
        name="swa_sample",
    )(sinks, q, kv_new, k_cache, v_cache)


def _route(logits):
    lane = _iota2(logits.shape, 1).astype(F32)
    neg = -jnp.inf

    def first_argmax(vals, valid):
        v = jnp.where(valid, vals, neg)
        m = jnp.max(v, axis=-1, keepdims=True)
        idx = jnp.min(jnp.where(jnp.logical_and(valid, v == m), lane, float(LANES)), axis=-1, keepdims=True)
        return m, idx

    is_group = lane < N_GROUPS
    gmax, gidx = first_argmax(logits, is_group)
    p_group = 1.0 / jnp.sum(jnp.where(is_group, jnp.exp(logits - gmax), 0.0), axis=-1, keepdims=True)
    lo = N_GROUPS + gidx * EXPERTS_PER_GROUP
    in_group = jnp.logical_and(lane >= lo, lane < lo + EXPERTS_PER_GROUP)
    m1, i1 = first_argmax(logits, in_group)
    esum = jnp.sum(jnp.where(in_group, jnp.exp(logits - m1), 0.0), axis=-1, keepdims=True)
    m2, i2 = first_argmax(logits, jnp.logical_and(in_group, lane != i1))
    p1 = 1.0 / esum
    p2 = jnp.exp(m2 - m1) / esum
    tot = p1 + p2
    return i1 - N_GROUPS, i2 - N_GROUPS, p_group * p1 / tot, p_group * p2 / tot


def _post_mixer_kernel(oa_ref, ob_ref, ga_ref, gb_ref, x_ref, gt_ref, sc_ref, sh_ref,
                       wa_ref, wb_ref, wo_ref, gpost_ref, gpre_ref, wr_ref, br_ref, cnt0_ref,
                       x1_ref, h2_ref, rt_ref, cnt_out_ref, meta_ref, *rest):
    cnt_ref = rest[-1]
    if len(rest) == 2:
        rest[0][...] = jnp.zeros_like(rest[0])
    step = pl.program_id(0)

    @pl.when(step == 0)
    def _():
        cnt_ref[...] = cnt0_ref[...]

    merged = (ga_ref[...].astype(F32) * jnp.dot(oa_ref[...], wa_ref[...], preferred_element_type=F32)
              + gb_ref[...].astype(F32) * jnp.dot(ob_ref[...], wb_ref[...], preferred_element_type=F32))
    mix = _bdot(merged, wo_ref[...])
    x1 = x_ref[...] + gt_ref[0] * _rms(mix, gpost_ref[...])
    x1_ref[...] = x1
    h2 = _rms(x1, gpre_ref[...]) * (1.0 + sc_ref[0]) + sh_ref[0]
    _rows_to_tiles(h2_ref, h2)
    h_hi = h2.astype(BF16)
    h_lo = (h2 - h_hi.astype(F32)).astype(BF16)
    part = jnp.dot(h_hi, wr_ref[...], preferred_element_type=F32)
    logits = (part[:, :LANES] + part[:, LANES:]
              + jnp.dot(h_lo, wr_ref[:, :LANES], preferred_element_type=F32) + br_ref[...])
    ia, ib, wa, wb = _route(logits)
    lane = _iota2(logits.shape, 1)
    tm = logits.shape[0]
    lane_f = lane.astype(F32)
    hot_a = (lane_f == ia).astype(F32)
    hot_b = (lane_f == ib).astype(F32)
    hot = hot_a + hot_b
    earlier = (_iota2((tm, tm), 0) > _iota2((tm, tm), 1)).astype(BF16)
    before = jnp.dot(earlier, hot.astype(BF16), preferred_element_type=F32) + cnt_ref[...]
    rank_a = jnp.sum(hot_a * before, axis=-1, keepdims=True)
    rank_b = jnp.sum(hot_b * before, axis=-1, keepdims=True)
    cnt_ref[...] = cnt_ref[...] + jnp.sum(hot, axis=0, keepdims=True)
    cnt_out_ref[...] = cnt_ref[...]
    rt_ref[...] = jnp.where(lane == 0, ia, jnp.where(lane == 1, ib, jnp.where(lane == 2, wa, jnp.where(
        lane == 3, wb, 0.0))))
    packed = jnp.where(lane == 0, rank_a * N_EXPERTS + ia, jnp.where(lane == 1, rank_b * N_EXPERTS + ib, 0.0))
    meta_ref[0] = packed.T[0:TOP_K, :].astype(I32)


def _post_mixer(oa, ob, ga, gb, x2d, gt, sc, sh, w_a, w_b, w_o, g_post, g_pre, w_rt, b_rt, cnt0, tm,
                zero_rows=0):
    rows = x2d.shape[0]
    n_tiles = rows // tm
    row_blk = lambda w: pl.BlockSpec((tm, w), lambda i: (i, 0))
    full = lambda r, c: pl.BlockSpec((r, c), lambda i: (0, 0))
    out_specs = [row_blk(D_MODEL), pl.BlockSpec((tm * TILE_ROWS, LANES), lambda i: (i, 0)), row_blk(LANES),
                 full(1, LANES), pl.BlockSpec((1, TOP_K, tm), lambda i: (i, 0, 0))]
    out_shape = [jax.ShapeDtypeStruct((rows, D_MODEL), F32),
                 jax.ShapeDtypeStruct((rows * TILE_ROWS, LANES), F32),
                 jax.ShapeDtypeStruct((rows, LANES), F32),
                 jax.ShapeDtypeStruct((1, LANES), F32),
                 jax.ShapeDtypeStruct((n_tiles, TOP_K, tm), I32)]
    if zero_rows:
        assert zero_rows % (n_tiles * TILE_ROWS) == 0
        out_specs.append(pl.BlockSpec((zero_rows // n_tiles, LANES), lambda i: (i, 0)))
        out_shape.append(jax.ShapeDtypeStruct((zero_rows, LANES), F32))
    return pl.pallas_call(
        _post_mixer_kernel,
        grid=(n_tiles,),
        in_specs=[row_blk(GDN_WIDTH), row_blk(SWA_WIDTH), row_blk(D_MODEL), row_blk(D_MODEL), row_blk(D_MODEL),
                  _mod_spec(gt, n_tiles), _mod_spec(sc, n_tiles), _mod_spec(sh, n_tiles),
                  full(GDN_WIDTH, D_MODEL), full(SWA_WIDTH, D_MODEL), full(D_MODEL, D_MODEL),
                  full(1, D_MODEL), full(1, D_MODEL), full(D_MODEL, 2 * LANES), full(1, LANES), full(1, LANES)],
        out_specs=out_specs,
        out_shape=out_shape,
        scratch_shapes=[pltpu.VMEM((1, LANES), F32)],
        compiler_params=_cparams("arbitrary"),
        name="post_mixer",
    )(oa, ob, ga, gb, x2d, gt, sc, sh, w_a, w_b, w_o,
      g_post.reshape(1, D_MODEL), g_pre.reshape(1, D_MODEL), w_rt, b_rt, cnt0)


TILE_ROWS = D_MODEL // LANES


def _tiles_to_rows(ref, first, rows):
    base = first * TILE_ROWS
    return jnp.concatenate([ref[pl.ds(base + c, rows, stride=TILE_ROWS), :] for c in range(TILE_ROWS)], axis=1)


def _rows_to_tiles(ref, mat):
    rows = mat.shape[0]
    for c in range(TILE_ROWS):
        ref[pl.ds(c, rows, stride=TILE_ROWS), :] = mat[:, c * LANES:(c + 1) * LANES]


def _tile_copy_loop(n, copies, start):
    def body(t, carry):
        for j, cp in enumerate(copies(t)):
            if start:
                cp.start(priority=j % 2)
            else:
                cp.wait()
        return carry

    lax.fori_loop(0, n, body, 0, unroll=8)


def _slots_kernel(pstart_ref, meta_ref, o_ref):
    packed = meta_ref[...]
    expert = packed & (N_EXPERTS - 1)
    first = jnp.zeros_like(packed)
    for e in range(N_EXPERTS):
        first = jnp.where(expert == e, pstart_ref[e], first)
    o_ref[...] = first + lax.shift_right_logical(packed, N_EXPERTS.bit_length() - 1)


def _slots(meta, pstarts):
    return pl.pallas_call(
        _slots_kernel,
        grid_spec=pltpu.PrefetchScalarGridSpec(
            num_scalar_prefetch=1,
            grid=(1,),
            in_specs=[pl.BlockSpec(meta.shape, lambda i, p: (0, 0, 0))],
            out_specs=pl.BlockSpec(meta.shape, lambda i, p: (0, 0, 0))),
        out_shape=jax.ShapeDtypeStruct(meta.shape, I32),
        compiler_params=_cparams("arbitrary"),
        name="moe_slots",
    )(pstarts, meta)


def _dispatch_kernel(slot_ref, h_ref, xs_in_hbm, xs_hbm, sem):
    del xs_in_hbm
    i = pl.program_id(0)
    tm = h_ref.shape[0] // TILE_ROWS
    sub_tiles = tm // ROW_TILE

    for start in (True, False):
        for j in range(sub_tiles):
            base = (i * sub_tiles + j) * TOP_K * ROW_TILE

            def copies(t, j=j, base=base):
                row = pl.multiple_of((j * ROW_TILE + t) * TILE_ROWS, TILE_ROWS)
                return [pltpu.make_async_copy(h_ref.at[pl.ds(row, TILE_ROWS)],
                                              xs_hbm.at[slot_ref[base + k * ROW_TILE + t]], sem)
                        for k in range(TOP_K)]

            _tile_copy_loop(ROW_TILE, copies, start)


def _dispatch(slots, h_tiles, xs, tm):
    n_tiles = h_tiles.shape[0] // (tm * TILE_ROWS)
    return pl.pallas_call(
        _dispatch_kernel,
        grid_spec=pltpu.PrefetchScalarGridSpec(
            num_scalar_prefetch=1,
            grid=(n_tiles,),
            in_specs=[pl.BlockSpec((tm * TILE_ROWS, LANES), lambda i, s: (i, 0)),
                      pl.BlockSpec(memory_space=pl.ANY)],
            out_specs=pl.BlockSpec(memory_space=pl.ANY),
            scratch_shapes=[pltpu.SemaphoreType.DMA(())]),
        out_shape=jax.ShapeDtypeStruct(xs.shape, xs.dtype),
        input_output_aliases={2: 0},
        compiler_params=_cparams("arbitrary"),
        name="moe_dispatch",
    )(slots, h_tiles, xs)


def _moe_kernel(blk_e_ref, n_used_ref, x_ref, wg_ref, wu_ref, wd_ref, y_ref, wgb, wub, wdb):
    b = pl.program_id(0)
    rows = x_ref.shape[0] // TILE_ROWS
    changed = jnp.logical_or(b == 0, blk_e_ref[b] != blk_e_ref[jnp.maximum(b - 1, 0)])

    @pl.when(changed)
    def _():
        wgb[...] = wg_ref[0].astype(BF16)
        wub[...] = wu_ref[0].astype(BF16)
        wdb[...] = wd_ref[0].astype(BF16)

    @pl.when(b < n_used_ref[0])
    def _():
        x = _tiles_to_rows(x_ref, 0, rows).astype(BF16)
        gate = jnp.dot(x, wgb[...], preferred_element_type=F32)
        up = jnp.dot(x, wub[...], preferred_element_type=F32)
        _rows_to_tiles(y_ref, _bdot(_silu(gate) * up, wdb[...]))

    @pl.when(b >= n_used_ref[0])
    def _():
        y_ref[...] = jnp.zeros_like(y_ref)


def _moe(xs_tiles, blk_e, n_used, w_gate, w_up, w_down):
    n_blocks = blk_e.shape[0]
    rows = MOE_ROWS
    wspec = lambda r, c: pl.BlockSpec((1, r, c), lambda b, be, nu: (be[b], 0, 0))
    xspec = pl.BlockSpec((rows * TILE_ROWS, LANES), lambda b, be, nu: (b, 0))
    return pl.pallas_call(
        _moe_kernel,
        grid_spec=pltpu.PrefetchScalarGridSpec(
            num_scalar_prefetch=2,
            grid=(n_blocks,),
            in_specs=[xspec, wspec(D_MODEL, EXPERT_FF), wspec(D_MODEL, EXPERT_FF), wspec(EXPERT_FF, D_MODEL)],
            out_specs=xspec,
            scratch_shapes=[pltpu.VMEM((D_MODEL, EXPERT_FF), BF16),
                            pltpu.VMEM((D_MODEL, EXPERT_FF), BF16),
                            pltpu.VMEM((EXPERT_FF, D_MODEL), BF16)]),
        out_shape=jax.ShapeDtypeStruct(xs_tiles.shape, F32),
        compiler_params=_cparams("arbitrary"),
        name="moe_experts",
    )(blk_e, n_used, xs_tiles, w_gate, w_up, w_down)


def _combine_kernel(slot_ref, y_hbm, x1_ref, rt_ref, gt_ref, gpost_ref, o_ref, ybuf, sems):
    i = pl.program_id(0)
    n = pl.num_programs(0)
    rows = ybuf.shape[1] // TILE_ROWS
    slot = i % 2

    def gather(step, buf_slot, start):
        def copy(r):
            dst = ybuf.at[buf_slot, pl.ds(pl.multiple_of(r * TILE_ROWS, TILE_ROWS), TILE_ROWS)]
            return pltpu.make_async_copy(y_hbm.at[slot_ref[step * rows + r]], dst, sems.at[buf_slot])

        _tile_copy_loop(rows // 2, lambda t: [copy(2 * t), copy(2 * t + 1)], start)

    @pl.when(i == 0)
    def _():
        gather(0, 0, True)

    @pl.when(i + 1 < n)
    def _():
        gather(i + 1, 1 - slot, True)

    gather(i, slot, False)
    half = rows // 2
    rt = rt_ref[...]
    buf = ybuf.at[slot]
    f = rt[:, 2:3] * _tiles_to_rows(buf, 0, half) + rt[:, 3:4] * _tiles_to_rows(buf, half, half)
    o_ref[...] = x1_ref[...] + gt_ref[0] * _rms(f, gpost_ref[...])


def _combine(slots, yb, x1, rt, gt, g_post):
    rows = x1.shape[0]
    tm = ROW_TILE
    n_tiles = rows // tm
    tiles_per_mod = n_tiles // gt.shape[0]
    return pl.pallas_call(
        _combine_kernel,
        grid_spec=pltpu.PrefetchScalarGridSpec(
            num_scalar_prefetch=1,
            grid=(n_tiles,),
            in_specs=[pl.BlockSpec(memory_space=pl.ANY),
                      pl.BlockSpec((tm, D_MODEL), lambda i, s: (i, 0)),
                      pl.BlockSpec((tm, LANES), lambda i, s: (i, 0)),
                      pl.BlockSpec((1, gt.shape[1], D_MODEL), lambda i, s: (i // tiles_per_mod, 0, 0)),
                      pl.BlockSpec((1, D_MODEL), lambda i, s: (0, 0))],
            out_specs=pl.BlockSpec((tm, D_MODEL), lambda i, s: (i, 0)),
            scratch_shapes=[pltpu.VMEM((2, TOP_K * tm * TILE_ROWS, LANES), F32),
                            pltpu.SemaphoreType.DMA((2,))]),
        out_shape=jax.ShapeDtypeStruct((rows, D_MODEL), F32),
        compiler_params=_cparams("arbitrary"),
        name="moe_combine",
    )(slots, yb, x1, rt, gt, g_post.reshape(1, D_MODEL))


def _dispatch_plan(counts, n_tok):
    pcounts = (counts + MOE_ROWS - 1) // MOE_ROWS * MOE_ROWS
    pends = jnp.cumsum(pcounts)
    pstarts = (pends - pcounts).astype(I32)
    n_blocks = n_tok * TOP_K // MOE_ROWS + N_EXPERTS
    blk_start = jnp.arange(n_blocks, dtype=I32) * MOE_ROWS
    blk_e = jnp.minimum(jnp.sum(blk_start[:, None] >= pends[None, :], axis=1), N_EXPERTS - 1).astype(I32)
    n_used = (pends[-1] // MOE_ROWS).astype(I32).reshape(1)
    return blk_e, n_used, pstarts


def _prep_in_weight_kernel(w3_ref, o_ref):
    w_ref = w3_ref.at[0]
    a0 = 4 * GDN_QK_WIDTH
    a1 = a0 + 2 * GDN_HEADS
    rows = w_ref.shape[0]
    for c in range(0, a0, PROJ_TILE):
        o_ref[:, c:c + PROJ_TILE] = w_ref[:, c:c + PROJ_TILE].astype(BF16)
    for c in range(a0, _C_AB, PROJ_TILE):
        o_ref[:, c:c + PROJ_TILE] = w_ref[:, c + a1 - a0:c + a1 - a0 + PROJ_TILE].astype(BF16)
    ab = jnp.concatenate([w_ref[:, a0:a1], jnp.zeros((rows, LANES - (a1 - a0)), F32)], axis=1)
    o_ref[:, _C_AB:IN_COLS] = ab.astype(BF16)


def _prep_in_weight(w_in):
    rows = 128
    return pl.pallas_call(
        _prep_in_weight_kernel,
        grid=(D_MODEL // rows,),
        in_specs=[pl.BlockSpec((1, rows, w_in.shape[2]), lambda i: (0, i, 0))],
        out_specs=pl.BlockSpec((rows, IN_COLS), lambda i: (i, 0)),
        out_shape=jax.ShapeDtypeStruct((D_MODEL, IN_COLS), BF16),
        compiler_params=_cparams("parallel"),
        name="prep_in_weight",
    )(w_in)


def _head_param_tile(a_log, dt_bias):
    tile = jnp.zeros((8, LANES), F32)
    return tile.at[0, :GDN_HEADS].set(a_log.astype(F32)).at[1, :GDN_HEADS].set(dt_bias.astype(F32))


def _router_weight(w_group, b_group, w_router, b_router):
    w = jnp.zeros((D_MODEL, LANES), F32)
    w = w.at[:, :N_GROUPS].set(w_group).at[:, N_GROUPS:N_GROUPS + N_EXPERTS].set(w_router)
    b = jnp.zeros((1, LANES), F32)
    b = b.at[0, :N_GROUPS].set(b_group).at[0, N_GROUPS:N_GROUPS + N_EXPERTS].set(b_router)
    w_hi = w.astype(BF16)
    w_lo = (w - w_hi.astype(F32)).astype(BF16)
    return jnp.concatenate([w_hi, w_lo], axis=1), b


def kernel(x_prompt, x_sample, state_gdn, state_conv, cache_k_win, cache_v_win, c_prompt, c_sample, w_ada, b_ada, g_mix_pre, g_mix_post, g_ffn_pre, g_ffn_post, w_in, conv_w, a_log, dt_bias, gdn_norm, sinks, w_br_gdn, w_br_swa, w_out, w_group, b_group, w_router, b_router, w_gate, w_up, w_down):
    depth = w_ada.shape[0]
    assert depth == 1, "single-layer trunk"
    bp, tp, _ = x_prompt.shape
    bs, ts, _ = x_sample.shape
    n_p = bp * tp
    n_s = bs * ts
    tm = ROW_TILE
    assert tp % tm == 0 and n_s % tm == 0 and ts >= GDN_CONV - 1 and ts + GDN_CONV - 1 <= GDN_SAMPLE_CHUNK
    assert ts & (ts - 1) == 0 and ts <= BF16_ROWS

    c_all = jnp.concatenate([c_prompt, c_sample], axis=0)
    c_rows = -(-c_all.shape[0] // 8) * 8
    c_all = jnp.pad(c_all, ((0, c_rows - c_all.shape[0]), (0, 0)))
    mod = _adaln(c_all, w_ada[0], b_ada[0])
    mods_p = [m[:bp].reshape(bp, 1, D_MODEL) for m in jnp.split(mod, 6, axis=-1)]
    mods_s = [jnp.repeat(m[bp:bp + bs], ts, axis=0).reshape(n_s // tm, tm, D_MODEL)
              for m in jnp.split(mod, 6, axis=-1)]

    w_prep = _prep_in_weight(w_in)
    head_params = _head_param_tile(a_log[0], dt_bias[0])
    w_a, w_b, w_o = w_br_gdn[0].astype(BF16), w_br_swa[0].astype(BF16), w_out[0].astype(BF16)
    w_rt, b_rt = _router_weight(w_group[0], b_group[0], w_router[0], b_router[0])
    sinks0 = sinks[0].astype(F32)

    xp2d = x_prompt.reshape(n_p, D_MODEL)
    sh1, sc1, gt1, sh2, sc2, gt2 = mods_p
    qkv_p, zs_p, qb_p, kvb_p, ga_p, gb_p, ab_p, conv_tail_p = _inproj(
        xp2d, g_mix_pre[0], sc1, sh1, w_prep, INPROJ_ROWS, conv_w=conv_w[0].astype(F32), n_seq=bp)
    qkv_p3 = qkv_p.reshape(bp, tp, GDN_CONV_DIM)
    u, w, qd, kd, qk, ge = _gdn_prep(qkv_p3, ab_p.reshape(bp, tp, LANES), head_params)
    oa_p, s_prompt = _gdn_scan(u, w, qd, kd, qk, ge, zs_p.reshape(bp, tp, GDN_WIDTH), gdn_norm[0])
    kvb_p3 = kvb_p.reshape(bp, tp, 2 * SWA_KV_WIDTH)
    ob_p = _swa_prompt(qb_p.reshape(bp, tp, SWA_WIDTH), kvb_p3, sinks0)
    n_slots = ((n_p + n_s) * TOP_K // MOE_ROWS + N_EXPERTS) * MOE_ROWS
    x1_p, h2_p, rt_p, cnt_p, meta_p, xs = _post_mixer(
        oa_p.reshape(n_p, GDN_WIDTH), ob_p.reshape(n_p, SWA_WIDTH), ga_p, gb_p, xp2d, gt1, sc2, sh2,
        w_a, w_b, w_o, g_mix_post[0], g_ffn_pre[0], w_rt, b_rt, jnp.zeros((1, LANES), F32), tm,
        zero_rows=n_slots * TILE_ROWS)

    xs2d = x_sample.reshape(n_s, D_MODEL)
    sh1s, sc1s, gt1s, sh2s, sc2s, gt2s = mods_s
    qkv_s, zs_s, qb_s, kvb_s, ga_s, gb_s, ab_s = _inproj(xs2d, g_mix_pre[0], sc1s, sh1s, w_prep, tm)
    cc = GDN_SAMPLE_CHUNK
    pad_rows = cc - ts - (GDN_CONV - 1)
    qkv_s3 = qkv_s.reshape(bs, ts, GDN_CONV_DIM)
    xp_s = jnp.concatenate([jnp.zeros((bs, pad_rows, GDN_CONV_DIM), BF16), state_conv[0].astype(BF16), qkv_s3],
                           axis=1)
    front = lambda a: jnp.pad(a, ((0, 0), (cc - ts, 0), (0, 0)))
    oa_s16, s_sample = _gdn_sample(xp_s, front(ab_s.reshape(bs, ts, LANES)), front(zs_s.reshape(bs, ts, GDN_WIDTH)),
                                   state_gdn[0].astype(F32), conv_w[0], head_params, gdn_norm[0], ts)
    oa_s = oa_s16[:, cc - ts:, :].reshape(n_s, GDN_WIDTH)
    ob_s, k_new_s, v_new_s = _swa_sample(
        qb_s.reshape(bs, ts, SWA_WIDTH), kvb_s.reshape(bs, ts, 2 * SWA_KV_WIDTH),
        cache_k_win[0].reshape(bs, WINDOW, SWA_KV_WIDTH).astype(F32),
        cache_v_win[0].reshape(bs, WINDOW, SWA_KV_WIDTH).astype(F32), sinks0)
    x1_s, h2_s, rt_s, cnt_all, meta_s = _post_mixer(
        oa_s, ob_s.reshape(n_s, SWA_WIDTH), ga_s, gb_s, xs2d, gt1s, sc2s, sh2s,
        w_a, w_b, w_o, g_mix_post[0], g_ffn_pre[0], w_rt, b_rt, cnt_p, tm)

    blk_e, n_used, pstarts = _dispatch_plan(cnt_all[0, :N_EXPERTS].astype(I32), n_p + n_s)
    slots_p = _slots(meta_p, pstarts).reshape(-1)
    slots_s = _slots(meta_s, pstarts).reshape(-1)
    xs = _dispatch(slots_p, h2_p, xs.reshape(n_slots, TILE_ROWS, LANES), min(DISPATCH_ROWS, n_p))
    xs = _dispatch(slots_s, h2_s, xs, min(DISPATCH_ROWS, n_s))
    yb = _moe(xs.reshape(n_slots * TILE_ROWS, LANES), blk_e, n_used, w_gate[0], w_up[0], w_down[0])
    yb = yb.reshape(n_slots, TILE_ROWS, LANES)
    y_p = _combine(slots_p, yb, x1_p, rt_p, gt2, g_ffn_post[0])
    y_s = _combine(slots_s, yb, x1_s, rt_s, gt2s, g_ffn_post[0])

    f32 = lambda a: a.astype(F32)
    kv_tail = kvb_p3[:, tp - WINDOW:, :]
    kv_heads = lambda a: f32(a).reshape(a.shape[0], WINDOW, SWA_KV_HEADS, SWA_HEAD_DIM)[None]
    return (y_p.reshape(bp, tp, D_MODEL), y_s.reshape(bs, ts, D_MODEL),
            s_prompt[None], conv_tail_p[:, 8 - (GDN_CONV - 1):, :][None],
            kv_heads(kv_tail[:, :, :SWA_KV_WIDTH]), kv_heads(kv_tail[:, :, SWA_KV_WIDTH:]),
            s_sample[None], f32(qkv_s3[:, ts - (GDN_CONV - 1):, :])[None],
            kv_heads(k_new_s), kv_heads(v_new_s))
```

```python
import functools

import jax
import jax.numpy as jnp
from jax import lax
from jax.experimental import pallas as pl
from jax.experimental.pallas import tpu as pltpu

F32 = jnp.float32
BF16 = jnp.bfloat16
I32 = jnp.int32

D_MODEL = 1024
NORM_EPS = 1e-6
GDN_HEADS = 8
GDN_DK = 128
GDN_DV = 128
GDN_CONV = 4
GDN_CHUNK = 64
GDN_QK_WIDTH = GDN_HEADS * GDN_DK
GDN_WIDTH = GDN_HEADS * GDN_DV
GDN_CONV_DIM = 2 * GDN_QK_WIDTH + GDN_WIDTH
SWA_Q_HEADS = 16
SWA_KV_HEADS = 4
SWA_HEAD_DIM = 64
SWA_GROUP = SWA_Q_HEADS // SWA_KV_HEADS
SWA_WIDTH = SWA_Q_HEADS * SWA_HEAD_DIM
SWA_KV_WIDTH = SWA_KV_HEADS * SWA_HEAD_DIM
WINDOW = 128
N_GROUPS = 4
EXPERTS_PER_GROUP = 8
N_EXPERTS = N_GROUPS * EXPERTS_PER_GROUP
TOP_K = 2
EXPERT_FF = 512

LANES = 128
BF16_ROWS = 16
VMEM_LIMIT = 56 * 1024 * 1024

_C_QKV = 0
_C_Z = _C_QKV + GDN_CONV_DIM
_C_QB = _C_Z + GDN_WIDTH
_C_KVB = _C_QB + SWA_WIDTH
_C_GA = _C_KVB + 2 * SWA_KV_WIDTH
_C_GB = _C_GA + D_MODEL
_C_AB = _C_GB + D_MODEL
IN_COLS = _C_AB + LANES
PROJ_TILE = 512

ROW_TILE = 512
INPROJ_ROWS = 512
GDN_PREP_ROWS = 256
GDN_SCAN_ROWS = 512
GDN_SAMPLE_CHUNK = 16
GDN_SAMPLE_GROUP = 8
SWA_SAMPLE_GROUP = 8
SWA_BLOCKS_PER_STEP = 4
MOE_ROWS = 512
DISPATCH_ROWS = 1024


def _cparams(*sem):
    return pltpu.CompilerParams(dimension_semantics=sem, vmem_limit_bytes=VMEM_LIMIT)


def _bdot(a, b):
    return jnp.dot(a.astype(BF16), b.astype(BF16), preferred_element_type=F32)


def _bdot_nt(a, b):
    return lax.dot_general(a.astype(BF16), b.astype(BF16), (((1,), (1,)), ((), ())),
                           preferred_element_type=F32)


def _bdot_tn(a, b):
    return lax.dot_general(a.astype(BF16), b.astype(BF16), (((0,), (0,)), ((), ())),
                           preferred_element_type=F32)


def _sigmoid(x):
    return 1.0 / (1.0 + jnp.exp(-x))


def _silu(x):
    return x * _sigmoid(x)


def _rms(x, gain):
    return x * lax.rsqrt(jnp.mean(x * x, axis=-1, keepdims=True) + NORM_EPS) * gain


def _iota2(shape, dim):
    return lax.broadcasted_iota(I32, shape, dim)


def _adaln_kernel(c_ref, w_ref, b_ref, o_ref):
    o_ref[...] = _bdot(_silu(c_ref[...]), w_ref[...]) + b_ref[...]


def _adaln(c_all, w_ada, b_ada):
    rows = c_all.shape[0]
    n_out = w_ada.shape[1]
    tn = D_MODEL
    return pl.pallas_call(
        _adaln_kernel,
        grid=(n_out // tn,),
        in_specs=[pl.BlockSpec((rows, D_MODEL), lambda j: (0, 0)),
                  pl.BlockSpec((D_MODEL, tn), lambda j: (0, j)),
                  pl.BlockSpec((1, tn), lambda j: (0, j))],
        out_specs=pl.BlockSpec((rows, tn), lambda j: (0, j)),
        out_shape=jax.ShapeDtypeStruct((rows, n_out), F32),
        compiler_params=_cparams("arbitrary"),
        name="adaln",
    )(c_all, w_ada, b_ada.reshape(1, n_out))


def _inproj_kernel(tiles_per_seq, x_ref, g_ref, sc_ref, sh_ref, w_ref, *rest):
    if tiles_per_seq:
        cw_ref, qkv_ref, z_ref, qb_ref, kvb_ref, ga_ref, gb_ref, ab_ref, tail_ref, carry_ref = rest
    else:
        qkv_ref, z_ref, qb_ref, kvb_ref, ga_ref, gb_ref, ab_ref = rest
    tm = x_ref.shape[0]
    h = (_rms(x_ref[...], g_ref[...]) * (1.0 + sc_ref[0]) + sh_ref[0]).astype(BF16)

    tasks = []

    def fill(ref, c0, width, fn):
        step = min(PROJ_TILE, width)
        tasks.append([(ref, c0, c, step, fn) for c in range(0, width, step)])

    def run(ref, c0, c, step, fn):
        acc = lax.dot_general(h, w_ref[c0 + c:c0 + c + step, :], (((1,), (1,)), ((), ())),
                              preferred_element_type=F32)
        ref[:, c:c + step] = fn(acc, c, step).astype(ref.dtype)

    def conv_act(acc, c, step):
        cols = slice(c, c + step)
        seq_start = pl.program_id(0) % tiles_per_seq == 0
        prev = jnp.where(seq_start, 0.0, carry_ref[:, cols])
        last = acc[tm - 8:tm]
        carry_ref[:, cols] = last
        tail_ref[0, :, cols] = last
        ext = jnp.concatenate([prev, acc], axis=0)
        y = cw_ref[GDN_CONV - 1:GDN_CONV, cols] * acc
        for j in range(GDN_CONV - 1):
            y = y + cw_ref[j:j + 1, cols] * ext[8 - (GDN_CONV - 1) + j:8 - (GDN_CONV - 1) + j + tm]
        y = _silu(y)
        if c >= 2 * GDN_QK_WIDTH:
            return y
        scale = GDN_DK ** -0.5 if c < GDN_QK_WIDTH else 1.0
        heads = [_l2n(y[:, d:d + GDN_DK]) * scale for d in range(0, step, GDN_DK)]
        return jnp.concatenate(heads, axis=1)

    ident = lambda v, c, step: v
    silu = lambda v, c, step: _silu(v)
    sigmoid = lambda v, c, step: _sigmoid(v)
    fill(qkv_ref, _C_QKV, GDN_CONV_DIM, conv_act if tiles_per_seq else ident)
    fill(z_ref, _C_Z, GDN_WIDTH, silu)
    fill(qb_ref, _C_QB, SWA_WIDTH, ident)
    fill(kvb_ref, _C_KVB, 2 * SWA_KV_WIDTH, ident)
    fill(ga_ref, _C_GA, D_MODEL, sigmoid)
    fill(gb_ref, _C_GB, D_MODEL, sigmoid)
    fill(ab_ref, _C_AB, LANES, ident)
    heavy, light = tasks[0], [t for seg in tasks[1:] for t in seg]
    while heavy or light:
        for queue in (light, heavy):
            if queue:
                run(*queue.pop(0))


def _mod_spec(mod, n_tiles):
    tiles_per_mod = n_tiles // mod.shape[0]
    return pl.BlockSpec((1, mod.shape[1], D_MODEL), lambda i: (i // tiles_per_mod, 0, 0))


def _inproj(x2d, gain, sc, sh, w_prep, tm, conv_w=None, n_seq=0):
    rows = x2d.shape[0]
    n_tiles = rows // tm
    widths = (GDN_CONV_DIM, GDN_WIDTH, SWA_WIDTH, 2 * SWA_KV_WIDTH, D_MODEL, D_MODEL, LANES)
    dtypes = (BF16, BF16, BF16, BF16, BF16, BF16, F32)
    in_specs = [pl.BlockSpec((tm, D_MODEL), lambda i: (i, 0)),
                pl.BlockSpec((1, D_MODEL), lambda i: (0, 0)),
                _mod_spec(sc, n_tiles), _mod_spec(sh, n_tiles),
                pl.BlockSpec((IN_COLS, D_MODEL), lambda i: (0, 0))]
    out_specs = [pl.BlockSpec((tm, w), lambda i: (i, 0)) for w in widths]
    out_shape = [jax.ShapeDtypeStruct((rows, w), dt) for w, dt in zip(widths, dtypes)]
    args = [x2d, gain.reshape(1, D_MODEL), sc, sh, w_prep]
    scratch = []
    tiles_per_seq = 0
    if conv_w is not None:
        tiles_per_seq = n_tiles // n_seq
        in_specs.append(pl.BlockSpec((GDN_CONV, GDN_CONV_DIM), lambda i: (0, 0)))
        out_specs.append(pl.BlockSpec((1, 8, GDN_CONV_DIM), lambda i: (i // tiles_per_seq, 0, 0)))
        out_shape.append(jax.ShapeDtypeStruct((n_seq, 8, GDN_CONV_DIM), F32))
        args.append(conv_w)
        scratch.append(pltpu.VMEM((8, GDN_CONV_DIM), F32))
    return pl.pallas_call(
        functools.partial(_inproj_kernel, tiles_per_seq),
        grid=(n_tiles,),
        in_specs=in_specs,
        out_specs=out_specs,
        out_shape=out_shape,
        scratch_shapes=scratch,
        compiler_params=_cparams("arbitrary"),
        name="inproj",
    )(*args)


def _cumsum_rows(g):
    c = g.shape[0]
    tril = (_iota2((c, c), 0) >= _iota2((c, c), 1)).astype(BF16)
    hi = g.astype(BF16)
    r1 = g - hi.astype(F32)
    mid = r1.astype(BF16)
    lo = (r1 - mid.astype(F32)).astype(BF16)
    dot = lambda p: jnp.dot(tril, p, preferred_element_type=F32)
    return dot(hi) + dot(mid) + dot(lo)


def _each(fn, *lists):
    return [fn(*args) for args in zip(*lists)]


def _pair_blockdiag(m, left):
    return jnp.concatenate([jnp.where(left, m, 0.0), jnp.where(left, 0.0, m)], axis=0)


def _unit_lower_inverse_offset(a_list, ii, jj, left):
    c = a_list[0].shape[0]

    def same_block(shift):
        return lax.shift_right_logical(ii, shift) == lax.shift_right_logical(jj, shift)

    base = same_block(1)
    n_list = _each(lambda a: jnp.where(base, -a, 0.0), a_list)
    shift = 1
    while (1 << shift) < c:
        outer, inner = same_block(shift + 1), same_block(shift)
        off_list = _each(lambda a: jnp.where(outer, jnp.where(inner, 0.0, a), 0.0), a_list)
        x_list = _each(lambda off, n: off + _bdot(off, _pair_blockdiag(n, left)), off_list, n_list)
        n_list = _each(lambda n, x: n - x - _bdot(n, _pair_blockdiag(x, left)), n_list, x_list)
        shift += 1
    return n_list


def _chunk_prep(q, k, v, gcol, grow, bcol):
    c = q[0].shape[0]
    assert len(q) % 2 == 0
    ii = _iota2((c, 2 * c), 0)
    lane = _iota2((c, 2 * c), 1)
    left = lane < c
    jj = lane & (c - 1)
    causal = ii >= jj
    strict = ii > jj
    first, second = slice(0, None, 2), slice(1, None, 2)
    kb = _each(lambda kk, b: kk * b, k, bcol)
    both = _each(lambda qa, ka, qb, kbb, x, y: _bdot_nt(jnp.concatenate([qa, ka, qb, kbb], axis=0),
                                                       jnp.concatenate([x, y], axis=0)),
                 q[first], kb[first], q[second], kb[second], k[first], k[second])
    decay = _each(lambda ga, gb, ra, rb: jnp.where(causal, jnp.exp(jnp.where(
        causal, jnp.where(left, ga, gb) - jnp.concatenate([ra, rb], axis=1), 0.0)), 0.0),
        gcol[first], gcol[second], grow[first], grow[second])
    qk = _each(lambda bo, d: jnp.where(left, bo[0:c], bo[2 * c:3 * c]) * d, both, decay)
    a = _each(lambda bo, d: jnp.where(strict, jnp.where(left, bo[c:2 * c], bo[3 * c:4 * c]) * d, 0.0), both, decay)
    n = _unit_lower_inverse_offset(a, ii, jj, left)
    eg = _each(jnp.exp, gcol)
    rhs = _each(lambda vv, b, kbb, e: jnp.concatenate([vv * b, kbb * e], axis=1), v, bcol, kb, eg)
    uw = _each(lambda ra, rb, nn: (lambda r: r + _bdot(_pair_blockdiag(nn, left), r))(
        jnp.concatenate([ra, rb], axis=0)), rhs[first], rhs[second], n)
    uw = [x[half] for x in uw for half in (slice(0, c), slice(c, 2 * c))]
    u = [x[:, :GDN_DV] for x in uw]
    w = [x[:, GDN_DV:] for x in uw]
    qd = _each(lambda qq, e: qq * e, q, eg)
    kd = _each(lambda kk, gc: kk * jnp.exp(gc[c - 1:c, :] - gc), k, gcol)
    return u, w, qd, kd, qk


def _chunk_step(s, u, w, qd, kd, qk, ge):
    c = u[0].shape[0]
    both = _each(lambda ww, qq, ss: _bdot(jnp.concatenate([ww, qq], axis=0), ss), w, qd, s)
    v_new = _each(lambda uu, bo: uu.astype(F32) - bo[:c], u, both)
    o = _each(lambda bo, m, vn: bo[c:] + _bdot(m, vn), both, qk, v_new)
    s_new = _each(lambda ss, g, kk, vn: ss * g + _bdot_tn(kk, vn), s, ge, kd, v_new)
    return o, s_new


def _conv_act(xp_ref, cw_ref, r0, rows, c0):
    cols = slice(c0, c0 + LANES)
    acc = cw_ref[3:4, cols] * xp_ref[r0:r0 + rows, cols]
    for j in range(GDN_CONV - 1):
        acc = acc + cw_ref[j:j + 1, cols] * xp_ref[r0 - 3 + j:r0 - 3 + j + rows, cols]
    return _silu(acc)


def _l2n(x):
    return x * lax.rsqrt(jnp.sum(x * x, axis=-1, keepdims=True) + NORM_EPS)


def _softplus(x):
    return jnp.maximum(x, 0.0) + jnp.log1p(jnp.exp(-jnp.abs(x)))


def _head_cols(hd):
    return (hd * GDN_DK, GDN_QK_WIDTH + hd * GDN_DK, 2 * GDN_QK_WIDTH + hd * GDN_DV)


def _activate_qkv(xp_ref, cw_ref, act_ref, r0, rows):
    for hd in range(GDN_HEADS):
        cq, ck, cv = _head_cols(hd)
        act_ref[0:rows, cq:cq + LANES] = _l2n(_conv_act(xp_ref, cw_ref, r0, rows, cq)) * (GDN_DK ** -0.5)
        act_ref[0:rows, ck:ck + LANES] = _l2n(_conv_act(xp_ref, cw_ref, r0, rows, ck))
        act_ref[0:rows, cv:cv + LANES] = _conv_act(xp_ref, cw_ref, r0, rows, cv)


def _decay_beta(ab, hp_ref):
    g = -jnp.exp(hp_ref[0:1, :]) * _softplus(ab + hp_ref[1:2, :])
    return g, _sigmoid(ab)


def _gdn_prep_kernel(act_ref, ab_ref, hp_ref, u_ref, w_ref, qd_ref, kd_ref, qk_ref, ge_ref):
    tb = act_ref.shape[1]
    cc = GDN_CHUNK
    g_all, beta_all = _decay_beta(ab_ref[0], hp_ref)

    chunks = [slice(ci * cc, (ci + 1) * cc) for ci in range(tb // cc)]
    gcs = _each(lambda rows: _cumsum_rows(g_all[rows, :]), chunks)
    gcts = _each(lambda gc: gc.T, gcs)
    for ci, gc in enumerate(gcs):
        ge_ref[0, ci] = jnp.exp(gc[cc - 1:cc, :])
    items = [(ci, hd) for ci in range(len(chunks)) for hd in range(GDN_HEADS)]
    col = lambda which: [act_ref[0, chunks[ci], _head_cols(hd)[which]:_head_cols(hd)[which] + LANES].astype(F32)
                         for ci, hd in items]
    u, w, qd, kd, qk = _chunk_prep(
        col(0), col(1), col(2),
        [gcs[ci][:, hd:hd + 1] for ci, hd in items], [gcts[ci][hd:hd + 1, :] for ci, hd in items],
        [beta_all[chunks[ci], GDN_HEADS + hd:GDN_HEADS + hd + 1] for ci, hd in items])
    for idx, (ci, hd) in enumerate(items):
        rows = chunks[ci]
        oc = slice(hd * GDN_DV, (hd + 1) * GDN_DV)
        u_ref[0, rows, oc] = u[idx].astype(BF16)
        w_ref[0, rows, oc] = w[idx].astype(BF16)
        qd_ref[0, rows, oc] = qd[idx].astype(BF16)
        kd_ref[0, rows, oc] = kd[idx].astype(BF16)
        if hd % 2 == 0:
            qk_ref[0, rows, hd * cc:(hd + 2) * cc] = qk[idx // 2].astype(BF16)


def _gdn_prep(qkv, ab, head_params):
    b, t, _ = qkv.shape
    tb = min(GDN_PREP_ROWS, t)
    nch = tb // GDN_CHUNK
    blk = lambda w: pl.BlockSpec((1, tb, w), lambda bi, i: (bi, i, 0))
    out_shapes = [jax.ShapeDtypeStruct((b, t, GDN_WIDTH), BF16)] * 4 + [
        jax.ShapeDtypeStruct((b, t, GDN_HEADS * GDN_CHUNK), BF16),
        jax.ShapeDtypeStruct((b, t // GDN_CHUNK, 1, LANES), F32)]
    return pl.pallas_call(
        _gdn_prep_kernel,
        grid=(b, t // tb),
        in_specs=[blk(GDN_CONV_DIM), blk(LANES), pl.BlockSpec((8, LANES), lambda bi, i: (0, 0))],
        out_specs=[blk(GDN_WIDTH)] * 4 + [
            blk(GDN_HEADS * GDN_CHUNK),
            pl.BlockSpec((1, nch, 1, LANES), lambda bi, i: (bi, i, 0, 0))],
        out_shape=out_shapes,
        compiler_params=_cparams("parallel", "parallel"),
        name="gdn_prep",
    )(qkv, ab, head_params)


def _gated_norm_store(o_ref, idx, o, gain, zs):
    o_ref[idx] = (_rms(o, gain) * zs.astype(F32)).astype(o_ref.dtype)


def _gdn_scan_kernel(u_ref, w_ref, qd_ref, kd_ref, qk_ref, ge_ref, zs_ref, gain_ref,
                     o_ref, s_out_ref, s_ref):
    nb, tb, _ = u_ref.shape
    cc = GDN_CHUNK
    step = pl.program_id(0)

    @pl.when(step == 0)
    def _():
        s_ref[...] = jnp.zeros_like(s_ref)

    gain = gain_ref[...]

    def chunk_body(ci, carry):
        rows = pl.ds(pl.multiple_of(ci * cc, cc), cc)
        items = [(bi, hd) for bi in range(nb) for hd in range(GDN_HEADS)]
        oc = lambda hd: slice(hd * GDN_DV, (hd + 1) * GDN_DV)
        ge_rows = [ge_ref[bi, ci] for bi in range(nb)]
        o, s_new = _chunk_step(
            [s_ref[bi * GDN_HEADS + hd] for bi, hd in items],
            [u_ref[bi, rows, oc(hd)] for bi, hd in items], [w_ref[bi, rows, oc(hd)] for bi, hd in items],
            [qd_ref[bi, rows, oc(hd)] for bi, hd in items], [kd_ref[bi, rows, oc(hd)] for bi, hd in items],
            [qk_ref[bi, rows, hd * cc:(hd + 1) * cc] for bi, hd in items],
            [ge_rows[bi][:, hd:hd + 1] for bi, hd in items])
        for idx, (bi, hd) in enumerate(items):
            s_ref[bi * GDN_HEADS + hd] = s_new[idx]
            _gated_norm_store(o_ref, (bi, rows, oc(hd)), o[idx], gain, zs_ref[bi, rows, oc(hd)])
        return carry

    lax.fori_loop(0, tb // cc, chunk_body, 0, unroll=2)

    @pl.when(step == pl.num_programs(0) - 1)
    def _():
        s_out_ref[...] = s_ref[...]


def _gdn_scan(u, w, qd, kd, qk, ge, zs, gain):
    b, t, _ = u.shape
    tb = min(GDN_SCAN_ROWS, t)
    nch = tb // GDN_CHUNK
    blk = lambda wd: pl.BlockSpec((b, tb, wd), lambda i: (0, i, 0))
    o, s = pl.pallas_call(
        _gdn_scan_kernel,
        grid=(t // tb,),
        in_specs=[blk(GDN_WIDTH)] * 4 + [
            blk(GDN_HEADS * GDN_CHUNK),
            pl.BlockSpec((b, nch, 1, LANES), lambda i: (0, i, 0, 0)),
            blk(GDN_WIDTH),
            pl.BlockSpec((1, GDN_DV), lambda i: (0, 0))],
        out_specs=[blk(GDN_WIDTH),
                   pl.BlockSpec((b * GDN_HEADS, GDN_DK, GDN_DV), lambda i: (0, 0, 0))],
        out_shape=[jax.ShapeDtypeStruct((b, t, GDN_WIDTH), BF16),
                   jax.ShapeDtypeStruct((b * GDN_HEADS, GDN_DK, GDN_DV), F32)],
        scratch_shapes=[pltpu.VMEM((b * GDN_HEADS, GDN_DK, GDN_DV), F32)],
        compiler_params=_cparams("arbitrary"),
        name="gdn_scan",
    )(u, w, qd, kd, qk, ge, zs, gain.reshape(1, GDN_DV))
    return o, s.reshape(b, GDN_HEADS, GDN_DK, GDN_DV)


def _gdn_sample_kernel(new_rows, xp_ref, ab_ref, zs_ref, s0_ref, cw_ref, hp_ref, gain_ref,
                       o_ref, s_out_ref, xs_ref, act_ref):
    grp = xp_ref.shape[0]
    cc = GDN_SAMPLE_CHUNK
    gain = gain_ref[...]
    rowmask = (_iota2((cc, 1), 0) >= cc - new_rows).astype(F32)
    seqs = list(range(grp))
    for bi in seqs:
        xs = xs_ref.at[bi]
        xs[0:8, :] = jnp.zeros((8, GDN_CONV_DIM), F32)
        xs[8:8 + cc, :] = xp_ref[bi].astype(F32)
        _activate_qkv(xs, cw_ref, act_ref.at[bi], 8, cc)
    gb = _each(lambda bi: _decay_beta(ab_ref[bi], hp_ref), seqs)
    gcs = _each(lambda x: _cumsum_rows(x[0] * rowmask), gb)
    gcts = _each(lambda gc: gc.T, gcs)
    ge_rows = _each(lambda gc: jnp.exp(gc[cc - 1:cc, :]), gcs)
    betas = _each(lambda x: x[1] * rowmask, gb)
    items = [(bi, hd) for bi in seqs for hd in range(GDN_HEADS)]
    col = lambda which: [act_ref[bi, :, _head_cols(hd)[which]:_head_cols(hd)[which] + LANES] * rowmask
                         for bi, hd in items]
    u, w, qd, kd, qk = _chunk_prep(
        col(0), col(1), col(2),
        [gcs[bi][:, hd:hd + 1] for bi, hd in items], [gcts[bi][hd:hd + 1, :] for bi, hd in items],
        [betas[bi][:, GDN_HEADS + hd:GDN_HEADS + hd + 1] for bi, hd in items])
    qk = [pair[:, half] for pair in qk for half in (slice(0, cc), slice(cc, 2 * cc))]
    o, s_new = _chunk_step([s0_ref[bi, hd] for bi, hd in items], u, w, qd, kd, qk,
                           [ge_rows[bi][:, hd:hd + 1] for bi, hd in items])
    for idx, (bi, hd) in enumerate(items):
        s_out_ref[bi, hd] = s_new[idx]
        oc = slice(hd * GDN_DV, (hd + 1) * GDN_DV)
        _gated_norm_store(o_ref, (bi, slice(None), oc), o[idx], gain, zs_ref[bi, :, oc])


def _gdn_sample(xp, ab, zs, s0, conv_w, head_params, gain, new_rows):
    b = xp.shape[0]
    cc = GDN_SAMPLE_CHUNK
    grp = GDN_SAMPLE_GROUP
    blk3 = lambda w: pl.BlockSpec((grp, cc, w), lambda i: (i, 0, 0))
    sblk = pl.BlockSpec((grp, GDN_HEADS, GDN_DK, GDN_DV), lambda i: (i, 0, 0, 0))
    return pl.pallas_call(
        functools.partial(_gdn_sample_kernel, new_rows),
        grid=(b // grp,),
        in_specs=[blk3(GDN_CONV_DIM), blk3(LANES), blk3(GDN_WIDTH), sblk,
                  pl.BlockSpec((GDN_CONV, GDN_CONV_DIM), lambda i: (0, 0)),
                  pl.BlockSpec((8, LANES), lambda i: (0, 0)),
                  pl.BlockSpec((1, GDN_DV), lambda i: (0, 0))],
        out_specs=[blk3(GDN_WIDTH), sblk],
        out_shape=[jax.ShapeDtypeStruct((b, cc, GDN_WIDTH), BF16),
                   jax.ShapeDtypeStruct((b, GDN_HEADS, GDN_DK, GDN_DV), F32)],
        scratch_shapes=[pltpu.VMEM((grp, cc + 8, GDN_CONV_DIM), F32),
                        pltpu.VMEM((grp, cc, GDN_CONV_DIM), F32)],
        compiler_params=_cparams("parallel"),
        name="gdn_sample",
    )(xp, ab, zs, s0, conv_w, head_params, gain.reshape(1, GDN_DV))


def _sink_attention(q, k, v, mask, sink_col):
    s = _each(lambda qq, kk: jnp.where(mask, _bdot_nt(qq, kk) * (SWA_HEAD_DIM ** -0.5), -jnp.inf), q, k)
    m = _each(lambda ss, sk: jnp.maximum(jnp.max(ss, axis=-1, keepdims=True), sk), s, sink_col)
    p = _each(lambda ss, mm: jnp.exp(ss - mm), s, m)
    denom = _each(lambda pp, sk, mm: jnp.sum(pp, axis=-1, keepdims=True) + jnp.exp(sk - mm), p, sink_col, m)
    return _each(lambda pp, vv, dd: _bdot(pp, vv) / dd, p, v, denom)


def _sink_column(sinks_ref, kv_head, rows_per_head):
    parts = [jnp.full((rows_per_head, 1), sinks_ref[kv_head * SWA_GROUP + g], F32) for g in range(SWA_GROUP)]
    return jnp.concatenate(parts, axis=0)


def _swa_prompt_kernel(sinks_ref, q_ref, kvp_ref, kvc_ref, o_ref):
    wnd = WINDOW
    nblk = q_ref.shape[1] // wnd
    step = pl.program_id(1)
    kv = jnp.concatenate([kvp_ref[0], kvc_ref[0]], axis=0)
    cols = SWA_GROUP * wnd
    kj = _iota2((2 * wnd, cols), 0)
    qi = _iota2((2 * wnd, cols), 1) & (wnd - 1)
    dist = qi + wnd - kj
    band = (dist >= 0) & (dist <= wnd)
    first_key = jnp.where(step > 0, 0, wnd)
    masks = [band & (kj >= first_key)] + [band] * (nblk - 1)
    head_cols = lambda h: slice(h * SWA_HEAD_DIM, (h + 1) * SWA_HEAD_DIM)
    q_heads = lambda hk: [hk * SWA_GROUP + g for g in range(SWA_GROUP)]
    scale = SWA_HEAD_DIM ** -0.5
    items = [(j, hk) for j in range(nblk) for hk in range(SWA_KV_HEADS)]
    q_rows = lambda j: slice(j * wnd, (j + 1) * wnd)
    k_rows = lambda j: slice(j * wnd, (j + 2) * wnd)
    q = [jnp.concatenate([q_ref[0, q_rows(j), head_cols(h)] for h in q_heads(hk)], axis=0) for j, hk in items]
    k = [kv[k_rows(j), head_cols(hk)] for j, hk in items]
    v = [kv[k_rows(j), SWA_KV_WIDTH + hk * SWA_HEAD_DIM:SWA_KV_WIDTH + (hk + 1) * SWA_HEAD_DIM] for j, hk in items]
    sinks_kv = [jnp.concatenate([jnp.full((1, wnd), sinks_ref[h], F32) for h in q_heads(hk)], axis=1)
                for hk in range(SWA_KV_HEADS)]
    sink = [sinks_kv[hk] for j, hk in items]
    mask = [masks[j] for j, hk in items]
    s = _each(lambda kk, qq, mk: jnp.where(mk, _bdot_nt(kk, qq) * scale, -jnp.inf), k, q, mask)
    m = _each(lambda ss, sk: jnp.maximum(jnp.max(ss, axis=0, keepdims=True), sk), s, sink)
    p = _each(lambda ss, mm: jnp.exp(ss - mm), s, m)
    denom = _each(lambda pp, sk, mm: jnp.sum(pp, axis=0, keepdims=True) + jnp.exp(sk - mm), p, sink, m)
    ot = _each(lambda vv, pp, dd: _bdot_tn(vv, pp) / dd, v, p, denom)
    for idx, (j, hk) in enumerate(items):
        for g in range(0, SWA_GROUP, 2):
            pair = jnp.concatenate([ot[idx][:, g * wnd:(g + 1) * wnd], ot[idx][:, (g + 1) * wnd:(g + 2) * wnd]],
                                   axis=0)
            h0 = hk * SWA_GROUP + g
            o_ref[0, q_rows(j), h0 * SWA_HEAD_DIM:(h0 + 2) * SWA_HEAD_DIM] = pair.T.astype(o_ref.dtype)


def _swa_prompt(q, kv, sinks):
    b, t, _ = q.shape
    nblk = SWA_BLOCKS_PER_STEP
    rows = nblk * WINDOW
    return pl.pallas_call(
        _swa_prompt_kernel,
        grid=(b, t // rows),
        in_specs=[pl.BlockSpec(memory_space=pltpu.SMEM),
                  pl.BlockSpec((1, rows, SWA_WIDTH), lambda bi, i: (bi, i, 0)),
                  pl.BlockSpec((1, WINDOW, 2 * SWA_KV_WIDTH), lambda bi, i: (bi, jnp.maximum(i * nblk - 1, 0), 0)),
                  pl.BlockSpec((1, rows, 2 * SWA_KV_WIDTH), lambda bi, i: (bi, i, 0))],
        out_specs=pl.BlockSpec((1, rows, SWA_WIDTH), lambda bi, i: (bi, i, 0)),
        out_shape=jax.ShapeDtypeStruct((b, t, SWA_WIDTH), BF16),
        compiler_params=_cparams("parallel", "parallel"),
        name="swa_prompt",
    )(sinks, q, kv, kv)


def _swa_sample_kernel(sinks_ref, q_ref, kvn_ref, kc_ref, vc_ref, o_ref, ko_ref, vo_ref):
    grp, t, _ = q_ref.shape
    wnd = WINDOW
    nk = wnd + BF16_ROWS
    rows = SWA_GROUP * t
    tq = _iota2((rows, nk), 0) & (t - 1)
    kj = _iota2((rows, nk), 1)
    dist = tq + wnd - kj
    mask = (dist >= 0) & (dist <= wnd)
    zpad = jnp.zeros((BF16_ROWS - t, SWA_KV_WIDTH), F32)
    kks, vvs = [], []
    for bi in range(grp):
        kvn = kvn_ref[bi].astype(F32)
        kk = jnp.concatenate([kc_ref[bi], kvn[:, :SWA_KV_WIDTH], zpad], axis=0)
        vv = jnp.concatenate([vc_ref[bi], kvn[:, SWA_KV_WIDTH:], zpad], axis=0)
        ko_ref[bi] = kk[t:t + wnd, :]
        vo_ref[bi] = vv[t:t + wnd, :]
        kks.append(kk)
        vvs.append(vv)
    items = [(bi, hk) for bi in range(grp) for hk in range(SWA_KV_HEADS)]
    head_cols = lambda h: slice(h * SWA_HEAD_DIM, (h + 1) * SWA_HEAD_DIM)
    q_heads = lambda hk: [hk * SWA_GROUP + g for g in range(SWA_GROUP)]
    sink_cols = [_sink_column(sinks_ref, hk, t) for hk in range(SWA_KV_HEADS)]
    o = _sink_attention(
        [jnp.concatenate([q_ref[bi, :, head_cols(h)] for h in q_heads(hk)], axis=0) for bi, hk in items],
        [kks[bi][:, head_cols(hk)] for bi, hk in items], [vvs[bi][:, head_cols(hk)] for bi, hk in items],
        mask, [sink_cols[hk] for bi, hk in items])
    for idx, (bi, hk) in enumerate(items):
        for g, h in enumerate(q_heads(hk)):
            o_ref[bi, :, head_cols(h)] = o[idx][g * t:(g + 1) * t].astype(o_ref.dtype)


def _swa_sample(q, kv_new, k_cache, v_cache, sinks):
    b, t, _ = q.shape
    grp = SWA_SAMPLE_GROUP
    blk = lambda r, w: pl.BlockSpec((grp, r, w), lambda i: (i, 0, 0))
    return pl.pallas_call(
        _swa_sample_kernel,
        grid=(b // grp,),
        in_specs=[pl.BlockSpec(memory_space=pltpu.SMEM),
                  blk(t, SWA_WIDTH), blk(t, 2 * SWA_KV_WIDTH), blk(WINDOW, SWA_KV_WIDTH), blk(WINDOW, SWA_KV_WIDTH)],
        out_specs=[blk(t, SWA_WIDTH), blk(WINDOW, SWA_KV_WIDTH), blk(WINDOW, SWA_KV_WIDTH)],
        out_shape=[jax.ShapeDtypeStruct((b, t, SWA_WIDTH), BF16),
                   jax.ShapeDtypeStruct((b, WINDOW, SWA_KV_WIDTH), F32),
                   jax.ShapeDtypeStruct((b, WINDOW, SWA_KV_WIDTH), F32)],
        compiler_params=_cparams("parallel"),
        name="swa_sample",
    )(sinks, q, kv_new, k_cache, v_cache)


def _route(logits):
    lane = _iota2(logits.shape, 1).astype(F32)
    neg = -jnp.inf

    def first_argmax(vals, valid):
        v = jnp.where(valid, vals, neg)
        m = jnp.max(v, axis=-1, keepdims=True)
        idx = jnp.min(jnp.where(jnp.logical_and(valid, v == m), lane, float(LANES)), axis=-1, keepdims=True)
        return m, idx

    is_group = lane < N_GROUPS
    gmax, gidx = first_argmax(logits, is_group)
    p_group = 1.0 / jnp.sum(jnp.where(is_group, jnp.exp(logits - gmax), 0.0), axis=-1, keepdims=True)
    lo = N_GROUPS + gidx * EXPERTS_PER_GROUP
    in_group = jnp.logical_and(lane >= lo, lane < lo + EXPERTS_PER_GROUP)
    m1, i1 = first_argmax(logits, in_group)
    esum = jnp.sum(jnp.where(in_group, jnp.exp(logits - m1), 0.0), axis=-1, keepdims=True)
    m2, i2 = first_argmax(logits, jnp.logical_and(in_group, lane != i1))
    p1 = 1.0 / esum
    p2 = jnp.exp(m2 - m1) / esum
    tot = p1 + p2
    return i1 - N_GROUPS, i2 - N_GROUPS, p_group * p1 / tot, p_group * p2 / tot


def _post_mixer_kernel(oa_ref, ob_ref, ga_ref, gb_ref, x_ref, gt_ref, sc_ref, sh_ref,
                       wa_ref, wb_ref, wo_ref, gpost_ref, gpre_ref, wr_ref, br_ref, cnt0_ref,
                       x1_ref, h2_ref, rt_ref, cnt_out_ref, meta_ref, *rest):
    cnt_ref = rest[-1]
    if len(rest) == 2:
        rest[0][...] = jnp.zeros_like(rest[0])
    step = pl.program_id(0)

    @pl.when(step == 0)
    def _():
        cnt_ref[...] = cnt0_ref[...]

    merged = (ga_ref[...].astype(F32) * jnp.dot(oa_ref[...], wa_ref[...], preferred_element_type=F32)
              + gb_ref[...].astype(F32) * jnp.dot(ob_ref[...], wb_ref[...], preferred_element_type=F32))
    mix = _bdot(merged, wo_ref[...])
    x1 = x_ref[...] + gt_ref[0] * _rms(mix, gpost_ref[...])
    x1_ref[...] = x1
    h2 = _rms(x1, gpre_ref[...]) * (1.0 + sc_ref[0]) + sh_ref[0]
    _rows_to_tiles(h2_ref, h2)
    h_hi = h2.astype(BF16)
    h_lo = (h2 - h_hi.astype(F32)).astype(BF16)
    part = jnp.dot(h_hi, wr_ref[...], preferred_element_type=F32)
    logits = (part[:, :LANES] + part[:, LANES:]
              + jnp.dot(h_lo, wr_ref[:, :LANES], preferred_element_type=F32) + br_ref[...])
    ia, ib, wa, wb = _route(logits)
    lane = _iota2(logits.shape, 1)
    tm = logits.shape[0]
    lane_f = lane.astype(F32)
    hot_a = (lane_f == ia).astype(F32)
    hot_b = (lane_f == ib).astype(F32)
    hot = hot_a + hot_b
    earlier = (_iota2((tm, tm), 0) > _iota2((tm, tm), 1)).astype(BF16)
    before = jnp.dot(earlier, hot.astype(BF16), preferred_element_type=F32) + cnt_ref[...]
    rank_a = jnp.sum(hot_a * before, axis=-1, keepdims=True)
    rank_b = jnp.sum(hot_b * before, axis=-1, keepdims=True)
    cnt_ref[...] = cnt_ref[...] + jnp.sum(hot, axis=0, keepdims=True)
    cnt_out_ref[...] = cnt_ref[...]
    rt_ref[...] = jnp.where(lane == 0, ia, jnp.where(lane == 1, ib, jnp.where(lane == 2, wa, jnp.where(
        lane == 3, wb, 0.0))))
    packed = jnp.where(lane == 0, rank_a * N_EXPERTS + ia, jnp.where(lane == 1, rank_b * N_EXPERTS + ib, 0.0))
    meta_ref[0] = packed.T[0:TOP_K, :].astype(I32)


def _post_mixer(oa, ob, ga, gb, x2d, gt, sc, sh, w_a, w_b, w_o, g_post, g_pre, w_rt, b_rt, cnt0, tm,
                zero_rows=0):
    rows = x2d.shape[0]
    n_tiles = rows // tm
    row_blk = lambda w: pl.BlockSpec((tm, w), lambda i: (i, 0))
    full = lambda r, c: pl.BlockSpec((r, c), lambda i: (0, 0))
    out_specs = [row_blk(D_MODEL), pl.BlockSpec((tm * TILE_ROWS, LANES), lambda i: (i, 0)), row_blk(LANES),
                 full(1, LANES), pl.BlockSpec((1, TOP_K, tm), lambda i: (i, 0, 0))]
    out_shape = [jax.ShapeDtypeStruct((rows, D_MODEL), F32),
                 jax.ShapeDtypeStruct((rows * TILE_ROWS, LANES), F32),
                 jax.ShapeDtypeStruct((rows, LANES), F32),
                 jax.ShapeDtypeStruct((1, LANES), F32),
                 jax.ShapeDtypeStruct((n_tiles, TOP_K, tm), I32)]
    if zero_rows:
        assert zero_rows % (n_tiles * TILE_ROWS) == 0
        out_specs.append(pl.BlockSpec((zero_rows // n_tiles, LANES), lambda i: (i, 0)))
        out_shape.append(jax.ShapeDtypeStruct((zero_rows, LANES), F32))
    return pl.pallas_call(
        _post_mixer_kernel,
        grid=(n_tiles,),
        in_specs=[row_blk(GDN_WIDTH), row_blk(SWA_WIDTH), row_blk(D_MODEL), row_blk(D_MODEL), row_blk(D_MODEL),
                  _mod_spec(gt, n_tiles), _mod_spec(sc, n_tiles), _mod_spec(sh, n_tiles),
                  full(GDN_WIDTH, D_MODEL), full(SWA_WIDTH, D_MODEL), full(D_MODEL, D_MODEL),
                  full(1, D_MODEL), full(1, D_MODEL), full(D_MODEL, 2 * LANES), full(1, LANES), full(1, LANES)],
        out_specs=out_specs,
        out_shape=out_shape,
        scratch_shapes=[pltpu.VMEM((1, LANES), F32)],
        compiler_params=_cparams("arbitrary"),
        name="post_mixer",
    )(oa, ob, ga, gb, x2d, gt, sc, sh, w_a, w_b, w_o,
      g_post.reshape(1, D_MODEL), g_pre.reshape(1, D_MODEL), w_rt, b_rt, cnt0)


TILE_ROWS = D_MODEL // LANES


def _tiles_to_rows(ref, first, rows):
    base = first * TILE_ROWS
    return jnp.concatenate([ref[pl.ds(base + c, rows, stride=TILE_ROWS), :] for c in range(TILE_ROWS)], axis=1)


def _rows_to_tiles(ref, mat):
    rows = mat.shape[0]
    for c in range(TILE_ROWS):
        ref[pl.ds(c, rows, stride=TILE_ROWS), :] = mat[:, c * LANES:(c + 1) * LANES]


def _tile_copy_loop(n, copies, start):
    def body(t, carry):
        for j, cp in enumerate(copies(t)):
            if start:
                cp.start(priority=j % 2)
            else:
                cp.wait()
        return carry

    lax.fori_loop(0, n, body, 0, unroll=8)


def _slots_kernel(pstart_ref, meta_ref, o_ref):
    packed = meta_ref[...]
    expert = packed & (N_EXPERTS - 1)
    first = jnp.zeros_like(packed)
    for e in range(N_EXPERTS):
        first = jnp.where(expert == e, pstart_ref[e], first)
    o_ref[...] = first + lax.shift_right_logical(packed, N_EXPERTS.bit_length() - 1)


def _slots(meta, pstarts):
    return pl.pallas_call(
        _slots_kernel,
        grid_spec=pltpu.PrefetchScalarGridSpec(
            num_scalar_prefetch=1,
            grid=(1,),
            in_specs=[pl.BlockSpec(meta.shape, lambda i, p: (0, 0, 0))],
            out_specs=pl.BlockSpec(meta.shape, lambda i, p: (0, 0, 0))),
        out_shape=jax.ShapeDtypeStruct(meta.shape, I32),
        compiler_params=_cparams("arbitrary"),
        name="moe_slots",
    )(pstarts, meta)


def _dispatch_kernel(slot_ref, h_ref, xs_in_hbm, xs_hbm, sem):
    del xs_in_hbm
    i = pl.program_id(0)
    tm = h_ref.shape[0] // TILE_ROWS
    sub_tiles = tm // ROW_TILE

    for start in (True, False):
        for j in range(sub_tiles):
            base = (i * sub_tiles + j) * TOP_K * ROW_TILE

            def copies(t, j=j, base=base):
                row = pl.multiple_of((j * ROW_TILE + t) * TILE_ROWS, TILE_ROWS)
                return [pltpu.make_async_copy(h_ref.at[pl.ds(row, TILE_ROWS)],
                                              xs_hbm.at[slot_ref[base + k * ROW_TILE + t]], sem)
                        for k in range(TOP_K)]

            _tile_copy_loop(ROW_TILE, copies, start)


def _dispatch(slots, h_tiles, xs, tm):
    n_tiles = h_tiles.shape[0] // (tm * TILE_ROWS)
    return pl.pallas_call(
        _dispatch_kernel,
        grid_spec=pltpu.PrefetchScalarGridSpec(
            num_scalar_prefetch=1,
            grid=(n_tiles,),
            in_specs=[pl.BlockSpec((tm * TILE_ROWS, LANES), lambda i, s: (i, 0)),
                      pl.BlockSpec(memory_space=pl.ANY)],
            out_specs=pl.BlockSpec(memory_space=pl.ANY),
            scratch_shapes=[pltpu.SemaphoreType.DMA(())]),
        out_shape=jax.ShapeDtypeStruct(xs.shape, xs.dtype),
        input_output_aliases={2: 0},
        compiler_params=_cparams("arbitrary"),
        name="moe_dispatch",
    )(slots, h_tiles, xs)


def _moe_kernel(blk_e_ref, n_used_ref, x_ref, wg_ref, wu_ref, wd_ref, y_ref, wgb, wub, wdb):
    b = pl.program_id(0)
    rows = x_ref.shape[0] // TILE_ROWS
    changed = jnp.logical_or(b == 0, blk_e_ref[b] != blk_e_ref[jnp.maximum(b - 1, 0)])

    @pl.when(changed)
    def _():
        wgb[...] = wg_ref[0].astype(BF16)
        wub[...] = wu_ref[0].astype(BF16)
        wdb[...] = wd_ref[0].astype(BF16)

    @pl.when(b < n_used_ref[0])
    def _():
        x = _tiles_to_rows(x_ref, 0, rows).astype(BF16)
        gate = jnp.dot(x, wgb[...], preferred_element_type=F32)
        up = jnp.dot(x, wub[...], preferred_element_type=F32)
        _rows_to_tiles(y_ref, _bdot(_silu(gate) * up, wdb[...]))

    @pl.when(b >= n_used_ref[0])
    def _():
        y_ref[...] = jnp.zeros_like(y_ref)


def _moe(xs_tiles, blk_e, n_used, w_gate, w_up, w_down):
    n_blocks = blk_e.shape[0]
    rows = MOE_ROWS
    wspec = lambda r, c: pl.BlockSpec((1, r, c), lambda b, be, nu: (be[b], 0, 0))
    xspec = pl.BlockSpec((rows * TILE_ROWS, LANES), lambda b, be, nu: (b, 0))
    return pl.pallas_call(
        _moe_kernel,
        grid_spec=pltpu.PrefetchScalarGridSpec(
            num_scalar_prefetch=2,
            grid=(n_blocks,),
            in_specs=[xspec, wspec(D_MODEL, EXPERT_FF), wspec(D_MODEL, EXPERT_FF), wspec(EXPERT_FF, D_MODEL)],
            out_specs=xspec,
            scratch_shapes=[pltpu.VMEM((D_MODEL, EXPERT_FF), BF16),
                            pltpu.VMEM((D_MODEL, EXPERT_FF), BF16),
                            pltpu.VMEM((EXPERT_FF, D_MODEL), BF16)]),
        out_shape=jax.ShapeDtypeStruct(xs_tiles.shape, F32),
        compiler_params=_cparams("arbitrary"),
        name="moe_experts",
    )(blk_e, n_used, xs_tiles, w_gate, w_up, w_down)


def _combine_kernel(slot_ref, y_hbm, x1_ref, rt_ref, gt_ref, gpost_ref, o_ref, ybuf, sems):
    i = pl.program_id(0)
    n = pl.num_programs(0)
    rows = ybuf.shape[1] // TILE_ROWS
    slot = i % 2

    def gather(step, buf_slot, start):
        def copy(r):
            dst = ybuf.at[buf_slot, pl.ds(pl.multiple_of(r * TILE_ROWS, TILE_ROWS), TILE_ROWS)]
            return pltpu.make_async_copy(y_hbm.at[slot_ref[step * rows + r]], dst, sems.at[buf_slot])

        _tile_copy_loop(rows // 2, lambda t: [copy(2 * t), copy(2 * t + 1)], start)

    @pl.when(i == 0)
    def _():
        gather(0, 0, True)

    @pl.when(i + 1 < n)
    def _():
        gather(i + 1, 1 - slot, True)

    gather(i, slot, False)
    half = rows // 2
    rt = rt_ref[...]
    buf = ybuf.at[slot]
    f = rt[:, 2:3] * _tiles_to_rows(buf, 0, half) + rt[:, 3:4] * _tiles_to_rows(buf, half, half)
    o_ref[...] = x1_ref[...] + gt_ref[0] * _rms(f, gpost_ref[...])


def _combine(slots, yb, x1, rt, gt, g_post):
    rows = x1.shape[0]
    tm = ROW_TILE
    n_tiles = rows // tm
    tiles_per_mod = n_tiles // gt.shape[0]
    return pl.pallas_call(
        _combine_kernel,
        grid_spec=pltpu.PrefetchScalarGridSpec(
            num_scalar_prefetch=1,
            grid=(n_tiles,),
            in_specs=[pl.BlockSpec(memory_space=pl.ANY),
                      pl.BlockSpec((tm, D_MODEL), lambda i, s: (i, 0)),
                      pl.BlockSpec((tm, LANES), lambda i, s: (i, 0)),
                      pl.BlockSpec((1, gt.shape[1], D_MODEL), lambda i, s: (i // tiles_per_mod, 0, 0)),
                      pl.BlockSpec((1, D_MODEL), lambda i, s: (0, 0))],
            out_specs=pl.BlockSpec((tm, D_MODEL), lambda i, s: (i, 0)),
            scratch_shapes=[pltpu.VMEM((2, TOP_K * tm * TILE_ROWS, LANES), F32),
                            pltpu.SemaphoreType.DMA((2,))]),
        out_shape=jax.ShapeDtypeStruct((rows, D_MODEL), F32),
        compiler_params=_cparams("arbitrary"),
        name="moe_combine",
    )(slots, yb, x1, rt, gt, g_post.reshape(1, D_MODEL))


def _dispatch_plan(counts, n_tok):
    pcounts = (counts + MOE_ROWS - 1) // MOE_ROWS * MOE_ROWS
    pends = jnp.cumsum(pcounts)
    pstarts = (pends - pcounts).astype(I32)
    n_blocks = n_tok * TOP_K // MOE_ROWS + N_EXPERTS
    blk_start = jnp.arange(n_blocks, dtype=I32) * MOE_ROWS
    blk_e = jnp.minimum(jnp.sum(blk_start[:, None] >= pends[None, :], axis=1), N_EXPERTS - 1).astype(I32)
    n_used = (pends[-1] // MOE_ROWS).astype(I32).reshape(1)
    return blk_e, n_used, pstarts


_AB_FIRST = 4 * GDN_QK_WIDTH
_AB_COUNT = 2 * GDN_HEADS


def _prep_in_weight_kernel(w_ref, o_ref):
    x = w_ref[...]
    last = pl.program_id(0) == pl.num_programs(0) - 1
    keep = jnp.logical_or(jnp.logical_not(last), _iota2(x.shape, 0) < _AB_COUNT)
    o_ref[...] = jnp.where(keep, x, 0.0).astype(BF16)


def _prep_in_weight(w_in_t):
    n_blocks = IN_COLS // LANES
    shifted_from = _AB_FIRST // LANES

    def src_row(i):
        return jnp.where(i == n_blocks - 1, _AB_FIRST, jnp.where(i >= shifted_from, i * LANES + _AB_COUNT, i * LANES))

    return pl.pallas_call(
        _prep_in_weight_kernel,
        grid=(n_blocks,),
        in_specs=[pl.BlockSpec((pl.Element(LANES), pl.Element(D_MODEL)),
                               lambda i: (pl.multiple_of(src_row(i), _AB_COUNT), 0))],
        out_specs=pl.BlockSpec((LANES, D_MODEL), lambda i: (i, 0)),
        out_shape=jax.ShapeDtypeStruct((IN_COLS, D_MODEL), BF16),
        compiler_params=_cparams("parallel"),
        name="prep_in_weight",
    )(w_in_t)


def _head_param_tile(a_log, dt_bias):
    tile = jnp.zeros((8, LANES), F32)
    return tile.at[0, :GDN_HEADS].set(a_log.astype(F32)).at[1, :GDN_HEADS].set(dt_bias.astype(F32))


def _router_weight(w_group, b_group, w_router, b_router):
    w = jnp.zeros((D_MODEL, LANES), F32)
    w = w.at[:, :N_GROUPS].set(w_group).at[:, N_GROUPS:N_GROUPS + N_EXPERTS].set(w_router)
    b = jnp.zeros((1, LANES), F32)
    b = b.at[0, :N_GROUPS].set(b_group).at[0, N_GROUPS:N_GROUPS + N_EXPERTS].set(b_router)
    w_hi = w.astype(BF16)
    w_lo = (w - w_hi.astype(F32)).astype(BF16)
    return jnp.concatenate([w_hi, w_lo], axis=1), b


def kernel(x_prompt, x_sample, state_gdn, state_conv, cache_k_win, cache_v_win, c_prompt, c_sample, w_ada, b_ada, g_mix_pre, g_mix_post, g_ffn_pre, g_ffn_post, w_in, conv_w, a_log, dt_bias, gdn_norm, sinks, w_br_gdn, w_br_swa, w_out, w_group, b_group, w_router, b_router, w_gate, w_up, w_down):
    depth = w_ada.shape[0]
    assert depth == 1, "single-layer trunk"
    bp, tp, _ = x_prompt.shape
    bs, ts, _ = x_sample.shape
    n_p = bp * tp
    n_s = bs * ts
    tm = ROW_TILE
    assert tp % tm == 0 and n_s % tm == 0 and ts >= GDN_CONV - 1 and ts + GDN_CONV - 1 <= GDN_SAMPLE_CHUNK
    assert ts & (ts - 1) == 0 and ts <= BF16_ROWS

    c_all = jnp.concatenate([c_prompt, c_sample], axis=0)
    c_rows = -(-c_all.shape[0] // 8) * 8
    c_all = jnp.pad(c_all, ((0, c_rows - c_all.shape[0]), (0, 0)))
    mod = _adaln(c_all, w_ada[0], b_ada[0])
    mods_p = [m[:bp].reshape(bp, 1, D_MODEL) for m in jnp.split(mod, 6, axis=-1)]
    mods_s = [jnp.repeat(m[bp:bp + bs], ts, axis=0).reshape(n_s // tm, tm, D_MODEL)
              for m in jnp.split(mod, 6, axis=-1)]

    w_prep = _prep_in_weight(jnp.swapaxes(w_in[0], 0, 1))
    head_params = _head_param_tile(a_log[0], dt_bias[0])
    w_a, w_b, w_o = w_br_gdn[0].astype(BF16), w_br_swa[0].astype(BF16), w_out[0].astype(BF16)
    w_rt, b_rt = _router_weight(w_group[0], b_group[0], w_router[0], b_router[0])
    sinks0 = sinks[0].astype(F32)

    xp2d = x_prompt.reshape(n_p, D_MODEL)
    sh1, sc1, gt1, sh2, sc2, gt2 = mods_p
    qkv_p, zs_p, qb_p, kvb_p, ga_p, gb_p, ab_p, conv_tail_p = _inproj(
        xp2d, g_mix_pre[0], sc1, sh1, w_prep, INPROJ_ROWS, conv_w=conv_w[0].astype(F32), n_seq=bp)
    qkv_p3 = qkv_p.reshape(bp, tp, GDN_CONV_DIM)
    u, w, qd, kd, qk, ge = _gdn_prep(qkv_p3, ab_p.reshape(bp, tp, LANES), head_params)
    oa_p, s_prompt = _gdn_scan(u, w, qd, kd, qk, ge, zs_p.reshape(bp, tp, GDN_WIDTH), gdn_norm[0])
    kvb_p3 = kvb_p.reshape(bp, tp, 2 * SWA_KV_WIDTH)
    ob_p = _swa_prompt(qb_p.reshape(bp, tp, SWA_WIDTH), kvb_p3, sinks0)
    n_slots = ((n_p + n_s) * TOP_K // MOE_ROWS + N_EXPERTS) * MOE_ROWS
    x1_p, h2_p, rt_p, cnt_p, meta_p, xs = _post_mixer(
        oa_p.reshape(n_p, GDN_WIDTH), ob_p.reshape(n_p, SWA_WIDTH), ga_p, gb_p, xp2d, gt1, sc2, sh2,
        w_a, w_b, w_o, g_mix_post[0], g_ffn_pre[0], w_rt, b_rt, jnp.zeros((1, LANES), F32), tm,
        zero_rows=n_slots * TILE_ROWS)

    xs2d = x_sample.reshape(n_s, D_MODEL)
    sh1s, sc1s, gt1s, sh2s, sc2s, gt2s = mods_s
    qkv_s, zs_s, qb_s, kvb_s, ga_s, gb_s, ab_s = _inproj(xs2d, g_mix_pre[0], sc1s, sh1s, w_prep, tm)
    cc = GDN_SAMPLE_CHUNK
    pad_rows = cc - ts - (GDN_CONV - 1)
    qkv_s3 = qkv_s.reshape(bs, ts, GDN_CONV_DIM)
    xp_s = jnp.concatenate([jnp.zeros((bs, pad_rows, GDN_CONV_DIM), BF16), state_conv[0].astype(BF16), qkv_s3],
                           axis=1)
    front = lambda a: jnp.pad(a, ((0, 0), (cc - ts, 0), (0, 0)))
    oa_s16, s_sample = _gdn_sample(xp_s, front(ab_s.reshape(bs, ts, LANES)), front(zs_s.reshape(bs, ts, GDN_WIDTH)),
                                   state_gdn[0].astype(F32), conv_w[0], head_params, gdn_norm[0], ts)
    oa_s = oa_s16[:, cc - ts:, :].reshape(n_s, GDN_WIDTH)
    ob_s, k_new_s, v_new_s = _swa_sample(
        qb_s.reshape(bs, ts, SWA_WIDTH), kvb_s.reshape(bs, ts, 2 * SWA_KV_WIDTH),
        cache_k_win[0].reshape(bs, WINDOW, SWA_KV_WIDTH).astype(F32),
        cache_v_win[0].reshape(bs, WINDOW, SWA_KV_WIDTH).astype(F32), sinks0)
    x1_s, h2_s, rt_s, cnt_all, meta_s = _post_mixer(
        oa_s, ob_s.reshape(n_s, SWA_WIDTH), ga_s, gb_s, xs2d, gt1s, sc2s, sh2s,
        w_a, w_b, w_o, g_mix_post[0], g_ffn_pre[0], w_rt, b_rt, cnt_p, tm)

    blk_e, n_used, pstarts = _dispatch_plan(cnt_all[0, :N_EXPERTS].astype(I32), n_p + n_s)
    slots_p = _slots(meta_p, pstarts).reshape(-1)
    slots_s = _slots(meta_s, pstarts).reshape(-1)
    xs = _dispatch(slots_p, h2_p, xs.reshape(n_slots, TILE_ROWS, LANES), min(DISPATCH_ROWS, n_p))
    xs = _dispatch(slots_s, h2_s, xs, min(DISPATCH_ROWS, n_s))
    yb = _moe(xs.reshape(n_slots * TILE_ROWS, LANES), blk_e, n_used, w_gate[0], w_up[0], w_down[0])
    yb = yb.reshape(n_slots, TILE_ROWS, LANES)
    y_p = _combine(slots_p, yb, x1_p, rt_p, gt2, g_ffn_post[0])
    y_s = _combine(slots_s, yb, x1_s, rt_s, gt2s, g_ffn_post[0])

    f32 = lambda a: a.astype(F32)
    kv_tail = kvb_p3[:, tp - WINDOW:, :]
    kv_heads = lambda a: f32(a).reshape(a.shape[0], WINDOW, SWA_KV_HEADS, SWA_HEAD_DIM)[None]
    return (y_p.reshape(bp, tp, D_MODEL), y_s.reshape(bs, ts, D_MODEL),
            s_prompt[None], conv_tail_p[:, 8 - (GDN_CONV - 1):, :][None],
            kv_heads(kv_tail[:, :, :SWA_KV_WIDTH]), kv_heads(kv_tail[:, :, SWA_KV_WIDTH:]),
            s_sample[None], f32(qkv_s3[:, ts - (GDN_CONV - 1):, :])[None],
            kv_heads(k_new_s), kv_heads(v_new_s))
```

```python
import functools

import jax
import jax.numpy as jnp
from jax import lax
from jax.experimental import pallas as pl
from jax.experimental.pallas import tpu as pltpu

F32 = jnp.float32
BF16 = jnp.bfloat16
I32 = jnp.int32

D_MODEL = 1024
NORM_EPS = 1e-6
GDN_HEADS = 8
GDN_DK = 128
GDN_DV = 128
GDN_CONV = 4
GDN_CHUNK = 64
GDN_QK_WIDTH = GDN_HEADS * GDN_DK
GDN_WIDTH = GDN_HEADS * GDN_DV
GDN_CONV_DIM = 2 * GDN_QK_WIDTH + GDN_WIDTH
SWA_Q_HEADS = 16
SWA_KV_HEADS = 4
SWA_HEAD_DIM = 64
SWA_GROUP = SWA_Q_HEADS // SWA_KV_HEADS
SWA_WIDTH = SWA_Q_HEADS * SWA_HEAD_DIM
SWA_KV_WIDTH = SWA_KV_HEADS * SWA_HEAD_DIM
WINDOW = 128
N_GROUPS = 4
EXPERTS_PER_GROUP = 8
N_EXPERTS = N_GROUPS * EXPERTS_PER_GROUP
TOP_K = 2
EXPERT_FF = 512

LANES = 128
BF16_ROWS = 16
VMEM_LIMIT = 56 * 1024 * 1024

_C_QKV = 0
_C_Z = _C_QKV + GDN_CONV_DIM
_C_QB = _C_Z + GDN_WIDTH
_C_KVB = _C_QB + SWA_WIDTH
_C_GA = _C_KVB + 2 * SWA_KV_WIDTH
_C_GB = _C_GA + D_MODEL
_C_AB = _C_GB + D_MODEL
IN_COLS = _C_AB + 512
PROJ_TILE = 512

ROW_TILE = 512
INPROJ_ROWS = 512
GDN_PREP_ROWS = 256
GDN_SCAN_ROWS = 512
GDN_SAMPLE_CHUNK = 16
GDN_SAMPLE_GROUP = 8
SWA_SAMPLE_GROUP = 8
SWA_BLOCKS_PER_STEP = 4
MOE_ROWS = 512
DISPATCH_ROWS = 1024


def _cparams(*sem):
    return pltpu.CompilerParams(dimension_semantics=sem, vmem_limit_bytes=VMEM_LIMIT)


def _bdot(a, b):
    return jnp.dot(a.astype(BF16), b.astype(BF16), preferred_element_type=F32)


def _bdot_nt(a, b):
    return lax.dot_general(a.astype(BF16), b.astype(BF16), (((1,), (1,)), ((), ())),
                           preferred_element_type=F32)


def _bdot_tn(a, b):
    return lax.dot_general(a.astype(BF16), b.astype(BF16), (((0,), (0,)), ((), ())),
                           preferred_element_type=F32)


def _sigmoid(x):
    return 1.0 / (1.0 + jnp.exp(-x))


def _silu(x):
    return x * _sigmoid(x)


def _rms(x, gain):
    return x * lax.rsqrt(jnp.mean(x * x, axis=-1, keepdims=True) + NORM_EPS) * gain


def _iota2(shape, dim):
    return lax.broadcasted_iota(I32, shape, dim)


def _adaln_kernel(c_ref, w_ref, b_ref, o_ref):
    o_ref[...] = _bdot(_silu(c_ref[...]), w_ref[...]) + b_ref[...]


def _adaln(c_all, w_ada, b_ada):
    rows = c_all.shape[0]
    n_out = w_ada.shape[1]
    tn = D_MODEL
    return pl.pallas_call(
        _adaln_kernel,
        grid=(n_out // tn,),
        in_specs=[pl.BlockSpec((rows, D_MODEL), lambda j: (0, 0)),
                  pl.BlockSpec((D_MODEL, tn), lambda j: (0, j)),
                  pl.BlockSpec((1, tn), lambda j: (0, j))],
        out_specs=pl.BlockSpec((rows, tn), lambda j: (0, j)),
        out_shape=jax.ShapeDtypeStruct((rows, n_out), F32),
        compiler_params=_cparams("arbitrary"),
        name="adaln",
    )(c_all, w_ada, b_ada.reshape(1, n_out))


def _inproj_kernel(tiles_per_seq, x_ref, g_ref, sc_ref, sh_ref, w_ref, *rest):
    if tiles_per_seq:
        cw_ref, qkv_ref, z_ref, qb_ref, kvb_ref, ga_ref, gb_ref, ab_ref, tail_ref, carry_ref = rest
    else:
        qkv_ref, z_ref, qb_ref, kvb_ref, ga_ref, gb_ref, ab_ref = rest
    tm = x_ref.shape[0]
    h = (_rms(x_ref[...], g_ref[...]) * (1.0 + sc_ref[0]) + sh_ref[0]).astype(BF16)

    tasks = []

    def fill(ref, c0, width, fn):
        step = min(PROJ_TILE, width)
        tasks.append([(ref, c0, c, step, fn) for c in range(0, width, step)])

    def run(ref, c0, c, step, fn):
        acc = jnp.dot(h, w_ref[:, c0 + c:c0 + c + step], preferred_element_type=F32)
        ref[:, c:c + step] = fn(acc, c, step).astype(ref.dtype)

    def conv_act(acc, c, step):
        cols = slice(c, c + step)
        seq_start = pl.program_id(0) % tiles_per_seq == 0
        prev = jnp.where(seq_start, 0.0, carry_ref[:, cols])
        last = acc[tm - 8:tm]
        carry_ref[:, cols] = last
        tail_ref[0, :, cols] = last
        ext = jnp.concatenate([prev, acc], axis=0)
        y = cw_ref[GDN_CONV - 1:GDN_CONV, cols] * acc
        for j in range(GDN_CONV - 1):
            y = y + cw_ref[j:j + 1, cols] * ext[8 - (GDN_CONV - 1) + j:8 - (GDN_CONV - 1) + j + tm]
        y = _silu(y)
        if c >= 2 * GDN_QK_WIDTH:
            return y
        scale = GDN_DK ** -0.5 if c < GDN_QK_WIDTH else 1.0
        heads = [_l2n(y[:, d:d + GDN_DK]) * scale for d in range(0, step, GDN_DK)]
        return jnp.concatenate(heads, axis=1)

    ident = lambda v, c, step: v
    silu = lambda v, c, step: _silu(v)
    sigmoid = lambda v, c, step: _sigmoid(v)
    fill(qkv_ref, _C_QKV, GDN_CONV_DIM, conv_act if tiles_per_seq else ident)
    fill(z_ref, _C_Z, GDN_WIDTH, silu)
    fill(qb_ref, _C_QB, SWA_WIDTH, ident)
    fill(kvb_ref, _C_KVB, 2 * SWA_KV_WIDTH, ident)
    fill(ga_ref, _C_GA, D_MODEL, sigmoid)
    fill(gb_ref, _C_GB, D_MODEL, sigmoid)
    fill(ab_ref, _C_AB, LANES, ident)
    heavy, light = tasks[0], [t for seg in tasks[1:] for t in seg]
    while heavy or light:
        for queue in (light, heavy):
            if queue:
                run(*queue.pop(0))


def _mod_spec(mod, n_tiles):
    tiles_per_mod = n_tiles // mod.shape[0]
    return pl.BlockSpec((1, mod.shape[1], D_MODEL), lambda i: (i // tiles_per_mod, 0, 0))


def _inproj(x2d, gain, sc, sh, w_prep, tm, conv_w=None, n_seq=0):
    rows = x2d.shape[0]
    n_tiles = rows // tm
    widths = (GDN_CONV_DIM, GDN_WIDTH, SWA_WIDTH, 2 * SWA_KV_WIDTH, D_MODEL, D_MODEL, LANES)
    dtypes = (BF16, BF16, BF16, BF16, BF16, BF16, F32)
    in_specs = [pl.BlockSpec((tm, D_MODEL), lambda i: (i, 0)),
                pl.BlockSpec((1, D_MODEL), lambda i: (0, 0)),
                _mod_spec(sc, n_tiles), _mod_spec(sh, n_tiles),
                pl.BlockSpec((D_MODEL, _C_AB + LANES), lambda i: (0, 0))]
    out_specs = [pl.BlockSpec((tm, w), lambda i: (i, 0)) for w in widths]
    out_shape = [jax.ShapeDtypeStruct((rows, w), dt) for w, dt in zip(widths, dtypes)]
    args = [x2d, gain.reshape(1, D_MODEL), sc, sh, w_prep]
    scratch = []
    tiles_per_seq = 0
    if conv_w is not None:
        tiles_per_seq = n_tiles // n_seq
        in_specs.append(pl.BlockSpec((GDN_CONV, GDN_CONV_DIM), lambda i: (0, 0)))
        out_specs.append(pl.BlockSpec((1, 8, GDN_CONV_DIM), lambda i: (i // tiles_per_seq, 0, 0)))
        out_shape.append(jax.ShapeDtypeStruct((n_seq, 8, GDN_CONV_DIM), F32))
        args.append(conv_w)
        scratch.append(pltpu.VMEM((8, GDN_CONV_DIM), F32))
    return pl.pallas_call(
        functools.partial(_inproj_kernel, tiles_per_seq),
        grid=(n_tiles,),
        in_specs=in_specs,
        out_specs=out_specs,
        out_shape=out_shape,
        scratch_shapes=scratch,
        compiler_params=_cparams("arbitrary"),
        name="inproj",
    )(*args)


def _cumsum_rows(g):
    c = g.shape[0]
    tril = (_iota2((c, c), 0) >= _iota2((c, c), 1)).astype(BF16)
    hi = g.astype(BF16)
    r1 = g - hi.astype(F32)
    mid = r1.astype(BF16)
    lo = (r1 - mid.astype(F32)).astype(BF16)
    dot = lambda p: jnp.dot(tril, p, preferred_element_type=F32)
    return dot(hi) + dot(mid) + dot(lo)


def _each(fn, *lists):
    return [fn(*args) for args in zip(*lists)]


def _pair_blockdiag(m, left):
    return jnp.concatenate([jnp.where(left, m, 0.0), jnp.where(left, 0.0, m)], axis=0)


def _unit_lower_inverse_offset(a_list, ii, jj, left):
    c = a_list[0].shape[0]

    def same_block(shift):
        return lax.shift_right_logical(ii, shift) == lax.shift_right_logical(jj, shift)

    base = same_block(1)
    n_list = _each(lambda a: jnp.where(base, -a, 0.0), a_list)
    shift = 1
    while (1 << shift) < c:
        outer, inner = same_block(shift + 1), same_block(shift)
        off_list = _each(lambda a: jnp.where(outer, jnp.where(inner, 0.0, a), 0.0), a_list)
        x_list = _each(lambda off, n: off + _bdot(off, _pair_blockdiag(n, left)), off_list, n_list)
        n_list = _each(lambda n, x: n - x - _bdot(n, _pair_blockdiag(x, left)), n_list, x_list)
        shift += 1
    return n_list


def _chunk_prep(q, k, v, gcol, grow, bcol):
    c = q[0].shape[0]
    assert len(q) % 2 == 0
    ii = _iota2((c, 2 * c), 0)
    lane = _iota2((c, 2 * c), 1)
    left = lane < c
    jj = lane & (c - 1)
    causal = ii >= jj
    strict = ii > jj
    first, second = slice(0, None, 2), slice(1, None, 2)
    kb = _each(lambda kk, b: kk * b, k, bcol)
    both = _each(lambda qa, ka, qb, kbb, x, y: _bdot_nt(jnp.concatenate([qa, ka, qb, kbb], axis=0),
                                                       jnp.concatenate([x, y], axis=0)),
                 q[first], kb[first], q[second], kb[second], k[first], k[second])
    decay = _each(lambda ga, gb, ra, rb: jnp.where(causal, jnp.exp(jnp.where(
        causal, jnp.where(left, ga, gb) - jnp.concatenate([ra, rb], axis=1), 0.0)), 0.0),
        gcol[first], gcol[second], grow[first], grow[second])
    qk = _each(lambda bo, d: jnp.where(left, bo[0:c], bo[2 * c:3 * c]) * d, both, decay)
    a = _each(lambda bo, d: jnp.where(strict, jnp.where(left, bo[c:2 * c], bo[3 * c:4 * c]) * d, 0.0), both, decay)
    n = _unit_lower_inverse_offset(a, ii, jj, left)
    eg = _each(jnp.exp, gcol)
    rhs = _each(lambda vv, b, kbb, e: jnp.concatenate([vv * b, kbb * e], axis=1), v, bcol, kb, eg)
    uw = _each(lambda ra, rb, nn: (lambda r: r + _bdot(_pair_blockdiag(nn, left), r))(
        jnp.concatenate([ra, rb], axis=0)), rhs[first], rhs[second], n)
    uw = [x[half] for x in uw for half in (slice(0, c), slice(c, 2 * c))]
    u = [x[:, :GDN_DV] for x in uw]
    w = [x[:, GDN_DV:] for x in uw]
    qd = _each(lambda qq, e: qq * e, q, eg)
    kd = _each(lambda kk, gc: kk * jnp.exp(gc[c - 1:c, :] - gc), k, gcol)
    return u, w, qd, kd, qk


def _chunk_step(s, u, w, qd, kd, qk, ge):
    c = u[0].shape[0]
    both = _each(lambda ww, qq, ss: _bdot(jnp.concatenate([ww, qq], axis=0), ss), w, qd, s)
    v_new = _each(lambda uu, bo: uu.astype(F32) - bo[:c], u, both)
    o = _each(lambda bo, m, vn: bo[c:] + _bdot(m, vn), both, qk, v_new)
    s_new = _each(lambda ss, g, kk, vn: ss * g + _bdot_tn(kk, vn), s, ge, kd, v_new)
    return o, s_new


def _conv_act(xp_ref, cw_ref, r0, rows, c0):
    cols = slice(c0, c0 + LANES)
    acc = cw_ref[3:4, cols] * xp_ref[r0:r0 + rows, cols]
    for j in range(GDN_CONV - 1):
        acc = acc + cw_ref[j:j + 1, cols] * xp_ref[r0 - 3 + j:r0 - 3 + j + rows, cols]
    return _silu(acc)


def _l2n(x):
    return x * lax.rsqrt(jnp.sum(x * x, axis=-1, keepdims=True) + NORM_EPS)


def _softplus(x):
    return jnp.maximum(x, 0.0) + jnp.log1p(jnp.exp(-jnp.abs(x)))


def _head_cols(hd):
    return (hd * GDN_DK, GDN_QK_WIDTH + hd * GDN_DK, 2 * GDN_QK_WIDTH + hd * GDN_DV)


def _activate_qkv(xp_ref, cw_ref, act_ref, r0, rows):
    for hd in range(GDN_HEADS):
        cq, ck, cv = _head_cols(hd)
        act_ref[0:rows, cq:cq + LANES] = _l2n(_conv_act(xp_ref, cw_ref, r0, rows, cq)) * (GDN_DK ** -0.5)
        act_ref[0:rows, ck:ck + LANES] = _l2n(_conv_act(xp_ref, cw_ref, r0, rows, ck))
        act_ref[0:rows, cv:cv + LANES] = _conv_act(xp_ref, cw_ref, r0, rows, cv)


def _decay_beta(ab, hp_ref):
    g = -jnp.exp(hp_ref[0:1, :]) * _softplus(ab + hp_ref[1:2, :])
    return g, _sigmoid(ab)


def _gdn_prep_kernel(act_ref, ab_ref, hp_ref, u_ref, w_ref, qd_ref, kd_ref, qk_ref, ge_ref):
    tb = act_ref.shape[1]
    cc = GDN_CHUNK
    g_all, beta_all = _decay_beta(ab_ref[0], hp_ref)

    chunks = [slice(ci * cc, (ci + 1) * cc) for ci in range(tb // cc)]
    gcs = _each(lambda rows: _cumsum_rows(g_all[rows, :]), chunks)
    gcts = _each(lambda gc: gc.T, gcs)
    for ci, gc in enumerate(gcs):
        ge_ref[0, ci] = jnp.exp(gc[cc - 1:cc, :])
    items = [(ci, hd) for ci in range(len(chunks)) for hd in range(GDN_HEADS)]
    col = lambda which: [act_ref[0, chunks[ci], _head_cols(hd)[which]:_head_cols(hd)[which] + LANES].astype(F32)
                         for ci, hd in items]
    u, w, qd, kd, qk = _chunk_prep(
        col(0), col(1), col(2),
        [gcs[ci][:, hd:hd + 1] for ci, hd in items], [gcts[ci][hd:hd + 1, :] for ci, hd in items],
        [beta_all[chunks[ci], GDN_HEADS + hd:GDN_HEADS + hd + 1] for ci, hd in items])
    for idx, (ci, hd) in enumerate(items):
        rows = chunks[ci]
        oc = slice(hd * GDN_DV, (hd + 1) * GDN_DV)
        u_ref[0, rows, oc] = u[idx].astype(BF16)
        w_ref[0, rows, oc] = w[idx].astype(BF16)
        qd_ref[0, rows, oc] = qd[idx].astype(BF16)
        kd_ref[0, rows, oc] = kd[idx].astype(BF16)
        if hd % 2 == 0:
            qk_ref[0, rows, hd * cc:(hd + 2) * cc] = qk[idx // 2].astype(BF16)


def _gdn_prep(qkv, ab, head_params):
    b, t, _ = qkv.shape
    tb = min(GDN_PREP_ROWS, t)
    nch = tb // GDN_CHUNK
    blk = lambda w: pl.BlockSpec((1, tb, w), lambda bi, i: (bi, i, 0))
    out_shapes = [jax.ShapeDtypeStruct((b, t, GDN_WIDTH), BF16)] * 4 + [
        jax.ShapeDtypeStruct((b, t, GDN_HEADS * GDN_CHUNK), BF16),
        jax.ShapeDtypeStruct((b, t // GDN_CHUNK, 1, LANES), F32)]
    return pl.pallas_call(
        _gdn_prep_kernel,
        grid=(b, t // tb),
        in_specs=[blk(GDN_CONV_DIM), blk(LANES), pl.BlockSpec((8, LANES), lambda bi, i: (0, 0))],
        out_specs=[blk(GDN_WIDTH)] * 4 + [
            blk(GDN_HEADS * GDN_CHUNK),
            pl.BlockSpec((1, nch, 1, LANES), lambda bi, i: (bi, i, 0, 0))],
        out_shape=out_shapes,
        compiler_params=_cparams("parallel", "parallel"),
        name="gdn_prep",
    )(qkv, ab, head_params)


def _gated_norm_store(o_ref, idx, o, gain, zs):
    o_ref[idx] = (_rms(o, gain) * zs.astype(F32)).astype(o_ref.dtype)


def _gdn_scan_kernel(u_ref, w_ref, qd_ref, kd_ref, qk_ref, ge_ref, zs_ref, gain_ref,
                     o_ref, s_out_ref, s_ref):
    nb, tb, _ = u_ref.shape
    cc = GDN_CHUNK
    step = pl.program_id(0)

    @pl.when(step == 0)
    def _():
        s_ref[...] = jnp.zeros_like(s_ref)

    gain = gain_ref[...]

    def chunk_body(ci, carry):
        rows = pl.ds(pl.multiple_of(ci * cc, cc), cc)
        items = [(bi, hd) for bi in range(nb) for hd in range(GDN_HEADS)]
        oc = lambda hd: slice(hd * GDN_DV, (hd + 1) * GDN_DV)
        ge_rows = [ge_ref[bi, ci] for bi in range(nb)]
        o, s_new = _chunk_step(
            [s_ref[bi * GDN_HEADS + hd] for bi, hd in items],
            [u_ref[bi, rows, oc(hd)] for bi, hd in items], [w_ref[bi, rows, oc(hd)] for bi, hd in items],
            [qd_ref[bi, rows, oc(hd)] for bi, hd in items], [kd_ref[bi, rows, oc(hd)] for bi, hd in items],
            [qk_ref[bi, rows, hd * cc:(hd + 1) * cc] for bi, hd in items],
            [ge_rows[bi][:, hd:hd + 1] for bi, hd in items])
        for idx, (bi, hd) in enumerate(items):
            s_ref[bi * GDN_HEADS + hd] = s_new[idx]
            _gated_norm_store(o_ref, (bi, rows, oc(hd)), o[idx], gain, zs_ref[bi, rows, oc(hd)])
        return carry

    lax.fori_loop(0, tb // cc, chunk_body, 0, unroll=2)

    @pl.when(step == pl.num_programs(0) - 1)
    def _():
        s_out_ref[...] = s_ref[...]


def _gdn_scan(u, w, qd, kd, qk, ge, zs, gain):
    b, t, _ = u.shape
    tb = min(GDN_SCAN_ROWS, t)
    nch = tb // GDN_CHUNK
    blk = lambda wd: pl.BlockSpec((b, tb, wd), lambda i: (0, i, 0))
    o, s = pl.pallas_call(
        _gdn_scan_kernel,
        grid=(t // tb,),
        in_specs=[blk(GDN_WIDTH)] * 4 + [
            blk(GDN_HEADS * GDN_CHUNK),
            pl.BlockSpec((b, nch, 1, LANES), lambda i: (0, i, 0, 0)),
            blk(GDN_WIDTH),
            pl.BlockSpec((1, GDN_DV), lambda i: (0, 0))],
        out_specs=[blk(GDN_WIDTH),
                   pl.BlockSpec((b * GDN_HEADS, GDN_DK, GDN_DV), lambda i: (0, 0, 0))],
        out_shape=[jax.ShapeDtypeStruct((b, t, GDN_WIDTH), BF16),
                   jax.ShapeDtypeStruct((b * GDN_HEADS, GDN_DK, GDN_DV), F32)],
        scratch_shapes=[pltpu.VMEM((b * GDN_HEADS, GDN_DK, GDN_DV), F32)],
        compiler_params=_cparams("arbitrary"),
        name="gdn_scan",
    )(u, w, qd, kd, qk, ge, zs, gain.reshape(1, GDN_DV))
    return o, s.reshape(b, GDN_HEADS, GDN_DK, GDN_DV)


def _gdn_sample_kernel(new_rows, xp_ref, ab_ref, zs_ref, s0_ref, cw_ref, hp_ref, gain_ref,
                       o_ref, s_out_ref, xs_ref, act_ref):
    grp = xp_ref.shape[0]
    cc = GDN_SAMPLE_CHUNK
    gain = gain_ref[...]
    rowmask = (_iota2((cc, 1), 0) >= cc - new_rows).astype(F32)
    seqs = list(range(grp))
    for bi in seqs:
        xs = xs_ref.at[bi]
        xs[0:8, :] = jnp.zeros((8, GDN_CONV_DIM), F32)
        xs[8:8 + cc, :] = xp_ref[bi].astype(F32)
        _activate_qkv(xs, cw_ref, act_ref.at[bi], 8, cc)
    gb = _each(lambda bi: _decay_beta(ab_ref[bi], hp_ref), seqs)
    gcs = _each(lambda x: _cumsum_rows(x[0] * rowmask), gb)
    gcts = _each(lambda gc: gc.T, gcs)
    ge_rows = _each(lambda gc: jnp.exp(gc[cc - 1:cc, :]), gcs)
    betas = _each(lambda x: x[1] * rowmask, gb)
    items = [(bi, hd) for bi in seqs for hd in range(GDN_HEADS)]
    col = lambda which: [act_ref[bi, :, _head_cols(hd)[which]:_head_cols(hd)[which] + LANES] * rowmask
                         for bi, hd in items]
    u, w, qd, kd, qk = _chunk_prep(
        col(0), col(1), col(2),
        [gcs[bi][:, hd:hd + 1] for bi, hd in items], [gcts[bi][hd:hd + 1, :] for bi, hd in items],
        [betas[bi][:, GDN_HEADS + hd:GDN_HEADS + hd + 1] for bi, hd in items])
    qk = [pair[:, half] for pair in qk for half in (slice(0, cc), slice(cc, 2 * cc))]
    o, s_new = _chunk_step([s0_ref[bi, hd] for bi, hd in items], u, w, qd, kd, qk,
                           [ge_rows[bi][:, hd:hd + 1] for bi, hd in items])
    for idx, (bi, hd) in enumerate(items):
        s_out_ref[bi, hd] = s_new[idx]
        oc = slice(hd * GDN_DV, (hd + 1) * GDN_DV)
        _gated_norm_store(o_ref, (bi, slice(None), oc), o[idx], gain, zs_ref[bi, :, oc])


def _gdn_sample(xp, ab, zs, s0, conv_w, head_params, gain, new_rows):
    b = xp.shape[0]
    cc = GDN_SAMPLE_CHUNK
    grp = GDN_SAMPLE_GROUP
    blk3 = lambda w: pl.BlockSpec((grp, cc, w), lambda i: (i, 0, 0))
    sblk = pl.BlockSpec((grp, GDN_HEADS, GDN_DK, GDN_DV), lambda i: (i, 0, 0, 0))
    return pl.pallas_call(
        functools.partial(_gdn_sample_kernel, new_rows),
        grid=(b // grp,),
        in_specs=[blk3(GDN_CONV_DIM), blk3(LANES), blk3(GDN_WIDTH), sblk,
                  pl.BlockSpec((GDN_CONV, GDN_CONV_DIM), lambda i: (0, 0)),
                  pl.BlockSpec((8, LANES), lambda i: (0, 0)),
                  pl.BlockSpec((1, GDN_DV), lambda i: (0, 0))],
        out_specs=[blk3(GDN_WIDTH), sblk],
        out_shape=[jax.ShapeDtypeStruct((b, cc, GDN_WIDTH), BF16),
                   jax.ShapeDtypeStruct((b, GDN_HEADS, GDN_DK, GDN_DV), F32)],
        scratch_shapes=[pltpu.VMEM((grp, cc + 8, GDN_CONV_DIM), F32),
                        pltpu.VMEM((grp, cc, GDN_CONV_DIM), F32)],
        compiler_params=_cparams("parallel"),
        name="gdn_sample",
    )(xp, ab, zs, s0, conv_w, head_params, gain.reshape(1, GDN_DV))


def _sink_attention(q, k, v, mask, sink_col):
    s = _each(lambda qq, kk: jnp.where(mask, _bdot_nt(qq, kk) * (SWA_HEAD_DIM ** -0.5), -jnp.inf), q, k)
    m = _each(lambda ss, sk: jnp.maximum(jnp.max(ss, axis=-1, keepdims=True), sk), s, sink_col)
    p = _each(lambda ss, mm: jnp.exp(ss - mm), s, m)
    denom = _each(lambda pp, sk, mm: jnp.sum(pp, axis=-1, keepdims=True) + jnp.exp(sk - mm), p, sink_col, m)
    return _each(lambda pp, vv, dd: _bdot(pp, vv) / dd, p, v, denom)


def _sink_column(sinks_ref, kv_head, rows_per_head):
    parts = [jnp.full((rows_per_head, 1), sinks_ref[kv_head * SWA_GROUP + g], F32) for g in range(SWA_GROUP)]
    return jnp.concatenate(parts, axis=0)


def _swa_prompt_kernel(sinks_ref, q_ref, kvp_ref, kvc_ref, o_ref):
    wnd = WINDOW
    nblk = q_ref.shape[1] // wnd
    step = pl.program_id(1)
    kv = jnp.concatenate([kvp_ref[0], kvc_ref[0]], axis=0)
    cols = SWA_GROUP * wnd
    kj = _iota2((2 * wnd, cols), 0)
    qi = _iota2((2 * wnd, cols), 1) & (wnd - 1)
    dist = qi + wnd - kj
    band = (dist >= 0) & (dist <= wnd)
    first_key = jnp.where(step > 0, 0, wnd)
    masks = [band & (kj >= first_key)] + [band] * (nblk - 1)
    head_cols = lambda h: slice(h * SWA_HEAD_DIM, (h + 1) * SWA_HEAD_DIM)
    q_heads = lambda hk: [hk * SWA_GROUP + g for g in range(SWA_GROUP)]
    scale = SWA_HEAD_DIM ** -0.5
    items = [(j, hk) for j in range(nblk) for hk in range(SWA_KV_HEADS)]
    q_rows = lambda j: slice(j * wnd, (j + 1) * wnd)
    k_rows = lambda j: slice(j * wnd, (j + 2) * wnd)
    q = [jnp.concatenate([q_ref[0, q_rows(j), head_cols(h)] for h in q_heads(hk)], axis=0) for j, hk in items]
    k = [kv[k_rows(j), head_cols(hk)] for j, hk in items]
    v = [kv[k_rows(j), SWA_KV_WIDTH + hk * SWA_HEAD_DIM:SWA_KV_WIDTH + (hk + 1) * SWA_HEAD_DIM] for j, hk in items]
    sinks_kv = [jnp.concatenate([jnp.full((1, wnd), sinks_ref[h], F32) for h in q_heads(hk)], axis=1)
                for hk in range(SWA_KV_HEADS)]
    sink = [sinks_kv[hk] for j, hk in items]
    mask = [masks[j] for j, hk in items]
    s = _each(lambda kk, qq, mk: jnp.where(mk, _bdot_nt(kk, qq) * scale, -jnp.inf), k, q, mask)
    m = _each(lambda ss, sk: jnp.maximum(jnp.max(ss, axis=0, keepdims=True), sk), s, sink)
    p = _each(lambda ss, mm: jnp.exp(ss - mm), s, m)
    denom = _each(lambda pp, sk, mm: jnp.sum(pp, axis=0, keepdims=True) + jnp.exp(sk - mm), p, sink, m)
    ot = _each(lambda vv, pp, dd: _bdot_tn(vv, pp) / dd, v, p, denom)
    for idx, (j, hk) in enumerate(items):
        for g in range(0, SWA_GROUP, 2):
            pair = jnp.concatenate([ot[idx][:, g * wnd:(g + 1) * wnd], ot[idx][:, (g + 1) * wnd:(g + 2) * wnd]],
                                   axis=0)
            h0 = hk * SWA_GROUP + g
            o_ref[0, q_rows(j), h0 * SWA_HEAD_DIM:(h0 + 2) * SWA_HEAD_DIM] = pair.T.astype(o_ref.dtype)


def _swa_prompt(q, kv, sinks):
    b, t, _ = q.shape
    nblk = SWA_BLOCKS_PER_STEP
    rows = nblk * WINDOW
    return pl.pallas_call(
        _swa_prompt_kernel,
        grid=(b, t // rows),
        in_specs=[pl.BlockSpec(memory_space=pltpu.SMEM),
                  pl.BlockSpec((1, rows, SWA_WIDTH), lambda bi, i: (bi, i, 0)),
                  pl.BlockSpec((1, WINDOW, 2 * SWA_KV_WIDTH), lambda bi, i: (bi, jnp.maximum(i * nblk - 1, 0), 0)),
                  pl.BlockSpec((1, rows, 2 * SWA_KV_WIDTH), lambda bi, i: (bi, i, 0))],
        out_specs=pl.BlockSpec((1, rows, SWA_WIDTH), lambda bi, i: (bi, i, 0)),
        out_shape=jax.ShapeDtypeStruct((b, t, SWA_WIDTH), BF16),
        compiler_params=_cparams("parallel", "parallel"),
        name="swa_prompt",
    )(sinks, q, kv, kv)


def _swa_sample_kernel(sinks_ref, q_ref, kvn_ref, kc_ref, vc_ref, o_ref, ko_ref, vo_ref):
    grp, t, _ = q_ref.shape
    wnd = WINDOW
    nk = wnd + BF16_ROWS
    rows = SWA_GROUP * t
    tq = _iota2((rows, nk), 0) & (t - 1)
    kj = _iota2((rows, nk), 1)
    dist = tq + wnd - kj
    mask = (dist >= 0) & (dist <= wnd)
    zpad = jnp.zeros((BF16_ROWS - t, SWA_KV_WIDTH), F32)
    kks, vvs = [], []
    for bi in range(grp):
        kvn = kvn_ref[bi].astype(F32)
        kk = jnp.concatenate([kc_ref[bi], kvn[:, :SWA_KV_WIDTH], zpad], axis=0)
        vv = jnp.concatenate([vc_ref[bi], kvn[:, SWA_KV_WIDTH:], zpad], axis=0)
        ko_ref[bi] = kk[t:t + wnd, :]
        vo_ref[bi] = vv[t:t + wnd, :]
        kks.append(kk)
        vvs.append(vv)
    items = [(bi, hk) for bi in range(grp) for hk in range(SWA_KV_HEADS)]
    head_cols = lambda h: slice(h * SWA_HEAD_DIM, (h + 1) * SWA_HEAD_DIM)
    q_heads = lambda hk: [hk * SWA_GROUP + g for g in range(SWA_GROUP)]
    sink_cols = [_sink_column(sinks_ref, hk, t) for hk in range(SWA_KV_HEADS)]
    o = _sink_attention(
        [jnp.concatenate([q_ref[bi, :, head_cols(h)] for h in q_heads(hk)], axis=0) for bi, hk in items],
        [kks[bi][:, head_cols(hk)] for bi, hk in items], [vvs[bi][:, head_cols(hk)] for bi, hk in items],
        mask, [sink_cols[hk] for bi, hk in items])
    for idx, (bi, hk) in enumerate(items):
        for g, h in enumerate(q_heads(hk)):
            o_ref[bi, :, head_cols(h)] = o[idx][g * t:(g + 1) * t].astype(o_ref.dtype)


def _swa_sample(q, kv_new, k_cache, v_cache, sinks):
    b, t, _ = q.shape
    grp = SWA_SAMPLE_GROUP
    blk = lambda r, w: pl.BlockSpec((grp, r, w), lambda i: (i, 0, 0))
    return pl.pallas_call(
        _swa_sample_kernel,
        grid=(b // grp,),
        in_specs=[pl.BlockSpec(memory_space=pltpu.SMEM),
                  blk(t, SWA_WIDTH), blk(t, 2 * SWA_KV_WIDTH), blk(WINDOW, SWA_KV_WIDTH), blk(WINDOW, SWA_KV_WIDTH)],
        out_specs=[blk(t, SWA_WIDTH), blk(WINDOW, SWA_KV_WIDTH), blk(WINDOW, SWA_KV_WIDTH)],
        out_shape=[jax.ShapeDtypeStruct((b, t, SWA_WIDTH), BF16),
                   jax.ShapeDtypeStruct((b, WINDOW, SWA_KV_WIDTH), F32),
                   jax.ShapeDtypeStruct((b, WINDOW, SWA_KV_WIDTH), F32)],
        compiler_params=_cparams("parallel"),
        name="swa_sample",
    )(sinks, q, kv_new, k_cache, v_cache)


def _route(logits):
    lane = _iota2(logits.shape, 1).astype(F32)
    neg = -jnp.inf

    def first_argmax(vals, valid):
        v = jnp.where(valid, vals, neg)
        m = jnp.max(v, axis=-1, keepdims=True)
        idx = jnp.min(jnp.where(jnp.logical_and(valid, v == m), lane, float(LANES)), axis=-1, keepdims=True)
        return m, idx

    is_group = lane < N_GROUPS
    gmax, gidx = first_argmax(logits, is_group)
    p_group = 1.0 / jnp.sum(jnp.where(is_group, jnp.exp(logits - gmax), 0.0), axis=-1, keepdims=True)
    lo = N_GROUPS + gidx * EXPERTS_PER_GROUP
    in_group = jnp.logical_and(lane >= lo, lane < lo + EXPERTS_PER_GROUP)
    m1, i1 = first_argmax(logits, in_group)
    esum = jnp.sum(jnp.where(in_group, jnp.exp(logits - m1), 0.0), axis=-1, keepdims=True)
    m2, i2 = first_argmax(logits, jnp.logical_and(in_group, lane != i1))
    p1 = 1.0 / esum
    p2 = jnp.exp(m2 - m1) / esum
    tot = p1 + p2
    return i1 - N_GROUPS, i2 - N_GROUPS, p_group * p1 / tot, p_group * p2 / tot


def _post_mixer_kernel(oa_ref, ob_ref, ga_ref, gb_ref, x_ref, gt_ref, sc_ref, sh_ref,
                       wa_ref, wb_ref, wo_ref, gpost_ref, gpre_ref, wr_ref, br_ref, cnt0_ref,
                       x1_ref, h2_ref, rt_ref, cnt_out_ref, meta_ref, *rest):
    cnt_ref = rest[-1]
    if len(rest) == 2:
        rest[0][...] = jnp.zeros_like(rest[0])
    step = pl.program_id(0)

    @pl.when(step == 0)
    def _():
        cnt_ref[...] = cnt0_ref[...]

    merged = (ga_ref[...].astype(F32) * jnp.dot(oa_ref[...], wa_ref[...], preferred_element_type=F32)
              + gb_ref[...].astype(F32) * jnp.dot(ob_ref[...], wb_ref[...], preferred_element_type=F32))
    mix = _bdot(merged, wo_ref[...])
    x1 = x_ref[...] + gt_ref[0] * _rms(mix, gpost_ref[...])
    x1_ref[...] = x1
    h2 = _rms(x1, gpre_ref[...]) * (1.0 + sc_ref[0]) + sh_ref[0]
    _rows_to_tiles(h2_ref, h2)
    h_hi = h2.astype(BF16)
    h_lo = (h2 - h_hi.astype(F32)).astype(BF16)
    part = jnp.dot(h_hi, wr_ref[...], preferred_element_type=F32)
    logits = (part[:, :LANES] + part[:, LANES:]
              + jnp.dot(h_lo, wr_ref[:, :LANES], preferred_element_type=F32) + br_ref[...])
    ia, ib, wa, wb = _route(logits)
    lane = _iota2(logits.shape, 1)
    tm = logits.shape[0]
    lane_f = lane.astype(F32)
    hot_a = (lane_f == ia).astype(F32)
    hot_b = (lane_f == ib).astype(F32)
    hot = hot_a + hot_b
    earlier = (_iota2((tm, tm), 0) > _iota2((tm, tm), 1)).astype(BF16)
    before = jnp.dot(earlier, hot.astype(BF16), preferred_element_type=F32) + cnt_ref[...]
    rank_a = jnp.sum(hot_a * before, axis=-1, keepdims=True)
    rank_b = jnp.sum(hot_b * before, axis=-1, keepdims=True)
    cnt_ref[...] = cnt_ref[...] + jnp.sum(hot, axis=0, keepdims=True)
    cnt_out_ref[...] = cnt_ref[...]
    rt_ref[...] = jnp.where(lane == 0, ia, jnp.where(lane == 1, ib, jnp.where(lane == 2, wa, jnp.where(
        lane == 3, wb, 0.0))))
    packed = jnp.where(lane == 0, rank_a * N_EXPERTS + ia, jnp.where(lane == 1, rank_b * N_EXPERTS + ib, 0.0))
    meta_ref[0] = packed.T[0:TOP_K, :].astype(I32)


def _post_mixer(oa, ob, ga, gb, x2d, gt, sc, sh, w_a, w_b, w_o, g_post, g_pre, w_rt, b_rt, cnt0, tm,
                zero_rows=0):
    rows = x2d.shape[0]
    n_tiles = rows // tm
    row_blk = lambda w: pl.BlockSpec((tm, w), lambda i: (i, 0))
    full = lambda r, c: pl.BlockSpec((r, c), lambda i: (0, 0))
    out_specs = [row_blk(D_MODEL), pl.BlockSpec((tm * TILE_ROWS, LANES), lambda i: (i, 0)), row_blk(LANES),
                 full(1, LANES), pl.BlockSpec((1, TOP_K, tm), lambda i: (i, 0, 0))]
    out_shape = [jax.ShapeDtypeStruct((rows, D_MODEL), F32),
                 jax.ShapeDtypeStruct((rows * TILE_ROWS, LANES), F32),
                 jax.ShapeDtypeStruct((rows, LANES), F32),
                 jax.ShapeDtypeStruct((1, LANES), F32),
                 jax.ShapeDtypeStruct((n_tiles, TOP_K, tm), I32)]
    if zero_rows:
        assert zero_rows % (n_tiles * TILE_ROWS) == 0
        out_specs.append(pl.BlockSpec((zero_rows // n_tiles, LANES), lambda i: (i, 0)))
        out_shape.append(jax.ShapeDtypeStruct((zero_rows, LANES), F32))
    return pl.pallas_call(
        _post_mixer_kernel,
        grid=(n_tiles,),
        in_specs=[row_blk(GDN_WIDTH), row_blk(SWA_WIDTH), row_blk(D_MODEL), row_blk(D_MODEL), row_blk(D_MODEL),
                  _mod_spec(gt, n_tiles), _mod_spec(sc, n_tiles), _mod_spec(sh, n_tiles),
                  full(GDN_WIDTH, D_MODEL), full(SWA_WIDTH, D_MODEL), full(D_MODEL, D_MODEL),
                  full(1, D_MODEL), full(1, D_MODEL), full(D_MODEL, 2 * LANES), full(1, LANES), full(1, LANES)],
        out_specs=out_specs,
        out_shape=out_shape,
        scratch_shapes=[pltpu.VMEM((1, LANES), F32)],
        compiler_params=_cparams("arbitrary"),
        name="post_mixer",
    )(oa, ob, ga, gb, x2d, gt, sc, sh, w_a, w_b, w_o,
      g_post.reshape(1, D_MODEL), g_pre.reshape(1, D_MODEL), w_rt, b_rt, cnt0)


TILE_ROWS = D_MODEL // LANES


def _tiles_to_rows(ref, first, rows):
    base = first * TILE_ROWS
    return jnp.concatenate([ref[pl.ds(base + c, rows, stride=TILE_ROWS), :] for c in range(TILE_ROWS)], axis=1)


def _rows_to_tiles(ref, mat):
    rows = mat.shape[0]
    for c in range(TILE_ROWS):
        ref[pl.ds(c, rows, stride=TILE_ROWS), :] = mat[:, c * LANES:(c + 1) * LANES]


def _tile_copy_loop(n, copies, start):
    def body(t, carry):
        for j, cp in enumerate(copies(t)):
            if start:
                cp.start(priority=j % 2)
            else:
                cp.wait()
        return carry

    lax.fori_loop(0, n, body, 0, unroll=8)


def _slots_kernel(pstart_ref, meta_ref, o_ref):
    packed = meta_ref[...]
    expert = packed & (N_EXPERTS - 1)
    first = jnp.zeros_like(packed)
    for e in range(N_EXPERTS):
        first = jnp.where(expert == e, pstart_ref[e], first)
    o_ref[...] = first + lax.shift_right_logical(packed, N_EXPERTS.bit_length() - 1)


def _slots(meta, pstarts):
    return pl.pallas_call(
        _slots_kernel,
        grid_spec=pltpu.PrefetchScalarGridSpec(
            num_scalar_prefetch=1,
            grid=(1,),
            in_specs=[pl.BlockSpec(meta.shape, lambda i, p: (0, 0, 0))],
            out_specs=pl.BlockSpec(meta.shape, lambda i, p: (0, 0, 0))),
        out_shape=jax.ShapeDtypeStruct(meta.shape, I32),
        compiler_params=_cparams("arbitrary"),
        name="moe_slots",
    )(pstarts, meta)


def _dispatch_kernel(slot_ref, h_ref, xs_in_hbm, xs_hbm, sem):
    del xs_in_hbm
    i = pl.program_id(0)
    tm = h_ref.shape[0] // TILE_ROWS
    sub_tiles = tm // ROW_TILE

    for start in (True, False):
        for j in range(sub_tiles):
            base = (i * sub_tiles + j) * TOP_K * ROW_TILE

            def copies(t, j=j, base=base):
                row = pl.multiple_of((j * ROW_TILE + t) * TILE_ROWS, TILE_ROWS)
                return [pltpu.make_async_copy(h_ref.at[pl.ds(row, TILE_ROWS)],
                                              xs_hbm.at[slot_ref[base + k * ROW_TILE + t]], sem)
                        for k in range(TOP_K)]

            _tile_copy_loop(ROW_TILE, copies, start)


def _dispatch(slots, h_tiles, xs, tm):
    n_tiles = h_tiles.shape[0] // (tm * TILE_ROWS)
    return pl.pallas_call(
        _dispatch_kernel,
        grid_spec=pltpu.PrefetchScalarGridSpec(
            num_scalar_prefetch=1,
            grid=(n_tiles,),
            in_specs=[pl.BlockSpec((tm * TILE_ROWS, LANES), lambda i, s: (i, 0)),
                      pl.BlockSpec(memory_space=pl.ANY)],
            out_specs=pl.BlockSpec(memory_space=pl.ANY),
            scratch_shapes=[pltpu.SemaphoreType.DMA(())]),
        out_shape=jax.ShapeDtypeStruct(xs.shape, xs.dtype),
        input_output_aliases={2: 0},
        compiler_params=_cparams("arbitrary"),
        name="moe_dispatch",
    )(slots, h_tiles, xs)


def _moe_kernel(blk_e_ref, n_used_ref, x_ref, wg_ref, wu_ref, wd_ref, y_ref, wgb, wub, wdb):
    b = pl.program_id(0)
    rows = x_ref.shape[0] // TILE_ROWS
    changed = jnp.logical_or(b == 0, blk_e_ref[b] != blk_e_ref[jnp.maximum(b - 1, 0)])

    @pl.when(changed)
    def _():
        wgb[...] = wg_ref[0].astype(BF16)
        wub[...] = wu_ref[0].astype(BF16)
        wdb[...] = wd_ref[0].astype(BF16)

    @pl.when(b < n_used_ref[0])
    def _():
        x = _tiles_to_rows(x_ref, 0, rows).astype(BF16)
        gate = jnp.dot(x, wgb[...], preferred_element_type=F32)
        up = jnp.dot(x, wub[...], preferred_element_type=F32)
        _rows_to_tiles(y_ref, _bdot(_silu(gate) * up, wdb[...]))

    @pl.when(b >= n_used_ref[0])
    def _():
        y_ref[...] = jnp.zeros_like(y_ref)


def _moe(xs_tiles, blk_e, n_used, w_gate, w_up, w_down):
    n_blocks = blk_e.shape[0]
    rows = MOE_ROWS
    wspec = lambda r, c: pl.BlockSpec((1, r, c), lambda b, be, nu: (be[b], 0, 0))
    xspec = pl.BlockSpec((rows * TILE_ROWS, LANES), lambda b, be, nu: (b, 0))
    return pl.pallas_call(
        _moe_kernel,
        grid_spec=pltpu.PrefetchScalarGridSpec(
            num_scalar_prefetch=2,
            grid=(n_blocks,),
            in_specs=[xspec, wspec(D_MODEL, EXPERT_FF), wspec(D_MODEL, EXPERT_FF), wspec(EXPERT_FF, D_MODEL)],
            out_specs=xspec,
            scratch_shapes=[pltpu.VMEM((D_MODEL, EXPERT_FF), BF16),
                            pltpu.VMEM((D_MODEL, EXPERT_FF), BF16),
                            pltpu.VMEM((EXPERT_FF, D_MODEL), BF16)]),
        out_shape=jax.ShapeDtypeStruct(xs_tiles.shape, F32),
        compiler_params=_cparams("arbitrary"),
        name="moe_experts",
    )(blk_e, n_used, xs_tiles, w_gate, w_up, w_down)


def _combine_kernel(slot_ref, y_hbm, x1_ref, rt_ref, gt_ref, gpost_ref, o_ref, ybuf, sems):
    i = pl.program_id(0)
    n = pl.num_programs(0)
    rows = ybuf.shape[1] // TILE_ROWS
    slot = i % 2

    def gather(step, buf_slot, start):
        def copy(r):
            dst = ybuf.at[buf_slot, pl.ds(pl.multiple_of(r * TILE_ROWS, TILE_ROWS), TILE_ROWS)]
            return pltpu.make_async_copy(y_hbm.at[slot_ref[step * rows + r]], dst, sems.at[buf_slot])

        _tile_copy_loop(rows // 2, lambda t: [copy(2 * t), copy(2 * t + 1)], start)

    @pl.when(i == 0)
    def _():
        gather(0, 0, True)

    @pl.when(i + 1 < n)
    def _():
        gather(i + 1, 1 - slot, True)

    gather(i, slot, False)
    half = rows // 2
    rt = rt_ref[...]
    buf = ybuf.at[slot]
    f = rt[:, 2:3] * _tiles_to_rows(buf, 0, half) + rt[:, 3:4] * _tiles_to_rows(buf, half, half)
    o_ref[...] = x1_ref[...] + gt_ref[0] * _rms(f, gpost_ref[...])


def _combine(slots, yb, x1, rt, gt, g_post):
    rows = x1.shape[0]
    tm = ROW_TILE
    n_tiles = rows // tm
    tiles_per_mod = n_tiles // gt.shape[0]
    return pl.pallas_call(
        _combine_kernel,
        grid_spec=pltpu.PrefetchScalarGridSpec(
            num_scalar_prefetch=1,
            grid=(n_tiles,),
            in_specs=[pl.BlockSpec(memory_space=pl.ANY),
                      pl.BlockSpec((tm, D_MODEL), lambda i, s: (i, 0)),
                      pl.BlockSpec((tm, LANES), lambda i, s: (i, 0)),
                      pl.BlockSpec((1, gt.shape[1], D_MODEL), lambda i, s: (i // tiles_per_mod, 0, 0)),
                      pl.BlockSpec((1, D_MODEL), lambda i, s: (0, 0))],
            out_specs=pl.BlockSpec((tm, D_MODEL), lambda i, s: (i, 0)),
            scratch_shapes=[pltpu.VMEM((2, TOP_K * tm * TILE_ROWS, LANES), F32),
                            pltpu.SemaphoreType.DMA((2,))]),
        out_shape=jax.ShapeDtypeStruct((rows, D_MODEL), F32),
        compiler_params=_cparams("arbitrary"),
        name="moe_combine",
    )(slots, yb, x1, rt, gt, g_post.reshape(1, D_MODEL))


def _dispatch_plan(counts, n_tok):
    pcounts = (counts + MOE_ROWS - 1) // MOE_ROWS * MOE_ROWS
    pends = jnp.cumsum(pcounts)
    pstarts = (pends - pcounts).astype(I32)
    n_blocks = n_tok * TOP_K // MOE_ROWS + N_EXPERTS
    blk_start = jnp.arange(n_blocks, dtype=I32) * MOE_ROWS
    blk_e = jnp.minimum(jnp.sum(blk_start[:, None] >= pends[None, :], axis=1), N_EXPERTS - 1).astype(I32)
    n_used = (pends[-1] // MOE_ROWS).astype(I32).reshape(1)
    return blk_e, n_used, pstarts


_AB_FIRST = 4 * GDN_QK_WIDTH
_AB_COUNT = 2 * GDN_HEADS


def _prep_in_weight_kernel(w_ref, o_ref):
    x = w_ref[...]
    last = pl.program_id(0) == pl.num_programs(0) - 1
    keep = jnp.logical_or(jnp.logical_not(last), _iota2(x.shape, 0) < _AB_COUNT)
    o_ref[...] = jnp.where(keep, x, 0.0).T.astype(BF16)


def _prep_in_weight(w_in_t):
    n_blocks = IN_COLS // PROJ_TILE
    shifted_from = _AB_FIRST // PROJ_TILE

    def src_row(i):
        return jnp.where(i == n_blocks - 1, _AB_FIRST,
                         jnp.where(i >= shifted_from, i * PROJ_TILE + _AB_COUNT, i * PROJ_TILE))

    return pl.pallas_call(
        _prep_in_weight_kernel,
        grid=(n_blocks,),
        in_specs=[pl.BlockSpec((pl.Element(PROJ_TILE), pl.Element(D_MODEL)),
                               lambda i: (pl.multiple_of(src_row(i), _AB_COUNT), 0))],
        out_specs=pl.BlockSpec((D_MODEL, PROJ_TILE), lambda i: (0, i)),
        out_shape=jax.ShapeDtypeStruct((D_MODEL, IN_COLS), BF16),
        compiler_params=_cparams("parallel"),
        name="prep_in_weight",
    )(w_in_t)


def _head_param_tile(a_log, dt_bias):
    tile = jnp.zeros((8, LANES), F32)
    return tile.at[0, :GDN_HEADS].set(a_log.astype(F32)).at[1, :GDN_HEADS].set(dt_bias.astype(F32))


def _router_weight(w_group, b_group, w_router, b_router):
    w = jnp.zeros((D_MODEL, LANES), F32)
    w = w.at[:, :N_GROUPS].set(w_group).at[:, N_GROUPS:N_GROUPS + N_EXPERTS].set(w_router)
    b = jnp.zeros((1, LANES), F32)
    b = b.at[0, :N_GROUPS].set(b_group).at[0, N_GROUPS:N_GROUPS + N_EXPERTS].set(b_router)
    w_hi = w.astype(BF16)
    w_lo = (w - w_hi.astype(F32)).astype(BF16)
    return jnp.concatenate([w_hi, w_lo], axis=1), b


def kernel(x_prompt, x_sample, state_gdn, state_conv, cache_k_win, cache_v_win, c_prompt, c_sample, w_ada, b_ada, g_mix_pre, g_mix_post, g_ffn_pre, g_ffn_post, w_in, conv_w, a_log, dt_bias, gdn_norm, sinks, w_br_gdn, w_br_swa, w_out, w_group, b_group, w_router, b_router, w_gate, w_up, w_down):
    depth = w_ada.shape[0]
    assert depth == 1, "single-layer trunk"
    bp, tp, _ = x_prompt.shape
    bs, ts, _ = x_sample.shape
    n_p = bp * tp
    n_s = bs * ts
    tm = ROW_TILE
    assert tp % tm == 0 and n_s % tm == 0 and ts >= GDN_CONV - 1 and ts + GDN_CONV - 1 <= GDN_SAMPLE_CHUNK
    assert ts & (ts - 1) == 0 and ts <= BF16_ROWS

    c_all = jnp.concatenate([c_prompt, c_sample], axis=0)
    c_rows = -(-c_all.shape[0] // 8) * 8
    c_all = jnp.pad(c_all, ((0, c_rows - c_all.shape[0]), (0, 0)))
    mod = _adaln(c_all, w_ada[0], b_ada[0])
    mods_p = [m[:bp].reshape(bp, 1, D_MODEL) for m in jnp.split(mod, 6, axis=-1)]
    mods_s = [jnp.repeat(m[bp:bp + bs], ts, axis=0).reshape(n_s // tm, tm, D_MODEL)
              for m in jnp.split(mod, 6, axis=-1)]

    w_prep = _prep_in_weight(jnp.swapaxes(w_in[0], 0, 1))
    head_params = _head_param_tile(a_log[0], dt_bias[0])
    w_a, w_b, w_o = w_br_gdn[0].astype(BF16), w_br_swa[0].astype(BF16), w_out[0].astype(BF16)
    w_rt, b_rt = _router_weight(w_group[0], b_group[0], w_router[0], b_router[0])
    sinks0 = sinks[0].astype(F32)

    xp2d = x_prompt.reshape(n_p, D_MODEL)
    sh1, sc1, gt1, sh2, sc2, gt2 = mods_p
    qkv_p, zs_p, qb_p, kvb_p, ga_p, gb_p, ab_p, conv_tail_p = _inproj(
        xp2d, g_mix_pre[0], sc1, sh1, w_prep, INPROJ_ROWS, conv_w=conv_w[0].astype(F32), n_seq=bp)
    qkv_p3 = qkv_p.reshape(bp, tp, GDN_CONV_DIM)
    u, w, qd, kd, qk, ge = _gdn_prep(qkv_p3, ab_p.reshape(bp, tp, LANES), head_params)
    oa_p, s_prompt = _gdn_scan(u, w, qd, kd, qk, ge, zs_p.reshape(bp, tp, GDN_WIDTH), gdn_norm[0])
    kvb_p3 = kvb_p.reshape(bp, tp, 2 * SWA_KV_WIDTH)
    ob_p = _swa_prompt(qb_p.reshape(bp, tp, SWA_WIDTH), kvb_p3, sinks0)
    n_slots = ((n_p + n_s) * TOP_K // MOE_ROWS + N_EXPERTS) * MOE_ROWS
    x1_p, h2_p, rt_p, cnt_p, meta_p, xs = _post_mixer(
        oa_p.reshape(n_p, GDN_WIDTH), ob_p.reshape(n_p, SWA_WIDTH), ga_p, gb_p, xp2d, gt1, sc2, sh2,
        w_a, w_b, w_o, g_mix_post[0], g_ffn_pre[0], w_rt, b_rt, jnp.zeros((1, LANES), F32), tm,
        zero_rows=n_slots * TILE_ROWS)

    xs2d = x_sample.reshape(n_s, D_MODEL)
    sh1s, sc1s, gt1s, sh2s, sc2s, gt2s = mods_s
    qkv_s, zs_s, qb_s, kvb_s, ga_s, gb_s, ab_s = _inproj(xs2d, g_mix_pre[0], sc1s, sh1s, w_prep, tm)
    cc = GDN_SAMPLE_CHUNK
    pad_rows = cc - ts - (GDN_CONV - 1)
    qkv_s3 = qkv_s.reshape(bs, ts, GDN_CONV_DIM)
    xp_s = jnp.concatenate([jnp.zeros((bs, pad_rows, GDN_CONV_DIM), BF16), state_conv[0].astype(BF16), qkv_s3],
                           axis=1)
    front = lambda a: jnp.pad(a, ((0, 0), (cc - ts, 0), (0, 0)))
    oa_s16, s_sample = _gdn_sample(xp_s, front(ab_s.reshape(bs, ts, LANES)), front(zs_s.reshape(bs, ts, GDN_WIDTH)),
                                   state_gdn[0].astype(F32), conv_w[0], head_params, gdn_norm[0], ts)
    oa_s = oa_s16[:, cc - ts:, :].reshape(n_s, GDN_WIDTH)
    ob_s, k_new_s, v_new_s = _swa_sample(
        qb_s.reshape(bs, ts, SWA_WIDTH), kvb_s.reshape(bs, ts, 2 * SWA_KV_WIDTH),
        cache_k_win[0].reshape(bs, WINDOW, SWA_KV_WIDTH).astype(F32),
        cache_v_win[0].reshape(bs, WINDOW, SWA_KV_WIDTH).astype(F32), sinks0)
    x1_s, h2_s, rt_s, cnt_all, meta_s = _post_mixer(
        oa_s, ob_s.reshape(n_s, SWA_WIDTH), ga_s, gb_s, xs2d, gt1s, sc2s, sh2s,
        w_a, w_b, w_o, g_mix_post[0], g_ffn_pre[0], w_rt, b_rt, cnt_p, tm)

    blk_e, n_used, pstarts = _dispatch_plan(cnt_all[0, :N_EXPERTS].astype(I32), n_p + n_s)
    slots_p = _slots(meta_p, pstarts).reshape(-1)
    slots_s = _slots(meta_s, pstarts).reshape(-1)
    xs = _dispatch(slots_p, h2_p, xs.reshape(n_slots, TILE_ROWS, LANES), min(DISPATCH_ROWS, n_p))
    xs = _dispatch(slots_s, h2_s, xs, min(DISPATCH_ROWS, n_s))
    yb = _moe(xs.reshape(n_slots * TILE_ROWS, LANES), blk_e, n_used, w_gate[0], w_up[0], w_down[0])
    yb = yb.reshape(n_slots, TILE_ROWS, LANES)
    y_p = _combine(slots_p, yb, x1_p, rt_p, gt2, g_ffn_post[0])
    y_s = _combine(slots_s, yb, x1_s, rt_s, gt2s, g_ffn_post[0])

    f32 = lambda a: a.astype(F32)
    kv_tail = kvb_p3[:, tp - WINDOW:, :]
    kv_heads = lambda a: f32(a).reshape(a.shape[0], WINDOW, SWA_KV_HEADS, SWA_HEAD_DIM)[None]
    return (y_p.reshape(bp, tp, D_MODEL), y_s.reshape(bs, ts, D_MODEL),
            s_prompt[None], conv_tail_p[:, 8 - (GDN_CONV - 1):, :][None],
            kv_heads(kv_tail[:, :, :SWA_KV_WIDTH]), kv_heads(kv_tail[:, :, SWA_KV_WIDTH:]),
            s_sample[None], f32(qkv_s3[:, ts - (GDN_CONV - 1):, :])[None],
            kv_heads(k_new_s), kv_heads(v_new_s))
```

```python
import functools

import jax
import jax.numpy as jnp
from jax import lax
from jax.experimental import pallas as pl
from jax.experimental.pallas import tpu as pltpu

F32 = jnp.float32
BF16 = jnp.bfloat16
I32 = jnp.int32

D_MODEL = 1024
NORM_EPS = 1e-6
GDN_HEADS = 8
GDN_DK = 128
GDN_DV = 128
GDN_CONV = 4
GDN_CHUNK = 64
GDN_QK_WIDTH = GDN_HEADS * GDN_DK
GDN_WIDTH = GDN_HEADS * GDN_DV
GDN_CONV_DIM = 2 * GDN_QK_WIDTH + GDN_WIDTH
SWA_Q_HEADS = 16
SWA_KV_HEADS = 4
SWA_HEAD_DIM = 64
SWA_GROUP = SWA_Q_HEADS // SWA_KV_HEADS
SWA_WIDTH = SWA_Q_HEADS * SWA_HEAD_DIM
SWA_KV_WIDTH = SWA_KV_HEADS * SWA_HEAD_DIM
WINDOW = 128
N_GROUPS = 4
EXPERTS_PER_GROUP = 8
N_EXPERTS = N_GROUPS * EXPERTS_PER_GROUP
TOP_K = 2
EXPERT_FF = 512

LANES = 128
BF16_ROWS = 16
VMEM_LIMIT = 56 * 1024 * 1024

_C_QKV = 0
_C_Z = _C_QKV + GDN_CONV_DIM
_C_QB = _C_Z + GDN_WIDTH
_C_KVB = _C_QB + SWA_WIDTH
_C_GA = _C_KVB + 2 * SWA_KV_WIDTH
_C_GB = _C_GA + D_MODEL
_C_AB = _C_GB + D_MODEL
IN_COLS = _C_AB + 512
PROJ_TILE = 512

ROW_TILE = 512
INPROJ_ROWS = 512
GDN_PREP_ROWS = 512
GDN_SCAN_ROWS = 512
GDN_SAMPLE_CHUNK = 16
GDN_SAMPLE_GROUP = 8
SWA_SAMPLE_GROUP = 8
SWA_BLOCKS_PER_STEP = 8
MOE_ROWS = 512
DISPATCH_ROWS = 2048


def _cparams(*sem):
    return pltpu.CompilerParams(dimension_semantics=sem, vmem_limit_bytes=VMEM_LIMIT)


def _bdot(a, b):
    return jnp.dot(a.astype(BF16), b.astype(BF16), preferred_element_type=F32)


def _bdot_nt(a, b):
    return lax.dot_general(a.astype(BF16), b.astype(BF16), (((1,), (1,)), ((), ())),
                           preferred_element_type=F32)


def _bdot_tn(a, b):
    return lax.dot_general(a.astype(BF16), b.astype(BF16), (((0,), (0,)), ((), ())),
                           preferred_element_type=F32)


def _sigmoid(x):
    return 1.0 / (1.0 + jnp.exp(-x))


def _silu(x):
    return x * _sigmoid(x)


def _rms(x, gain):
    return x * lax.rsqrt(jnp.mean(x * x, axis=-1, keepdims=True) + NORM_EPS) * gain


def _iota2(shape, dim):
    return lax.broadcasted_iota(I32, shape, dim)


def _adaln_kernel(c_ref, w_ref, b_ref, o_ref):
    o_ref[...] = _bdot(_silu(c_ref[...]), w_ref[...]) + b_ref[...]


def _adaln(c_all, w_ada, b_ada):
    rows = c_all.shape[0]
    n_out = w_ada.shape[1]
    tn = D_MODEL
    return pl.pallas_call(
        _adaln_kernel,
        grid=(n_out // tn,),
        in_specs=[pl.BlockSpec((rows, D_MODEL), lambda j: (0, 0)),
                  pl.BlockSpec((D_MODEL, tn), lambda j: (0, j)),
                  pl.BlockSpec((1, tn), lambda j: (0, j))],
        out_specs=pl.BlockSpec((rows, tn), lambda j: (0, j)),
        out_shape=jax.ShapeDtypeStruct((rows, n_out), F32),
        compiler_params=_cparams("arbitrary"),
        name="adaln",
    )(c_all, w_ada, b_ada.reshape(1, n_out))


def _inproj_kernel(tiles_per_seq, x_ref, g_ref, sc_ref, sh_ref, w_ref, *rest):
    if tiles_per_seq:
        cw_ref, qkv_ref, z_ref, qb_ref, kvb_ref, ga_ref, gb_ref, ab_ref, tail_ref, carry_ref = rest
    else:
        qkv_ref, z_ref, qb_ref, kvb_ref, ga_ref, gb_ref, ab_ref = rest
    tm = x_ref.shape[0]
    h = (_rms(x_ref[...], g_ref[...]) * (1.0 + sc_ref[0]) + sh_ref[0]).astype(BF16)

    tasks = []

    def fill(ref, c0, width, fn):
        step = min(PROJ_TILE, width)
        tasks.append([(ref, c0, c, step, fn) for c in range(0, width, step)])

    def run(ref, c0, c, step, fn):
        acc = jnp.dot(h, w_ref[:, c0 + c:c0 + c + step], preferred_element_type=F32)
        ref[:, c:c + step] = fn(acc, c, step).astype(ref.dtype)

    def conv_act(acc, c, step):
        cols = slice(c, c + step)
        seq_start = pl.program_id(0) % tiles_per_seq == 0
        prev = jnp.where(seq_start, 0.0, carry_ref[:, cols])
        last = acc[tm - 8:tm]
        carry_ref[:, cols] = last
        tail_ref[0, :, cols] = last
        ext = jnp.concatenate([prev, acc], axis=0)
        y = cw_ref[GDN_CONV - 1:GDN_CONV, cols] * acc
        for j in range(GDN_CONV - 1):
            y = y + cw_ref[j:j + 1, cols] * ext[8 - (GDN_CONV - 1) + j:8 - (GDN_CONV - 1) + j + tm]
        y = _silu(y)
        if c >= 2 * GDN_QK_WIDTH:
            return y
        scale = GDN_DK ** -0.5 if c < GDN_QK_WIDTH else 1.0
        heads = [_l2n(y[:, d:d + GDN_DK]) * scale for d in range(0, step, GDN_DK)]
        return jnp.concatenate(heads, axis=1)

    ident = lambda v, c, step: v
    silu = lambda v, c, step: _silu(v)
    sigmoid = lambda v, c, step: _sigmoid(v)
    fill(qkv_ref, _C_QKV, GDN_CONV_DIM, conv_act if tiles_per_seq else ident)
    fill(z_ref, _C_Z, GDN_WIDTH, silu)
    fill(qb_ref, _C_QB, SWA_WIDTH, ident)
    fill(kvb_ref, _C_KVB, 2 * SWA_KV_WIDTH, ident)
    fill(ga_ref, _C_GA, D_MODEL, sigmoid)
    fill(gb_ref, _C_GB, D_MODEL, sigmoid)
    fill(ab_ref, _C_AB, LANES, ident)
    heavy, light = tasks[0], [t for seg in tasks[1:] for t in seg]
    while heavy or light:
        for queue in (light, heavy):
            if queue:
                run(*queue.pop(0))


def _mod_spec(mod, n_tiles):
    tiles_per_mod = n_tiles // mod.shape[0]
    return pl.BlockSpec((1, mod.shape[1], D_MODEL), lambda i: (i // tiles_per_mod, 0, 0))


def _inproj(x2d, gain, sc, sh, w_prep, tm, conv_w=None, n_seq=0):
    rows = x2d.shape[0]
    n_tiles = rows // tm
    widths = (GDN_CONV_DIM, GDN_WIDTH, SWA_WIDTH, 2 * SWA_KV_WIDTH, D_MODEL, D_MODEL, LANES)
    dtypes = (BF16, BF16, BF16, BF16, BF16, BF16, F32)
    in_specs = [pl.BlockSpec((tm, D_MODEL), lambda i: (i, 0)),
                pl.BlockSpec((1, D_MODEL), lambda i: (0, 0)),
                _mod_spec(sc, n_tiles), _mod_spec(sh, n_tiles),
                pl.BlockSpec((D_MODEL, _C_AB + LANES), lambda i: (0, 0))]
    out_specs = [pl.BlockSpec((tm, w), lambda i: (i, 0)) for w in widths]
    out_shape = [jax.ShapeDtypeStruct((rows, w), dt) for w, dt in zip(widths, dtypes)]
    args = [x2d, gain.reshape(1, D_MODEL), sc, sh, w_prep]
    scratch = []
    tiles_per_seq = 0
    if conv_w is not None:
        tiles_per_seq = n_tiles // n_seq
        in_specs.append(pl.BlockSpec((GDN_CONV, GDN_CONV_DIM), lambda i: (0, 0)))
        out_specs.append(pl.BlockSpec((1, 8, GDN_CONV_DIM), lambda i: (i // tiles_per_seq, 0, 0)))
        out_shape.append(jax.ShapeDtypeStruct((n_seq, 8, GDN_CONV_DIM), F32))
        args.append(conv_w)
        scratch.append(pltpu.VMEM((8, GDN_CONV_DIM), F32))
    return pl.pallas_call(
        functools.partial(_inproj_kernel, tiles_per_seq),
        grid=(n_tiles,),
        in_specs=in_specs,
        out_specs=out_specs,
        out_shape=out_shape,
        scratch_shapes=scratch,
        compiler_params=_cparams("arbitrary"),
        name="inproj",
    )(*args)


def _cumsum_rows(g):
    c = g.shape[0]
    tril = (_iota2((c, c), 0) >= _iota2((c, c), 1)).astype(BF16)
    hi = g.astype(BF16)
    r1 = g - hi.astype(F32)
    mid = r1.astype(BF16)
    lo = (r1 - mid.astype(F32)).astype(BF16)
    dot = lambda p: jnp.dot(tril, p, preferred_element_type=F32)
    return dot(hi) + dot(mid) + dot(lo)


def _each(fn, *lists):
    return [fn(*args) for args in zip(*lists)]


def _pair_blockdiag(m, left):
    return jnp.concatenate([jnp.where(left, m, 0.0), jnp.where(left, 0.0, m)], axis=0)


def _unit_lower_inverse_offset(a_list, ii, jj, left):
    c = a_list[0].shape[0]

    def same_block(shift):
        return lax.shift_right_logical(ii, shift) == lax.shift_right_logical(jj, shift)

    base = same_block(1)
    n_list = _each(lambda a: jnp.where(base, -a, 0.0), a_list)
    shift = 1
    while (1 << shift) < c:
        outer, inner = same_block(shift + 1), same_block(shift)
        off_list = _each(lambda a: jnp.where(outer, jnp.where(inner, 0.0, a), 0.0), a_list)
        x_list = _each(lambda off, n: off + _bdot(off, _pair_blockdiag(n, left)), off_list, n_list)
        n_list = _each(lambda n, x: n - x - _bdot(n, _pair_blockdiag(x, left)), n_list, x_list)
        shift += 1
    return n_list


def _chunk_prep(q, k, v, gcol, grow, bcol):
    c = q[0].shape[0]
    assert len(q) % 2 == 0
    ii = _iota2((c, 2 * c), 0)
    lane = _iota2((c, 2 * c), 1)
    left = lane < c
    jj = lane & (c - 1)
    causal = ii >= jj
    strict = ii > jj
    first, second = slice(0, None, 2), slice(1, None, 2)
    kb = _each(lambda kk, b: kk * b, k, bcol)
    both = _each(lambda qa, ka, qb, kbb, x, y: _bdot_nt(jnp.concatenate([qa, ka, qb, kbb], axis=0),
                                                       jnp.concatenate([x, y], axis=0)),
                 q[first], kb[first], q[second], kb[second], k[first], k[second])
    decay = _each(lambda ga, gb, ra, rb: jnp.where(causal, jnp.exp(jnp.where(
        causal, jnp.where(left, ga, gb) - jnp.concatenate([ra, rb], axis=1), 0.0)), 0.0),
        gcol[first], gcol[second], grow[first], grow[second])
    qk = _each(lambda bo, d: jnp.where(left, bo[0:c], bo[2 * c:3 * c]) * d, both, decay)
    a = _each(lambda bo, d: jnp.where(strict, jnp.where(left, bo[c:2 * c], bo[3 * c:4 * c]) * d, 0.0), both, decay)
    n = _unit_lower_inverse_offset(a, ii, jj, left)
    eg = _each(jnp.exp, gcol)
    rhs = _each(lambda vv, b, kbb, e: jnp.concatenate([vv * b, kbb * e], axis=1), v, bcol, kb, eg)
    uw = _each(lambda ra, rb, nn: (lambda r: r + _bdot(_pair_blockdiag(nn, left), r))(
        jnp.concatenate([ra, rb], axis=0)), rhs[first], rhs[second], n)
    uw = [x[half] for x in uw for half in (slice(0, c), slice(c, 2 * c))]
    u = [x[:, :GDN_DV] for x in uw]
    w = [x[:, GDN_DV:] for x in uw]
    qd = _each(lambda qq, e: qq * e, q, eg)
    kd = _each(lambda kk, gc: kk * jnp.exp(gc[c - 1:c, :] - gc), k, gcol)
    return u, w, qd, kd, qk


def _chunk_step(s, u, w, qd, kd, qk, ge):
    c = u[0].shape[0]
    both = _each(lambda ww, qq, ss: _bdot(jnp.concatenate([ww, qq], axis=0), ss), w, qd, s)
    v_new = _each(lambda uu, bo: uu.astype(F32) - bo[:c], u, both)
    o = _each(lambda bo, m, vn: bo[c:] + _bdot(m, vn), both, qk, v_new)
    s_new = _each(lambda ss, g, kk, vn: ss * g + _bdot_tn(kk, vn), s, ge, kd, v_new)
    return o, s_new


def _conv_act(xp_ref, cw_ref, r0, rows, c0):
    cols = slice(c0, c0 + LANES)
    acc = cw_ref[3:4, cols] * xp_ref[r0:r0 + rows, cols]
    for j in range(GDN_CONV - 1):
        acc = acc + cw_ref[j:j + 1, cols] * xp_ref[r0 - 3 + j:r0 - 3 + j + rows, cols]
    return _silu(acc)


def _l2n(x):
    return x * lax.rsqrt(jnp.sum(x * x, axis=-1, keepdims=True) + NORM_EPS)


def _softplus(x):
    return jnp.maximum(x, 0.0) + jnp.log1p(jnp.exp(-jnp.abs(x)))


def _head_cols(hd):
    return (hd * GDN_DK, GDN_QK_WIDTH + hd * GDN_DK, 2 * GDN_QK_WIDTH + hd * GDN_DV)


def _activate_qkv(xp_ref, cw_ref, act_ref, r0, rows):
    for hd in range(GDN_HEADS):
        cq, ck, cv = _head_cols(hd)
        act_ref[0:rows, cq:cq + LANES] = _l2n(_conv_act(xp_ref, cw_ref, r0, rows, cq)) * (GDN_DK ** -0.5)
        act_ref[0:rows, ck:ck + LANES] = _l2n(_conv_act(xp_ref, cw_ref, r0, rows, ck))
        act_ref[0:rows, cv:cv + LANES] = _conv_act(xp_ref, cw_ref, r0, rows, cv)


def _decay_beta(ab, hp_ref):
    g = -jnp.exp(hp_ref[0:1, :]) * _softplus(ab + hp_ref[1:2, :])
    return g, _sigmoid(ab)


def _gdn_prep_kernel(act_ref, ab_ref, hp_ref, u_ref, w_ref, qd_ref, kd_ref, qk_ref, ge_ref):
    tb = act_ref.shape[1]
    cc = GDN_CHUNK
    g_all, beta_all = _decay_beta(ab_ref[0], hp_ref)

    chunks = [slice(ci * cc, (ci + 1) * cc) for ci in range(tb // cc)]
    gcs = _each(lambda rows: _cumsum_rows(g_all[rows, :]), chunks)
    gcts = _each(lambda gc: gc.T, gcs)
    for ci, gc in enumerate(gcs):
        ge_ref[0, ci] = jnp.exp(gc[cc - 1:cc, :])
    items = [(ci, hd) for ci in range(len(chunks)) for hd in range(GDN_HEADS)]
    col = lambda which: [act_ref[0, chunks[ci], _head_cols(hd)[which]:_head_cols(hd)[which] + LANES].astype(F32)
                         for ci, hd in items]
    u, w, qd, kd, qk = _chunk_prep(
        col(0), col(1), col(2),
        [gcs[ci][:, hd:hd + 1] for ci, hd in items], [gcts[ci][hd:hd + 1, :] for ci, hd in items],
        [beta_all[chunks[ci], GDN_HEADS + hd:GDN_HEADS + hd + 1] for ci, hd in items])
    for idx, (ci, hd) in enumerate(items):
        rows = chunks[ci]
        oc = slice(hd * GDN_DV, (hd + 1) * GDN_DV)
        u_ref[0, rows, oc] = u[idx].astype(BF16)
        w_ref[0, rows, oc] = w[idx].astype(BF16)
        qd_ref[0, rows, oc] = qd[idx].astype(BF16)
        kd_ref[0, rows, oc] = kd[idx].astype(BF16)
        if hd % 2 == 0:
            qk_ref[0, rows, hd * cc:(hd + 2) * cc] = qk[idx // 2].astype(BF16)


def _gdn_prep(qkv, ab, head_params):
    b, t, _ = qkv.shape
    tb = min(GDN_PREP_ROWS, t)
    nch = tb // GDN_CHUNK
    blk = lambda w: pl.BlockSpec((1, tb, w), lambda bi, i: (bi, i, 0))
    out_shapes = [jax.ShapeDtypeStruct((b, t, GDN_WIDTH), BF16)] * 4 + [
        jax.ShapeDtypeStruct((b, t, GDN_HEADS * GDN_CHUNK), BF16),
        jax.ShapeDtypeStruct((b, t // GDN_CHUNK, 1, LANES), F32)]
    return pl.pallas_call(
        _gdn_prep_kernel,
        grid=(b, t // tb),
        in_specs=[blk(GDN_CONV_DIM), blk(LANES), pl.BlockSpec((8, LANES), lambda bi, i: (0, 0))],
        out_specs=[blk(GDN_WIDTH)] * 4 + [
            blk(GDN_HEADS * GDN_CHUNK),
            pl.BlockSpec((1, nch, 1, LANES), lambda bi, i: (bi, i, 0, 0))],
        out_shape=out_shapes,
        compiler_params=_cparams("parallel", "parallel"),
        name="gdn_prep",
    )(qkv, ab, head_params)


def _gated_norm_store(o_ref, idx, o, gain, zs):
    o_ref[idx] = (_rms(o, gain) * zs.astype(F32)).astype(o_ref.dtype)


def _gdn_scan_kernel(u_ref, w_ref, qd_ref, kd_ref, qk_ref, ge_ref, zs_ref, gain_ref,
                     o_ref, s_out_ref, s_ref):
    nb, tb, _ = u_ref.shape
    cc = GDN_CHUNK
    step = pl.program_id(0)

    @pl.when(step == 0)
    def _():
        s_ref[...] = jnp.zeros_like(s_ref)

    gain = gain_ref[...]

    def chunk_body(ci, carry):
        rows = pl.ds(pl.multiple_of(ci * cc, cc), cc)
        items = [(bi, hd) for bi in range(nb) for hd in range(GDN_HEADS)]
        oc = lambda hd: slice(hd * GDN_DV, (hd + 1) * GDN_DV)
        ge_rows = [ge_ref[bi, ci] for bi in range(nb)]
        o, s_new = _chunk_step(
            [s_ref[bi * GDN_HEADS + hd] for bi, hd in items],
            [u_ref[bi, rows, oc(hd)] for bi, hd in items], [w_ref[bi, rows, oc(hd)] for bi, hd in items],
            [qd_ref[bi, rows, oc(hd)] for bi, hd in items], [kd_ref[bi, rows, oc(hd)] for bi, hd in items],
            [qk_ref[bi, rows, hd * cc:(hd + 1) * cc] for bi, hd in items],
            [ge_rows[bi][:, hd:hd + 1] for bi, hd in items])
        for idx, (bi, hd) in enumerate(items):
            s_ref[bi * GDN_HEADS + hd] = s_new[idx]
            _gated_norm_store(o_ref, (bi, rows, oc(hd)), o[idx], gain, zs_ref[bi, rows, oc(hd)])
        return carry

    lax.fori_loop(0, tb // cc, chunk_body, 0, unroll=2)

    @pl.when(step == pl.num_programs(0) - 1)
    def _():
        s_out_ref[...] = s_ref[...]


def _gdn_scan(u, w, qd, kd, qk, ge, zs, gain):
    b, t, _ = u.shape
    tb = min(GDN_SCAN_ROWS, t)
    nch = tb // GDN_CHUNK
    blk = lambda wd: pl.BlockSpec((b, tb, wd), lambda i: (0, i, 0))
    o, s = pl.pallas_call(
        _gdn_scan_kernel,
        grid=(t // tb,),
        in_specs=[blk(GDN_WIDTH)] * 4 + [
            blk(GDN_HEADS * GDN_CHUNK),
            pl.BlockSpec((b, nch, 1, LANES), lambda i: (0, i, 0, 0)),
            blk(GDN_WIDTH),
            pl.BlockSpec((1, GDN_DV), lambda i: (0, 0))],
        out_specs=[blk(GDN_WIDTH),
                   pl.BlockSpec((b * GDN_HEADS, GDN_DK, GDN_DV), lambda i: (0, 0, 0))],
        out_shape=[jax.ShapeDtypeStruct((b, t, GDN_WIDTH), BF16),
                   jax.ShapeDtypeStruct((b * GDN_HEADS, GDN_DK, GDN_DV), F32)],
        scratch_shapes=[pltpu.VMEM((b * GDN_HEADS, GDN_DK, GDN_DV), F32)],
        compiler_params=_cparams("arbitrary"),
        name="gdn_scan",
    )(u, w, qd, kd, qk, ge, zs, gain.reshape(1, GDN_DV))
    return o, s.reshape(b, GDN_HEADS, GDN_DK, GDN_DV)


def _gdn_sample_kernel(new_rows, xp_ref, ab_ref, zs_ref, s0_ref, cw_ref, hp_ref, gain_ref,
                       o_ref, s_out_ref, xs_ref, act_ref):
    grp = xp_ref.shape[0]
    cc = GDN_SAMPLE_CHUNK
    gain = gain_ref[...]
    rowmask = (_iota2((cc, 1), 0) >= cc - new_rows).astype(F32)
    seqs = list(range(grp))
    for bi in seqs:
        xs = xs_ref.at[bi]
        xs[0:8, :] = jnp.zeros((8, GDN_CONV_DIM), F32)
        xs[8:8 + cc, :] = xp_ref[bi].astype(F32)
        _activate_qkv(xs, cw_ref, act_ref.at[bi], 8, cc)
    gb = _each(lambda bi: _decay_beta(ab_ref[bi], hp_ref), seqs)
    gcs = _each(lambda x: _cumsum_rows(x[0] * rowmask), gb)
    gcts = _each(lambda gc: gc.T, gcs)
    ge_rows = _each(lambda gc: jnp.exp(gc[cc - 1:cc, :]), gcs)
    betas = _each(lambda x: x[1] * rowmask, gb)
    items = [(bi, hd) for bi in seqs for hd in range(GDN_HEADS)]
    col = lambda which: [act_ref[bi, :, _head_cols(hd)[which]:_head_cols(hd)[which] + LANES] * rowmask
                         for bi, hd in items]
    u, w, qd, kd, qk = _chunk_prep(
        col(0), col(1), col(2),
        [gcs[bi][:, hd:hd + 1] for bi, hd in items], [gcts[bi][hd:hd + 1, :] for bi, hd in items],
        [betas[bi][:, GDN_HEADS + hd:GDN_HEADS + hd + 1] for bi, hd in items])
    qk = [pair[:, half] for pair in qk for half in (slice(0, cc), slice(cc, 2 * cc))]
    o, s_new = _chunk_step([s0_ref[bi, hd] for bi, hd in items], u, w, qd, kd, qk,
                           [ge_rows[bi][:, hd:hd + 1] for bi, hd in items])
    for idx, (bi, hd) in enumerate(items):
        s_out_ref[bi, hd] = s_new[idx]
        oc = slice(hd * GDN_DV, (hd + 1) * GDN_DV)
        _gated_norm_store(o_ref, (bi, slice(None), oc), o[idx], gain, zs_ref[bi, :, oc])


def _gdn_sample(xp, ab, zs, s0, conv_w, head_params, gain, new_rows):
    b = xp.shape[0]
    cc = GDN_SAMPLE_CHUNK
    grp = GDN_SAMPLE_GROUP
    blk3 = lambda w: pl.BlockSpec((grp, cc, w), lambda i: (i, 0, 0))
    sblk = pl.BlockSpec((grp, GDN_HEADS, GDN_DK, GDN_DV), lambda i: (i, 0, 0, 0))
    return pl.pallas_call(
        functools.partial(_gdn_sample_kernel, new_rows),
        grid=(b // grp,),
        in_specs=[blk3(GDN_CONV_DIM), blk3(LANES), blk3(GDN_WIDTH), sblk,
                  pl.BlockSpec((GDN_CONV, GDN_CONV_DIM), lambda i: (0, 0)),
                  pl.BlockSpec((8, LANES), lambda i: (0, 0)),
                  pl.BlockSpec((1, GDN_DV), lambda i: (0, 0))],
        out_specs=[blk3(GDN_WIDTH), sblk],
        out_shape=[jax.ShapeDtypeStruct((b, cc, GDN_WIDTH), BF16),
                   jax.ShapeDtypeStruct((b, GDN_HEADS, GDN_DK, GDN_DV), F32)],
        scratch_shapes=[pltpu.VMEM((grp, cc + 8, GDN_CONV_DIM), F32),
                        pltpu.VMEM((grp, cc, GDN_CONV_DIM), F32)],
        compiler_params=_cparams("parallel"),
        name="gdn_sample",
    )(xp, ab, zs, s0, conv_w, head_params, gain.reshape(1, GDN_DV))


def _sink_attention(q, k, v, mask, sink_col):
    s = _each(lambda qq, kk: jnp.where(mask, _bdot_nt(qq, kk) * (SWA_HEAD_DIM ** -0.5), -jnp.inf), q, k)
    m = _each(lambda ss, sk: jnp.maximum(jnp.max(ss, axis=-1, keepdims=True), sk), s, sink_col)
    p = _each(lambda ss, mm: jnp.exp(ss - mm), s, m)
    denom = _each(lambda pp, sk, mm: jnp.sum(pp, axis=-1, keepdims=True) + jnp.exp(sk - mm), p, sink_col, m)
    return _each(lambda pp, vv, dd: _bdot(pp, vv) / dd, p, v, denom)


def _sink_column(sinks_ref, kv_head, rows_per_head):
    parts = [jnp.full((rows_per_head, 1), sinks_ref[kv_head * SWA_GROUP + g], F32) for g in range(SWA_GROUP)]
    return jnp.concatenate(parts, axis=0)


def _swa_prompt_kernel(sinks_ref, q_ref, kvp_ref, kvc_ref, o_ref):
    wnd = WINDOW
    nblk = q_ref.shape[1] // wnd
    step = pl.program_id(1)
    kv = jnp.concatenate([kvp_ref[0], kvc_ref[0]], axis=0)
    cols = SWA_GROUP * wnd
    kj = _iota2((2 * wnd, cols), 0)
    qi = _iota2((2 * wnd, cols), 1) & (wnd - 1)
    dist = qi + wnd - kj
    band = (dist >= 0) & (dist <= wnd)
    first_key = jnp.where(step > 0, 0, wnd)
    masks = [band & (kj >= first_key)] + [band] * (nblk - 1)
    head_cols = lambda h: slice(h * SWA_HEAD_DIM, (h + 1) * SWA_HEAD_DIM)
    q_heads = lambda hk: [hk * SWA_GROUP + g for g in range(SWA_GROUP)]
    scale = SWA_HEAD_DIM ** -0.5
    items = [(j, hk) for j in range(nblk) for hk in range(SWA_KV_HEADS)]
    q_rows = lambda j: slice(j * wnd, (j + 1) * wnd)
    k_rows = lambda j: slice(j * wnd, (j + 2) * wnd)
    q = [jnp.concatenate([q_ref[0, q_rows(j), head_cols(h)] for h in q_heads(hk)], axis=0) for j, hk in items]
    k = [kv[k_rows(j), head_cols(hk)] for j, hk in items]
    v = [kv[k_rows(j), SWA_KV_WIDTH + hk * SWA_HEAD_DIM:SWA_KV_WIDTH + (hk + 1) * SWA_HEAD_DIM] for j, hk in items]
    sinks_kv = [jnp.concatenate([jnp.full((1, wnd), sinks_ref[h], F32) for h in q_heads(hk)], axis=1)
                for hk in range(SWA_KV_HEADS)]
    sink = [sinks_kv[hk] for j, hk in items]
    mask = [masks[j] for j, hk in items]
    s = _each(lambda kk, qq, mk: jnp.where(mk, _bdot_nt(kk, qq) * scale, -jnp.inf), k, q, mask)
    m = _each(lambda ss, sk: jnp.maximum(jnp.max(ss, axis=0, keepdims=True), sk), s, sink)
    p = _each(lambda ss, mm: jnp.exp(ss - mm), s, m)
    denom = _each(lambda pp, sk, mm: jnp.sum(pp, axis=0, keepdims=True) + jnp.exp(sk - mm), p, sink, m)
    ot = _each(lambda vv, pp, dd: _bdot_tn(vv, pp) / dd, v, p, denom)
    for idx, (j, hk) in enumerate(items):
        for g in range(0, SWA_GROUP, 2):
            pair = jnp.concatenate([ot[idx][:, g * wnd:(g + 1) * wnd], ot[idx][:, (g + 1) * wnd:(g + 2) * wnd]],
                                   axis=0)
            h0 = hk * SWA_GROUP + g
            o_ref[0, q_rows(j), h0 * SWA_HEAD_DIM:(h0 + 2) * SWA_HEAD_DIM] = pair.T.astype(o_ref.dtype)


def _swa_prompt(q, kv, sinks):
    b, t, _ = q.shape
    nblk = min(SWA_BLOCKS_PER_STEP, t // WINDOW)
    rows = nblk * WINDOW
    assert t % rows == 0
    return pl.pallas_call(
        _swa_prompt_kernel,
        grid=(b, t // rows),
        in_specs=[pl.BlockSpec(memory_space=pltpu.SMEM),
                  pl.BlockSpec((1, rows, SWA_WIDTH), lambda bi, i: (bi, i, 0)),
                  pl.BlockSpec((1, WINDOW, 2 * SWA_KV_WIDTH), lambda bi, i: (bi, jnp.maximum(i * nblk - 1, 0), 0)),
                  pl.BlockSpec((1, rows, 2 * SWA_KV_WIDTH), lambda bi, i: (bi, i, 0))],
        out_specs=pl.BlockSpec((1, rows, SWA_WIDTH), lambda bi, i: (bi, i, 0)),
        out_shape=jax.ShapeDtypeStruct((b, t, SWA_WIDTH), BF16),
        compiler_params=_cparams("parallel", "parallel"),
        name="swa_prompt",
    )(sinks, q, kv, kv)


def _swa_sample_kernel(sinks_ref, q_ref, kvn_ref, kc_ref, vc_ref, o_ref, ko_ref, vo_ref):
    grp, t, _ = q_ref.shape
    wnd = WINDOW
    nk = wnd + BF16_ROWS
    rows = SWA_GROUP * t
    tq = _iota2((rows, nk), 0) & (t - 1)
    kj = _iota2((rows, nk), 1)
    dist = tq + wnd - kj
    mask = (dist >= 0) & (dist <= wnd)
    zpad = jnp.zeros((BF16_ROWS - t, SWA_KV_WIDTH), F32)
    kks, vvs = [], []
    for bi in range(grp):
        kvn = kvn_ref[bi].astype(F32)
        kk = jnp.concatenate([kc_ref[bi], kvn[:, :SWA_KV_WIDTH], zpad], axis=0)
        vv = jnp.concatenate([vc_ref[bi], kvn[:, SWA_KV_WIDTH:], zpad], axis=0)
        ko_ref[bi] = kk[t:t + wnd, :]
        vo_ref[bi] = vv[t:t + wnd, :]
        kks.append(kk)
        vvs.append(vv)
    items = [(bi, hk) for bi in range(grp) for hk in range(SWA_KV_HEADS)]
    head_cols = lambda h: slice(h * SWA_HEAD_DIM, (h + 1) * SWA_HEAD_DIM)
    q_heads = lambda hk: [hk * SWA_GROUP + g for g in range(SWA_GROUP)]
    sink_cols = [_sink_column(sinks_ref, hk, t) for hk in range(SWA_KV_HEADS)]
    o = _sink_attention(
        [jnp.concatenate([q_ref[bi, :, head_cols(h)] for h in q_heads(hk)], axis=0) for bi, hk in items],
        [kks[bi][:, head_cols(hk)] for bi, hk in items], [vvs[bi][:, head_cols(hk)] for bi, hk in items],
        mask, [sink_cols[hk] for bi, hk in items])
    for idx, (bi, hk) in enumerate(items):
        for g, h in enumerate(q_heads(hk)):
            o_ref[bi, :, head_cols(h)] = o[idx][g * t:(g + 1) * t].astype(o_ref.dtype)


def _swa_sample(q, kv_new, k_cache, v_cache, sinks):
    b, t, _ = q.shape
    grp = SWA_SAMPLE_GROUP
    blk = lambda r, w: pl.BlockSpec((grp, r, w), lambda i: (i, 0, 0))
    return pl.pallas_call(
        _swa_sample_kernel,
        grid=(b // grp,),
        in_specs=[pl.BlockSpec(memory_space=pltpu.SMEM),
                  blk(t, SWA_WIDTH), blk(t, 2 * SWA_KV_WIDTH), blk(WINDOW, SWA_KV_WIDTH), blk(WINDOW, SWA_KV_WIDTH)],
        out_specs=[blk(t, SWA_WIDTH), blk(WINDOW, SWA_KV_WIDTH), blk(WINDOW, SWA_KV_WIDTH)],
        out_shape=[jax.ShapeDtypeStruct((b, t, SWA_WIDTH), BF16),
                   jax.ShapeDtypeStruct((b, WINDOW, SWA_KV_WIDTH), F32),
                   jax.ShapeDtypeStruct((b, WINDOW, SWA_KV_WIDTH), F32)],
        compiler_params=_cparams("parallel"),
        name="swa_sample",
    )(sinks, q, kv_new, k_cache, v_cache)


def _route(logits):
    lane = _iota2(logits.shape, 1).astype(F32)
    neg = -jnp.inf

    def first_argmax(vals, valid):
        v = jnp.where(valid, vals, neg)
        m = jnp.max(v, axis=-1, keepdims=True)
        idx = jnp.min(jnp.where(jnp.logical_and(valid, v == m), lane, float(LANES)), axis=-1, keepdims=True)
        return m, idx

    is_group = lane < N_GROUPS
    gmax, gidx = first_argmax(logits, is_group)
    p_group = 1.0 / jnp.sum(jnp.where(is_group, jnp.exp(logits - gmax), 0.0), axis=-1, keepdims=True)
    lo = N_GROUPS + gidx * EXPERTS_PER_GROUP
    in_group = jnp.logical_and(lane >= lo, lane < lo + EXPERTS_PER_GROUP)
    m1, i1 = first_argmax(logits, in_group)
    esum = jnp.sum(jnp.where(in_group, jnp.exp(logits - m1), 0.0), axis=-1, keepdims=True)
    m2, i2 = first_argmax(logits, jnp.logical_and(in_group, lane != i1))
    p1 = 1.0 / esum
    p2 = jnp.exp(m2 - m1) / esum
    tot = p1 + p2
    return i1 - N_GROUPS, i2 - N_GROUPS, p_group * p1 / tot, p_group * p2 / tot


def _post_mixer_kernel(oa_ref, ob_ref, ga_ref, gb_ref, x_ref, gt_ref, sc_ref, sh_ref,
                       wa_ref, wb_ref, wo_ref, gpost_ref, gpre_ref, wr_ref, br_ref, cnt0_ref,
                       x1_ref, h2_ref, rt_ref, cnt_out_ref, meta_ref, *rest):
    cnt_ref = rest[-1]
    if len(rest) == 2:
        rest[0][...] = jnp.zeros_like(rest[0])
    step = pl.program_id(0)

    @pl.when(step == 0)
    def _():
        cnt_ref[...] = cnt0_ref[...]

    merged = (ga_ref[...].astype(F32) * jnp.dot(oa_ref[...], wa_ref[...], preferred_element_type=F32)
              + gb_ref[...].astype(F32) * jnp.dot(ob_ref[...], wb_ref[...], preferred_element_type=F32))
    mix = _bdot(merged, wo_ref[...])
    x1 = x_ref[...] + gt_ref[0] * _rms(mix, gpost_ref[...])
    x1_ref[...] = x1
    h2 = _rms(x1, gpre_ref[...]) * (1.0 + sc_ref[0]) + sh_ref[0]
    _rows_to_tiles(h2_ref, h2)
    h_hi = h2.astype(BF16)
    h_lo = (h2 - h_hi.astype(F32)).astype(BF16)
    part = jnp.dot(h_hi, wr_ref[...], preferred_element_type=F32)
    logits = (part[:, :LANES] + part[:, LANES:]
              + jnp.dot(h_lo, wr_ref[:, :LANES], preferred_element_type=F32) + br_ref[...])
    ia, ib, wa, wb = _route(logits)
    lane = _iota2(logits.shape, 1)
    tm = logits.shape[0]
    lane_f = lane.astype(F32)
    hot_a = (lane_f == ia).astype(F32)
    hot_b = (lane_f == ib).astype(F32)
    hot = hot_a + hot_b
    earlier = (_iota2((tm, tm), 0) > _iota2((tm, tm), 1)).astype(BF16)
    before = jnp.dot(earlier, hot.astype(BF16), preferred_element_type=F32) + cnt_ref[...]
    rank_a = jnp.sum(hot_a * before, axis=-1, keepdims=True)
    rank_b = jnp.sum(hot_b * before, axis=-1, keepdims=True)
    cnt_ref[...] = cnt_ref[...] + jnp.sum(hot, axis=0, keepdims=True)
    cnt_out_ref[...] = cnt_ref[...]
    rt_ref[...] = jnp.where(lane == 0, ia, jnp.where(lane == 1, ib, jnp.where(lane == 2, wa, jnp.where(
        lane == 3, wb, 0.0))))
    packed = jnp.where(lane == 0, rank_a * N_EXPERTS + ia, jnp.where(lane == 1, rank_b * N_EXPERTS + ib, 0.0))
    meta_ref[0] = packed.T[0:TOP_K, :].astype(I32)


def _post_mixer(oa, ob, ga, gb, x2d, gt, sc, sh, w_a, w_b, w_o, g_post, g_pre, w_rt, b_rt, cnt0, tm,
                zero_rows=0):
    rows = x2d.shape[0]
    n_tiles = rows // tm
    row_blk = lambda w: pl.BlockSpec((tm, w), lambda i: (i, 0))
    full = lambda r, c: pl.BlockSpec((r, c), lambda i: (0, 0))
    out_specs = [row_blk(D_MODEL), pl.BlockSpec((tm * TILE_ROWS, LANES), lambda i: (i, 0)), row_blk(LANES),
                 full(1, LANES), pl.BlockSpec((1, TOP_K, tm), lambda i: (i, 0, 0))]
    out_shape = [jax.ShapeDtypeStruct((rows, D_MODEL), F32),
                 jax.ShapeDtypeStruct((rows * TILE_ROWS, LANES), F32),
                 jax.ShapeDtypeStruct((rows, LANES), F32),
                 jax.ShapeDtypeStruct((1, LANES), F32),
                 jax.ShapeDtypeStruct((n_tiles, TOP_K, tm), I32)]
    if zero_rows:
        assert zero_rows % (n_tiles * TILE_ROWS) == 0
        out_specs.append(pl.BlockSpec((zero_rows // n_tiles, LANES), lambda i: (i, 0)))
        out_shape.append(jax.ShapeDtypeStruct((zero_rows, LANES), F32))
    return pl.pallas_call(
        _post_mixer_kernel,
        grid=(n_tiles,),
        in_specs=[row_blk(GDN_WIDTH), row_blk(SWA_WIDTH), row_blk(D_MODEL), row_blk(D_MODEL), row_blk(D_MODEL),
                  _mod_spec(gt, n_tiles), _mod_spec(sc, n_tiles), _mod_spec(sh, n_tiles),
                  full(GDN_WIDTH, D_MODEL), full(SWA_WIDTH, D_MODEL), full(D_MODEL, D_MODEL),
                  full(1, D_MODEL), full(1, D_MODEL), full(D_MODEL, 2 * LANES), full(1, LANES), full(1, LANES)],
        out_specs=out_specs,
        out_shape=out_shape,
        scratch_shapes=[pltpu.VMEM((1, LANES), F32)],
        compiler_params=_cparams("arbitrary"),
        name="post_mixer",
    )(oa, ob, ga, gb, x2d, gt, sc, sh, w_a, w_b, w_o,
      g_post.reshape(1, D_MODEL), g_pre.reshape(1, D_MODEL), w_rt, b_rt, cnt0)


TILE_ROWS = D_MODEL // LANES


def _tiles_to_rows(ref, first, rows):
    base = first * TILE_ROWS
    return jnp.concatenate([ref[pl.ds(base + c, rows, stride=TILE_ROWS), :] for c in range(TILE_ROWS)], axis=1)


def _rows_to_tiles(ref, mat):
    rows = mat.shape[0]
    for c in range(TILE_ROWS):
        ref[pl.ds(c, rows, stride=TILE_ROWS), :] = mat[:, c * LANES:(c + 1) * LANES]


def _tile_copy_loop(n, copies, start):
    def body(t, carry):
        for j, cp in enumerate(copies(t)):
            if start:
                cp.start(priority=j % 2)
            else:
                cp.wait()
        return carry

    lax.fori_loop(0, n, body, 0, unroll=8)


def _slots_kernel(pstart_ref, meta_ref, o_ref):
    packed = meta_ref[...]
    expert = packed & (N_EXPERTS - 1)
    first = jnp.zeros_like(packed)
    for e in range(N_EXPERTS):
        first = jnp.where(expert == e, pstart_ref[e], first)
    o_ref[...] = first + lax.shift_right_logical(packed, N_EXPERTS.bit_length() - 1)


def _slots(meta, pstarts):
    return pl.pallas_call(
        _slots_kernel,
        grid_spec=pltpu.PrefetchScalarGridSpec(
            num_scalar_prefetch=1,
            grid=(1,),
            in_specs=[pl.BlockSpec(meta.shape, lambda i, p: (0, 0, 0))],
            out_specs=pl.BlockSpec(meta.shape, lambda i, p: (0, 0, 0))),
        out_shape=jax.ShapeDtypeStruct(meta.shape, I32),
        compiler_params=_cparams("arbitrary"),
        name="moe_slots",
    )(pstarts, meta)


def _dispatch_kernel(slot_ref, h_ref, xs_in_hbm, xs_hbm, sem):
    del xs_in_hbm
    i = pl.program_id(0)
    tm = h_ref.shape[0] // TILE_ROWS
    sub_tiles = tm // ROW_TILE

    for start in (True, False):
        for j in range(sub_tiles):
            base = (i * sub_tiles + j) * TOP_K * ROW_TILE

            def copies(t, j=j, base=base):
                row = pl.multiple_of((j * ROW_TILE + t) * TILE_ROWS, TILE_ROWS)
                return [pltpu.make_async_copy(h_ref.at[pl.ds(row, TILE_ROWS)],
                                              xs_hbm.at[slot_ref[base + k * ROW_TILE + t]], sem)
                        for k in range(TOP_K)]

            _tile_copy_loop(ROW_TILE, copies, start)


def _dispatch(slots, h_tiles, xs, tm):
    n_tiles = h_tiles.shape[0] // (tm * TILE_ROWS)
    return pl.pallas_call(
        _dispatch_kernel,
        grid_spec=pltpu.PrefetchScalarGridSpec(
            num_scalar_prefetch=1,
            grid=(n_tiles,),
            in_specs=[pl.BlockSpec((tm * TILE_ROWS, LANES), lambda i, s: (i, 0)),
                      pl.BlockSpec(memory_space=pl.ANY)],
            out_specs=pl.BlockSpec(memory_space=pl.ANY),
            scratch_shapes=[pltpu.SemaphoreType.DMA(())]),
        out_shape=jax.ShapeDtypeStruct(xs.shape, xs.dtype),
        input_output_aliases={2: 0},
        compiler_params=_cparams("arbitrary"),
        name="moe_dispatch",
    )(slots, h_tiles, xs)


def _moe_kernel(blk_e_ref, n_used_ref, x_ref, wg_ref, wu_ref, wd_ref, y_ref, wgb, wub, wdb):
    b = pl.program_id(0)
    rows = x_ref.shape[0] // TILE_ROWS
    changed = jnp.logical_or(b == 0, blk_e_ref[b] != blk_e_ref[jnp.maximum(b - 1, 0)])

    @pl.when(changed)
    def _():
        wgb[...] = wg_ref[0].astype(BF16)
        wub[...] = wu_ref[0].astype(BF16)
        wdb[...] = wd_ref[0].astype(BF16)

    @pl.when(b < n_used_ref[0])
    def _():
        x = _tiles_to_rows(x_ref, 0, rows).astype(BF16)
        gate = jnp.dot(x, wgb[...], preferred_element_type=F32)
        up = jnp.dot(x, wub[...], preferred_element_type=F32)
        _rows_to_tiles(y_ref, _bdot(_silu(gate) * up, wdb[...]))

    @pl.when(b >= n_used_ref[0])
    def _():
        y_ref[...] = jnp.zeros_like(y_ref)


def _moe(xs_tiles, blk_e, n_used, w_gate, w_up, w_down):
    n_blocks = blk_e.shape[0]
    rows = MOE_ROWS
    wspec = lambda r, c: pl.BlockSpec((1, r, c), lambda b, be, nu: (be[b], 0, 0))
    xspec = pl.BlockSpec((rows * TILE_ROWS, LANES), lambda b, be, nu: (b, 0))
    return pl.pallas_call(
        _moe_kernel,
        grid_spec=pltpu.PrefetchScalarGridSpec(
            num_scalar_prefetch=2,
            grid=(n_blocks,),
            in_specs=[xspec, wspec(D_MODEL, EXPERT_FF), wspec(D_MODEL, EXPERT_FF), wspec(EXPERT_FF, D_MODEL)],
            out_specs=xspec,
            scratch_shapes=[pltpu.VMEM((D_MODEL, EXPERT_FF), BF16),
                            pltpu.VMEM((D_MODEL, EXPERT_FF), BF16),
                            pltpu.VMEM((EXPERT_FF, D_MODEL), BF16)]),
        out_shape=jax.ShapeDtypeStruct(xs_tiles.shape, F32),
        compiler_params=_cparams("arbitrary"),
        name="moe_experts",
    )(blk_e, n_used, xs_tiles, w_gate, w_up, w_down)


def _combine_kernel(slot_ref, y_hbm, x1_ref, rt_ref, gt_ref, gpost_ref, o_ref, ybuf, sems):
    i = pl.program_id(0)
    n = pl.num_programs(0)
    rows = ybuf.shape[1] // TILE_ROWS
    slot = i % 2

    def gather(step, buf_slot, start):
        def copy(r):
            dst = ybuf.at[buf_slot, pl.ds(pl.multiple_of(r * TILE_ROWS, TILE_ROWS), TILE_ROWS)]
            return pltpu.make_async_copy(y_hbm.at[slot_ref[step * rows + r]], dst, sems.at[buf_slot])

        _tile_copy_loop(rows // 2, lambda t: [copy(2 * t), copy(2 * t + 1)], start)

    @pl.when(i == 0)
    def _():
        gather(0, 0, True)

    @pl.when(i + 1 < n)
    def _():
        gather(i + 1, 1 - slot, True)

    gather(i, slot, False)
    half = rows // 2
    rt = rt_ref[...]
    buf = ybuf.at[slot]
    f = rt[:, 2:3] * _tiles_to_rows(buf, 0, half) + rt[:, 3:4] * _tiles_to_rows(buf, half, half)
    o_ref[...] = x1_ref[...] + gt_ref[0] * _rms(f, gpost_ref[...])


def _combine(slots, yb, x1, rt, gt, g_post):
    rows = x1.shape[0]
    tm = ROW_TILE
    n_tiles = rows // tm
    tiles_per_mod = n_tiles // gt.shape[0]
    return pl.pallas_call(
        _combine_kernel,
        grid_spec=pltpu.PrefetchScalarGridSpec(
            num_scalar_prefetch=1,
            grid=(n_tiles,),
            in_specs=[pl.BlockSpec(memory_space=pl.ANY),
                      pl.BlockSpec((tm, D_MODEL), lambda i, s: (i, 0)),
                      pl.BlockSpec((tm, LANES), lambda i, s: (i, 0)),
                      pl.BlockSpec((1, gt.shape[1], D_MODEL), lambda i, s: (i // tiles_per_mod, 0, 0)),
                      pl.BlockSpec((1, D_MODEL), lambda i, s: (0, 0))],
            out_specs=pl.BlockSpec((tm, D_MODEL), lambda i, s: (i, 0)),
            scratch_shapes=[pltpu.VMEM((2, TOP_K * tm * TILE_ROWS, LANES), F32),
                            pltpu.SemaphoreType.DMA((2,))]),
        out_shape=jax.ShapeDtypeStruct((rows, D_MODEL), F32),
        compiler_params=_cparams("arbitrary"),
        name="moe_combine",
    )(slots, yb, x1, rt, gt, g_post.reshape(1, D_MODEL))


def _dispatch_plan(counts, n_tok):
    pcounts = (counts + MOE_ROWS - 1) // MOE_ROWS * MOE_ROWS
    pends = jnp.cumsum(pcounts)
    pstarts = (pends - pcounts).astype(I32)
    n_blocks = n_tok * TOP_K // MOE_ROWS + N_EXPERTS
    blk_start = jnp.arange(n_blocks, dtype=I32) * MOE_ROWS
    blk_e = jnp.minimum(jnp.sum(blk_start[:, None] >= pends[None, :], axis=1), N_EXPERTS - 1).astype(I32)
    n_used = (pends[-1] // MOE_ROWS).astype(I32).reshape(1)
    return blk_e, n_used, pstarts


_AB_FIRST = 4 * GDN_QK_WIDTH
_AB_COUNT = 2 * GDN_HEADS


def _prep_in_weight_kernel(w_ref, o_ref):
    x = w_ref[...]
    last = pl.program_id(0) == pl.num_programs(0) - 1
    keep = jnp.logical_or(jnp.logical_not(last), _iota2(x.shape, 0) < _AB_COUNT)
    o_ref[...] = jnp.where(keep, x, 0.0).T.astype(BF16)


def _prep_in_weight(w_in_t):
    n_blocks = IN_COLS // PROJ_TILE
    shifted_from = _AB_FIRST // PROJ_TILE

    def src_row(i):
        return jnp.where(i == n_blocks - 1, _AB_FIRST,
                         jnp.where(i >= shifted_from, i * PROJ_TILE + _AB_COUNT, i * PROJ_TILE))

    return pl.pallas_call(
        _prep_in_weight_kernel,
        grid=(n_blocks,),
        in_specs=[pl.BlockSpec((pl.Element(PROJ_TILE), pl.Element(D_MODEL)),
                               lambda i: (pl.multiple_of(src_row(i), _AB_COUNT), 0))],
        out_specs=pl.BlockSpec((D_MODEL, PROJ_TILE), lambda i: (0, i)),
        out_shape=jax.ShapeDtypeStruct((D_MODEL, IN_COLS), BF16),
        compiler_params=_cparams("parallel"),
        name="prep_in_weight",
    )(w_in_t)


def _head_param_tile(a_log, dt_bias):
    tile = jnp.zeros((8, LANES), F32)
    return tile.at[0, :GDN_HEADS].set(a_log.astype(F32)).at[1, :GDN_HEADS].set(dt_bias.astype(F32))


def _router_weight(w_group, b_group, w_router, b_router):
    w = jnp.zeros((D_MODEL, LANES), F32)
    w = w.at[:, :N_GROUPS].set(w_group).at[:, N_GROUPS:N_GROUPS + N_EXPERTS].set(w_router)
    b = jnp.zeros((1, LANES), F32)
    b = b.at[0, :N_GROUPS].set(b_group).at[0, N_GROUPS:N_GROUPS + N_EXPERTS].set(b_router)
    w_hi = w.astype(BF16)
    w_lo = (w - w_hi.astype(F32)).astype(BF16)
    return jnp.concatenate([w_hi, w_lo], axis=1), b


def kernel(x_prompt, x_sample, state_gdn, state_conv, cache_k_win, cache_v_win, c_prompt, c_sample, w_ada, b_ada, g_mix_pre, g_mix_post, g_ffn_pre, g_ffn_post, w_in, conv_w, a_log, dt_bias, gdn_norm, sinks, w_br_gdn, w_br_swa, w_out, w_group, b_group, w_router, b_router, w_gate, w_up, w_down):
    depth = w_ada.shape[0]
    assert depth == 1, "single-layer trunk"
    bp, tp, _ = x_prompt.shape
    bs, ts, _ = x_sample.shape
    n_p = bp * tp
    n_s = bs * ts
    tm = ROW_TILE
    assert tp % tm == 0 and n_s % tm == 0 and ts >= GDN_CONV - 1 and ts + GDN_CONV - 1 <= GDN_SAMPLE_CHUNK
    assert ts & (ts - 1) == 0 and ts <= BF16_ROWS

    c_all = jnp.concatenate([c_prompt, c_sample], axis=0)
    c_rows = -(-c_all.shape[0] // 8) * 8
    c_all = jnp.pad(c_all, ((0, c_rows - c_all.shape[0]), (0, 0)))
    mod = _adaln(c_all, w_ada[0], b_ada[0])
    mods_p = [m[:bp].reshape(bp, 1, D_MODEL) for m in jnp.split(mod, 6, axis=-1)]
    mods_s = [jnp.repeat(m[bp:bp + bs], ts, axis=0).reshape(n_s // tm, tm, D_MODEL)
              for m in jnp.split(mod, 6, axis=-1)]

    w_prep = _prep_in_weight(jnp.swapaxes(w_in[0], 0, 1))
    head_params = _head_param_tile(a_log[0], dt_bias[0])
    w_a, w_b, w_o = w_br_gdn[0].astype(BF16), w_br_swa[0].astype(BF16), w_out[0].astype(BF16)
    w_rt, b_rt = _router_weight(w_group[0], b_group[0], w_router[0], b_router[0])
    sinks0 = sinks[0].astype(F32)

    xp2d = x_prompt.reshape(n_p, D_MODEL)
    sh1, sc1, gt1, sh2, sc2, gt2 = mods_p
    qkv_p, zs_p, qb_p, kvb_p, ga_p, gb_p, ab_p, conv_tail_p = _inproj(
        xp2d, g_mix_pre[0], sc1, sh1, w_prep, INPROJ_ROWS, conv_w=conv_w[0].astype(F32), n_seq=bp)
    qkv_p3 = qkv_p.reshape(bp, tp, GDN_CONV_DIM)
    u, w, qd, kd, qk, ge = _gdn_prep(qkv_p3, ab_p.reshape(bp, tp, LANES), head_params)
    oa_p, s_prompt = _gdn_scan(u, w, qd, kd, qk, ge, zs_p.reshape(bp, tp, GDN_WIDTH), gdn_norm[0])
    kvb_p3 = kvb_p.reshape(bp, tp, 2 * SWA_KV_WIDTH)
    ob_p = _swa_prompt(qb_p.reshape(bp, tp, SWA_WIDTH), kvb_p3, sinks0)
    n_slots = ((n_p + n_s) * TOP_K // MOE_ROWS + N_EXPERTS) * MOE_ROWS
    x1_p, h2_p, rt_p, cnt_p, meta_p, xs = _post_mixer(
        oa_p.reshape(n_p, GDN_WIDTH), ob_p.reshape(n_p, SWA_WIDTH), ga_p, gb_p, xp2d, gt1, sc2, sh2,
        w_a, w_b, w_o, g_mix_post[0], g_ffn_pre[0], w_rt, b_rt, jnp.zeros((1, LANES), F32), tm,
        zero_rows=n_slots * TILE_ROWS)

    xs2d = x_sample.reshape(n_s, D_MODEL)
    sh1s, sc1s, gt1s, sh2s, sc2s, gt2s = mods_s
    qkv_s, zs_s, qb_s, kvb_s, ga_s, gb_s, ab_s = _inproj(xs2d, g_mix_pre[0], sc1s, sh1s, w_prep, tm)
    cc = GDN_SAMPLE_CHUNK
    pad_rows = cc - ts - (GDN_CONV - 1)
    qkv_s3 = qkv_s.reshape(bs, ts, GDN_CONV_DIM)
    xp_s = jnp.concatenate([jnp.zeros((bs, pad_rows, GDN_CONV_DIM), BF16), state_conv[0].astype(BF16), qkv_s3],
                           axis=1)
    front = lambda a: jnp.pad(a, ((0, 0), (cc - ts, 0), (0, 0)))
    oa_s16, s_sample = _gdn_sample(xp_s, front(ab_s.reshape(bs, ts, LANES)), front(zs_s.reshape(bs, ts, GDN_WIDTH)),
                                   state_gdn[0].astype(F32), conv_w[0], head_params, gdn_norm[0], ts)
    oa_s = oa_s16[:, cc - ts:, :].reshape(n_s, GDN_WIDTH)
    ob_s, k_new_s, v_new_s = _swa_sample(
        qb_s.reshape(bs, ts, SWA_WIDTH), kvb_s.reshape(bs, ts, 2 * SWA_KV_WIDTH),
        cache_k_win[0].reshape(bs, WINDOW, SWA_KV_WIDTH).astype(F32),
        cache_v_win[0].reshape(bs, WINDOW, SWA_KV_WIDTH).astype(F32), sinks0)
    x1_s, h2_s, rt_s, cnt_all, meta_s = _post_mixer(
        oa_s, ob_s.reshape(n_s, SWA_WIDTH), ga_s, gb_s, xs2d, gt1s, sc2s, sh2s,
        w_a, w_b, w_o, g_mix_post[0], g_ffn_pre[0], w_rt, b_rt, cnt_p, tm)

    blk_e, n_used, pstarts = _dispatch_plan(cnt_all[0, :N_EXPERTS].astype(I32), n_p + n_s)
    slots_p = _slots(meta_p, pstarts).reshape(-1)
    slots_s = _slots(meta_s, pstarts).reshape(-1)
    xs = _dispatch(slots_p, h2_p, xs.reshape(n_slots, TILE_ROWS, LANES), min(DISPATCH_ROWS, n_p))
    xs = _dispatch(slots_s, h2_s, xs, min(DISPATCH_ROWS, n_s))
    yb = _moe(xs.reshape(n_slots * TILE_ROWS, LANES), blk_e, n_used, w_gate[0], w_up[0], w_down[0])
    yb = yb.reshape(n_slots, TILE_ROWS, LANES)
    y_p = _combine(slots_p, yb, x1_p, rt_p, gt2, g_ffn_post[0])
    y_s = _combine(slots_s, yb, x1_s, rt_s, gt2s, g_ffn_post[0])

    f32 = lambda a: a.astype(F32)
    kv_tail = kvb_p3[:, tp - WINDOW:, :]
    kv_heads = lambda a: f32(a).reshape(a.shape[0], WINDOW, SWA_KV_HEADS, SWA_HEAD_DIM)[None]
    return (y_p.reshape(bp, tp, D_MODEL), y_s.reshape(bs, ts, D_MODEL),
            s_prompt[None], conv_tail_p[:, 8 - (GDN_CONV - 1):, :][None],
            kv_heads(kv_tail[:, :, :SWA_KV_WIDTH]), kv_heads(kv_tail[:, :, SWA_KV_WIDTH:]),
            s_sample[None], f32(qkv_s3[:, ts - (GDN_CONV - 1):, :])[None],
            kv_heads(k_new_s), kv_heads(v_new_s))
```

```python
import functools

import jax
import jax.numpy as jnp
from jax import lax
from jax.experimental import pallas as pl
from jax.experimental.pallas import tpu as pltpu

F32 = jnp.float32
BF16 = jnp.bfloat16
I32 = jnp.int32

D_MODEL = 1024
NORM_EPS = 1e-6
GDN_HEADS = 8
GDN_DK = 128
GDN_DV = 128
GDN_CONV = 4
GDN_CHUNK = 64
GDN_QK_WIDTH = GDN_HEADS * GDN_DK
GDN_WIDTH = GDN_HEADS * GDN_DV
GDN_CONV_DIM = 2 * GDN_QK_WIDTH + GDN_WIDTH
SWA_Q_HEADS = 16
SWA_KV_HEADS = 4
SWA_HEAD_DIM = 64
SWA_GROUP = SWA_Q_HEADS // SWA_KV_HEADS
SWA_WIDTH = SWA_Q_HEADS * SWA_HEAD_DIM
SWA_KV_WIDTH = SWA_KV_HEADS * SWA_HEAD_DIM
WINDOW = 128
N_GROUPS = 4
EXPERTS_PER_GROUP = 8
N_EXPERTS = N_GROUPS * EXPERTS_PER_GROUP
TOP_K = 2
EXPERT_FF = 512

LANES = 128
BF16_ROWS = 16
VMEM_LIMIT = 56 * 1024 * 1024

_C_QKV = 0
_C_Z = _C_QKV + GDN_CONV_DIM
_C_QB = _C_Z + GDN_WIDTH
_C_KVB = _C_QB + SWA_WIDTH
_C_GA = _C_KVB + 2 * SWA_KV_WIDTH
_C_GB = _C_GA + D_MODEL
_C_AB = _C_GB + D_MODEL
IN_COLS = _C_AB + 512
PROJ_TILE = 512

ROW_TILE = 512
INPROJ_ROWS = 512
GDN_PREP_ROWS = 512
GDN_SCAN_ROWS = 512
GDN_SAMPLE_CHUNK = 16
GDN_SAMPLE_GROUP = 8
SWA_SAMPLE_GROUP = 16
SWA_BLOCKS_PER_STEP = 8
MOE_ROWS = 512
DISPATCH_ROWS = 2048


def _cparams(*sem):
    return pltpu.CompilerParams(dimension_semantics=sem, vmem_limit_bytes=VMEM_LIMIT)


def _bdot(a, b):
    return jnp.dot(a.astype(BF16), b.astype(BF16), preferred_element_type=F32)


def _bdot_nt(a, b):
    return lax.dot_general(a.astype(BF16), b.astype(BF16), (((1,), (1,)), ((), ())),
                           preferred_element_type=F32)


def _bdot_tn(a, b):
    return lax.dot_general(a.astype(BF16), b.astype(BF16), (((0,), (0,)), ((), ())),
                           preferred_element_type=F32)


def _sigmoid(x):
    return 1.0 / (1.0 + jnp.exp(-x))


def _silu(x):
    return x * _sigmoid(x)


def _rms(x, gain):
    return x * lax.rsqrt(jnp.mean(x * x, axis=-1, keepdims=True) + NORM_EPS) * gain


def _iota2(shape, dim):
    return lax.broadcasted_iota(I32, shape, dim)


def _adaln_kernel(c_ref, w_ref, b_ref, o_ref):
    o_ref[...] = _bdot(_silu(c_ref[...]), w_ref[...]) + b_ref[...]


def _adaln(c_all, w_ada, b_ada):
    rows = c_all.shape[0]
    n_out = w_ada.shape[1]
    tn = D_MODEL
    return pl.pallas_call(
        _adaln_kernel,
        grid=(n_out // tn,),
        in_specs=[pl.BlockSpec((rows, D_MODEL), lambda j: (0, 0)),
                  pl.BlockSpec((D_MODEL, tn), lambda j: (0, j)),
                  pl.BlockSpec((1, tn), lambda j: (0, j))],
        out_specs=pl.BlockSpec((rows, tn), lambda j: (0, j)),
        out_shape=jax.ShapeDtypeStruct((rows, n_out), F32),
        compiler_params=_cparams("arbitrary"),
        name="adaln",
    )(c_all, w_ada, b_ada.reshape(1, n_out))


def _inproj_kernel(tiles_per_seq, x_ref, g_ref, sc_ref, sh_ref, w_ref, *rest):
    if tiles_per_seq:
        cw_ref, qkv_ref, z_ref, qb_ref, kvb_ref, ga_ref, gb_ref, ab_ref, tail_ref, carry_ref = rest
    else:
        qkv_ref, z_ref, qb_ref, kvb_ref, ga_ref, gb_ref, ab_ref = rest
    tm = x_ref.shape[0]
    h = (_rms(x_ref[...], g_ref[...]) * (1.0 + sc_ref[0]) + sh_ref[0]).astype(BF16)

    tasks = []

    def fill(ref, c0, width, fn):
        step = min(PROJ_TILE, width)
        tasks.append([(ref, c0, c, step, fn) for c in range(0, width, step)])

    def run(ref, c0, c, step, fn):
        acc = jnp.dot(h, w_ref[:, c0 + c:c0 + c + step], preferred_element_type=F32)
        ref[:, c:c + step] = fn(acc, c, step).astype(ref.dtype)

    def conv_act(acc, c, step):
        cols = slice(c, c + step)
        seq_start = pl.program_id(0) % tiles_per_seq == 0
        prev = jnp.where(seq_start, 0.0, carry_ref[:, cols])
        last = acc[tm - 8:tm]
        carry_ref[:, cols] = last
        tail_ref[0, :, cols] = last
        ext = jnp.concatenate([prev, acc], axis=0)
        w0, w1, w2, w3 = (cw_ref[j:j + 1, cols] for j in range(GDN_CONV))
        ext2 = pltpu.roll(ext, 2, 0)
        odd = pltpu.roll(w2 * ext + w0 * ext2, 1, 0)
        y = _silu(w3 * acc + w1 * ext2[8:] + odd[8:])
        if c >= 2 * GDN_QK_WIDTH:
            return y
        scale = GDN_DK ** -0.5 if c < GDN_QK_WIDTH else 1.0
        heads = [_l2n(y[:, d:d + GDN_DK]) * scale for d in range(0, step, GDN_DK)]
        return jnp.concatenate(heads, axis=1)

    ident = lambda v, c, step: v
    silu = lambda v, c, step: _silu(v)
    sigmoid = lambda v, c, step: _sigmoid(v)
    fill(qkv_ref, _C_QKV, GDN_CONV_DIM, conv_act if tiles_per_seq else ident)
    fill(z_ref, _C_Z, GDN_WIDTH, silu)
    fill(qb_ref, _C_QB, SWA_WIDTH, ident)
    fill(kvb_ref, _C_KVB, 2 * SWA_KV_WIDTH, ident)
    fill(ga_ref, _C_GA, D_MODEL, sigmoid)
    fill(gb_ref, _C_GB, D_MODEL, sigmoid)
    fill(ab_ref, _C_AB, LANES, ident)
    heavy, light = tasks[0], [t for seg in tasks[1:] for t in seg]
    while heavy or light:
        for queue in (light, heavy):
            if queue:
                run(*queue.pop(0))


def _mod_spec(mod, n_tiles):
    tiles_per_mod = n_tiles // mod.shape[0]
    return pl.BlockSpec((1, mod.shape[1], D_MODEL), lambda i: (i // tiles_per_mod, 0, 0))


def _inproj(x2d, gain, sc, sh, w_prep, tm, conv_w=None, n_seq=0):
    rows = x2d.shape[0]
    n_tiles = rows // tm
    widths = (GDN_CONV_DIM, GDN_WIDTH, SWA_WIDTH, 2 * SWA_KV_WIDTH, D_MODEL, D_MODEL, LANES)
    dtypes = (BF16, BF16, BF16, BF16, BF16, BF16, F32)
    in_specs = [pl.BlockSpec((tm, D_MODEL), lambda i: (i, 0)),
                pl.BlockSpec((1, D_MODEL), lambda i: (0, 0)),
                _mod_spec(sc, n_tiles), _mod_spec(sh, n_tiles),
                pl.BlockSpec((D_MODEL, _C_AB + LANES), lambda i: (0, 0))]
    out_specs = [pl.BlockSpec((tm, w), lambda i: (i, 0)) for w in widths]
    out_shape = [jax.ShapeDtypeStruct((rows, w), dt) for w, dt in zip(widths, dtypes)]
    args = [x2d, gain.reshape(1, D_MODEL), sc, sh, w_prep]
    scratch = []
    tiles_per_seq = 0
    if conv_w is not None:
        tiles_per_seq = n_tiles // n_seq
        in_specs.append(pl.BlockSpec((GDN_CONV, GDN_CONV_DIM), lambda i: (0, 0)))
        out_specs.append(pl.BlockSpec((1, 8, GDN_CONV_DIM), lambda i: (i // tiles_per_seq, 0, 0)))
        out_shape.append(jax.ShapeDtypeStruct((n_seq, 8, GDN_CONV_DIM), F32))
        args.append(conv_w)
        scratch.append(pltpu.VMEM((8, GDN_CONV_DIM), F32))
    return pl.pallas_call(
        functools.partial(_inproj_kernel, tiles_per_seq),
        grid=(n_tiles,),
        in_specs=in_specs,
        out_specs=out_specs,
        out_shape=out_shape,
        scratch_shapes=scratch,
        compiler_params=_cparams("arbitrary"),
        name="inproj",
    )(*args)


def _cumsum_rows(g):
    c = g.shape[0]
    tril = (_iota2((c, c), 0) >= _iota2((c, c), 1)).astype(BF16)
    hi = g.astype(BF16)
    r1 = g - hi.astype(F32)
    mid = r1.astype(BF16)
    lo = (r1 - mid.astype(F32)).astype(BF16)
    dot = lambda p: jnp.dot(tril, p, preferred_element_type=F32)
    return dot(hi) + dot(mid) + dot(lo)


def _each(fn, *lists):
    return [fn(*args) for args in zip(*lists)]


def _pair_blockdiag(m, left):
    return jnp.concatenate([jnp.where(left, m, 0.0), jnp.where(left, 0.0, m)], axis=0)


def _unit_lower_inverse_offset(a_list, ii, jj, left):
    c = a_list[0].shape[0]

    def same_block(shift):
        return lax.shift_right_logical(ii, shift) == lax.shift_right_logical(jj, shift)

    base = same_block(1)
    n_list = _each(lambda a: jnp.where(base, -a, 0.0), a_list)
    shift = 1
    while (1 << shift) < c:
        outer, inner = same_block(shift + 1), same_block(shift)
        off_list = _each(lambda a: jnp.where(outer, jnp.where(inner, 0.0, a), 0.0), a_list)
        x_list = _each(lambda off, n: off + _bdot(off, _pair_blockdiag(n, left)), off_list, n_list)
        n_list = _each(lambda n, x: n - x - _bdot(n, _pair_blockdiag(x, left)), n_list, x_list)
        shift += 1
    return n_list


def _chunk_prep(q, k, v, gcol, grow, bcol):
    c = q[0].shape[0]
    assert len(q) % 2 == 0
    ii = _iota2((c, 2 * c), 0)
    lane = _iota2((c, 2 * c), 1)
    left = lane < c
    jj = lane & (c - 1)
    causal = ii >= jj
    strict = ii > jj
    first, second = slice(0, None, 2), slice(1, None, 2)
    kb = _each(lambda kk, b: kk * b, k, bcol)
    both = _each(lambda qa, ka, qb, kbb, x, y: _bdot_nt(jnp.concatenate([qa, ka, qb, kbb], axis=0),
                                                       jnp.concatenate([x, y], axis=0)),
                 q[first], kb[first], q[second], kb[second], k[first], k[second])
    decay = _each(lambda ga, gb, ra, rb: jnp.where(causal, jnp.exp(jnp.where(
        causal, jnp.where(left, ga, gb) - jnp.concatenate([ra, rb], axis=1), 0.0)), 0.0),
        gcol[first], gcol[second], grow[first], grow[second])
    qk = _each(lambda bo, d: jnp.where(left, bo[0:c], bo[2 * c:3 * c]) * d, both, decay)
    a = _each(lambda bo, d: jnp.where(strict, jnp.where(left, bo[c:2 * c], bo[3 * c:4 * c]) * d, 0.0), both, decay)
    n = _unit_lower_inverse_offset(a, ii, jj, left)
    eg = _each(jnp.exp, gcol)
    rhs = _each(lambda vv, b, kbb, e: jnp.concatenate([vv * b, kbb * e], axis=1), v, bcol, kb, eg)
    uw = _each(lambda ra, rb, nn: (lambda r: r + _bdot(_pair_blockdiag(nn, left), r))(
        jnp.concatenate([ra, rb], axis=0)), rhs[first], rhs[second], n)
    uw = [x[half] for x in uw for half in (slice(0, c), slice(c, 2 * c))]
    u = [x[:, :GDN_DV] for x in uw]
    w = [x[:, GDN_DV:] for x in uw]
    qd = _each(lambda qq, e: qq * e, q, eg)
    kd = _each(lambda kk, gc: kk * jnp.exp(gc[c - 1:c, :] - gc), k, gcol)
    return u, w, qd, kd, qk


def _chunk_step(s, u, w, qd, kd, qk, ge):
    c = u[0].shape[0]
    both = _each(lambda ww, qq, ss: _bdot(jnp.concatenate([ww, qq], axis=0), ss), w, qd, s)
    v_new = _each(lambda uu, bo: uu.astype(F32) - bo[:c], u, both)
    o = _each(lambda bo, m, vn: bo[c:] + _bdot(m, vn), both, qk, v_new)
    s_new = _each(lambda ss, g, kk, vn: ss * g + _bdot_tn(kk, vn), s, ge, kd, v_new)
    return o, s_new


def _conv_act(xp_ref, cw_ref, r0, rows, c0):
    cols = slice(c0, c0 + LANES)
    acc = cw_ref[3:4, cols] * xp_ref[r0:r0 + rows, cols]
    for j in range(GDN_CONV - 1):
        acc = acc + cw_ref[j:j + 1, cols] * xp_ref[r0 - 3 + j:r0 - 3 + j + rows, cols]
    return _silu(acc)


def _l2n(x):
    return x * lax.rsqrt(jnp.sum(x * x, axis=-1, keepdims=True) + NORM_EPS)


def _softplus(x):
    return jnp.maximum(x, 0.0) + jnp.log1p(jnp.exp(-jnp.abs(x)))


def _head_cols(hd):
    return (hd * GDN_DK, GDN_QK_WIDTH + hd * GDN_DK, 2 * GDN_QK_WIDTH + hd * GDN_DV)


def _activate_qkv(xp_ref, cw_ref, act_ref, r0, rows):
    for hd in range(GDN_HEADS):
        cq, ck, cv = _head_cols(hd)
        act_ref[0:rows, cq:cq + LANES] = _l2n(_conv_act(xp_ref, cw_ref, r0, rows, cq)) * (GDN_DK ** -0.5)
        act_ref[0:rows, ck:ck + LANES] = _l2n(_conv_act(xp_ref, cw_ref, r0, rows, ck))
        act_ref[0:rows, cv:cv + LANES] = _conv_act(xp_ref, cw_ref, r0, rows, cv)


def _decay_beta(ab, hp_ref):
    g = -jnp.exp(hp_ref[0:1, :]) * _softplus(ab + hp_ref[1:2, :])
    return g, _sigmoid(ab)


def _gdn_prep_kernel(act_ref, ab_ref, hp_ref, u_ref, w_ref, qd_ref, kd_ref, qk_ref, ge_ref):
    tb = act_ref.shape[1]
    cc = GDN_CHUNK
    g_all, beta_all = _decay_beta(ab_ref[0], hp_ref)

    chunks = [slice(ci * cc, (ci + 1) * cc) for ci in range(tb // cc)]
    gcs = _each(lambda rows: _cumsum_rows(g_all[rows, :]), chunks)
    gcts = _each(lambda gc: gc.T, gcs)
    for ci, gc in enumerate(gcs):
        ge_ref[0, ci] = jnp.exp(gc[cc - 1:cc, :])
    items = [(ci, hd) for ci in range(len(chunks)) for hd in range(GDN_HEADS)]
    col = lambda which: [act_ref[0, chunks[ci], _head_cols(hd)[which]:_head_cols(hd)[which] + LANES].astype(F32)
                         for ci, hd in items]
    u, w, qd, kd, qk = _chunk_prep(
        col(0), col(1), col(2),
        [gcs[ci][:, hd:hd + 1] for ci, hd in items], [gcts[ci][hd:hd + 1, :] for ci, hd in items],
        [beta_all[chunks[ci], GDN_HEADS + hd:GDN_HEADS + hd + 1] for ci, hd in items])
    for idx, (ci, hd) in enumerate(items):
        rows = chunks[ci]
        oc = slice(hd * GDN_DV, (hd + 1) * GDN_DV)
        u_ref[0, rows, oc] = u[idx].astype(BF16)
        w_ref[0, rows, oc] = w[idx].astype(BF16)
        qd_ref[0, rows, oc] = qd[idx].astype(BF16)
        kd_ref[0, rows, oc] = kd[idx].astype(BF16)
        if hd % 2 == 0:
            qk_ref[0, rows, hd * cc:(hd + 2) * cc] = qk[idx // 2].astype(BF16)


def _gdn_prep(qkv, ab, head_params):
    b, t, _ = qkv.shape
    tb = min(GDN_PREP_ROWS, t)
    nch = tb // GDN_CHUNK
    blk = lambda w: pl.BlockSpec((1, tb, w), lambda bi, i: (bi, i, 0))
    out_shapes = [jax.ShapeDtypeStruct((b, t, GDN_WIDTH), BF16)] * 4 + [
        jax.ShapeDtypeStruct((b, t, GDN_HEADS * GDN_CHUNK), BF16),
        jax.ShapeDtypeStruct((b, t // GDN_CHUNK, 1, LANES), F32)]
    return pl.pallas_call(
        _gdn_prep_kernel,
        grid=(b, t // tb),
        in_specs=[blk(GDN_CONV_DIM), blk(LANES), pl.BlockSpec((8, LANES), lambda bi, i: (0, 0))],
        out_specs=[blk(GDN_WIDTH)] * 4 + [
            blk(GDN_HEADS * GDN_CHUNK),
            pl.BlockSpec((1, nch, 1, LANES), lambda bi, i: (bi, i, 0, 0))],
        out_shape=out_shapes,
        compiler_params=_cparams("parallel", "parallel"),
        name="gdn_prep",
    )(qkv, ab, head_params)


def _gated_norm_store(o_ref, idx, o, gain, zs):
    o_ref[idx] = (_rms(o, gain) * zs.astype(F32)).astype(o_ref.dtype)


def _gdn_scan_kernel(u_ref, w_ref, qd_ref, kd_ref, qk_ref, ge_ref, zs_ref, gain_ref,
                     o_ref, s_out_ref, s_ref):
    nb, tb, _ = u_ref.shape
    cc = GDN_CHUNK
    step = pl.program_id(0)

    @pl.when(step == 0)
    def _():
        s_ref[...] = jnp.zeros_like(s_ref)

    gain = gain_ref[...]

    def chunk_body(ci, carry):
        rows = pl.ds(pl.multiple_of(ci * cc, cc), cc)
        items = [(bi, hd) for bi in range(nb) for hd in range(GDN_HEADS)]
        oc = lambda hd: slice(hd * GDN_DV, (hd + 1) * GDN_DV)
        ge_rows = [ge_ref[bi, ci] for bi in range(nb)]
        o, s_new = _chunk_step(
            [s_ref[bi * GDN_HEADS + hd] for bi, hd in items],
            [u_ref[bi, rows, oc(hd)] for bi, hd in items], [w_ref[bi, rows, oc(hd)] for bi, hd in items],
            [qd_ref[bi, rows, oc(hd)] for bi, hd in items], [kd_ref[bi, rows, oc(hd)] for bi, hd in items],
            [qk_ref[bi, rows, hd * cc:(hd + 1) * cc] for bi, hd in items],
            [ge_rows[bi][:, hd:hd + 1] for bi, hd in items])
        for idx, (bi, hd) in enumerate(items):
            s_ref[bi * GDN_HEADS + hd] = s_new[idx]
            _gated_norm_store(o_ref, (bi, rows, oc(hd)), o[idx], gain, zs_ref[bi, rows, oc(hd)])
        return carry

    lax.fori_loop(0, tb // cc, chunk_body, 0, unroll=2)

    @pl.when(step == pl.num_programs(0) - 1)
    def _():
        s_out_ref[...] = s_ref[...]


def _gdn_scan(u, w, qd, kd, qk, ge, zs, gain):
    b, t, _ = u.shape
    tb = min(GDN_SCAN_ROWS, t)
    nch = tb // GDN_CHUNK
    blk = lambda wd: pl.BlockSpec((b, tb, wd), lambda i: (0, i, 0))
    o, s = pl.pallas_call(
        _gdn_scan_kernel,
        grid=(t // tb,),
        in_specs=[blk(GDN_WIDTH)] * 4 + [
            blk(GDN_HEADS * GDN_CHUNK),
            pl.BlockSpec((b, nch, 1, LANES), lambda i: (0, i, 0, 0)),
            blk(GDN_WIDTH),
            pl.BlockSpec((1, GDN_DV), lambda i: (0, 0))],
        out_specs=[blk(GDN_WIDTH),
                   pl.BlockSpec((b * GDN_HEADS, GDN_DK, GDN_DV), lambda i: (0, 0, 0))],
        out_shape=[jax.ShapeDtypeStruct((b, t, GDN_WIDTH), BF16),
                   jax.ShapeDtypeStruct((b * GDN_HEADS, GDN_DK, GDN_DV), F32)],
        scratch_shapes=[pltpu.VMEM((b * GDN_HEADS, GDN_DK, GDN_DV), F32)],
        compiler_params=_cparams("arbitrary"),
        name="gdn_scan",
    )(u, w, qd, kd, qk, ge, zs, gain.reshape(1, GDN_DV))
    return o, s.reshape(b, GDN_HEADS, GDN_DK, GDN_DV)


def _gdn_sample_kernel(new_rows, xp_ref, ab_ref, zs_ref, s0_ref, cw_ref, hp_ref, gain_ref,
                       o_ref, s_out_ref, xs_ref, act_ref):
    grp = xp_ref.shape[0]
    cc = GDN_SAMPLE_CHUNK
    gain = gain_ref[...]
    rowmask = (_iota2((cc, 1), 0) >= cc - new_rows).astype(F32)
    seqs = list(range(grp))
    for bi in seqs:
        xs = xs_ref.at[bi]
        xs[0:8, :] = jnp.zeros((8, GDN_CONV_DIM), F32)
        xs[8:8 + cc, :] = xp_ref[bi].astype(F32)
        _activate_qkv(xs, cw_ref, act_ref.at[bi], 8, cc)
    gb = _each(lambda bi: _decay_beta(ab_ref[bi], hp_ref), seqs)
    gcs = _each(lambda x: _cumsum_rows(x[0] * rowmask), gb)
    gcts = _each(lambda gc: gc.T, gcs)
    ge_rows = _each(lambda gc: jnp.exp(gc[cc - 1:cc, :]), gcs)
    betas = _each(lambda x: x[1] * rowmask, gb)
    items = [(bi, hd) for bi in seqs for hd in range(GDN_HEADS)]
    col = lambda which: [act_ref[bi, :, _head_cols(hd)[which]:_head_cols(hd)[which] + LANES] * rowmask
                         for bi, hd in items]
    u, w, qd, kd, qk = _chunk_prep(
        col(0), col(1), col(2),
        [gcs[bi][:, hd:hd + 1] for bi, hd in items], [gcts[bi][hd:hd + 1, :] for bi, hd in items],
        [betas[bi][:, GDN_HEADS + hd:GDN_HEADS + hd + 1] for bi, hd in items])
    qk = [pair[:, half] for pair in qk for half in (slice(0, cc), slice(cc, 2 * cc))]
    o, s_new = _chunk_step([s0_ref[bi, hd] for bi, hd in items], u, w, qd, kd, qk,
                           [ge_rows[bi][:, hd:hd + 1] for bi, hd in items])
    for idx, (bi, hd) in enumerate(items):
        s_out_ref[bi, hd] = s_new[idx]
        oc = slice(hd * GDN_DV, (hd + 1) * GDN_DV)
        _gated_norm_store(o_ref, (bi, slice(None), oc), o[idx], gain, zs_ref[bi, :, oc])


def _gdn_sample(xp, ab, zs, s0, conv_w, head_params, gain, new_rows):
    b = xp.shape[0]
    cc = GDN_SAMPLE_CHUNK
    grp = GDN_SAMPLE_GROUP
    blk3 = lambda w: pl.BlockSpec((grp, cc, w), lambda i: (i, 0, 0))
    sblk = pl.BlockSpec((grp, GDN_HEADS, GDN_DK, GDN_DV), lambda i: (i, 0, 0, 0))
    return pl.pallas_call(
        functools.partial(_gdn_sample_kernel, new_rows),
        grid=(b // grp,),
        in_specs=[blk3(GDN_CONV_DIM), blk3(LANES), blk3(GDN_WIDTH), sblk,
                  pl.BlockSpec((GDN_CONV, GDN_CONV_DIM), lambda i: (0, 0)),
                  pl.BlockSpec((8, LANES), lambda i: (0, 0)),
                  pl.BlockSpec((1, GDN_DV), lambda i: (0, 0))],
        out_specs=[blk3(GDN_WIDTH), sblk],
        out_shape=[jax.ShapeDtypeStruct((b, cc, GDN_WIDTH), BF16),
                   jax.ShapeDtypeStruct((b, GDN_HEADS, GDN_DK, GDN_DV), F32)],
        scratch_shapes=[pltpu.VMEM((grp, cc + 8, GDN_CONV_DIM), F32),
                        pltpu.VMEM((grp, cc, GDN_CONV_DIM), F32)],
        compiler_params=_cparams("parallel"),
        name="gdn_sample",
    )(xp, ab, zs, s0, conv_w, head_params, gain.reshape(1, GDN_DV))


def _sink_attention(q, k, v, mask, sink_col):
    s = _each(lambda qq, kk: jnp.where(mask, _bdot_nt(qq, kk) * (SWA_HEAD_DIM ** -0.5), -jnp.inf), q, k)
    m = _each(lambda ss, sk: jnp.maximum(jnp.max(ss, axis=-1, keepdims=True), sk), s, sink_col)
    p = _each(lambda ss, mm: jnp.exp(ss - mm), s, m)
    denom = _each(lambda pp, sk, mm: jnp.sum(pp, axis=-1, keepdims=True) + jnp.exp(sk - mm), p, sink_col, m)
    return _each(lambda pp, vv, dd: _bdot(pp, vv) / dd, p, v, denom)


def _sink_column(sinks_ref, kv_head, rows_per_head):
    parts = [jnp.full((rows_per_head, 1), sinks_ref[kv_head * SWA_GROUP + g], F32) for g in range(SWA_GROUP)]
    return jnp.concatenate(parts, axis=0)


def _swa_prompt_kernel(sinks_ref, q_ref, kvp_ref, kvc_ref, o_ref):
    wnd = WINDOW
    nblk = q_ref.shape[1] // wnd
    step = pl.program_id(1)
    kv = jnp.concatenate([kvp_ref[0], kvc_ref[0]], axis=0)
    cols = SWA_GROUP * wnd
    kj = _iota2((2 * wnd, cols), 0)
    qi = _iota2((2 * wnd, cols), 1) & (wnd - 1)
    dist = qi + wnd - kj
    band = (dist >= 0) & (dist <= wnd)
    first_key = jnp.where(step > 0, 0, wnd)
    masks = [band & (kj >= first_key)] + [band] * (nblk - 1)
    head_cols = lambda h: slice(h * SWA_HEAD_DIM, (h + 1) * SWA_HEAD_DIM)
    q_heads = lambda hk: [hk * SWA_GROUP + g for g in range(SWA_GROUP)]
    scale = SWA_HEAD_DIM ** -0.5
    items = [(j, hk) for j in range(nblk) for hk in range(SWA_KV_HEADS)]
    q_rows = lambda j: slice(j * wnd, (j + 1) * wnd)
    k_rows = lambda j: slice(j * wnd, (j + 2) * wnd)
    q = [jnp.concatenate([q_ref[0, q_rows(j), head_cols(h)] for h in q_heads(hk)], axis=0) for j, hk in items]
    k = [kv[k_rows(j), head_cols(hk)] for j, hk in items]
    v = [kv[k_rows(j), SWA_KV_WIDTH + hk * SWA_HEAD_DIM:SWA_KV_WIDTH + (hk + 1) * SWA_HEAD_DIM] for j, hk in items]
    sinks_kv = [jnp.concatenate([jnp.full((1, wnd), sinks_ref[h], F32) for h in q_heads(hk)], axis=1)
                for hk in range(SWA_KV_HEADS)]
    sink = [sinks_kv[hk] for j, hk in items]
    mask = [masks[j] for j, hk in items]
    s = _each(lambda kk, qq, mk: jnp.where(mk, _bdot_nt(kk, qq) * scale, -jnp.inf), k, q, mask)
    m = _each(lambda ss, sk: jnp.maximum(jnp.max(ss, axis=0, keepdims=True), sk), s, sink)
    p = _each(lambda ss, mm: jnp.exp(ss - mm), s, m)
    denom = _each(lambda pp, sk, mm: jnp.sum(pp, axis=0, keepdims=True) + jnp.exp(sk - mm), p, sink, m)
    ot = _each(lambda vv, pp, dd: _bdot_tn(vv, pp) / dd, v, p, denom)
    for idx, (j, hk) in enumerate(items):
        for g in range(0, SWA_GROUP, 2):
            pair = jnp.concatenate([ot[idx][:, g * wnd:(g + 1) * wnd], ot[idx][:, (g + 1) * wnd:(g + 2) * wnd]],
                                   axis=0)
            h0 = hk * SWA_GROUP + g
            o_ref[0, q_rows(j), h0 * SWA_HEAD_DIM:(h0 + 2) * SWA_HEAD_DIM] = pair.T.astype(o_ref.dtype)


def _swa_prompt(q, kv, sinks):
    b, t, _ = q.shape
    nblk = min(SWA_BLOCKS_PER_STEP, t // WINDOW)
    rows = nblk * WINDOW
    assert t % rows == 0
    return pl.pallas_call(
        _swa_prompt_kernel,
        grid=(b, t // rows),
        in_specs=[pl.BlockSpec(memory_space=pltpu.SMEM),
                  pl.BlockSpec((1, rows, SWA_WIDTH), lambda bi, i: (bi, i, 0)),
                  pl.BlockSpec((1, WINDOW, 2 * SWA_KV_WIDTH), lambda bi, i: (bi, jnp.maximum(i * nblk - 1, 0), 0)),
                  pl.BlockSpec((1, rows, 2 * SWA_KV_WIDTH), lambda bi, i: (bi, i, 0))],
        out_specs=pl.BlockSpec((1, rows, SWA_WIDTH), lambda bi, i: (bi, i, 0)),
        out_shape=jax.ShapeDtypeStruct((b, t, SWA_WIDTH), BF16),
        compiler_params=_cparams("parallel", "parallel"),
        name="swa_prompt",
    )(sinks, q, kv, kv)


def _swa_sample_kernel(sinks_ref, q_ref, kvn_ref, kc_ref, vc_ref, o_ref, ko_ref, vo_ref):
    grp, t, _ = q_ref.shape
    wnd = WINDOW
    nk = wnd + BF16_ROWS
    rows = SWA_GROUP * t
    tq = _iota2((rows, nk), 0) & (t - 1)
    kj = _iota2((rows, nk), 1)
    dist = tq + wnd - kj
    mask = (dist >= 0) & (dist <= wnd)
    zpad = jnp.zeros((BF16_ROWS - t, SWA_KV_WIDTH), F32)
    kks, vvs = [], []
    for bi in range(grp):
        kvn = kvn_ref[bi].astype(F32)
        kk = jnp.concatenate([kc_ref[bi], kvn[:, :SWA_KV_WIDTH], zpad], axis=0)
        vv = jnp.concatenate([vc_ref[bi], kvn[:, SWA_KV_WIDTH:], zpad], axis=0)
        ko_ref[bi] = kk[t:t + wnd, :]
        vo_ref[bi] = vv[t:t + wnd, :]
        kks.append(kk)
        vvs.append(vv)
    items = [(bi, hk) for bi in range(grp) for hk in range(SWA_KV_HEADS)]
    head_cols = lambda h: slice(h * SWA_HEAD_DIM, (h + 1) * SWA_HEAD_DIM)
    q_heads = lambda hk: [hk * SWA_GROUP + g for g in range(SWA_GROUP)]
    sink_cols = [_sink_column(sinks_ref, hk, t) for hk in range(SWA_KV_HEADS)]
    o = _sink_attention(
        [jnp.concatenate([q_ref[bi, :, head_cols(h)] for h in q_heads(hk)], axis=0) for bi, hk in items],
        [kks[bi][:, head_cols(hk)] for bi, hk in items], [vvs[bi][:, head_cols(hk)] for bi, hk in items],
        mask, [sink_cols[hk] for bi, hk in items])
    for idx, (bi, hk) in enumerate(items):
        for g, h in enumerate(q_heads(hk)):
            o_ref[bi, :, head_cols(h)] = o[idx][g * t:(g + 1) * t].astype(o_ref.dtype)


def _swa_sample(q, kv_new, k_cache, v_cache, sinks):
    b, t, _ = q.shape
    grp = SWA_SAMPLE_GROUP
    blk = lambda r, w: pl.BlockSpec((grp, r, w), lambda i: (i, 0, 0))
    return pl.pallas_call(
        _swa_sample_kernel,
        grid=(b // grp,),
        in_specs=[pl.BlockSpec(memory_space=pltpu.SMEM),
                  blk(t, SWA_WIDTH), blk(t, 2 * SWA_KV_WIDTH), blk(WINDOW, SWA_KV_WIDTH), blk(WINDOW, SWA_KV_WIDTH)],
        out_specs=[blk(t, SWA_WIDTH), blk(WINDOW, SWA_KV_WIDTH), blk(WINDOW, SWA_KV_WIDTH)],
        out_shape=[jax.ShapeDtypeStruct((b, t, SWA_WIDTH), BF16),
                   jax.ShapeDtypeStruct((b, WINDOW, SWA_KV_WIDTH), F32),
                   jax.ShapeDtypeStruct((b, WINDOW, SWA_KV_WIDTH), F32)],
        compiler_params=_cparams("parallel"),
        name="swa_sample",
    )(sinks, q, kv_new, k_cache, v_cache)


def _route(logits):
    lane = _iota2(logits.shape, 1).astype(F32)
    neg = -jnp.inf

    def first_argmax(vals, valid):
        v = jnp.where(valid, vals, neg)
        m = jnp.max(v, axis=-1, keepdims=True)
        idx = jnp.min(jnp.where(jnp.logical_and(valid, v == m), lane, float(LANES)), axis=-1, keepdims=True)
        return m, idx

    is_group = lane < N_GROUPS
    gmax, gidx = first_argmax(logits, is_group)
    p_group = 1.0 / jnp.sum(jnp.where(is_group, jnp.exp(logits - gmax), 0.0), axis=-1, keepdims=True)
    lo = N_GROUPS + gidx * EXPERTS_PER_GROUP
    in_group = jnp.logical_and(lane >= lo, lane < lo + EXPERTS_PER_GROUP)
    m1, i1 = first_argmax(logits, in_group)
    esum = jnp.sum(jnp.where(in_group, jnp.exp(logits - m1), 0.0), axis=-1, keepdims=True)
    m2, i2 = first_argmax(logits, jnp.logical_and(in_group, lane != i1))
    p1 = 1.0 / esum
    p2 = jnp.exp(m2 - m1) / esum
    tot = p1 + p2
    return i1 - N_GROUPS, i2 - N_GROUPS, p_group * p1 / tot, p_group * p2 / tot


def _post_mixer_kernel(oa_ref, ob_ref, ga_ref, gb_ref, x_ref, gt_ref, sc_ref, sh_ref,
                       wa_ref, wb_ref, wo_ref, gpost_ref, gpre_ref, wr_ref, br_ref, cnt0_ref,
                       x1_ref, h2_ref, rt_ref, cnt_out_ref, meta_ref, *rest):
    cnt_ref = rest[-1]
    if len(rest) == 2:
        rest[0][...] = jnp.zeros_like(rest[0])
    step = pl.program_id(0)

    @pl.when(step == 0)
    def _():
        cnt_ref[...] = cnt0_ref[...]

    merged = (ga_ref[...].astype(F32) * jnp.dot(oa_ref[...], wa_ref[...], preferred_element_type=F32)
              + gb_ref[...].astype(F32) * jnp.dot(ob_ref[...], wb_ref[...], preferred_element_type=F32))
    mix = _bdot(merged, wo_ref[...])
    x1 = x_ref[...] + gt_ref[0] * _rms(mix, gpost_ref[...])
    x1_ref[...] = x1
    h2 = _rms(x1, gpre_ref[...]) * (1.0 + sc_ref[0]) + sh_ref[0]
    _rows_to_tiles(h2_ref, h2)
    h_hi = h2.astype(BF16)
    h_lo = (h2 - h_hi.astype(F32)).astype(BF16)
    part = jnp.dot(h_hi, wr_ref[...], preferred_element_type=F32)
    logits = (part[:, :LANES] + part[:, LANES:]
              + jnp.dot(h_lo, wr_ref[:, :LANES], preferred_element_type=F32) + br_ref[...])
    ia, ib, wa, wb = _route(logits)
    lane = _iota2(logits.shape, 1)
    tm = logits.shape[0]
    lane_f = lane.astype(F32)
    hot_a = (lane_f == ia).astype(F32)
    hot_b = (lane_f == ib).astype(F32)
    hot = hot_a + hot_b
    earlier = (_iota2((tm, tm), 0) > _iota2((tm, tm), 1)).astype(BF16)
    before = jnp.dot(earlier, hot.astype(BF16), preferred_element_type=F32) + cnt_ref[...]
    rank_a = jnp.sum(hot_a * before, axis=-1, keepdims=True)
    rank_b = jnp.sum(hot_b * before, axis=-1, keepdims=True)
    cnt_ref[...] = cnt_ref[...] + jnp.sum(hot, axis=0, keepdims=True)
    cnt_out_ref[...] = cnt_ref[...]
    rt_ref[...] = jnp.where(lane == 0, ia, jnp.where(lane == 1, ib, jnp.where(lane == 2, wa, jnp.where(
        lane == 3, wb, 0.0))))
    packed = jnp.where(lane == 0, rank_a * N_EXPERTS + ia, jnp.where(lane == 1, rank_b * N_EXPERTS + ib, 0.0))
    meta_ref[0] = packed.T[0:TOP_K, :].astype(I32)


def _post_mixer(oa, ob, ga, gb, x2d, gt, sc, sh, w_a, w_b, w_o, g_post, g_pre, w_rt, b_rt, cnt0, tm,
                zero_rows=0):
    rows = x2d.shape[0]
    n_tiles = rows // tm
    row_blk = lambda w: pl.BlockSpec((tm, w), lambda i: (i, 0))
    full = lambda r, c: pl.BlockSpec((r, c), lambda i: (0, 0))
    out_specs = [row_blk(D_MODEL), pl.BlockSpec((tm * TILE_ROWS, LANES), lambda i: (i, 0)), row_blk(LANES),
                 full(1, LANES), pl.BlockSpec((1, TOP_K, tm), lambda i: (i, 0, 0))]
    out_shape = [jax.ShapeDtypeStruct((rows, D_MODEL), F32),
                 jax.ShapeDtypeStruct((rows * TILE_ROWS, LANES), F32),
                 jax.ShapeDtypeStruct((rows, LANES), F32),
                 jax.ShapeDtypeStruct((1, LANES), F32),
                 jax.ShapeDtypeStruct((n_tiles, TOP_K, tm), I32)]
    if zero_rows:
        assert zero_rows % (n_tiles * TILE_ROWS) == 0
        out_specs.append(pl.BlockSpec((zero_rows // n_tiles, LANES), lambda i: (i, 0)))
        out_shape.append(jax.ShapeDtypeStruct((zero_rows, LANES), F32))
    return pl.pallas_call(
        _post_mixer_kernel,
        grid=(n_tiles,),
        in_specs=[row_blk(GDN_WIDTH), row_blk(SWA_WIDTH), row_blk(D_MODEL), row_blk(D_MODEL), row_blk(D_MODEL),
                  _mod_spec(gt, n_tiles), _mod_spec(sc, n_tiles), _mod_spec(sh, n_tiles),
                  full(GDN_WIDTH, D_MODEL), full(SWA_WIDTH, D_MODEL), full(D_MODEL, D_MODEL),
                  full(1, D_MODEL), full(1, D_MODEL), full(D_MODEL, 2 * LANES), full(1, LANES), full(1, LANES)],
        out_specs=out_specs,
        out_shape=out_shape,
        scratch_shapes=[pltpu.VMEM((1, LANES), F32)],
        compiler_params=_cparams("arbitrary"),
        name="post_mixer",
    )(oa, ob, ga, gb, x2d, gt, sc, sh, w_a, w_b, w_o,
      g_post.reshape(1, D_MODEL), g_pre.reshape(1, D_MODEL), w_rt, b_rt, cnt0)


TILE_ROWS = D_MODEL // LANES


def _tiles_to_rows(ref, first, rows):
    base = first * TILE_ROWS
    return jnp.concatenate([ref[pl.ds(base + c, rows, stride=TILE_ROWS), :] for c in range(TILE_ROWS)], axis=1)


def _rows_to_tiles(ref, mat):
    rows = mat.shape[0]
    for c in range(TILE_ROWS):
        ref[pl.ds(c, rows, stride=TILE_ROWS), :] = mat[:, c * LANES:(c + 1) * LANES]


def _tile_copy_loop(n, copies, start):
    def body(t, carry):
        for j, cp in enumerate(copies(t)):
            if start:
                cp.start(priority=j % 2)
            else:
                cp.wait()
        return carry

    lax.fori_loop(0, n, body, 0, unroll=8)


def _slots_kernel(pstart_ref, meta_ref, o_ref):
    packed = meta_ref[...]
    expert = packed & (N_EXPERTS - 1)
    first = jnp.zeros_like(packed)
    for e in range(N_EXPERTS):
        first = jnp.where(expert == e, pstart_ref[e], first)
    o_ref[...] = first + lax.shift_right_logical(packed, N_EXPERTS.bit_length() - 1)


def _slots(meta, pstarts):
    return pl.pallas_call(
        _slots_kernel,
        grid_spec=pltpu.PrefetchScalarGridSpec(
            num_scalar_prefetch=1,
            grid=(1,),
            in_specs=[pl.BlockSpec(meta.shape, lambda i, p: (0, 0, 0))],
            out_specs=pl.BlockSpec(meta.shape, lambda i, p: (0, 0, 0))),
        out_shape=jax.ShapeDtypeStruct(meta.shape, I32),
        compiler_params=_cparams("arbitrary"),
        name="moe_slots",
    )(pstarts, meta)


def _dispatch_kernel(slot_ref, h_ref, xs_in_hbm, xs_hbm, sem):
    del xs_in_hbm
    i = pl.program_id(0)
    tm = h_ref.shape[0] // TILE_ROWS
    sub_tiles = tm // ROW_TILE

    for start in (True, False):
        for j in range(sub_tiles):
            base = (i * sub_tiles + j) * TOP_K * ROW_TILE

            def copies(t, j=j, base=base):
                row = pl.multiple_of((j * ROW_TILE + t) * TILE_ROWS, TILE_ROWS)
                return [pltpu.make_async_copy(h_ref.at[pl.ds(row, TILE_ROWS)],
                                              xs_hbm.at[slot_ref[base + k * ROW_TILE + t]], sem)
                        for k in range(TOP_K)]

            _tile_copy_loop(ROW_TILE, copies, start)


def _dispatch(slots, h_tiles, xs, tm):
    n_tiles = h_tiles.shape[0] // (tm * TILE_ROWS)
    return pl.pallas_call(
        _dispatch_kernel,
        grid_spec=pltpu.PrefetchScalarGridSpec(
            num_scalar_prefetch=1,
            grid=(n_tiles,),
            in_specs=[pl.BlockSpec((tm * TILE_ROWS, LANES), lambda i, s: (i, 0)),
                      pl.BlockSpec(memory_space=pl.ANY)],
            out_specs=pl.BlockSpec(memory_space=pl.ANY),
            scratch_shapes=[pltpu.SemaphoreType.DMA(())]),
        out_shape=jax.ShapeDtypeStruct(xs.shape, xs.dtype),
        input_output_aliases={2: 0},
        compiler_params=_cparams("arbitrary"),
        name="moe_dispatch",
    )(slots, h_tiles, xs)


def _moe_kernel(blk_e_ref, n_used_ref, x_ref, wg_ref, wu_ref, wd_ref, y_ref, wgb, wub, wdb):
    b = pl.program_id(0)
    rows = x_ref.shape[0] // TILE_ROWS
    changed = jnp.logical_or(b == 0, blk_e_ref[b] != blk_e_ref[jnp.maximum(b - 1, 0)])

    @pl.when(changed)
    def _():
        wgb[...] = wg_ref[0].astype(BF16)
        wub[...] = wu_ref[0].astype(BF16)
        wdb[...] = wd_ref[0].astype(BF16)

    @pl.when(b < n_used_ref[0])
    def _():
        x = _tiles_to_rows(x_ref, 0, rows).astype(BF16)
        gate = jnp.dot(x, wgb[...], preferred_element_type=F32)
        up = jnp.dot(x, wub[...], preferred_element_type=F32)
        _rows_to_tiles(y_ref, _bdot(_silu(gate) * up, wdb[...]))

    @pl.when(b >= n_used_ref[0])
    def _():
        y_ref[...] = jnp.zeros_like(y_ref)


def _moe(xs_tiles, blk_e, n_used, w_gate, w_up, w_down):
    n_blocks = blk_e.shape[0]
    rows = MOE_ROWS
    wspec = lambda r, c: pl.BlockSpec((1, r, c), lambda b, be, nu: (be[b], 0, 0))
    xspec = pl.BlockSpec((rows * TILE_ROWS, LANES), lambda b, be, nu: (b, 0))
    return pl.pallas_call(
        _moe_kernel,
        grid_spec=pltpu.PrefetchScalarGridSpec(
            num_scalar_prefetch=2,
            grid=(n_blocks,),
            in_specs=[xspec, wspec(D_MODEL, EXPERT_FF), wspec(D_MODEL, EXPERT_FF), wspec(EXPERT_FF, D_MODEL)],
            out_specs=xspec,
            scratch_shapes=[pltpu.VMEM((D_MODEL, EXPERT_FF), BF16),
                            pltpu.VMEM((D_MODEL, EXPERT_FF), BF16),
                            pltpu.VMEM((EXPERT_FF, D_MODEL), BF16)]),
        out_shape=jax.ShapeDtypeStruct(xs_tiles.shape, F32),
        compiler_params=_cparams("arbitrary"),
        name="moe_experts",
    )(blk_e, n_used, xs_tiles, w_gate, w_up, w_down)


def _combine_kernel(slot_ref, y_hbm, x1_ref, rt_ref, gt_ref, gpost_ref, o_ref, ybuf, sems):
    i = pl.program_id(0)
    n = pl.num_programs(0)
    rows = ybuf.shape[1] // TILE_ROWS
    slot = i % 2

    def gather(step, buf_slot, start):
        def copy(r):
            dst = ybuf.at[buf_slot, pl.ds(pl.multiple_of(r * TILE_ROWS, TILE_ROWS), TILE_ROWS)]
            return pltpu.make_async_copy(y_hbm.at[slot_ref[step * rows + r]], dst, sems.at[buf_slot])

        _tile_copy_loop(rows // 2, lambda t: [copy(2 * t), copy(2 * t + 1)], start)

    @pl.when(i == 0)
    def _():
        gather(0, 0, True)

    @pl.when(i + 1 < n)
    def _():
        gather(i + 1, 1 - slot, True)

    gather(i, slot, False)
    half = rows // 2
    rt = rt_ref[...]
    buf = ybuf.at[slot]
    f = rt[:, 2:3] * _tiles_to_rows(buf, 0, half) + rt[:, 3:4] * _tiles_to_rows(buf, half, half)
    o_ref[...] = x1_ref[...] + gt_ref[0] * _rms(f, gpost_ref[...])


def _combine(slots, yb, x1, rt, gt, g_post):
    rows = x1.shape[0]
    tm = ROW_TILE
    n_tiles = rows // tm
    tiles_per_mod = n_tiles // gt.shape[0]
    return pl.pallas_call(
        _combine_kernel,
        grid_spec=pltpu.PrefetchScalarGridSpec(
            num_scalar_prefetch=1,
            grid=(n_tiles,),
            in_specs=[pl.BlockSpec(memory_space=pl.ANY),
                      pl.BlockSpec((tm, D_MODEL), lambda i, s: (i, 0)),
                      pl.BlockSpec((tm, LANES), lambda i, s: (i, 0)),
                      pl.BlockSpec((1, gt.shape[1], D_MODEL), lambda i, s: (i // tiles_per_mod, 0, 0)),
                      pl.BlockSpec((1, D_MODEL), lambda i, s: (0, 0))],
            out_specs=pl.BlockSpec((tm, D_MODEL), lambda i, s: (i, 0)),
            scratch_shapes=[pltpu.VMEM((2, TOP_K * tm * TILE_ROWS, LANES), F32),
                            pltpu.SemaphoreType.DMA((2,))]),
        out_shape=jax.ShapeDtypeStruct((rows, D_MODEL), F32),
        compiler_params=_cparams("arbitrary"),
        name="moe_combine",
    )(slots, yb, x1, rt, gt, g_post.reshape(1, D_MODEL))


def _dispatch_plan(counts, n_tok):
    pcounts = (counts + MOE_ROWS - 1) // MOE_ROWS * MOE_ROWS
    pends = jnp.cumsum(pcounts)
    pstarts = (pends - pcounts).astype(I32)
    n_blocks = n_tok * TOP_K // MOE_ROWS + N_EXPERTS
    blk_start = jnp.arange(n_blocks, dtype=I32) * MOE_ROWS
    blk_e = jnp.minimum(jnp.sum(blk_start[:, None] >= pends[None, :], axis=1), N_EXPERTS - 1).astype(I32)
    n_used = (pends[-1] // MOE_ROWS).astype(I32).reshape(1)
    return blk_e, n_used, pstarts


_AB_FIRST = 4 * GDN_QK_WIDTH
_AB_COUNT = 2 * GDN_HEADS


def _prep_in_weight_kernel(w_ref, o_ref):
    x = w_ref[...]
    last = pl.program_id(0) == pl.num_programs(0) - 1
    keep = jnp.logical_or(jnp.logical_not(last), _iota2(x.shape, 0) < _AB_COUNT)
    o_ref[...] = jnp.where(keep, x, 0.0).T.astype(BF16)


def _prep_in_weight(w_in_t):
    n_blocks = IN_COLS // PROJ_TILE
    shifted_from = _AB_FIRST // PROJ_TILE

    def src_row(i):
        return jnp.where(i == n_blocks - 1, _AB_FIRST,
                         jnp.where(i >= shifted_from, i * PROJ_TILE + _AB_COUNT, i * PROJ_TILE))

    return pl.pallas_call(
        _prep_in_weight_kernel,
        grid=(n_blocks,),
        in_specs=[pl.BlockSpec((pl.Element(PROJ_TILE), pl.Element(D_MODEL)),
                               lambda i: (pl.multiple_of(src_row(i), _AB_COUNT), 0))],
        out_specs=pl.BlockSpec((D_MODEL, PROJ_TILE), lambda i: (0, i)),
        out_shape=jax.ShapeDtypeStruct((D_MODEL, IN_COLS), BF16),
        compiler_params=_cparams("parallel"),
        name="prep_in_weight",
    )(w_in_t)


def _head_param_tile(a_log, dt_bias):
    tile = jnp.zeros((8, LANES), F32)
    return tile.at[0, :GDN_HEADS].set(a_log.astype(F32)).at[1, :GDN_HEADS].set(dt_bias.astype(F32))


def _router_weight(w_group, b_group, w_router, b_router):
    w = jnp.zeros((D_MODEL, LANES), F32)
    w = w.at[:, :N_GROUPS].set(w_group).at[:, N_GROUPS:N_GROUPS + N_EXPERTS].set(w_router)
    b = jnp.zeros((1, LANES), F32)
    b = b.at[0, :N_GROUPS].set(b_group).at[0, N_GROUPS:N_GROUPS + N_EXPERTS].set(b_router)
    w_hi = w.astype(BF16)
    w_lo = (w - w_hi.astype(F32)).astype(BF16)
    return jnp.concatenate([w_hi, w_lo], axis=1), b


def kernel(x_prompt, x_sample, state_gdn, state_conv, cache_k_win, cache_v_win, c_prompt, c_sample, w_ada, b_ada, g_mix_pre, g_mix_post, g_ffn_pre, g_ffn_post, w_in, conv_w, a_log, dt_bias, gdn_norm, sinks, w_br_gdn, w_br_swa, w_out, w_group, b_group, w_router, b_router, w_gate, w_up, w_down):
    depth = w_ada.shape[0]
    assert depth == 1, "single-layer trunk"
    bp, tp, _ = x_prompt.shape
    bs, ts, _ = x_sample.shape
    n_p = bp * tp
    n_s = bs * ts
    tm = ROW_TILE
    assert tp % tm == 0 and n_s % tm == 0 and ts >= GDN_CONV - 1 and ts + GDN_CONV - 1 <= GDN_SAMPLE_CHUNK
    assert ts & (ts - 1) == 0 and ts <= BF16_ROWS

    c_all = jnp.concatenate([c_prompt, c_sample], axis=0)
    c_rows = -(-c_all.shape[0] // 8) * 8
    c_all = jnp.pad(c_all, ((0, c_rows - c_all.shape[0]), (0, 0)))
    mod = _adaln(c_all, w_ada[0], b_ada[0])
    mods_p = [m[:bp].reshape(bp, 1, D_MODEL) for m in jnp.split(mod, 6, axis=-1)]
    mods_s = [jnp.repeat(m[bp:bp + bs], ts, axis=0).reshape(n_s // tm, tm, D_MODEL)
              for m in jnp.split(mod, 6, axis=-1)]

    w_prep = _prep_in_weight(jnp.swapaxes(w_in[0], 0, 1))
    head_params = _head_param_tile(a_log[0], dt_bias[0])
    w_a, w_b, w_o = w_br_gdn[0].astype(BF16), w_br_swa[0].astype(BF16), w_out[0].astype(BF16)
    w_rt, b_rt = _router_weight(w_group[0], b_group[0], w_router[0], b_router[0])
    sinks0 = sinks[0].astype(F32)

    xp2d = x_prompt.reshape(n_p, D_MODEL)
    sh1, sc1, gt1, sh2, sc2, gt2 = mods_p
    qkv_p, zs_p, qb_p, kvb_p, ga_p, gb_p, ab_p, conv_tail_p = _inproj(
        xp2d, g_mix_pre[0], sc1, sh1, w_prep, INPROJ_ROWS, conv_w=conv_w[0].astype(F32), n_seq=bp)
    qkv_p3 = qkv_p.reshape(bp, tp, GDN_CONV_DIM)
    u, w, qd, kd, qk, ge = _gdn_prep(qkv_p3, ab_p.reshape(bp, tp, LANES), head_params)
    oa_p, s_prompt = _gdn_scan(u, w, qd, kd, qk, ge, zs_p.reshape(bp, tp, GDN_WIDTH), gdn_norm[0])
    kvb_p3 = kvb_p.reshape(bp, tp, 2 * SWA_KV_WIDTH)
    ob_p = _swa_prompt(qb_p.reshape(bp, tp, SWA_WIDTH), kvb_p3, sinks0)
    n_slots = ((n_p + n_s) * TOP_K // MOE_ROWS + N_EXPERTS) * MOE_ROWS
    x1_p, h2_p, rt_p, cnt_p, meta_p, xs = _post_mixer(
        oa_p.reshape(n_p, GDN_WIDTH), ob_p.reshape(n_p, SWA_WIDTH), ga_p, gb_p, xp2d, gt1, sc2, sh2,
        w_a, w_b, w_o, g_mix_post[0], g_ffn_pre[0], w_rt, b_rt, jnp.zeros((1, LANES), F32), tm,
        zero_rows=n_slots * TILE_ROWS)

    xs2d = x_sample.reshape(n_s, D_MODEL)
    sh1s, sc1s, gt1s, sh2s, sc2s, gt2s = mods_s
    qkv_s, zs_s, qb_s, kvb_s, ga_s, gb_s, ab_s = _inproj(xs2d, g_mix_pre[0], sc1s, sh1s, w_prep, tm)
    cc = GDN_SAMPLE_CHUNK
    pad_rows = cc - ts - (GDN_CONV - 1)
    qkv_s3 = qkv_s.reshape(bs, ts, GDN_CONV_DIM)
    xp_s = jnp.concatenate([jnp.zeros((bs, pad_rows, GDN_CONV_DIM), BF16), state_conv[0].astype(BF16), qkv_s3],
                           axis=1)
    front = lambda a: jnp.pad(a, ((0, 0), (cc - ts, 0), (0, 0)))
    oa_s16, s_sample = _gdn_sample(xp_s, front(ab_s.reshape(bs, ts, LANES)), front(zs_s.reshape(bs, ts, GDN_WIDTH)),
                                   state_gdn[0].astype(F32), conv_w[0], head_params, gdn_norm[0], ts)
    oa_s = oa_s16[:, cc - ts:, :].reshape(n_s, GDN_WIDTH)
    ob_s, k_new_s, v_new_s = _swa_sample(
        qb_s.reshape(bs, ts, SWA_WIDTH), kvb_s.reshape(bs, ts, 2 * SWA_KV_WIDTH),
        cache_k_win[0].reshape(bs, WINDOW, SWA_KV_WIDTH).astype(F32),
        cache_v_win[0].reshape(bs, WINDOW, SWA_KV_WIDTH).astype(F32), sinks0)
    x1_s, h2_s, rt_s, cnt_all, meta_s = _post_mixer(
        oa_s, ob_s.reshape(n_s, SWA_WIDTH), ga_s, gb_s, xs2d, gt1s, sc2s, sh2s,
        w_a, w_b, w_o, g_mix_post[0], g_ffn_pre[0], w_rt, b_rt, cnt_p, tm)

    blk_e, n_used, pstarts = _dispatch_plan(cnt_all[0, :N_EXPERTS].astype(I32), n_p + n_s)
    slots_p = _slots(meta_p, pstarts).reshape(-1)
    slots_s = _slots(meta_s, pstarts).reshape(-1)
    xs = _dispatch(slots_p, h2_p, xs.reshape(n_slots, TILE_ROWS, LANES), min(DISPATCH_ROWS, n_p))
    xs = _dispatch(slots_s, h2_s, xs, min(DISPATCH_ROWS, n_s))
    yb = _moe(xs.reshape(n_slots * TILE_ROWS, LANES), blk_e, n_used, w_gate[0], w_up[0], w_down[0])
    yb = yb.reshape(n_slots, TILE_ROWS, LANES)
    y_p = _combine(slots_p, yb, x1_p, rt_p, gt2, g_ffn_post[0])
    y_s = _combine(slots_s, yb, x1_s, rt_s, gt2s, g_ffn_post[0])

    f32 = lambda a: a.astype(F32)
    kv_tail = kvb_p3[:, tp - WINDOW:, :]
    kv_heads = lambda a: f32(a).reshape(a.shape[0], WINDOW, SWA_KV_HEADS, SWA_HEAD_DIM)[None]
    return (y_p.reshape(bp, tp, D_MODEL), y_s.reshape(bs, ts, D_MODEL),
            s_prompt[None], conv_tail_p[:, 8 - (GDN_CONV - 1):, :][None],
            kv_heads(kv_tail[:, :, :SWA_KV_WIDTH]), kv_heads(kv_tail[:, :, SWA_KV_WIDTH:]),
            s_sample[None], f32(qkv_s3[:, ts - (GDN_CONV - 1):, :])[None],
            kv_heads(k_new_s), kv_heads(v_new_s))
```
